```python
import math
import jax, jax.numpy as jnp
from jax import lax
import numpy as np

D_MODEL = 4096
BATCH = 8
SEQ = 4096
DEPTH = 4

N_MIXERS = 4
N_META = 16
BRANCH = D_MODEL
ALPHA = (2.0 * DEPTH) ** 0.25
BETA = (8.0 * DEPTH) ** -0.25
LN_EPS = 1e-5
RMS_EPS = 1e-6
ROPE_BASE = 10000.0

LRU_HEADS = 16
LRU_BLOCK = BRANCH // LRU_HEADS
CONV_W = 4
LRU_C = 8.0

POOL_WINDOWS = (2, 4, 8, 16)
POOL_GROUP = BRANCH // len(POOL_WINDOWS)

MLA_HEADS = 32
MLA_NOPE = 128
MLA_ROPE = 64
MLA_V = BRANCH // MLA_HEADS
Q_LORA = 1024
KV_LORA = 512
ATT_BLOCK = 128

RET_HEADS = 16
RET_DK = BRANCH // RET_HEADS
RET_DV = BRANCH // RET_HEADS
RET_CHUNK = 128

kernel_name = 'hybrid_interleaved_rglru_pool_mla_retention'


def layer_norm(x, g, b):
    xf = x.astype(jnp.float32)
    mu = jnp.mean(xf, -1, keepdims=True)
    var = jnp.mean(jnp.square(xf - mu), -1, keepdims=True)
    return ((xf - mu) * lax.rsqrt(var + LN_EPS) * g.astype(jnp.float32) + b.astype(jnp.float32)).astype(x.dtype)


def rms_norm(x, g):
    xf = x.astype(jnp.float32)
    return (xf * lax.rsqrt(jnp.mean(jnp.square(xf), -1, keepdims=True) + RMS_EPS) * g.astype(jnp.float32)).astype(x.dtype)


def rope_tables(T, d):
    inv = ROPE_BASE ** (-jnp.arange(0, d, 2, dtype=jnp.float32) / d)
    ang = jnp.arange(T, dtype=jnp.float32)[:, None] * inv[None, :]
    return jnp.cos(ang), jnp.sin(ang)


def apply_rope(x, cos, sin):
    x1, x2 = jnp.split(x.astype(jnp.float32), 2, axis=-1)
    return jnp.concatenate([x1 * cos - x2 * sin, x2 * cos + x1 * sin], -1).astype(x.dtype)


def rglru_mixer(h, w_in, conv_w, conv_b, w_a, b_a, w_x, b_x, lam, w_out):
    B, T, _ = h.shape
    u, g = jnp.split(h @ w_in, 2, axis=-1)
    u = lax.conv_general_dilated(u, conv_w, window_strides=(1,), padding=[(CONV_W - 1, 0)],
                                 dimension_numbers=('NWC', 'WIO', 'NWC'),
                                 feature_group_count=BRANCH) + conv_b
    ub = u.reshape(B, T, LRU_HEADS, LRU_BLOCK)
    r = jax.nn.sigmoid(jnp.einsum('bthi,hij->bthj', ub, w_a).reshape(B, T, BRANCH) + b_a)
    i = jax.nn.sigmoid(jnp.einsum('bthi,hij->bthj', ub, w_x).reshape(B, T, BRANCH) + b_x)
    log_a = LRU_C * r.astype(jnp.float32) * (-jax.nn.softplus(-lam.astype(jnp.float32)))
    a = jnp.exp(log_a)
    xin = (u * i).astype(jnp.float32) * jnp.sqrt(-jnp.expm1(2.0 * log_a))

    def combine(left, right):
        a1, b1 = left
        a2, b2 = right
        return a1 * a2, a2 * b1 + b2

    _, hs = lax.associative_scan(combine, (a, xin), axis=1)
    return (hs.astype(h.dtype) * jax.nn.silu(g)) @ w_out


def pool_mixer(h, w_in, w_grp, scale, w_out):
    B, T, _ = h.shape
    u, g = jnp.split(h @ w_in, 2, axis=-1)
    uf = u.astype(jnp.float32)
    c = jnp.cumsum(uf, axis=1)
    t1 = jnp.arange(1, T + 1, dtype=jnp.float32)[:, None]
    outs = []
    for gi, w in enumerate(POOL_WINDOWS):
        sl = slice(gi * POOL_GROUP, (gi + 1) * POOL_GROUP)
        cg = c[..., sl]
        shifted = jnp.pad(cg, ((0, 0), (w, 0), (0, 0)))[:, :T]
        mean = (cg - shifted) / jnp.minimum(t1, float(w))
        outs.append(mean - uf[..., sl])
    p = jnp.stack(outs, axis=2).astype(h.dtype)
    mixed = jnp.einsum('btgc,gcd->btgd', p, w_grp).reshape(B, T, BRANCH) * scale
    return (mixed * jax.nn.silu(g)) @ w_out


def mla_mixer(h, w_in, q_norm, w_uq, kv_norm, w_ukv, w_out, cos, sin):
    B, T, _ = h.shape
    g, cq, ckv, kr = jnp.split(h @ w_in, [BRANCH, BRANCH + Q_LORA, BRANCH + Q_LORA + KV_LORA], axis=-1)
    q = (rms_norm(cq, q_norm) @ w_uq).reshape(B, T, MLA_HEADS, MLA_NOPE + MLA_ROPE)
    q_nope = q[..., :MLA_NOPE]
    q_rope = apply_rope(q[..., MLA_NOPE:], cos[:, None, :], sin[:, None, :])
    kv = (rms_norm(ckv, kv_norm) @ w_ukv).reshape(B, T, MLA_HEADS, MLA_NOPE + MLA_V)
    k_nope, v = kv[..., :MLA_NOPE], kv[..., MLA_NOPE:]
    k_rope = apply_rope(kr, cos, sin)
    sc = (MLA_NOPE + MLA_ROPE) ** -0.5
    n_blk = (T - N_META) // ATT_BLOCK
    blocks = [(0, N_META)] + [(N_META + j * ATT_BLOCK, N_META + (j + 1) * ATT_BLOCK) for j in range(n_blk)]
    outs = []
    for qs, qe in blocks:
        s = (jnp.einsum('bqhd,bkhd->bhqk', q_nope[:, qs:qe], k_nope[:, :qe], preferred_element_type=jnp.float32)
             + jnp.einsum('bqhd,bkd->bhqk', q_rope[:, qs:qe], k_rope[:, :qe], preferred_element_type=jnp.float32)) * sc
        mask = jnp.arange(qs, qe)[:, None] >= jnp.arange(qe)[None, :]
        s = jnp.where(mask, s, -1e30)
        pr = jax.nn.softmax(s, axis=-1).astype(v.dtype)
        outs.append(jnp.einsum('bhqk,bkhd->bqhd', pr, v[:, :qe]))
    o = jnp.concatenate(outs, axis=1).reshape(B, T, BRANCH)
    return (o * jax.nn.silu(g)) @ w_out


def retention_mixer(h, w_in, w_out, cos, sin):
    B, T, _ = h.shape
    q, k, v, g = jnp.split(h @ w_in, 4, axis=-1)
    q = apply_rope(q.reshape(B, T, RET_HEADS, RET_DK), cos[:, None, :], sin[:, None, :])
    k = apply_rope(k.reshape(B, T, RET_HEADS, RET_DK), cos[:, None, :], sin[:, None, :]) * (RET_DK ** -0.5)
    v = v.reshape(B, T, RET_HEADS, RET_DV)
    log_g = jnp.log(1.0 - 2.0 ** (-5.0 - jnp.arange(RET_HEADS, dtype=jnp.float32)))
    qf, kf, vf = [t.astype(jnp.float32).transpose(0, 2, 1, 3) for t in (q, k, v)]

    def chunk_step(state, qkv):
        qc, kc, vc = qkv
        C = qc.shape[2]
        j = jnp.arange(C, dtype=jnp.float32)
        diff = j[:, None] - j[None, :]
        decay = jnp.where(diff >= 0, jnp.exp(jnp.maximum(diff, 0.0)[None] * log_g[:, None, None]), 0.0)
        intra = jnp.einsum('bhcm,bhme->bhce', jnp.einsum('bhcd,bhmd->bhcm', qc, kc) * decay, vc)
        inter = jnp.einsum('bhcd,bhde->bhce', qc, state) * jnp.exp((j + 1.0)[None, :, None] * log_g[:, None, None])
        kw = kc * jnp.exp((C - 1.0 - j)[None, :, None] * log_g[:, None, None])
        new_state = jnp.exp(C * log_g)[None, :, None, None] * state + jnp.einsum('bhcd,bhce->bhde', kw, vc)
        return new_state, intra + inter

    state0 = jnp.zeros((B, RET_HEADS, RET_DK, RET_DV), jnp.float32)
    state, o_meta = chunk_step(state0, (qf[:, :, :N_META], kf[:, :, :N_META], vf[:, :, :N_META]))
    nc = (T - N_META) // RET_CHUNK

    def to_chunks(t):
        return t[:, :, N_META:].reshape(B, RET_HEADS, nc, RET_CHUNK, t.shape[-1]).transpose(2, 0, 1, 3, 4)

    _, o_real = lax.scan(chunk_step, state, (to_chunks(qf), to_chunks(kf), to_chunks(vf)))
    o_real = o_real.transpose(1, 2, 0, 3, 4).reshape(B, RET_HEADS, nc * RET_CHUNK, RET_DV)
    o = jnp.concatenate([o_meta, o_real], axis=2)
    mu = jnp.mean(o, -1, keepdims=True)
    var = jnp.mean(jnp.square(o - mu), -1, keepdims=True)
    o = (o - mu) * lax.rsqrt(var + LN_EPS)
    o = o.transpose(0, 2, 1, 3).reshape(B, T, BRANCH).astype(h.dtype)
    return (o * jax.nn.silu(g)) @ w_out


def _fwd_setup_inputs(seed: int = 0) -> dict:
    key = jax.random.key(seed)
    ks = iter(jax.random.split(key, 48))

    def nrm(shape, s):
        return jax.random.normal(next(ks), shape, jnp.float32) * s

    def gain(n):
        return 1.0 + nrm((n,), 0.02)

    d_in = D_MODEL ** -0.5
    d_out = BETA * BRANCH ** -0.5
    rad = jax.random.uniform(next(ks), (BRANCH,), jnp.float32, 0.9, 0.999)
    a0 = rad ** (1.0 / LRU_C)
    lam = jnp.log(a0) - jnp.log1p(-a0)
    return {
        'x': nrm((BATCH, SEQ, D_MODEL), 1.0),
        'meta_tokens': nrm((N_META, D_MODEL), 1.0),
        'l0_w_in': nrm((D_MODEL, 2 * BRANCH), d_in),
        'l0_conv_w': nrm((CONV_W, 1, BRANCH), CONV_W ** -0.5),
        'l0_conv_b': nrm((BRANCH,), 0.01),
        'l0_w_a': nrm((LRU_HEADS, LRU_BLOCK, LRU_BLOCK), LRU_BLOCK ** -0.5),
        'l0_b_a': nrm((BRANCH,), 0.01),
        'l0_w_x': nrm((LRU_HEADS, LRU_BLOCK, LRU_BLOCK), LRU_BLOCK ** -0.5),
        'l0_b_x': nrm((BRANCH,), 0.01),
        'l0_lam': lam,
        'l0_w_out': nrm((BRANCH, D_MODEL), d_out),
        'l0_ln_g': gain(D_MODEL),
        'l0_ln_b': nrm((D_MODEL,), 0.01),
        'l1_w_in': nrm((D_MODEL, 2 * BRANCH), d_in),
        'l1_w_grp': nrm((len(POOL_WINDOWS), POOL_GROUP, POOL_GROUP), POOL_GROUP ** -0.5),
        'l1_scale': gain(BRANCH),
        'l1_w_out': nrm((BRANCH, D_MODEL), d_out),
        'l1_ln_g': gain(D_MODEL),
        'l1_ln_b': nrm((D_MODEL,), 0.01),
        'l2_w_in': nrm((D_MODEL, BRANCH + Q_LORA + KV_LORA + MLA_ROPE), d_in),
        'l2_q_norm': gain(Q_LORA),
        'l2_w_uq': nrm((Q_LORA, MLA_HEADS * (MLA_NOPE + MLA_ROPE)), Q_LORA ** -0.5),
        'l2_kv_norm': gain(KV_LORA),
        'l2_w_ukv': nrm((KV_LORA, MLA_HEADS * (MLA_NOPE + MLA_V)), KV_LORA ** -0.5),
        'l2_w_out': nrm((BRANCH, D_MODEL), d_out),
        'l2_ln_g': gain(D_MODEL),
        'l2_ln_b': nrm((D_MODEL,), 0.01),
        'l3_w_in': nrm((D_MODEL, 4 * BRANCH), d_in),
        'l3_w_out': nrm((BRANCH, D_MODEL), d_out),
        'l3_ln_g': gain(D_MODEL),
        'l3_ln_b': nrm((D_MODEL,), 0.01),
    }


def _fwd_reference(x, meta_tokens,
              l0_w_in, l0_conv_w, l0_conv_b, l0_w_a, l0_b_a, l0_w_x, l0_b_x, l0_lam, l0_w_out, l0_ln_g, l0_ln_b,
              l1_w_in, l1_w_grp, l1_scale, l1_w_out, l1_ln_g, l1_ln_b,
              l2_w_in, l2_q_norm, l2_w_uq, l2_kv_norm, l2_w_ukv, l2_w_out, l2_ln_g, l2_ln_b,
              l3_w_in, l3_w_out, l3_ln_g, l3_ln_b):
    B = x.shape[0]
    meta = jnp.broadcast_to(meta_tokens.astype(x.dtype)[None], (B, N_META, D_MODEL))
    h = jnp.concatenate([meta, x], axis=1)
    T = h.shape[1]
    cos_mla, sin_mla = rope_tables(T, MLA_ROPE)
    cos_ret, sin_ret = rope_tables(T, RET_DK)
    layer_fns = (
        lambda t: rglru_mixer(t, l0_w_in, l0_conv_w, l0_conv_b, l0_w_a, l0_b_a, l0_w_x, l0_b_x, l0_lam, l0_w_out),
        lambda t: pool_mixer(t, l1_w_in, l1_w_grp, l1_scale, l1_w_out),
        lambda t: mla_mixer(t, l2_w_in, l2_q_norm, l2_w_uq, l2_kv_norm, l2_w_ukv, l2_w_out, cos_mla, sin_mla),
        lambda t: retention_mixer(t, l3_w_in, l3_w_out, cos_ret, sin_ret),
    )
    norms = ((l0_ln_g, l0_ln_b), (l1_ln_g, l1_ln_b), (l2_ln_g, l2_ln_b), (l3_ln_g, l3_ln_b))
    for i in range(DEPTH):
        h = layer_norm(ALPHA * h + layer_fns[i](h), norms[i][0], norms[i][1])
    return h[:, N_META:]


import jax as _jax
import jax.numpy as _jnp

TWIN_FORMAT = 'train_step'
FWD_PARAMS = ['x', 'meta_tokens', 'l0_w_in', 'l0_conv_w', 'l0_conv_b', 'l0_w_a', 'l0_b_a', 'l0_w_x', 'l0_b_x', 'l0_lam', 'l0_w_out', 'l0_ln_g', 'l0_ln_b', 'l1_w_in', 'l1_w_grp', 'l1_scale', 'l1_w_out', 'l1_ln_g', 'l1_ln_b', 'l2_w_in', 'l2_q_norm', 'l2_w_uq', 'l2_kv_norm', 'l2_w_ukv', 'l2_w_out', 'l2_ln_g', 'l2_ln_b', 'l3_w_in', 'l3_w_out', 'l3_ln_g', 'l3_ln_b']
TWIN_WEIGHTS = ['meta_tokens', 'l0_w_in', 'l0_conv_w', 'l0_conv_b', 'l0_w_a', 'l0_b_a', 'l0_w_x', 'l0_b_x', 'l0_lam', 'l0_w_out', 'l0_ln_g', 'l0_ln_b', 'l1_w_in', 'l1_w_grp', 'l1_scale', 'l1_w_out', 'l1_ln_g', 'l1_ln_b', 'l2_w_in', 'l2_q_norm', 'l2_w_uq', 'l2_kv_norm', 'l2_w_ukv', 'l2_w_out', 'l2_ln_g', 'l2_ln_b', 'l3_w_in', 'l3_w_out', 'l3_ln_g', 'l3_ln_b']
TWIN_DIFF_INPUT = 'x'
TWIN_INPUTS = ['x', 'meta_tokens', 'l0_w_in', 'l0_conv_w', 'l0_conv_b', 'l0_w_a', 'l0_b_a', 'l0_w_x', 'l0_b_x', 'l0_lam', 'l0_w_out', 'l0_ln_g', 'l0_ln_b', 'l1_w_in', 'l1_w_grp', 'l1_scale', 'l1_w_out', 'l1_ln_g', 'l1_ln_b', 'l2_w_in', 'l2_q_norm', 'l2_w_uq', 'l2_kv_norm', 'l2_w_ukv', 'l2_w_out', 'l2_ln_g', 'l2_ln_b', 'l3_w_in', 'l3_w_out', 'l3_ln_g', 'l3_ln_b', 'loss_target', 'm_meta_tokens', 'm_l0_w_in', 'm_l0_conv_w', 'm_l0_conv_b', 'm_l0_w_a', 'm_l0_b_a', 'm_l0_w_x', 'm_l0_b_x', 'm_l0_lam', 'm_l0_w_out', 'm_l0_ln_g', 'm_l0_ln_b', 'm_l1_w_in', 'm_l1_w_grp', 'm_l1_scale', 'm_l1_w_out', 'm_l1_ln_g', 'm_l1_ln_b', 'm_l2_w_in', 'm_l2_q_norm', 'm_l2_w_uq', 'm_l2_kv_norm', 'm_l2_w_ukv', 'm_l2_w_out', 'm_l2_ln_g', 'm_l2_ln_b', 'm_l3_w_in', 'm_l3_w_out', 'm_l3_ln_g', 'm_l3_ln_b', 'v_meta_tokens', 'v_l0_w_in', 'v_l0_conv_w', 'v_l0_conv_b', 'v_l0_w_a', 'v_l0_b_a', 'v_l0_w_x', 'v_l0_b_x', 'v_l0_lam', 'v_l0_w_out', 'v_l0_ln_g', 'v_l0_ln_b', 'v_l1_w_in', 'v_l1_w_grp', 'v_l1_scale', 'v_l1_w_out', 'v_l1_ln_g', 'v_l1_ln_b', 'v_l2_w_in', 'v_l2_q_norm', 'v_l2_w_uq', 'v_l2_kv_norm', 'v_l2_w_ukv', 'v_l2_w_out', 'v_l2_ln_g', 'v_l2_ln_b', 'v_l3_w_in', 'v_l3_w_out', 'v_l3_ln_g', 'v_l3_ln_b']
TWIN_OUTPUTS = ['loss', 'grad_x', 'grad_meta_tokens', 'grad_l0_w_in', 'grad_l0_conv_w', 'grad_l0_conv_b', 'grad_l0_w_a', 'grad_l0_b_a', 'grad_l0_w_x', 'grad_l0_b_x', 'grad_l0_lam', 'grad_l0_w_out', 'grad_l0_ln_g', 'grad_l0_ln_b', 'grad_l1_w_in', 'grad_l1_w_grp', 'grad_l1_scale', 'grad_l1_w_out', 'grad_l1_ln_g', 'grad_l1_ln_b', 'grad_l2_w_in', 'grad_l2_q_norm', 'grad_l2_w_uq', 'grad_l2_kv_norm', 'grad_l2_w_ukv', 'grad_l2_w_out', 'grad_l2_ln_g', 'grad_l2_ln_b', 'grad_l3_w_in', 'grad_l3_w_out', 'grad_l3_ln_g', 'grad_l3_ln_b', 'delta_meta_tokens', 'delta_l0_w_in', 'delta_l0_conv_w', 'delta_l0_conv_b', 'delta_l0_w_a', 'delta_l0_b_a', 'delta_l0_w_x', 'delta_l0_b_x', 'delta_l0_lam', 'delta_l0_w_out', 'delta_l0_ln_g', 'delta_l0_ln_b', 'delta_l1_w_in', 'delta_l1_w_grp', 'delta_l1_scale', 'delta_l1_w_out', 'delta_l1_ln_g', 'delta_l1_ln_b', 'delta_l2_w_in', 'delta_l2_q_norm', 'delta_l2_w_uq', 'delta_l2_kv_norm', 'delta_l2_w_ukv', 'delta_l2_w_out', 'delta_l2_ln_g', 'delta_l2_ln_b', 'delta_l3_w_in', 'delta_l3_w_out', 'delta_l3_ln_g', 'delta_l3_ln_b', 'new_m_meta_tokens', 'new_m_l0_w_in', 'new_m_l0_conv_w', 'new_m_l0_conv_b', 'new_m_l0_w_a', 'new_m_l0_b_a', 'new_m_l0_w_x', 'new_m_l0_b_x', 'new_m_l0_lam', 'new_m_l0_w_out', 'new_m_l0_ln_g', 'new_m_l0_ln_b', 'new_m_l1_w_in', 'new_m_l1_w_grp', 'new_m_l1_scale', 'new_m_l1_w_out', 'new_m_l1_ln_g', 'new_m_l1_ln_b', 'new_m_l2_w_in', 'new_m_l2_q_norm', 'new_m_l2_w_uq', 'new_m_l2_kv_norm', 'new_m_l2_w_ukv', 'new_m_l2_w_out', 'new_m_l2_ln_g', 'new_m_l2_ln_b', 'new_m_l3_w_in', 'new_m_l3_w_out', 'new_m_l3_ln_g', 'new_m_l3_ln_b', 'new_v_meta_tokens', 'new_v_l0_w_in', 'new_v_l0_conv_w', 'new_v_l0_conv_b', 'new_v_l0_w_a', 'new_v_l0_b_a', 'new_v_l0_w_x', 'new_v_l0_b_x', 'new_v_l0_lam', 'new_v_l0_w_out', 'new_v_l0_ln_g', 'new_v_l0_ln_b', 'new_v_l1_w_in', 'new_v_l1_w_grp', 'new_v_l1_scale', 'new_v_l1_w_out', 'new_v_l1_ln_g', 'new_v_l1_ln_b', 'new_v_l2_w_in', 'new_v_l2_q_norm', 'new_v_l2_w_uq', 'new_v_l2_kv_norm', 'new_v_l2_w_ukv', 'new_v_l2_w_out', 'new_v_l2_ln_g', 'new_v_l2_ln_b', 'new_v_l3_w_in', 'new_v_l3_w_out', 'new_v_l3_ln_g', 'new_v_l3_ln_b']
TWIN_LEAF_KINDS = {'loss': 'loss', 'grad_x': 'grad_x', 'grad_meta_tokens': 'grad_w', 'grad_l0_w_in': 'grad_w', 'grad_l0_conv_w': 'grad_w', 'grad_l0_conv_b': 'grad_w', 'grad_l0_w_a': 'grad_w', 'grad_l0_b_a': 'grad_w', 'grad_l0_w_x': 'grad_w', 'grad_l0_b_x': 'grad_w', 'grad_l0_lam': 'grad_w', 'grad_l0_w_out': 'grad_w', 'grad_l0_ln_g': 'grad_w', 'grad_l0_ln_b': 'grad_w', 'grad_l1_w_in': 'grad_w', 'grad_l1_w_grp': 'grad_w', 'grad_l1_scale': 'grad_w', 'grad_l1_w_out': 'grad_w', 'grad_l1_ln_g': 'grad_w', 'grad_l1_ln_b': 'grad_w', 'grad_l2_w_in': 'grad_w', 'grad_l2_q_norm': 'grad_w', 'grad_l2_w_uq': 'grad_w', 'grad_l2_kv_norm': 'grad_w', 'grad_l2_w_ukv': 'grad_w', 'grad_l2_w_out': 'grad_w', 'grad_l2_ln_g': 'grad_w', 'grad_l2_ln_b': 'grad_w', 'grad_l3_w_in': 'grad_w', 'grad_l3_w_out': 'grad_w', 'grad_l3_ln_g': 'grad_w', 'grad_l3_ln_b': 'grad_w', 'delta_meta_tokens': 'delta_w', 'delta_l0_w_in': 'delta_w', 'delta_l0_conv_w': 'delta_w', 'delta_l0_conv_b': 'delta_w', 'delta_l0_w_a': 'delta_w', 'delta_l0_b_a': 'delta_w', 'delta_l0_w_x': 'delta_w', 'delta_l0_b_x': 'delta_w', 'delta_l0_lam': 'delta_w', 'delta_l0_w_out': 'delta_w', 'delta_l0_ln_g': 'delta_w', 'delta_l0_ln_b': 'delta_w', 'delta_l1_w_in': 'delta_w', 'delta_l1_w_grp': 'delta_w', 'delta_l1_scale': 'delta_w', 'delta_l1_w_out': 'delta_w', 'delta_l1_ln_g': 'delta_w', 'delta_l1_ln_b': 'delta_w', 'delta_l2_w_in': 'delta_w', 'delta_l2_q_norm': 'delta_w', 'delta_l2_w_uq': 'delta_w', 'delta_l2_kv_norm': 'delta_w', 'delta_l2_w_ukv': 'delta_w', 'delta_l2_w_out': 'delta_w', 'delta_l2_ln_g': 'delta_w', 'delta_l2_ln_b': 'delta_w', 'delta_l3_w_in': 'delta_w', 'delta_l3_w_out': 'delta_w', 'delta_l3_ln_g': 'delta_w', 'delta_l3_ln_b': 'delta_w', 'new_m_meta_tokens': 'new_m', 'new_m_l0_w_in': 'new_m', 'new_m_l0_conv_w': 'new_m', 'new_m_l0_conv_b': 'new_m', 'new_m_l0_w_a': 'new_m', 'new_m_l0_b_a': 'new_m', 'new_m_l0_w_x': 'new_m', 'new_m_l0_b_x': 'new_m', 'new_m_l0_lam': 'new_m', 'new_m_l0_w_out': 'new_m', 'new_m_l0_ln_g': 'new_m', 'new_m_l0_ln_b': 'new_m', 'new_m_l1_w_in': 'new_m', 'new_m_l1_w_grp': 'new_m', 'new_m_l1_scale': 'new_m', 'new_m_l1_w_out': 'new_m', 'new_m_l1_ln_g': 'new_m', 'new_m_l1_ln_b': 'new_m', 'new_m_l2_w_in': 'new_m', 'new_m_l2_q_norm': 'new_m', 'new_m_l2_w_uq': 'new_m', 'new_m_l2_kv_norm': 'new_m', 'new_m_l2_w_ukv': 'new_m', 'new_m_l2_w_out': 'new_m', 'new_m_l2_ln_g': 'new_m', 'new_m_l2_ln_b': 'new_m', 'new_m_l3_w_in': 'new_m', 'new_m_l3_w_out': 'new_m', 'new_m_l3_ln_g': 'new_m', 'new_m_l3_ln_b': 'new_m', 'new_v_meta_tokens': 'new_v', 'new_v_l0_w_in': 'new_v', 'new_v_l0_conv_w': 'new_v', 'new_v_l0_conv_b': 'new_v', 'new_v_l0_w_a': 'new_v', 'new_v_l0_b_a': 'new_v', 'new_v_l0_w_x': 'new_v', 'new_v_l0_b_x': 'new_v', 'new_v_l0_lam': 'new_v', 'new_v_l0_w_out': 'new_v', 'new_v_l0_ln_g': 'new_v', 'new_v_l0_ln_b': 'new_v', 'new_v_l1_w_in': 'new_v', 'new_v_l1_w_grp': 'new_v', 'new_v_l1_scale': 'new_v', 'new_v_l1_w_out': 'new_v', 'new_v_l1_ln_g': 'new_v', 'new_v_l1_ln_b': 'new_v', 'new_v_l2_w_in': 'new_v', 'new_v_l2_q_norm': 'new_v', 'new_v_l2_w_uq': 'new_v', 'new_v_l2_kv_norm': 'new_v', 'new_v_l2_w_ukv': 'new_v', 'new_v_l2_w_out': 'new_v', 'new_v_l2_ln_g': 'new_v', 'new_v_l2_ln_b': 'new_v', 'new_v_l3_w_in': 'new_v', 'new_v_l3_w_out': 'new_v', 'new_v_l3_ln_g': 'new_v', 'new_v_l3_ln_b': 'new_v'}


def _forward(args):
    return _fwd_reference(*[args[k] for k in FWD_PARAMS])


def _output_shape():
    out = _jax.eval_shape(lambda: _forward(_fwd_setup_inputs(0)))
    return out.shape, out.dtype

N_MICROBATCH = 1
ADAM_LR = 0.001
ADAM_B1 = 0.9
ADAM_B2 = 0.999
ADAM_EPS = 1e-08
ADAM_WD = 0.01
ADAM_STEP = 10
PER_EXAMPLE_BATCH_AXIS = {'x': 0, 'loss_target': 0}
SHARED_INPUTS = []
_WEIGHT_DTYPES = {'meta_tokens': _jnp.float32, 'l0_w_in': _jnp.float32, 'l0_conv_w': _jnp.float32, 'l0_conv_b': _jnp.float32, 'l0_w_a': _jnp.float32, 'l0_b_a': _jnp.float32, 'l0_w_x': _jnp.float32, 'l0_b_x': _jnp.float32, 'l0_lam': _jnp.float32, 'l0_w_out': _jnp.float32, 'l0_ln_g': _jnp.float32, 'l0_ln_b': _jnp.float32, 'l1_w_in': _jnp.float32, 'l1_w_grp': _jnp.float32, 'l1_scale': _jnp.float32, 'l1_w_out': _jnp.float32, 'l1_ln_g': _jnp.float32, 'l1_ln_b': _jnp.float32, 'l2_w_in': _jnp.float32, 'l2_q_norm': _jnp.float32, 'l2_w_uq': _jnp.float32, 'l2_kv_norm': _jnp.float32, 'l2_w_ukv': _jnp.float32, 'l2_w_out': _jnp.float32, 'l2_ln_g': _jnp.float32, 'l2_ln_b': _jnp.float32, 'l3_w_in': _jnp.float32, 'l3_w_out': _jnp.float32, 'l3_ln_g': _jnp.float32, 'l3_ln_b': _jnp.float32}
MOMENT_SCALE = {'meta_tokens': 9.750763e-04, 'l0_w_in': 4.322926e-03, 'l0_conv_w': 4.766841e-03, 'l0_conv_b': 5.815668e-02, 'l0_w_a': 1.256071e-03, 'l0_b_a': 1.121850e-03, 'l0_w_x': 2.232811e-03, 'l0_b_x': 1.807829e-03, 'l0_lam': 2.256320e-03, 'l0_w_out': 1.017996e-02, 'l0_ln_g': 2.592664e-01, 'l0_ln_b': 6.440687e-02, 'l1_w_in': 6.004963e-03, 'l1_w_grp': 5.910455e-03, 'l1_scale': 6.039564e-03, 'l1_w_out': 1.403876e-02, 'l1_ln_g': 2.655487e-01, 'l1_ln_b': 6.433107e-02, 'l2_w_in': 1.952682e-03, 'l2_q_norm': 2.183614e-03, 'l2_w_uq': 8.923609e-04, 'l2_kv_norm': 4.525442e-03, 'l2_w_ukv': 1.065835e-03, 'l2_w_out': 2.883241e-03, 'l2_ln_g': 2.657517e-01, 'l2_ln_b': 6.428192e-02, 'l3_w_in': 6.622087e-03, 'l3_w_out': 1.554544e-02, 'l3_ln_g': 8.008020e+00, 'l3_ln_b': 1.300363e-01}


def _to_microbatches(a, axis):
    t = _jnp.moveaxis(a, axis, 0)
    t = t.reshape((N_MICROBATCH, t.shape[0] // N_MICROBATCH) + t.shape[1:])
    return _jnp.moveaxis(t, 1, axis + 1)


def setup_inputs(seed: int = 0) -> dict:
    inp = _fwd_setup_inputs(seed)
    key = _jax.random.fold_in(_jax.random.key(seed), 7919)
    shape, _ = _output_shape()
    out = dict(inp)
    out["loss_target"] = _jax.random.normal(_jax.random.fold_in(key, 0), shape, _jnp.float32)
    for i, name in enumerate(TWIN_WEIGHTS):
        w = inp[name].astype(_jnp.float32)
        if MOMENT_SCALE is None:
            s = _jnp.sqrt(_jnp.mean(_jnp.square(w)) + 1e-30)
        else:
            s = MOMENT_SCALE[name]
        km, kv = _jax.random.split(_jax.random.fold_in(key, i + 1))
        out[name] = w
        out["m_" + name] = s * _jax.random.normal(km, w.shape, _jnp.float32)
        out["v_" + name] = (s * s) * _jax.random.uniform(kv, w.shape, _jnp.float32, 0.5, 1.5)
    if N_MICROBATCH > 1:
        for name, axis in PER_EXAMPLE_BATCH_AXIS.items():
            out[name] = _to_microbatches(out[name], axis)
    return {'x': out['x'], 'meta_tokens': out['meta_tokens'], 'l0_w_in': out['l0_w_in'], 'l0_conv_w': out['l0_conv_w'], 'l0_conv_b': out['l0_conv_b'], 'l0_w_a': out['l0_w_a'], 'l0_b_a': out['l0_b_a'], 'l0_w_x': out['l0_w_x'], 'l0_b_x': out['l0_b_x'], 'l0_lam': out['l0_lam'], 'l0_w_out': out['l0_w_out'], 'l0_ln_g': out['l0_ln_g'], 'l0_ln_b': out['l0_ln_b'], 'l1_w_in': out['l1_w_in'], 'l1_w_grp': out['l1_w_grp'], 'l1_scale': out['l1_scale'], 'l1_w_out': out['l1_w_out'], 'l1_ln_g': out['l1_ln_g'], 'l1_ln_b': out['l1_ln_b'], 'l2_w_in': out['l2_w_in'], 'l2_q_norm': out['l2_q_norm'], 'l2_w_uq': out['l2_w_uq'], 'l2_kv_norm': out['l2_kv_norm'], 'l2_w_ukv': out['l2_w_ukv'], 'l2_w_out': out['l2_w_out'], 'l2_ln_g': out['l2_ln_g'], 'l2_ln_b': out['l2_ln_b'], 'l3_w_in': out['l3_w_in'], 'l3_w_out': out['l3_w_out'], 'l3_ln_g': out['l3_ln_g'], 'l3_ln_b': out['l3_ln_b'], 'loss_target': out['loss_target'], 'm_meta_tokens': out['m_meta_tokens'], 'm_l0_w_in': out['m_l0_w_in'], 'm_l0_conv_w': out['m_l0_conv_w'], 'm_l0_conv_b': out['m_l0_conv_b'], 'm_l0_w_a': out['m_l0_w_a'], 'm_l0_b_a': out['m_l0_b_a'], 'm_l0_w_x': out['m_l0_w_x'], 'm_l0_b_x': out['m_l0_b_x'], 'm_l0_lam': out['m_l0_lam'], 'm_l0_w_out': out['m_l0_w_out'], 'm_l0_ln_g': out['m_l0_ln_g'], 'm_l0_ln_b': out['m_l0_ln_b'], 'm_l1_w_in': out['m_l1_w_in'], 'm_l1_w_grp': out['m_l1_w_grp'], 'm_l1_scale': out['m_l1_scale'], 'm_l1_w_out': out['m_l1_w_out'], 'm_l1_ln_g': out['m_l1_ln_g'], 'm_l1_ln_b': out['m_l1_ln_b'], 'm_l2_w_in': out['m_l2_w_in'], 'm_l2_q_norm': out['m_l2_q_norm'], 'm_l2_w_uq': out['m_l2_w_uq'], 'm_l2_kv_norm': out['m_l2_kv_norm'], 'm_l2_w_ukv': out['m_l2_w_ukv'], 'm_l2_w_out': out['m_l2_w_out'], 'm_l2_ln_g': out['m_l2_ln_g'], 'm_l2_ln_b': out['m_l2_ln_b'], 'm_l3_w_in': out['m_l3_w_in'], 'm_l3_w_out': out['m_l3_w_out'], 'm_l3_ln_g': out['m_l3_ln_g'], 'm_l3_ln_b': out['m_l3_ln_b'], 'v_meta_tokens': out['v_meta_tokens'], 'v_l0_w_in': out['v_l0_w_in'], 'v_l0_conv_w': out['v_l0_conv_w'], 'v_l0_conv_b': out['v_l0_conv_b'], 'v_l0_w_a': out['v_l0_w_a'], 'v_l0_b_a': out['v_l0_b_a'], 'v_l0_w_x': out['v_l0_w_x'], 'v_l0_b_x': out['v_l0_b_x'], 'v_l0_lam': out['v_l0_lam'], 'v_l0_w_out': out['v_l0_w_out'], 'v_l0_ln_g': out['v_l0_ln_g'], 'v_l0_ln_b': out['v_l0_ln_b'], 'v_l1_w_in': out['v_l1_w_in'], 'v_l1_w_grp': out['v_l1_w_grp'], 'v_l1_scale': out['v_l1_scale'], 'v_l1_w_out': out['v_l1_w_out'], 'v_l1_ln_g': out['v_l1_ln_g'], 'v_l1_ln_b': out['v_l1_ln_b'], 'v_l2_w_in': out['v_l2_w_in'], 'v_l2_q_norm': out['v_l2_q_norm'], 'v_l2_w_uq': out['v_l2_w_uq'], 'v_l2_kv_norm': out['v_l2_kv_norm'], 'v_l2_w_ukv': out['v_l2_w_ukv'], 'v_l2_w_out': out['v_l2_w_out'], 'v_l2_ln_g': out['v_l2_ln_g'], 'v_l2_ln_b': out['v_l2_ln_b'], 'v_l3_w_in': out['v_l3_w_in'], 'v_l3_w_out': out['v_l3_w_out'], 'v_l3_ln_g': out['v_l3_ln_g'], 'v_l3_ln_b': out['v_l3_ln_b']}


def _loss(weights, diff, rest, loss_target):
    with _jax.named_scope("forward"):
        args = {**rest, TWIN_DIFF_INPUT: diff, **{k: w.astype(_WEIGHT_DTYPES[k]) for k, w in weights.items()}}
        y = _forward(args)
    with _jax.named_scope("loss_head"):
        err = _jnp.square(y.astype(_jnp.float32) - loss_target)
        return 0.5 * _jnp.sum(_jnp.mean(err, axis=-1)) if err.ndim else 0.5 * err


def _adamw(w, g, m, v):
    m = ADAM_B1 * m + (1.0 - ADAM_B1) * g
    v = ADAM_B2 * v + (1.0 - ADAM_B2) * _jnp.square(g)
    m_hat = m / (1.0 - ADAM_B1 ** ADAM_STEP)
    v_hat = v / (1.0 - ADAM_B2 ** ADAM_STEP)
    delta = -ADAM_LR * (m_hat / (_jnp.sqrt(v_hat) + ADAM_EPS) + ADAM_WD * w)
    return delta, m, v


def reference(x, meta_tokens, l0_w_in, l0_conv_w, l0_conv_b, l0_w_a, l0_b_a, l0_w_x, l0_b_x, l0_lam, l0_w_out, l0_ln_g, l0_ln_b, l1_w_in, l1_w_grp, l1_scale, l1_w_out, l1_ln_g, l1_ln_b, l2_w_in, l2_q_norm, l2_w_uq, l2_kv_norm, l2_w_ukv, l2_w_out, l2_ln_g, l2_ln_b, l3_w_in, l3_w_out, l3_ln_g, l3_ln_b, loss_target, m_meta_tokens, m_l0_w_in, m_l0_conv_w, m_l0_conv_b, m_l0_w_a, m_l0_b_a, m_l0_w_x, m_l0_b_x, m_l0_lam, m_l0_w_out, m_l0_ln_g, m_l0_ln_b, m_l1_w_in, m_l1_w_grp, m_l1_scale, m_l1_w_out, m_l1_ln_g, m_l1_ln_b, m_l2_w_in, m_l2_q_norm, m_l2_w_uq, m_l2_kv_norm, m_l2_w_ukv, m_l2_w_out, m_l2_ln_g, m_l2_ln_b, m_l3_w_in, m_l3_w_out, m_l3_ln_g, m_l3_ln_b, v_meta_tokens, v_l0_w_in, v_l0_conv_w, v_l0_conv_b, v_l0_w_a, v_l0_b_a, v_l0_w_x, v_l0_b_x, v_l0_lam, v_l0_w_out, v_l0_ln_g, v_l0_ln_b, v_l1_w_in, v_l1_w_grp, v_l1_scale, v_l1_w_out, v_l1_ln_g, v_l1_ln_b, v_l2_w_in, v_l2_q_norm, v_l2_w_uq, v_l2_kv_norm, v_l2_w_ukv, v_l2_w_out, v_l2_ln_g, v_l2_ln_b, v_l3_w_in, v_l3_w_out, v_l3_ln_g, v_l3_ln_b):
    given = dict(x=x, meta_tokens=meta_tokens, l0_w_in=l0_w_in, l0_conv_w=l0_conv_w, l0_conv_b=l0_conv_b, l0_w_a=l0_w_a, l0_b_a=l0_b_a, l0_w_x=l0_w_x, l0_b_x=l0_b_x, l0_lam=l0_lam, l0_w_out=l0_w_out, l0_ln_g=l0_ln_g, l0_ln_b=l0_ln_b, l1_w_in=l1_w_in, l1_w_grp=l1_w_grp, l1_scale=l1_scale, l1_w_out=l1_w_out, l1_ln_g=l1_ln_g, l1_ln_b=l1_ln_b, l2_w_in=l2_w_in, l2_q_norm=l2_q_norm, l2_w_uq=l2_w_uq, l2_kv_norm=l2_kv_norm, l2_w_ukv=l2_w_ukv, l2_w_out=l2_w_out, l2_ln_g=l2_ln_g, l2_ln_b=l2_ln_b, l3_w_in=l3_w_in, l3_w_out=l3_w_out, l3_ln_g=l3_ln_g, l3_ln_b=l3_ln_b, loss_target=loss_target, m_meta_tokens=m_meta_tokens, m_l0_w_in=m_l0_w_in, m_l0_conv_w=m_l0_conv_w, m_l0_conv_b=m_l0_conv_b, m_l0_w_a=m_l0_w_a, m_l0_b_a=m_l0_b_a, m_l0_w_x=m_l0_w_x, m_l0_b_x=m_l0_b_x, m_l0_lam=m_l0_lam, m_l0_w_out=m_l0_w_out, m_l0_ln_g=m_l0_ln_g, m_l0_ln_b=m_l0_ln_b, m_l1_w_in=m_l1_w_in, m_l1_w_grp=m_l1_w_grp, m_l1_scale=m_l1_scale, m_l1_w_out=m_l1_w_out, m_l1_ln_g=m_l1_ln_g, m_l1_ln_b=m_l1_ln_b, m_l2_w_in=m_l2_w_in, m_l2_q_norm=m_l2_q_norm, m_l2_w_uq=m_l2_w_uq, m_l2_kv_norm=m_l2_kv_norm, m_l2_w_ukv=m_l2_w_ukv, m_l2_w_out=m_l2_w_out, m_l2_ln_g=m_l2_ln_g, m_l2_ln_b=m_l2_ln_b, m_l3_w_in=m_l3_w_in, m_l3_w_out=m_l3_w_out, m_l3_ln_g=m_l3_ln_g, m_l3_ln_b=m_l3_ln_b, v_meta_tokens=v_meta_tokens, v_l0_w_in=v_l0_w_in, v_l0_conv_w=v_l0_conv_w, v_l0_conv_b=v_l0_conv_b, v_l0_w_a=v_l0_w_a, v_l0_b_a=v_l0_b_a, v_l0_w_x=v_l0_w_x, v_l0_b_x=v_l0_b_x, v_l0_lam=v_l0_lam, v_l0_w_out=v_l0_w_out, v_l0_ln_g=v_l0_ln_g, v_l0_ln_b=v_l0_ln_b, v_l1_w_in=v_l1_w_in, v_l1_w_grp=v_l1_w_grp, v_l1_scale=v_l1_scale, v_l1_w_out=v_l1_w_out, v_l1_ln_g=v_l1_ln_g, v_l1_ln_b=v_l1_ln_b, v_l2_w_in=v_l2_w_in, v_l2_q_norm=v_l2_q_norm, v_l2_w_uq=v_l2_w_uq, v_l2_kv_norm=v_l2_kv_norm, v_l2_w_ukv=v_l2_w_ukv, v_l2_w_out=v_l2_w_out, v_l2_ln_g=v_l2_ln_g, v_l2_ln_b=v_l2_ln_b, v_l3_w_in=v_l3_w_in, v_l3_w_out=v_l3_w_out, v_l3_ln_g=v_l3_ln_g, v_l3_ln_b=v_l3_ln_b)
    weights = {n: given[n] for n in TWIN_WEIGHTS}
    shared = {n: given[n] for n in SHARED_INPUTS}
    per_example = {n: given[n] for n in ['x']}
    grad_fn = _jax.value_and_grad(_loss, argnums=(0, 1))

    def one_microbatch(ex, loss_target):
        ex = dict(ex)
        diff = ex.pop(TWIN_DIFF_INPUT)
        return grad_fn(weights, diff, {**shared, **ex}, loss_target)

    if N_MICROBATCH == 1:
        loss, (grad_w, grad_x) = one_microbatch(per_example, given["loss_target"])
    else:
        def body(carry, xs):
            loss_sum, grad_sum = carry
            l_k, (gw_k, gx_k) = one_microbatch(xs[0], xs[1])
            with _jax.named_scope("update"):
                return (loss_sum + l_k, _jax.tree.map(_jnp.add, grad_sum, gw_k)), gx_k

        init = (_jnp.zeros((), _jnp.float32), _jax.tree.map(_jnp.zeros_like, weights))
        (loss, grad_w), grad_x = _jax.lax.scan(body, init, (per_example, given["loss_target"]))
    with _jax.named_scope("update"):
        delta_w, new_m, new_v = {}, {}, {}
        for n in TWIN_WEIGHTS:
            delta_w[n], new_m[n], new_v[n] = _adamw(weights[n], grad_w[n], given["m_" + n], given["v_" + n])
    return (loss, grad_x, *[grad_w[n] for n in TWIN_WEIGHTS], *[delta_w[n] for n in TWIN_WEIGHTS],
            *[new_m[n] for n in TWIN_WEIGHTS], *[new_v[n] for n in TWIN_WEIGHTS])
```

```python
import functools
import math
from typing import NamedTuple

import jax
import jax.numpy as jnp
import numpy as np
from jax import lax
from jax.experimental import pallas as pl
from jax.experimental.pallas import tpu as pltpu

F32 = jnp.float32
BF16 = jnp.bfloat16

LN_EPS = 1e-5
RMS_EPS = 1e-6
ROPE_BASE = 10000.0
LRU_C = 8.0
CONV_W = 4
HIST = 16
ADAM_LR, ADAM_B1, ADAM_B2, ADAM_EPS, ADAM_WD, ADAM_STEP = 0.001, 0.9, 0.999, 1e-08, 0.01, 10
VMEM_LIMIT_V7X = 56 * 1024 * 1024
MESH = pl.DeviceIdType.MESH


class Cfg(NamedTuple):
    d: int
    seq: int
    n_meta: int
    tp: int
    depth: int
    lru_heads: int
    pool_windows: tuple
    mla_heads: int
    nope: int
    rope: int
    q_lora: int
    kv_lora: int
    cw: int
    ret_heads: int
    chunk: int
    tq: int
    tr: int
    tt: int
    tc: int
    tm: int
    tn: int
    la: int


REAL = Cfg(d=4096, seq=4096, n_meta=16, tp=4224, depth=4, lru_heads=16, pool_windows=(2, 4, 8, 16),
           mla_heads=32, nope=128, rope=64, q_lora=1024, kv_lora=512, cw=2048, ret_heads=16, chunk=128,
           tq=384, tr=128, tt=1056, tc=512, tm=1408, tn=1024, la=128)


def _div_tile(n, target, align):
    best = None
    for t in range(align, min(n, target) + 1, align):
        if n % t == 0:
            best = t
    return best or n


def _cparams(*sem):
    return pltpu.CompilerParams(dimension_semantics=sem, vmem_limit_bytes=VMEM_LIMIT_V7X)


def _sig(x):
    return 1.0 / (1.0 + jnp.exp(-x))


def _expm1(x):
    p = x * (1.0 + x * (1.0 / 2) * (1.0 + x * (1.0 / 3) * (1.0 + x * (1.0 / 4) * (1.0 + x * (1.0 / 5) * (1.0 + x * (1.0 / 6))))))
    return jnp.where(x > -0.1, p, jnp.exp(x) - 1.0)


def _colsum8(x):
    r, w = x.shape
    return x.reshape(r // 8, 8, w).sum(axis=0)


def _mm_call(a, b, grid, a_spec, b_spec, o_spec, out_sds, acc_shape, dims, name):
    nk = grid[-1]
    kax = len(grid) - 1

    def body(a_ref, b_ref, o_ref, acc_ref):
        k = pl.program_id(kax)

        @pl.when(k == 0)
        def _():
            acc_ref[...] = jnp.zeros_like(acc_ref)

        acc_ref[...] += lax.dot_general(a_ref[...].astype(BF16), b_ref[...].astype(BF16), dims,
                                        preferred_element_type=F32)

        @pl.when(k == nk - 1)
        def _():
            o_ref[...] = acc_ref[...].astype(o_ref.dtype)

    sem = ("parallel",) * kax + ("arbitrary",)
    return pl.pallas_call(body, grid=grid, in_specs=[a_spec, b_spec], out_specs=o_spec, out_shape=out_sds,
                          scratch_shapes=[pltpu.VMEM(acc_shape, F32)], compiler_params=_cparams(*sem), name=name)(a, b)


_NN = (((1,), (0,)), ((), ()))
_NT = (((1,), (1,)), ((), ()))
_TN = (((0,), (0,)), ((), ()))


def mm_nn(cfg, a, w, name, out_dtype=F32, a_col0=0):
    m = a.shape[0]
    g, k, ns = w.shape
    tm, tn, tk = _div_tile(m, cfg.tm, 16), _div_tile(ns, cfg.tn, cfg.la), _div_tile(k, cfg.tn, cfg.la)
    npb, a0 = ns // tn, a_col0 // tk
    assert a_col0 % tk == 0
    return _mm_call(a, w, (m // tm, g * npb, k // tk),
                    pl.BlockSpec((tm, tk), lambda i, j, kk: (i, a0 + kk)),
                    pl.BlockSpec((None, tk, tn), lambda i, j, kk: (j // npb, kk, j % npb)),
                    pl.BlockSpec((tm, tn), lambda i, j, kk: (i, j)),
                    jax.ShapeDtypeStruct((m, g * ns), out_dtype), (tm, tn), _NN, name)


def mm_nt(cfg, dy, w, name, out_dtype=F32):
    m = dy.shape[0]
    g, k, ns = w.shape
    tm, tn, tk = _div_tile(m, cfg.tm, 16), _div_tile(k, cfg.tn, cfg.la), _div_tile(ns, cfg.tn, cfg.la)
    kpb = ns // tk
    return _mm_call(dy, w, (m // tm, k // tn, g * kpb),
                    pl.BlockSpec((tm, tk), lambda i, j, kk: (i, kk)),
                    pl.BlockSpec((None, tn, tk), lambda i, j, kk: (kk // kpb, j, kk % kpb)),
                    pl.BlockSpec((tm, tn), lambda i, j, kk: (i, j)),
                    jax.ShapeDtypeStruct((m, k), out_dtype), (tm, tn), _NT, name)


def mm_tn(cfg, x, dy, g, name, x_col0=0, k=None):
    m = x.shape[0]
    k = x.shape[1] if k is None else k
    ns = dy.shape[1] // g
    tmo, tn, tk = _div_tile(k, cfg.tn, cfg.la), _div_tile(ns, cfg.tn, cfg.la), _div_tile(m, cfg.tm, 16)
    npb, x0 = ns // tn, x_col0 // tmo
    assert x_col0 % tmo == 0
    return _mm_call(x, dy, (k // tmo, g * npb, m // tk),
                    pl.BlockSpec((tk, tmo), lambda i, j, kk: (kk, x0 + i)),
                    pl.BlockSpec((tk, tn), lambda i, j, kk: (kk, j)),
                    pl.BlockSpec((None, tmo, tn), lambda i, j, kk: (j // npb, i, j % npb)),
                    jax.ShapeDtypeStruct((g, k, ns), F32), (tmo, tn), _TN, name)


def bd_nn(cfg, x, w, name, out_dtype=F32):
    m = x.shape[0]
    g, kg, ng = w.shape
    tm, tn = _div_tile(m, cfg.tm, 16), _div_tile(ng, cfg.tn, cfg.la)
    npb = ng // tn
    return _mm_call(x, w, (m // tm, g * npb, 1),
                    pl.BlockSpec((tm, kg), lambda i, j, kk: (i, j // npb)),
                    pl.BlockSpec((None, kg, tn), lambda i, j, kk: (j // npb, 0, j % npb)),
                    pl.BlockSpec((tm, tn), lambda i, j, kk: (i, j)),
                    jax.ShapeDtypeStruct((m, g * ng), out_dtype), (tm, tn), _NN, name)


def bd_nt(cfg, dy, w, name, out_dtype=F32):
    m = dy.shape[0]
    g, kg, ng = w.shape
    tm, tn = _div_tile(m, cfg.tm, 16), _div_tile(kg, cfg.tn, cfg.la)
    npb = kg // tn
    return _mm_call(dy, w, (m // tm, g * npb, 1),
                    pl.BlockSpec((tm, ng), lambda i, j, kk: (i, j // npb)),
                    pl.BlockSpec((None, tn, ng), lambda i, j, kk: (j // npb, j % npb, 0)),
                    pl.BlockSpec((tm, tn), lambda i, j, kk: (i, j)),
                    jax.ShapeDtypeStruct((m, g * kg), out_dtype), (tm, tn), _NT, name)


def bd_tn(cfg, x, dy, g, name):
    m = x.shape[0]
    kg, ng = x.shape[1] // g, dy.shape[1] // g
    tmo, tn, tk = _div_tile(kg, cfg.tn, cfg.la), _div_tile(ng, cfg.tn, cfg.la), _div_tile(m, cfg.tm, 16)
    mpb, npb = kg // tmo, ng // tn
    return _mm_call(x, dy, (g * mpb, npb, m // tk),
                    pl.BlockSpec((tk, tmo), lambda i, j, kk: (kk, i)),
                    pl.BlockSpec((tk, tn), lambda i, j, kk: (kk, (i // mpb) * npb + j)),
                    pl.BlockSpec((None, tmo, tn), lambda i, j, kk: (i // mpb, i % mpb, j)),
                    jax.ShapeDtypeStruct((g, kg, ng), F32), (tmo, tn), _TN, name)


def _rowwise(cfg, body, ins, outs, name, n_acc=0):
    tr = cfg.tr
    n = cfg.tp // tr
    in_specs, args = [], []
    for spec in ins:
        arr = spec[0]
        if spec[1] is None:
            in_specs.append(pl.BlockSpec(arr.shape, lambda i, nd=arr.ndim: (0,) * nd))
        else:
            in_specs.append(pl.BlockSpec((tr, spec[1]), lambda i, cb=spec[2]: (i, cb)))
        args.append(arr)
    out_specs, out_shape = [], []
    for o in outs:
        if len(o) == 2:
            out_specs.append(pl.BlockSpec((tr, o[0]), lambda i: (i, 0)))
            out_shape.append(jax.ShapeDtypeStruct((cfg.tp, o[0]), o[1]))
        else:
            out_specs.append(pl.BlockSpec((8, o[0]), lambda i: (0, 0)))
            out_shape.append(jax.ShapeDtypeStruct((8, o[0]), F32))
    return pl.pallas_call(body, grid=(n,), in_specs=in_specs, out_specs=out_specs, out_shape=out_shape,
                          compiler_params=_cparams("arbitrary" if n_acc else "parallel"), name=name)(*args)


def ln_fwd(cfg, h, o, g, b, alpha, name):
    d = cfg.d

    def body(h_ref, o_ref, g_ref, b_ref, y_ref, yb_ref, xh_ref, rs_ref):
        z = alpha * h_ref[...] + o_ref[...]
        mu = jnp.mean(z, axis=-1, keepdims=True)
        zc = z - mu
        var = jnp.mean(zc * zc, axis=-1, keepdims=True)
        rstd = lax.rsqrt(var + LN_EPS)
        xh = zc * rstd
        y = xh * g_ref[...] + b_ref[...]
        y_ref[...] = y
        yb_ref[...] = y.astype(BF16)
        xh_ref[...] = xh
        rs_ref[...] = rstd

    return _rowwise(cfg, body, [(h, d, 0), (o, d, 0), (g, None), (b, None)], [(d, F32), (d, BF16), (d, F32), (1, F32)], name)


def ln_bwd(cfg, d_res, d_mm, xhat, rstd, g, alpha, name):
    d = cfg.d
    two = d_res is not None

    def body(*refs):
        if two:
            dr_ref, dm_ref, xh_ref, rs_ref, g_ref, dz_ref, dzb_ref, dg_ref, db_ref = refs
            dy = alpha * dr_ref[...] + dm_ref[...]
        else:
            dm_ref, xh_ref, rs_ref, g_ref, dz_ref, dzb_ref, dg_ref, db_ref = refs
            dy = dm_ref[...]
        xh = xh_ref[...]

        @pl.when(pl.program_id(0) == 0)
        def _():
            dg_ref[...] = jnp.zeros_like(dg_ref)
            db_ref[...] = jnp.zeros_like(db_ref)

        dg_ref[...] += _colsum8(dy * xh)
        db_ref[...] += _colsum8(dy)
        dxh = dy * g_ref[...]
        m1 = jnp.mean(dxh, axis=-1, keepdims=True)
        m2 = jnp.mean(dxh * xh, axis=-1, keepdims=True)
        dz = rs_ref[...] * (dxh - m1 - xh * m2)
        dz_ref[...] = dz
        dzb_ref[...] = dz.astype(BF16)

    ins = ([(d_res, d, 0)] if two else []) + [(d_mm, d, 0), (xhat, d, 0), (rstd, 1, 0), (g, None)]
    return _rowwise(cfg, body, ins, [(d, F32), (d, BF16), (d,), (d,)], name, n_acc=2)


def loss_grad(cfg, y, tgt, name):
    d, tr = cfg.d, cfg.tr
    lo, hi = cfg.n_meta, cfg.n_meta + cfg.seq

    def body(y_ref, t_ref, dy_ref, acc_ref):
        i = pl.program_id(0)

        @pl.when(i == 0)
        def _():
            acc_ref[...] = jnp.zeros_like(acc_ref)

        row = i * tr + lax.broadcasted_iota(jnp.int32, (tr, 1), 0)
        err = jnp.where((row >= lo) & (row < hi), y_ref[...] - t_ref[...], 0.0)
        dy_ref[...] = err * (1.0 / d)
        acc_ref[...] += _colsum8(err * err)

    return _rowwise(cfg, body, [(y, d, 0), (tgt, d, 0)], [(d, F32), (d,)], name, n_acc=1)


def axpy(cfg, a, b, alpha, name):
    d = cfg.d

    def body(a_ref, b_ref, o_ref):
        o_ref[...] = alpha * a_ref[...] + b_ref[...]

    return _rowwise(cfg, body, [(a, d, 0), (b, d, 0)], [(d, F32)], name)[0]


def _time_call(cfg, body, ins, outs, accs, scratch, name, reverse=False, groups=1):
    d = cfg.d
    tc = _div_tile(d // groups, cfg.tc, cfg.la)
    tt = _div_tile(cfg.tp, cfg.tt, 16)
    nc, nt = d // tc, cfg.tp // tt
    tmap = (lambda t: nt - 1 - t) if reverse else (lambda t: t)
    in_specs, args = [], []
    for spec in ins:
        arr = spec[0]
        if len(spec) == 2:
            in_specs.append(pl.BlockSpec((tt, tc), lambda c, t, off=spec[1] * nc: (tmap(t), off + c)))
        else:
            in_specs.append(pl.BlockSpec((arr.shape[0], tc), lambda c, t: (0, c)))
        args.append(arr)
    out_specs = [pl.BlockSpec((tt, tc), lambda c, t: (tmap(t), c)) for _ in outs]
    out_shape = [jax.ShapeDtypeStruct((cfg.tp, d), dt) for dt in outs]
    for _ in range(accs):
        out_specs.append(pl.BlockSpec((8, tc), lambda c, t: (0, c)))
        out_shape.append(jax.ShapeDtypeStruct((8, d), F32))
    return pl.pallas_call(functools.partial(body, tt=tt, tc=tc, nt=nt), grid=(nc, nt), in_specs=in_specs,
                          out_specs=out_specs, out_shape=out_shape,
                          scratch_shapes=[pltpu.VMEM(s, F32) for s in scratch(tt, tc)],
                          compiler_params=_cparams("parallel", "arbitrary"), name=name)(*args)


def _push_history(s_ref, new, t, tt):
    @pl.when(t == 0)
    def _():
        s_ref[pl.ds(0, HIST), :] = jnp.zeros((HIST, s_ref.shape[1]), F32)

    @pl.when(t > 0)
    def _():
        s_ref[pl.ds(0, HIST), :] = s_ref[pl.ds(tt, HIST), :]

    s_ref[pl.ds(HIST, tt), :] = new


def _push_future(s_ref, new, t, tt):
    @pl.when(t == 0)
    def _():
        s_ref[pl.ds(tt, HIST), :] = jnp.zeros((HIST, s_ref.shape[1]), F32)

    @pl.when(t > 0)
    def _():
        s_ref[pl.ds(tt, HIST), :] = s_ref[pl.ds(0, HIST), :]

    s_ref[pl.ds(0, tt), :] = new


def conv_fwd(cfg, ug, w4, cb, name):
    def body(x_ref, w_ref, b_ref, u_ref, ub_ref, s_ref, *, tt, tc, nt):
        t = pl.program_id(1)
        _push_history(s_ref, x_ref[...], t, tt)
        acc = b_ref[...] + w_ref[pl.ds(0, 1), :] * s_ref[pl.ds(HIST - 3, tt), :]
        for k in range(1, CONV_W):
            acc = acc + w_ref[pl.ds(k, 1), :] * s_ref[pl.ds(HIST - 3 + k, tt), :]
        u_ref[...] = acc
        ub_ref[...] = acc.astype(BF16)

    return _time_call(cfg, body, [(ug, 0), (w4,), (cb,)], [F32, BF16], 0, lambda tt, tc: [(tt + HIST, tc)], name)


def conv_bwd(cfg, du_a, du_b, du_c, ug, w4, name):
    def body(a_ref, b_ref, c_ref, x_ref, w_ref, dx_ref, acc_ref, s_ref, *, tt, tc, nt):
        t = pl.program_id(1)

        @pl.when(t == 0)
        def _():
            acc_ref[...] = jnp.zeros_like(acc_ref)

        du = a_ref[...] + b_ref[...] + c_ref[...]
        _push_future(s_ref, du, t, tt)
        x = x_ref[...]
        acc_ref[pl.ds(4, 1), :] += jnp.sum(du, axis=0, keepdims=True)
        dx = None
        for k in range(CONV_W):
            sh = s_ref[pl.ds(3 - k, tt), :]
            term = w_ref[pl.ds(k, 1), :] * sh
            dx = term if dx is None else dx + term
            acc_ref[pl.ds(k, 1), :] += jnp.sum(x * sh, axis=0, keepdims=True)
        dx_ref[...] = dx.astype(BF16)

    return _time_call(cfg, body, [(du_a, 0), (du_b, 0), (du_c, 0), (ug, 0), (w4,)], [BF16], 1,
                      lambda tt, tc: [(tt + HIST, tc)], name, reverse=True)


def _scan8(a, b, rows):
    for k in (1, 2, 4):
        ar = jnp.where(rows >= k, pltpu.roll(a, k, 0), 1.0)
        br = jnp.where(rows >= k, pltpu.roll(b, k, 0), 0.0)
        b = a * br + b
        a = a * ar
    return a, b


def _rscan8(a, b, rows):
    for k in (1, 2, 4):
        ar = jnp.where(rows < 8 - k, pltpu.roll(a, 8 - k, 0), 1.0)
        br = jnp.where(rows < 8 - k, pltpu.roll(b, 8 - k, 0), 0.0)
        b = b + a * br
        a = a * ar
    return a, b


def _lru_gates(u, pa, px, ba, bx, c_lam):
    r = _sig(pa + ba)
    i = _sig(px + bx)
    la = LRU_C * r * c_lam
    em = _expm1(2.0 * la)
    return r, i, jnp.exp(la), em, jnp.sqrt(-em)


def _neg_softplus_neg(lam):
    e = jnp.exp(-jnp.abs(lam))
    u = 1.0 + e
    l1p = jnp.where(u == 1.0, e, jnp.log(u) * (e / jnp.where(u == 1.0, 1.0, u - 1.0)))
    return -(jnp.maximum(-lam, 0.0) + l1p)


def lru_fwd(cfg, u, pa, px, ug, ba, bx, lam, name):
    def body(u_ref, pa_ref, px_ref, g_ref, ba_ref, bx_ref, lam_ref, hs_ref, hp_ref, y_ref, car_ref, *, tt, tc, nt):
        @pl.when(pl.program_id(1) == 0)
        def _():
            car_ref[...] = jnp.zeros_like(car_ref)

        rows = lax.broadcasted_iota(jnp.int32, (8, tc), 0)
        ba, bx = ba_ref[...], bx_ref[...]
        c_lam = _neg_softplus_neg(lam_ref[...])

        def step(it, h):
            ys = []
            for half in range(2):
                rs = pl.ds(pl.multiple_of(it * 16 + half * 8, 8), 8)
                uu = u_ref[rs, :]
                r, i, a, em, s = _lru_gates(uu, pa_ref[rs, :], px_ref[rs, :], ba, bx, c_lam)
                acum, b = _scan8(a, uu * i * s, rows)
                hs = acum * h + b
                hs_ref[rs, :] = hs
                hp_ref[rs, :] = jnp.where(rows >= 1, pltpu.roll(hs, 1, 0), h)
                g = g_ref[rs, :]
                ys.append(hs * (g * _sig(g)))
                h = jnp.broadcast_to(hs[7:8, :], (8, tc))
            y_ref[pl.ds(pl.multiple_of(it * 16, 16), 16), :] = jnp.concatenate(ys, axis=0).astype(BF16)
            return h

        car_ref[...] = lax.fori_loop(0, tt // 16, step, car_ref[...])

    return _time_call(cfg, body, [(u, 0), (pa, 0), (px, 0), (ug, 1), (ba,), (bx,), (lam,)], [F32, F32, BF16], 0,
                      lambda tt, tc: [(8, tc)], name)


def lru_bwd(cfg, dy, ug, hs, hprev, u, pa, px, ba, bx, lam, name):
    def body(dy_ref, g_ref, hs_ref, hp_ref, u_ref, pa_ref, px_ref, ba_ref, bx_ref, lam_ref,
             dg_ref, dpa_ref, dpx_ref, du_ref, acc_ref, ecar_ref, acar_ref, *, tt, tc, nt):
        @pl.when(pl.program_id(1) == 0)
        def _():
            ecar_ref[...] = jnp.zeros_like(ecar_ref)
            acar_ref[...] = jnp.zeros_like(acar_ref)
            acc_ref[...] = jnp.zeros_like(acc_ref)

        rows = lax.broadcasted_iota(jnp.int32, (8, tc), 0)
        ba, bx = ba_ref[...], bx_ref[...]
        c_lam = _neg_softplus_neg(lam_ref[...])

        def step(it, carry):
            ecar, acar, s_lam, s_ba, s_bx = carry
            jt = tt // 16 - 1 - it
            dgs, dpas, dpxs = [None, None], [None, None], [None, None]
            for half in (1, 0):
                rs = pl.ds(pl.multiple_of(jt * 16 + half * 8, 8), 8)
                uu = u_ref[rs, :]
                r, i, a, em, s = _lru_gates(uu, pa_ref[rs, :], px_ref[rs, :], ba, bx, c_lam)
                g = g_ref[rs, :]
                sg = _sig(g)
                dy = dy_ref[rs, :]
                dgs[half] = dy * hs_ref[rs, :] * (sg * (1.0 + g * (1.0 - sg)))
                a_next = jnp.where(rows < 7, pltpu.roll(a, 7, 0), acar)
                acum, e = _rscan8(a_next, dy * (g * sg), rows)
                e = e + acum * ecar
                ecar = jnp.broadcast_to(e[0:1, :], (8, tc))
                acar = jnp.broadcast_to(a[0:1, :], (8, tc))
                du_ref[rs, :] = e * i * s
                d_la = e * hp_ref[rs, :] * a - (e * uu * i) * ((em + 1.0) / s)
                s_lam = s_lam + d_la * (LRU_C * r)
                d_pa = d_la * (LRU_C * c_lam) * r * (1.0 - r)
                d_px = e * uu * s * i * (1.0 - i)
                s_ba = s_ba + d_pa
                s_bx = s_bx + d_px
                dpas[half], dpxs[half] = d_pa, d_px
            r16 = pl.ds(pl.multiple_of(jt * 16, 16), 16)
            dg_ref[r16, :] = jnp.concatenate(dgs, axis=0).astype(BF16)
            dpa_ref[r16, :] = jnp.concatenate(dpas, axis=0).astype(BF16)
            dpx_ref[r16, :] = jnp.concatenate(dpxs, axis=0).astype(BF16)
            return ecar, acar, s_lam, s_ba, s_bx

        z = jnp.zeros((8, tc), F32)
        ecar, acar, s_lam, s_ba, s_bx = lax.fori_loop(0, tt // 16, step, (ecar_ref[...], acar_ref[...], z, z, z))
        ecar_ref[...] = ecar
        acar_ref[...] = acar
        acc_ref[pl.ds(0, 1), :] += jnp.sum(s_lam, axis=0, keepdims=True) * _sig(-lam_ref[...])
        acc_ref[pl.ds(1, 1), :] += jnp.sum(s_ba, axis=0, keepdims=True)
        acc_ref[pl.ds(2, 1), :] += jnp.sum(s_bx, axis=0, keepdims=True)

    return _time_call(cfg, body, [(dy, 0), (ug, 1), (hs, 0), (hprev, 0), (u, 0), (pa, 0), (px, 0), (ba,), (bx,), (lam,)],
                      [BF16, BF16, BF16, F32], 1, lambda tt, tc: [(8, tc), (8, tc)], name, reverse=True)


def _pool_select(cfg, grp, fn):
    for gi, w in enumerate(cfg.pool_windows):
        @pl.when(grp == gi)
        def _(w=w):
            fn(w)


def pool_fwd(cfg, ug, name):
    ng = len(cfg.pool_windows)

    def body(x_ref, p_ref, s_ref, *, tt, tc, nt):
        t = pl.program_id(1)
        grp = pl.program_id(0) // (cfg.d // ng // tc)
        x = x_ref[...]
        _push_history(s_ref, x, t, tt)
        row1 = (t * tt + 1 + lax.broadcasted_iota(jnp.int32, (tt, 1), 0)).astype(F32)

        def write(w):
            ws = x
            for j in range(1, w):
                ws = ws + s_ref[pl.ds(HIST - j, tt), :]
            p_ref[...] = (ws / jnp.minimum(row1, float(w)) - x).astype(BF16)

        _pool_select(cfg, grp, write)

    return _time_call(cfg, body, [(ug, 0)], [BF16], 0, lambda tt, tc: [(tt + HIST, tc)], name, groups=ng)[0]


def pool_bwd(cfg, dp, name):
    ng = len(cfg.pool_windows)

    def body(dp_ref, du_ref, s_ref, *, tt, tc, nt):
        t = pl.program_id(1)
        grp = pl.program_id(0) // (cfg.d // ng // tc)
        dp = dp_ref[...]
        row1 = ((nt - 1 - t) * tt + 1 + lax.broadcasted_iota(jnp.int32, (tt, 1), 0)).astype(F32)

        def write(w):
            dm = dp / jnp.minimum(row1, float(w))
            _push_future(s_ref, dm, t, tt)
            ws = dm
            for j in range(1, w):
                ws = ws + s_ref[pl.ds(j, tt), :]
            du_ref[...] = (ws - dp).astype(BF16)

        _pool_select(cfg, grp, write)

    return _time_call(cfg, body, [(dp, 0)], [BF16], 0, lambda tt, tc: [(tt + HIST, tc)], name, reverse=True, groups=ng)[0]


def gate_fwd(cfg, v, gsrc, gblk, scale, name):
    d = cfg.d

    def body(*refs):
        if scale is None:
            v_ref, g_ref, y_ref = refs
            v = v_ref[...]
        else:
            v_ref, g_ref, s_ref, y_ref = refs
            v = v_ref[...] * s_ref[...]
        g = g_ref[...]
        y_ref[...] = (v * (g * _sig(g))).astype(BF16)

    ins = [(v, d, 0), (gsrc, d, gblk)] + ([] if scale is None else [(scale, None)])
    return _rowwise(cfg, body, ins, [(d, BF16)], name)[0]


def gate_bwd(cfg, dy, v, gsrc, gblk, scale, dv_dtype, name):
    d = cfg.d

    def body(*refs):
        if scale is None:
            dy_ref, v_ref, g_ref, dv_ref, dg_ref = refs
            vs = v_ref[...]
        else:
            dy_ref, v_ref, g_ref, s_ref, dv_ref, dg_ref, acc_ref = refs
            vs = v_ref[...] * s_ref[...]
        g = g_ref[...]
        sg = _sig(g)
        dy = dy_ref[...]
        dvs = dy * (g * sg)
        dg_ref[...] = (dy * vs * (sg * (1.0 + g * (1.0 - sg)))).astype(BF16)
        if scale is None:
            dv_ref[...] = dvs.astype(dv_dtype)
        else:
            @pl.when(pl.program_id(0) == 0)
            def _():
                acc_ref[...] = jnp.zeros_like(acc_ref)

            acc_ref[...] += _colsum8(dvs * v_ref[...])
            dv_ref[...] = (dvs * s_ref[...]).astype(dv_dtype)

    ins = [(dy, d, 0), (v, d, 0), (gsrc, d, gblk)] + ([] if scale is None else [(scale, None)])
    outs = [(d, dv_dtype), (d, BF16)] + ([] if scale is None else [(d,)])
    return _rowwise(cfg, body, ins, outs, name, n_acc=0 if scale is None else 1)


def _swap_halves(x, lo, half):
    w = x.shape[1]
    lane = lax.broadcasted_iota(jnp.int32, x.shape, 1)
    sw = jnp.where(lane < lo + half, pltpu.roll(x, w - half, 1), pltpu.roll(x, half, 1))
    return jnp.where((lane >= lo) & (lane < lo + 2 * half), sw, 0.0)


def mla_norm(cfg, proj, qn, kvn, name):
    ql, kl = cfg.q_lora, cfg.kv_lora

    def body(c_ref, qn_ref, kn_ref, q_ref, k_ref):
        cq = c_ref[:, 0:ql]
        ck = c_ref[:, ql:ql + kl]
        q_ref[...] = (cq * lax.rsqrt(jnp.mean(cq * cq, axis=-1, keepdims=True) + RMS_EPS) * qn_ref[...]).astype(BF16)
        k_ref[...] = (ck * lax.rsqrt(jnp.mean(ck * ck, axis=-1, keepdims=True) + RMS_EPS) * kn_ref[...]).astype(BF16)

    return _rowwise(cfg, body, [(proj, cfg.cw, cfg.d // cfg.cw), (qn, None), (kvn, None)], [(ql, BF16), (kl, BF16)], name)


def mla_norm_bwd(cfg, proj, d_cqn, d_ckvn, d_kr, qn, kvn, name):
    ql, kl, cw, npe = cfg.q_lora, cfg.kv_lora, cfg.cw, cfg.nope

    def one(x, dy, gamma):
        rstd = lax.rsqrt(jnp.mean(x * x, axis=-1, keepdims=True) + RMS_EPS)
        xn = x * rstd
        dxn = dy * gamma
        return rstd * (dxn - xn * jnp.mean(dxn * xn, axis=-1, keepdims=True)), dy * xn

    def body(c_ref, dq_ref, dk_ref, dkr_ref, qn_ref, kn_ref, dc_ref, acc_ref):
        @pl.when(pl.program_id(0) == 0)
        def _():
            acc_ref[...] = jnp.zeros_like(acc_ref)

        dcq, gq = one(c_ref[:, 0:ql], dq_ref[...], qn_ref[...])
        dck, gk = one(c_ref[:, ql:ql + kl], dk_ref[...], kn_ref[...])
        dc_ref[:, 0:ql] = dcq.astype(BF16)
        dc_ref[:, ql:ql + kl] = dck.astype(BF16)
        dc_ref[:, ql + kl:ql + kl + npe] = dkr_ref[...].astype(BF16)
        rest = cw - (ql + kl + npe)
        if rest:
            dc_ref[:, ql + kl + npe:cw] = jnp.zeros((dc_ref.shape[0], rest), BF16)
        acc_ref[:, 0:ql] += _colsum8(gq)
        acc_ref[:, ql:ql + kl] += _colsum8(gk)

    return _rowwise(cfg, body, [(proj, cw, cfg.d // cw), (d_cqn, ql, 0), (d_ckvn, kl, 0), (d_kr, npe, 0), (qn, None), (kvn, None)],
                    [(cw, BF16), (cw,)], name, n_acc=1)


def _mla_tables(cfg):
    r2, npe = cfg.rope // 2, cfg.nope
    inv = ROPE_BASE ** (-jnp.arange(0, cfg.rope, 2, dtype=F32) / cfg.rope)
    ang = jnp.arange(cfg.tp, dtype=F32)[:, None] * inv[None, :]
    cos, sin = jnp.cos(ang), jnp.sin(ang)
    one = jnp.ones((cfg.tp, npe - cfg.rope), F32)
    ck = jnp.concatenate([cos, cos, one], axis=1)
    sk = jnp.concatenate([-sin, sin, 0.0 * one], axis=1)
    cq = jnp.concatenate([jnp.ones((cfg.tp, npe), F32), ck], axis=1)
    sq = jnp.concatenate([jnp.zeros((cfg.tp, npe), F32), sk], axis=1)
    return cq, sq, ck, sk


def mla_prep(cfg, q_raw, kv_raw, proj, tabs, name):
    h, npe, r2, ql, kl = cfg.mla_heads, cfg.nope, cfg.rope // 2, cfg.q_lora, cfg.kv_lora
    hp = 2 * npe

    def body(q_ref, kv_ref, c_ref, cq_ref, sq_ref, ck_ref, sk_ref, qo_ref, ko_ref, vo_ref):
        kr = c_ref[:, ql + kl:ql + kl + npe]
        kr = (kr * ck_ref[...] + _swap_halves(kr, 0, r2) * sk_ref[...]).astype(BF16)
        cq, sq = cq_ref[...], sq_ref[...]
        for i in range(h):
            q = q_ref[:, i * hp:(i + 1) * hp]
            qo_ref[:, i * hp:(i + 1) * hp] = (q * cq + _swap_halves(q, npe, r2) * sq).astype(BF16)
            ko_ref[:, i * hp:i * hp + npe] = kv_ref[:, i * npe:(i + 1) * npe].astype(BF16)
            ko_ref[:, i * hp + npe:(i + 1) * hp] = kr
        vo_ref[...] = kv_ref[:, h * npe:2 * h * npe].astype(BF16)

    cq, sq, ck, sk = tabs
    return _rowwise(cfg, body, [(q_raw, h * hp, 0), (kv_raw, 2 * h * npe, 0), (proj, cfg.cw, cfg.d // cfg.cw),
                                (cq, hp, 0), (sq, hp, 0), (ck, npe, 0), (sk, npe, 0)],
                    [(h * hp, BF16), (h * hp, BF16), (h * npe, BF16)], name)


def mla_prep_bwd(cfg, dq_full, dk_full, tabs, name):
    h, npe, r2 = cfg.mla_heads, cfg.nope, cfg.rope // 2
    hp = 2 * npe

    def body(dq_ref, dk_ref, cq_ref, sq_ref, ck_ref, sk_ref, dqo_ref, dko_ref, dkr_ref):
        cq, sq = cq_ref[...], sq_ref[...]
        dkr = None
        for i in range(h):
            dq = dq_ref[:, i * hp:(i + 1) * hp]
            dqo_ref[:, i * hp:(i + 1) * hp] = (dq * cq + _swap_halves(dq * sq, npe, r2)).astype(BF16)
            dko_ref[:, i * npe:(i + 1) * npe] = dk_ref[:, i * hp:i * hp + npe].astype(BF16)
            part = dk_ref[:, i * hp + npe:(i + 1) * hp]
            dkr = part if dkr is None else dkr + part
        dkr_ref[...] = dkr * ck_ref[...] + _swap_halves(dkr * sk_ref[...], 0, r2)

    cq, sq, ck, sk = tabs
    return _rowwise(cfg, body, [(dq_full, h * hp, 0), (dk_full, h * hp, 0), (cq, hp, 0), (sq, hp, 0), (ck, npe, 0), (sk, npe, 0)],
                    [(h * hp, BF16), (h * npe, BF16), (npe, F32)], name)


def _attn_scores(cfg, q, k, qi, ki, tq):
    s = lax.dot_general(q, k, _NT, preferred_element_type=F32) * ((cfg.nope + cfg.rope) ** -0.5)
    row = qi * tq + lax.broadcasted_iota(jnp.int32, (tq, tq), 0)
    col = ki * tq + lax.broadcasted_iota(jnp.int32, (tq, tq), 1)
    return jnp.where(row >= col, s, -1e30)


def attn_fwd(cfg, q, k, v, name):
    h, npe, tq = cfg.mla_heads, cfg.nope, cfg.tq
    hp, n = 2 * npe, cfg.tp // tq

    def body(q_ref, k_ref, v_ref, o_ref, lse_ref, m_ref, l_ref, acc_ref):
        qi, ki = pl.program_id(1), pl.program_id(2)

        @pl.when(ki == 0)
        def _():
            m_ref[...] = jnp.full_like(m_ref, -1e30)
            l_ref[...] = jnp.zeros_like(l_ref)
            acc_ref[...] = jnp.zeros_like(acc_ref)

        @pl.when(ki <= qi)
        def _():
            s = _attn_scores(cfg, q_ref[...], k_ref[...], qi, ki, tq)
            m_new = jnp.maximum(m_ref[...], jnp.max(s, axis=-1, keepdims=True))
            alpha = jnp.exp(m_ref[...] - m_new)
            p = jnp.exp(s - m_new)
            l_ref[...] = alpha * l_ref[...] + jnp.sum(p, axis=-1, keepdims=True)
            acc_ref[...] = alpha * acc_ref[...] + lax.dot_general(p.astype(BF16), v_ref[...], _NN, preferred_element_type=F32)
            m_ref[...] = m_new

        @pl.when(ki == qi)
        def _():
            o_ref[...] = acc_ref[...] / l_ref[...]
            lse_ref[...] = m_ref[...] + jnp.log(l_ref[...])

    return pl.pallas_call(
        body, grid=(h, n, n),
        in_specs=[pl.BlockSpec((tq, hp), lambda hh, qi, ki: (qi, hh)),
                  pl.BlockSpec((tq, hp), lambda hh, qi, ki: (jnp.minimum(ki, qi), hh)),
                  pl.BlockSpec((tq, npe), lambda hh, qi, ki: (jnp.minimum(ki, qi), hh))],
        out_specs=[pl.BlockSpec((tq, npe), lambda hh, qi, ki: (qi, hh)),
                   pl.BlockSpec((None, tq, 1), lambda hh, qi, ki: (hh, qi, 0))],
        out_shape=[jax.ShapeDtypeStruct((cfg.tp, h * npe), F32), jax.ShapeDtypeStruct((h, cfg.tp, 1), F32)],
        scratch_shapes=[pltpu.VMEM((tq, 1), F32), pltpu.VMEM((tq, 1), F32), pltpu.VMEM((tq, npe), F32)],
        compiler_params=_cparams("parallel", "parallel", "arbitrary"), name=name)(q, k, v)


def attn_bwd_dq(cfg, q, k, v, do, o, lse, name):
    h, npe, tq = cfg.mla_heads, cfg.nope, cfg.tq
    hp, n = 2 * npe, cfg.tp // tq
    sc = (cfg.nope + cfg.rope) ** -0.5

    def body(q_ref, k_ref, v_ref, do_ref, o_ref, lse_ref, dq_ref, dl_ref, acc_ref):
        qi, ki = pl.program_id(1), pl.program_id(2)

        @pl.when(ki == 0)
        def _():
            acc_ref[...] = jnp.zeros_like(acc_ref)
            dl_ref[...] = jnp.sum(do_ref[...].astype(F32) * o_ref[...], axis=-1, keepdims=True)

        @pl.when(ki <= qi)
        def _():
            s = _attn_scores(cfg, q_ref[...], k_ref[...], qi, ki, tq)
            p = jnp.exp(s - lse_ref[...])
            dp = lax.dot_general(do_ref[...], v_ref[...], _NT, preferred_element_type=F32)
            ds = p * (dp - dl_ref[...]) * sc
            acc_ref[...] += lax.dot_general(ds.astype(BF16), k_ref[...], _NN, preferred_element_type=F32)

        @pl.when(ki == qi)
        def _():
            dq_ref[...] = acc_ref[...]

    kmap = lambda hh, qi, ki: (jnp.minimum(ki, qi), hh)
    qmap = lambda hh, qi, ki: (qi, hh)
    rmap = lambda hh, qi, ki: (hh, qi, 0)
    return pl.pallas_call(
        body, grid=(h, n, n),
        in_specs=[pl.BlockSpec((tq, hp), qmap), pl.BlockSpec((tq, hp), kmap), pl.BlockSpec((tq, npe), kmap),
                  pl.BlockSpec((tq, npe), qmap), pl.BlockSpec((tq, npe), qmap), pl.BlockSpec((None, tq, 1), rmap)],
        out_specs=[pl.BlockSpec((tq, hp), qmap), pl.BlockSpec((None, tq, 1), rmap)],
        out_shape=[jax.ShapeDtypeStruct((cfg.tp, h * hp), F32), jax.ShapeDtypeStruct((h, cfg.tp, 1), F32)],
        scratch_shapes=[pltpu.VMEM((tq, hp), F32)],
        compiler_params=_cparams("parallel", "parallel", "arbitrary"), name=name)(q, k, v, do, o, lse)


def attn_bwd_dkv(cfg, q, k, v, do, lse, delta, name):
    h, npe, tq = cfg.mla_heads, cfg.nope, cfg.tq
    hp, n = 2 * npe, cfg.tp // tq
    sc = (cfg.nope + cfg.rope) ** -0.5

    def body(q_ref, k_ref, v_ref, do_ref, lse_ref, dl_ref, dk_ref, dv_ref, dk_acc, dv_acc):
        ki, qi = pl.program_id(1), pl.program_id(2)

        @pl.when(qi == 0)
        def _():
            dk_acc[...] = jnp.zeros_like(dk_acc)
            dv_acc[...] = jnp.zeros_like(dv_acc)

        @pl.when(qi >= ki)
        def _():
            s = _attn_scores(cfg, q_ref[...], k_ref[...], qi, ki, tq)
            p = jnp.exp(s - lse_ref[...])
            dv_acc[...] += lax.dot_general(p.astype(BF16), do_ref[...], _TN, preferred_element_type=F32)
            dp = lax.dot_general(do_ref[...], v_ref[...], _NT, preferred_element_type=F32)
            ds = p * (dp - dl_ref[...]) * sc
            dk_acc[...] += lax.dot_general(ds.astype(BF16), q_ref[...], _TN, preferred_element_type=F32)

        @pl.when(qi == n - 1)
        def _():
            dk_ref[...] = dk_acc[...]
            dv_ref[...] = dv_acc[...].astype(BF16)

    qmap = lambda hh, ki, qi: (jnp.maximum(qi, ki), hh)
    kmap = lambda hh, ki, qi: (ki, hh)
    rmap = lambda hh, ki, qi: (hh, jnp.maximum(qi, ki), 0)
    return pl.pallas_call(
        body, grid=(h, n, n),
        in_specs=[pl.BlockSpec((tq, hp), qmap), pl.BlockSpec((tq, hp), kmap), pl.BlockSpec((tq, npe), kmap),
                  pl.BlockSpec((tq, npe), qmap), pl.BlockSpec((None, tq, 1), rmap), pl.BlockSpec((None, tq, 1), rmap)],
        out_specs=[pl.BlockSpec((tq, hp), kmap), pl.BlockSpec((tq, npe), kmap)],
        out_shape=[jax.ShapeDtypeStruct((cfg.tp, h * hp), F32), jax.ShapeDtypeStruct((cfg.tp, h * npe), BF16)],
        scratch_shapes=[pltpu.VMEM((tq, hp), F32), pltpu.VMEM((tq, npe), F32)],
        compiler_params=_cparams("parallel", "parallel", "arbitrary"), name=name)(q, k, v, do, lse, delta)


def _ret_tables(cfg):
    c = cfg.chunk
    lg = jnp.log(1.0 - 2.0 ** (-5.0 - jnp.arange(cfg.ret_heads, dtype=F32)))[:, None, None]
    j = jnp.arange(c, dtype=F32)
    diff = j[:, None] - j[None, :]
    dm = jnp.where(diff >= 0, jnp.exp(jnp.maximum(diff, 0.0)[None] * lg), 0.0)
    lq = jnp.exp((j + 1.0)[None, :, None] * lg)
    lk = jnp.exp((c - 1.0 - j)[None, :, None] * lg)
    gc = jnp.exp(c * lg)
    return dm, lq, lk, gc


def _ret_rope_tables(cfg):
    dk = cfg.d // cfg.ret_heads
    inv = ROPE_BASE ** (-jnp.arange(0, dk, 2, dtype=F32) / dk)
    ang = jnp.arange(cfg.tp, dtype=F32)[:, None] * inv[None, :]
    return jnp.cos(ang), jnp.sin(ang)


def ret_prep(cfg, qkvg, cos, sin, name):
    d, h = cfg.d, cfg.ret_heads
    dk = d // h
    hd = dk // 2
    ksc = dk ** -0.5

    def body(q_ref, k_ref, v_ref, c_ref, s_ref, qo_ref, ko_ref, vo_ref):
        c, s = c_ref[...], s_ref[...]
        for i in range(h):
            for src, dst, f in ((q_ref, qo_ref, 1.0), (k_ref, ko_ref, ksc)):
                x1 = src[:, i * dk:i * dk + hd]
                x2 = src[:, i * dk + hd:(i + 1) * dk]
                dst[:, i * dk:i * dk + hd] = ((x1 * c - x2 * s) * f).astype(BF16)
                dst[:, i * dk + hd:(i + 1) * dk] = ((x2 * c + x1 * s) * f).astype(BF16)
        vo_ref[...] = v_ref[...].astype(BF16)

    return _rowwise(cfg, body, [(qkvg, d, 0), (qkvg, d, 1), (qkvg, d, 2), (cos, hd, 0), (sin, hd, 0)],
                    [(d, BF16), (d, BF16), (d, BF16)], name)


def ret_prep_bwd(cfg, dq, dk_, dv, dg, cos, sin, name):
    d, h = cfg.d, cfg.ret_heads
    dk = d // h
    hd = dk // 2
    ksc = dk ** -0.5

    def body(dq_ref, dk_ref, dv_ref, dg_ref, c_ref, s_ref, o_ref):
        c, s = c_ref[...], s_ref[...]
        for i in range(h):
            for src, off, f in ((dq_ref, 0, 1.0), (dk_ref, d, ksc)):
                y1 = src[:, i * dk:i * dk + hd]
                y2 = src[:, i * dk + hd:(i + 1) * dk]
                o_ref[:, off + i * dk:off + i * dk + hd] = ((y1 * c + y2 * s) * f).astype(BF16)
                o_ref[:, off + i * dk + hd:off + (i + 1) * dk] = ((y2 * c - y1 * s) * f).astype(BF16)
        o_ref[:, 2 * d:3 * d] = dv_ref[...]
        o_ref[:, 3 * d:4 * d] = dg_ref[...]

    return _rowwise(cfg, body, [(dq, d, 0), (dk_, d, 0), (dv, d, 0), (dg, d, 0), (cos, hd, 0), (sin, hd, 0)],
                    [(4 * d, BF16)], name)[0]


def _ret_specs(cfg, reverse):
    c, h = cfg.chunk, cfg.ret_heads
    dk = cfg.d // h
    n = cfg.tp // c
    tm = (lambda t: n - 1 - t) if reverse else (lambda t: t)
    blk = pl.BlockSpec((c, dk), lambda hh, t: (tm(t), hh))
    row = pl.BlockSpec((None, c, 1), lambda hh, t: (hh, tm(t), 0))
    tabs = [pl.BlockSpec((None, c, c), lambda hh, t: (hh, 0, 0)), pl.BlockSpec((None, c, 1), lambda hh, t: (hh, 0, 0)),
            pl.BlockSpec((None, c, 1), lambda hh, t: (hh, 0, 0)), pl.BlockSpec((None, 1, 1), lambda hh, t: (hh, 0, 0))]
    return blk, row, tabs, n, dk


def _ret_state_update(s_ref, k, v, lk, gc):
    kw = (k.astype(F32) * lk).astype(BF16)
    s_ref[...] = gc * s_ref[...] + lax.dot_general(kw, v, _TN, preferred_element_type=F32)


def ret_fwd(cfg, q, k, v, tabs, name):
    blk, row, tspecs, n, dk = _ret_specs(cfg, False)

    def body(q_ref, k_ref, v_ref, dm_ref, lq_ref, lk_ref, gc_ref, on_ref, rs_ref, s_ref):
        @pl.when(pl.program_id(1) == 0)
        def _():
            s_ref[...] = jnp.zeros_like(s_ref)

        qq, kk, vv = q_ref[...], k_ref[...], v_ref[...]
        a = lax.dot_general(qq, kk, _NT, preferred_element_type=F32) * dm_ref[...]
        o = lax.dot_general(a.astype(BF16), vv, _NN, preferred_element_type=F32)
        o = o + lax.dot_general(qq, s_ref[...].astype(BF16), _NN, preferred_element_type=F32) * lq_ref[...]
        _ret_state_update(s_ref, kk, vv, lk_ref[...], gc_ref[...])
        mu = jnp.mean(o, axis=-1, keepdims=True)
        oc = o - mu
        rstd = lax.rsqrt(jnp.mean(oc * oc, axis=-1, keepdims=True) + LN_EPS)
        on_ref[...] = oc * rstd
        rs_ref[...] = rstd

    return pl.pallas_call(body, grid=(cfg.ret_heads, n), in_specs=[blk, blk, blk] + tspecs, out_specs=[blk, row],
                          out_shape=[jax.ShapeDtypeStruct((cfg.tp, cfg.d), F32), jax.ShapeDtypeStruct((cfg.ret_heads, cfg.tp, 1), F32)],
                          scratch_shapes=[pltpu.VMEM((dk, dk), F32)], compiler_params=_cparams("parallel", "arbitrary"),
                          name=name)(q, k, v, *tabs)


def ret_bwd_q(cfg, q, k, v, d_on, on, rstd, tabs, name):
    blk, row, tspecs, n, dk = _ret_specs(cfg, False)

    def body(q_ref, k_ref, v_ref, don_ref, on_ref, rs_ref, dm_ref, lq_ref, lk_ref, gc_ref, do_ref, dq_ref, s_ref):
        @pl.when(pl.program_id(1) == 0)
        def _():
            s_ref[...] = jnp.zeros_like(s_ref)

        kk, vv = k_ref[...], v_ref[...]
        don, on = don_ref[...], on_ref[...]
        do = rs_ref[...] * (don - jnp.mean(don, axis=-1, keepdims=True) - on * jnp.mean(don * on, axis=-1, keepdims=True))
        dob = do.astype(BF16)
        do_ref[...] = dob
        ds = lax.dot_general(dob, vv, _NT, preferred_element_type=F32) * dm_ref[...]
        dq = lax.dot_general(ds.astype(BF16), kk, _NN, preferred_element_type=F32)
        dq_ref[...] = dq + lax.dot_general(dob, s_ref[...].astype(BF16), _NT, preferred_element_type=F32) * lq_ref[...]
        _ret_state_update(s_ref, kk, vv, lk_ref[...], gc_ref[...])

    return pl.pallas_call(body, grid=(cfg.ret_heads, n), in_specs=[blk, blk, blk, blk, blk, row] + tspecs, out_specs=[blk, blk],
                          out_shape=[jax.ShapeDtypeStruct((cfg.tp, cfg.d), BF16), jax.ShapeDtypeStruct((cfg.tp, cfg.d), F32)],
                          scratch_shapes=[pltpu.VMEM((dk, dk), F32)], compiler_params=_cparams("parallel", "arbitrary"),
                          name=name)(q, k, v, d_on, on, rstd, *tabs)


def ret_bwd_kv(cfg, q, k, v, do, tabs, name):
    blk, row, tspecs, n, dk = _ret_specs(cfg, True)

    def body(q_ref, k_ref, v_ref, do_ref, dm_ref, lq_ref, lk_ref, gc_ref, dk_ref, dv_ref, g_ref):
        @pl.when(pl.program_id(1) == 0)
        def _():
            g_ref[...] = jnp.zeros_like(g_ref)

        qq, kk, vv, dob = q_ref[...], k_ref[...], v_ref[...], do_ref[...]
        dm, lk = dm_ref[...], lk_ref[...]
        gb = g_ref[...].astype(BF16)
        a = lax.dot_general(qq, kk, _NT, preferred_element_type=F32) * dm
        ds = lax.dot_general(dob, vv, _NT, preferred_element_type=F32) * dm
        dkk = lax.dot_general(ds.astype(BF16), qq, _TN, preferred_element_type=F32)
        dk_ref[...] = dkk + lax.dot_general(vv, gb, _NT, preferred_element_type=F32) * lk
        kw = (kk.astype(F32) * lk).astype(BF16)
        dvv = lax.dot_general(a.astype(BF16), dob, _TN, preferred_element_type=F32)
        dv_ref[...] = (dvv + lax.dot_general(kw, gb, _NN, preferred_element_type=F32)).astype(BF16)
        qw = (qq.astype(F32) * lq_ref[...]).astype(BF16)
        g_ref[...] = gc_ref[...] * g_ref[...] + lax.dot_general(qw, dob, _TN, preferred_element_type=F32)

    return pl.pallas_call(body, grid=(cfg.ret_heads, n), in_specs=[blk, blk, blk, blk] + tspecs, out_specs=[blk, blk],
                          out_shape=[jax.ShapeDtypeStruct((cfg.tp, cfg.d), F32), jax.ShapeDtypeStruct((cfg.tp, cfg.d), BF16)],
                          scratch_shapes=[pltpu.VMEM((dk, dk), F32)], compiler_params=_cparams("parallel", "arbitrary"),
                          name=name)(q, k, v, do, *tabs)


def _flat2(a):
    return a.reshape(-1, a.shape[-1])


def _ew_call(body, ins, out_dtypes, name, lead=None):
    r, c = ins[-1].shape[-2:]
    tr = _div_tile(r, max(16, (1 << 18) // c), 16)
    specs = []
    for a in ins:
        if a.ndim == 3:
            specs.append(pl.BlockSpec((a.shape[0], tr, c), lambda i: (0, i, 0)))
        else:
            specs.append(pl.BlockSpec((tr, c), lambda i: (i, 0)))
    return pl.pallas_call(body, grid=(r // tr,), in_specs=specs,
                          out_specs=[pl.BlockSpec((tr, c), lambda i: (i, 0)) for _ in out_dtypes],
                          out_shape=[jax.ShapeDtypeStruct((r, c), dt) for dt in out_dtypes],
                          compiler_params=_cparams("parallel"), name=name)(*ins)


def adamw(w, g, m, v, name):
    c1 = 1.0 - ADAM_B1 ** ADAM_STEP
    c2 = 1.0 - ADAM_B2 ** ADAM_STEP

    def body(w_ref, g_ref, m_ref, v_ref, d_ref, mo_ref, vo_ref):
        gg = g_ref[...]
        mn = ADAM_B1 * m_ref[...] + (1.0 - ADAM_B1) * gg
        vn = ADAM_B2 * v_ref[...] + (1.0 - ADAM_B2) * (gg * gg)
        d_ref[...] = -ADAM_LR * ((mn / c1) / (jnp.sqrt(vn / c2) + ADAM_EPS) + ADAM_WD * w_ref[...])
        mo_ref[...] = mn
        vo_ref[...] = vn

    outs = _ew_call(body, [_flat2(w), _flat2(g), _flat2(m), _flat2(v)], [F32, F32, F32], name)
    return [o.reshape(w.shape) for o in outs]


def add_pair_bf16(a, b, name):
    def body(a_ref, b_ref, o_ref):
        o_ref[...] = (a_ref[...] + b_ref[...]).astype(BF16)

    return _ew_call(body, [_flat2(a), _flat2(b)], [BF16], name)[0].reshape(a.shape)


def sum_slots(x, name):
    x3 = x.reshape(x.shape[0], -1, x.shape[-1])

    def body(x_ref, o_ref):
        acc = x_ref[0].astype(F32)
        for s in range(1, x3.shape[0]):
            acc = acc + x_ref[s].astype(F32)
        o_ref[...] = acc

    return _ew_call(body, [x3], [F32], name)[0].reshape(x.shape[1:])


def _coords():
    return lax.axis_index("x"), lax.axis_index("y"), lax.axis_index("c")


def _chip_peer(x, y, k):
    return (x ^ (k >> 1), y ^ (k & 1))


_ANY = pl.BlockSpec(memory_space=pl.ANY)


def allreduce_small(x, name):
    r, _, w = x.shape

    def body(x_ref, o_ref, mine_ref, gat_ref, send_sems, recv_sems):
        mx, my, mc = _coords()
        me = 4 * mx + 2 * my + mc
        mine_ref[...] = jnp.sum(x_ref[...], axis=1)
        gat_ref[me] = mine_ref[...]
        copies = []
        for k in range(1, 8):
            peer = (mx ^ (k >> 2), my ^ ((k >> 1) & 1), mc ^ (k & 1))
            cp = pltpu.make_async_remote_copy(src_ref=mine_ref, dst_ref=gat_ref.at[me], send_sem=send_sems.at[k - 1],
                                              recv_sem=recv_sems.at[k - 1], device_id=peer, device_id_type=MESH)
            cp.start()
            copies.append(cp)
        for k in range(1, 8):
            pltpu.make_async_remote_copy(src_ref=mine_ref, dst_ref=gat_ref.at[me ^ k], send_sem=send_sems.at[k - 1],
                                         recv_sem=recv_sems.at[k - 1], device_id=(mx, my, mc), device_id_type=MESH).wait_recv()
        for cp in copies:
            cp.wait_send()
        acc = gat_ref[0]
        for s in range(1, 8):
            acc = acc + gat_ref[s]
        o_ref[...] = acc

    return pl.pallas_call(body, out_shape=jax.ShapeDtypeStruct((r, w), F32),
                          in_specs=[pl.BlockSpec(memory_space=pltpu.VMEM)], out_specs=pl.BlockSpec(memory_space=pltpu.VMEM),
                          scratch_shapes=[pltpu.VMEM((r, w), F32), pltpu.VMEM((8, r, w), F32),
                                          pltpu.SemaphoreType.DMA((7,)), pltpu.SemaphoreType.DMA((7,))],
                          compiler_params=pltpu.CompilerParams(vmem_limit_bytes=VMEM_LIMIT_V7X), name=name)(x)


def allgather_chips(shards, name):
    n = len(shards)

    def body(*refs):
        ins, outs = refs[:n], refs[n:2 * n]
        loc_sems, s1_send, s1_recv, s2_send, s2_recv = refs[2 * n:]
        mx, my, mc = _coords()
        p = 2 * mx + my
        sib = (mx, my, 1 - mc)
        local, sends = [], []
        for a in range(n):
            h0 = ins[a].shape[0] // 2
            half = pl.ds(mc * h0, h0)
            loc = pltpu.make_async_copy(ins[a], outs[a].at[p], loc_sems.at[a])
            loc.start()
            local.append(loc)
            for k in (1, 2, 3):
                qx, qy = _chip_peer(mx, my, k)
                cp = pltpu.make_async_remote_copy(src_ref=ins[a].at[half], dst_ref=outs[a].at[p, half],
                                                  send_sem=s1_send.at[a, k - 1], recv_sem=s1_recv.at[a, k - 1],
                                                  device_id=(qx, qy, mc), device_id_type=MESH)
                cp.start()
                sends.append(cp)
        for a in range(n):
            h0 = ins[a].shape[0] // 2
            half = pl.ds(mc * h0, h0)
            for k in (1, 2, 3):
                q = p ^ k
                landed = outs[a].at[q, half]
                pltpu.make_async_remote_copy(src_ref=landed, dst_ref=landed, send_sem=s1_send.at[a, k - 1],
                                             recv_sem=s1_recv.at[a, k - 1], device_id=sib, device_id_type=MESH).wait_recv()
                fw = pltpu.make_async_remote_copy(src_ref=landed, dst_ref=landed, send_sem=s2_send.at[a, k - 1],
                                                  recv_sem=s2_recv.at[a, k - 1], device_id=sib, device_id_type=MESH)
                fw.start()
                sends.append(fw)
        for a in range(n):
            h0 = ins[a].shape[0] // 2
            other = pl.ds((1 - mc) * h0, h0)
            for k in (1, 2, 3):
                theirs = outs[a].at[p ^ k, other]
                pltpu.make_async_remote_copy(src_ref=theirs, dst_ref=theirs, send_sem=s2_send.at[a, k - 1],
                                             recv_sem=s2_recv.at[a, k - 1], device_id=sib, device_id_type=MESH).wait_recv()
        for cp in sends:
            cp.wait_send()
        for loc in local:
            loc.wait()

    return pl.pallas_call(body, out_shape=[jax.ShapeDtypeStruct((4,) + s.shape, s.dtype) for s in shards],
                          in_specs=[_ANY] * n, out_specs=[_ANY] * n,
                          scratch_shapes=[pltpu.SemaphoreType.DMA((n,)), pltpu.SemaphoreType.DMA((n, 3)), pltpu.SemaphoreType.DMA((n, 3)),
                                          pltpu.SemaphoreType.DMA((n, 3)), pltpu.SemaphoreType.DMA((n, 3))],
                          name=name)(*shards)


def exchange_halves(grads, name):
    n = len(grads)

    def body(*refs):
        ins, mine, theirs = refs[:n], refs[n:2 * n], refs[2 * n:3 * n]
        loc_sems, send_sems, recv_sems = refs[3 * n:]
        mx, my, mc = _coords()
        sib = (mx, my, 1 - mc)
        pend = []
        for a in range(n):
            loc = pltpu.make_async_copy(ins[a].at[:, pl.ds(mc, 1)], mine[a], loc_sems.at[a])
            loc.start()
            cp = pltpu.make_async_remote_copy(src_ref=ins[a].at[:, pl.ds(1 - mc, 1)], dst_ref=theirs[a], send_sem=send_sems.at[a],
                                              recv_sem=recv_sems.at[a], device_id=sib, device_id_type=MESH)
            cp.start()
            pend.append((loc, cp))
        for loc, cp in pend:
            cp.wait()
            loc.wait()

    shp = [jax.ShapeDtypeStruct((4, 1) + g.shape[2:], g.dtype) for g in grads]
    outs = pl.pallas_call(body, out_shape=shp + shp, in_specs=[_ANY] * n, out_specs=[_ANY] * (2 * n),
                          scratch_shapes=[pltpu.SemaphoreType.DMA((n,)), pltpu.SemaphoreType.DMA((n,)), pltpu.SemaphoreType.DMA((n,))],
                          name=name)(*grads)
    return outs[:n], outs[n:]


def scatter_chips(parts, name):
    n = len(parts)

    def body(*refs):
        ins, outs = refs[:n], refs[n:2 * n]
        loc_sems, send_sems, recv_sems = refs[2 * n:]
        mx, my, mc = _coords()
        p = 2 * mx + my
        pend = []
        for a in range(n):
            loc = pltpu.make_async_copy(ins[a].at[p], outs[a].at[0], loc_sems.at[a])
            loc.start()
            pend.append(loc)
            for k in (1, 2, 3):
                qx, qy = _chip_peer(mx, my, k)
                cp = pltpu.make_async_remote_copy(src_ref=ins[a].at[p ^ k], dst_ref=outs[a].at[k], send_sem=send_sems.at[a, k - 1],
                                                  recv_sem=recv_sems.at[a, k - 1], device_id=(qx, qy, mc), device_id_type=MESH)
                cp.start()
                pend.append(cp)
        for cp in pend:
            cp.wait()

    return pl.pallas_call(body, out_shape=[jax.ShapeDtypeStruct(g.shape, g.dtype) for g in parts], in_specs=[_ANY] * n,
                          out_specs=[_ANY] * n,
                          scratch_shapes=[pltpu.SemaphoreType.DMA((n,)), pltpu.SemaphoreType.DMA((n, 3)), pltpu.SemaphoreType.DMA((n, 3))],
                          name=name)(*parts)


def join_halves(halves, name):
    n = len(halves)

    def body(*refs):
        ins, outs = refs[:n], refs[n:2 * n]
        loc_sems, send_sems, recv_sems = refs[2 * n:]
        mx, my, mc = _coords()
        sib = (mx, my, 1 - mc)
        pend = []
        for a in range(n):
            dst = outs[a].at[pl.ds(mc, 1)]
            loc = pltpu.make_async_copy(ins[a], dst, loc_sems.at[a])
            loc.start()
            cp = pltpu.make_async_remote_copy(src_ref=ins[a], dst_ref=dst, send_sem=send_sems.at[a], recv_sem=recv_sems.at[a],
                                              device_id=sib, device_id_type=MESH)
            cp.start()
            pend.append((loc, cp, outs[a].at[pl.ds(1 - mc, 1)]))
        for a, (loc, cp, theirs) in enumerate(pend):
            cp.wait_send()
            pltpu.make_async_remote_copy(src_ref=ins[a], dst_ref=theirs, send_sem=send_sems.at[a], recv_sem=recv_sems.at[a],
                                         device_id=sib, device_id_type=MESH).wait_recv()
            loc.wait()

    return pl.pallas_call(body, out_shape=[jax.ShapeDtypeStruct((2,) + g.shape[1:], g.dtype) for g in halves],
                          in_specs=[_ANY] * n, out_specs=[_ANY] * n,
                          scratch_shapes=[pltpu.SemaphoreType.DMA((n,)), pltpu.SemaphoreType.DMA((n,)), pltpu.SemaphoreType.DMA((n,))],
                          name=name)(*halves)


def reduce_scatter_grads(grads, tag):
    mine, theirs = exchange_halves(grads, f"rs_pair_{tag}")
    parts = [add_pair_bf16(a, b, f"rs_add_{tag}_{i}") for i, (a, b) in enumerate(zip(mine, theirs))]
    landed = scatter_chips(parts, f"rs_chips_{tag}")
    halves = [sum_slots(x, f"rs_sum_{tag}_{i}") for i, x in enumerate(landed)]
    return join_halves(halves, f"rs_join_{tag}")


W_NAMES = ['meta_tokens', 'l0_w_in', 'l0_conv_w', 'l0_conv_b', 'l0_w_a', 'l0_b_a', 'l0_w_x', 'l0_b_x', 'l0_lam', 'l0_w_out',
           'l0_ln_g', 'l0_ln_b', 'l1_w_in', 'l1_w_grp', 'l1_scale', 'l1_w_out', 'l1_ln_g', 'l1_ln_b', 'l2_w_in', 'l2_q_norm',
           'l2_w_uq', 'l2_kv_norm', 'l2_w_ukv', 'l2_w_out', 'l2_ln_g', 'l2_ln_b', 'l3_w_in', 'l3_w_out', 'l3_ln_g', 'l3_ln_b']
BIG = {0: ['l0_w_in', 'l0_w_a', 'l0_w_x', 'l0_w_out'], 1: ['l1_w_in', 'l1_w_grp', 'l1_w_out'],
       2: ['l2_w_in', 'l2_w_uq', 'l2_w_ukv', 'l2_w_out'], 3: ['l3_w_in', 'l3_w_out']}
SMALL_SHARDED = ['meta_tokens', 'l0_conv_w']
SMALL_REPL = ['l0_conv_b', 'l0_b_a', 'l0_b_x', 'l0_lam', 'l0_ln_g', 'l0_ln_b', 'l1_scale', 'l1_ln_g', 'l1_ln_b',
              'l2_q_norm', 'l2_kv_norm', 'l2_ln_g', 'l2_ln_b', 'l3_ln_g', 'l3_ln_b']


def _rows8(rows, width):
    out = []
    for r in rows:
        r = r.reshape(-1, r.shape[-1])
        out.append(jnp.pad(r, ((0, 8 - r.shape[0]), (0, width - r.shape[1]))))
    return jnp.stack(out)


def _unblock(g4, axis):
    return jnp.concatenate([g4[i] for i in range(4)], axis=axis)


def _block(full, axis):
    return jnp.stack(jnp.split(full, 4, axis=axis))


def _step(cfg, a):
    d, tp, nm, seq = cfg.d, cfg.tp, cfg.n_meta, cfg.seq
    alpha = (2.0 * cfg.depth) ** 0.25
    mx, my, mc = _coords()
    p = 2 * mx + my
    dq4 = d // 4
    hm, npe, ql, kl = cfg.mla_heads, cfg.nope, cfg.q_lora, cfg.kv_lora
    hp = 2 * npe
    qk = npe + cfg.rope
    vec = lambda name: a[name].reshape(1, -1)

    sm = jnp.concatenate([a['meta_tokens'], a['l0_conv_w'].reshape(CONV_W, dq4)], axis=0)
    placed = lax.dynamic_update_slice(jnp.zeros((nm + CONV_W, d), F32), sm * (mc == 0).astype(F32), (0, p * dq4))
    gathered = allreduce_small(_rows8([placed[i:i + 1] for i in range(nm + CONV_W)], d), "gather_small")
    meta_full, conv_w4 = gathered[:nm], gathered[nm:]

    wg = {}
    for layer, names in BIG.items():
        got = allgather_chips([a[n].astype(BF16) for n in names], f"ag_l{layer}")
        wg.update(dict(zip(names, got)))
    heads_first = lambda w: jnp.moveaxis(w, 0, 1).reshape(w.shape[1], w.shape[0] * w.shape[2], w.shape[3])
    w0_in, w0_a, w0_x = wg['l0_w_in'], heads_first(wg['l0_w_a']), heads_first(wg['l0_w_x'])
    w1_in, w1_grp = wg['l1_w_in'], heads_first(wg['l1_w_grp'])
    w_out = [wg[f'l{i}_w_out'].reshape(1, d, d) for i in range(4)]
    w2_in = jnp.pad(_unblock(wg['l2_w_in'], 1), ((0, 0), (0, cfg.cw - (ql + kl + cfg.rope))))[None]
    w2_uq = jnp.pad(_unblock(wg['l2_w_uq'], 1).reshape(ql, hm, qk), ((0, 0), (0, 0), (0, hp - qk))).reshape(1, ql, hm * hp)
    w2_ukv = _unblock(wg['l2_w_ukv'], 1).reshape(kl, hm, 2, npe).transpose(0, 2, 1, 3).reshape(1, kl, 2 * hm * npe)
    w3_in = wg['l3_w_in']

    zpad = jnp.zeros((tp - nm - seq, d), F32)
    h0 = jnp.concatenate([meta_full, a['x'][0], zpad], axis=0)
    h0_bf = h0.astype(BF16)

    ug0 = mm_nn(cfg, h0_bf, w0_in, "l0_in")
    u0, u0_bf = conv_fwd(cfg, ug0, conv_w4, vec('l0_conv_b'), "l0_conv")
    pa0 = bd_nn(cfg, u0_bf, w0_a, "l0_gate_a")
    px0 = bd_nn(cfg, u0_bf, w0_x, "l0_gate_x")
    hs0, hprev0, y0 = lru_fwd(cfg, u0, pa0, px0, ug0, vec('l0_b_a'), vec('l0_b_x'), vec('l0_lam'), "l0_lru")
    o0 = mm_nn(cfg, y0, w_out[0], "l0_out")
    h1, h1_bf, xh0, rs0 = ln_fwd(cfg, h0, o0, vec('l0_ln_g'), vec('l0_ln_b'), alpha, "l0_ln")

    ug1 = mm_nn(cfg, h1_bf, w1_in, "l1_in")
    p1 = pool_fwd(cfg, ug1, "l1_pool")
    mm1 = bd_nn(cfg, p1, w1_grp, "l1_grp")
    y1 = gate_fwd(cfg, mm1, ug1, 1, vec('l1_scale'), "l1_gate")
    o1 = mm_nn(cfg, y1, w_out[1], "l1_out")
    h2, h2_bf, xh1, rs1 = ln_fwd(cfg, h1, o1, vec('l1_ln_g'), vec('l1_ln_b'), alpha, "l1_ln")

    proj = mm_nn(cfg, h2_bf, w2_in, "l2_in")
    cqn, ckvn = mla_norm(cfg, proj, vec('l2_q_norm'), vec('l2_kv_norm'), "l2_norm")
    q_raw = mm_nn(cfg, cqn, w2_uq, "l2_uq")
    kv_raw = mm_nn(cfg, ckvn, w2_ukv, "l2_ukv")
    tabs = _mla_tables(cfg)
    qf, kf, vb = mla_prep(cfg, q_raw, kv_raw, proj, tabs, "l2_prep")
    o_att, lse = attn_fwd(cfg, qf, kf, vb, "l2_attn")
    y2 = gate_fwd(cfg, o_att, proj, 0, None, "l2_gate")
    o2 = mm_nn(cfg, y2, w_out[2], "l2_out")
    h3, h3_bf, xh2, rs2 = ln_fwd(cfg, h2, o2, vec('l2_ln_g'), vec('l2_ln_b'), alpha, "l2_ln")

    qkvg = mm_nn(cfg, h3_bf, w3_in, "l3_in")
    cos, sin = _ret_rope_tables(cfg)
    qr, kr, vr = ret_prep(cfg, qkvg, cos, sin, "l3_prep")
    rt = _ret_tables(cfg)
    on, rsr = ret_fwd(cfg, qr, kr, vr, rt, "l3_ret")
    y3 = gate_fwd(cfg, on, qkvg, 3, None, "l3_gate")
    o3 = mm_nn(cfg, y3, w_out[3], "l3_out")
    h4, _, xh3, rs3 = ln_fwd(cfg, h3, o3, vec('l3_ln_g'), vec('l3_ln_b'), alpha, "l3_ln")

    tgt = jnp.concatenate([jnp.zeros((nm, d), F32), a['loss_target'][0], zpad], axis=0)
    dy4, lacc = loss_grad(cfg, h4, tgt, "loss")
    loss = lax.psum(0.5 * jnp.sum(lacc) / d, ("x", "y", "c"))

    grads, small = {}, {}

    def big_rs(layer, full):
        names = BIG[layer]
        got = reduce_scatter_grads([g.reshape((4, 2, g.shape[1] // 2) + g.shape[2:]) for g in full], f"l{layer}")
        for n, g in zip(names, got):
            grads[n] = g.reshape(a[n].shape)

    dz3, dz3_bf, small['l3_ln_g'], small['l3_ln_b'] = ln_bwd(cfg, None, dy4, xh3, rs3, vec('l3_ln_g'), alpha, "l3_ln_b")
    g_out = mm_tn(cfg, y3, dz3_bf, 1, "l3_dw_out")
    dyb = mm_nt(cfg, dz3_bf, w_out[3], "l3_dy")
    d_on, dg = gate_bwd(cfg, dyb, on, qkvg, 3, None, F32, "l3_gate_b")
    do_bf, dqr = ret_bwd_q(cfg, qr, kr, vr, d_on, on, rsr, rt, "l3_ret_bq")
    dkr, dvr = ret_bwd_kv(cfg, qr, kr, vr, do_bf, rt, "l3_ret_bkv")
    d_in = ret_prep_bwd(cfg, dqr, dkr, dvr, dg, cos, sin, "l3_prep_b")
    g_in = mm_tn(cfg, h3_bf, d_in, 4, "l3_dw_in")
    dh = mm_nt(cfg, d_in, w3_in, "l3_dh")
    big_rs(3, [g_in, g_out.reshape(4, dq4, d)])

    dz2, dz2_bf, small['l2_ln_g'], small['l2_ln_b'] = ln_bwd(cfg, dz3, dh, xh2, rs2, vec('l2_ln_g'), alpha, "l2_ln_b")
    g_out = mm_tn(cfg, y2, dz2_bf, 1, "l2_dw_out")
    dyb = mm_nt(cfg, dz2_bf, w_out[2], "l2_dy")
    do_bf, dg = gate_bwd(cfg, dyb, o_att, proj, 0, None, BF16, "l2_gate_b")
    dq_full, delta = attn_bwd_dq(cfg, qf, kf, vb, do_bf, o_att, lse, "l2_attn_bq")
    dk_full, dv_bf = attn_bwd_dkv(cfg, qf, kf, vb, do_bf, lse, delta, "l2_attn_bkv")
    dq_raw, dk_nope, d_kr = mla_prep_bwd(cfg, dq_full, dk_full, tabs, "l2_prep_b")
    dkv_raw = jnp.concatenate([dk_nope, dv_bf], axis=1)
    g_uq = mm_tn(cfg, cqn, dq_raw, 1, "l2_dw_uq")
    d_cqn = mm_nt(cfg, dq_raw, w2_uq, "l2_dcq")
    g_ukv = mm_tn(cfg, ckvn, dkv_raw, 1, "l2_dw_ukv")
    d_ckvn = mm_nt(cfg, dkv_raw, w2_ukv, "l2_dckv")
    d_c, nacc = mla_norm_bwd(cfg, proj, d_cqn, d_ckvn, d_kr, vec('l2_q_norm'), vec('l2_kv_norm'), "l2_norm_b")
    small['l2_q_norm'], small['l2_kv_norm'] = nacc[:, :ql], nacc[:, ql:ql + kl]
    d_in = jnp.concatenate([dg, d_c], axis=1)
    g_in = mm_tn(cfg, h2_bf, d_in, 1, "l2_dw_in")
    dh = mm_nt(cfg, d_in, w2_in, "l2_dh")
    g_in = _block(g_in[0][:, :d + ql + kl + cfg.rope], 1)
    g_uq = _block(g_uq.reshape(ql, hm, hp)[:, :, :qk].reshape(ql, hm * qk), 1)
    g_ukv = _block(g_ukv.reshape(kl, 2, hm, npe).transpose(0, 2, 1, 3).reshape(kl, 2 * hm * npe), 1)
    big_rs(2, [g_in, g_uq, g_ukv, g_out.reshape(4, dq4, d)])

    dz1, dz1_bf, small['l1_ln_g'], small['l1_ln_b'] = ln_bwd(cfg, dz2, dh, xh1, rs1, vec('l1_ln_g'), alpha, "l1_ln_b")
    g_out = mm_tn(cfg, y1, dz1_bf, 1, "l1_dw_out")
    dyb = mm_nt(cfg, dz1_bf, w_out[1], "l1_dy")
    d_mm1, dg, small['l1_scale'] = gate_bwd(cfg, dyb, mm1, ug1, 1, vec('l1_scale'), BF16, "l1_gate_b")
    g_grp = bd_tn(cfg, p1, d_mm1, len(cfg.pool_windows), "l1_dw_grp")
    dp = bd_nt(cfg, d_mm1, w1_grp, "l1_dp")
    du = pool_bwd(cfg, dp, "l1_pool_b")
    d_in = jnp.concatenate([du, dg], axis=1)
    g_in = mm_tn(cfg, h1_bf, d_in, 4, "l1_dw_in")
    dh = mm_nt(cfg, d_in, w1_in, "l1_dh")
    blocks_first = lambda g: jnp.moveaxis(g.reshape(g.shape[0], 4, g.shape[1] // 4, g.shape[2]), 1, 0)
    big_rs(1, [g_in, blocks_first(g_grp), g_out.reshape(4, dq4, d)])

    dz0, dz0_bf, small['l0_ln_g'], small['l0_ln_b'] = ln_bwd(cfg, dz1, dh, xh0, rs0, vec('l0_ln_g'), alpha, "l0_ln_b")
    g_out = mm_tn(cfg, y0, dz0_bf, 1, "l0_dw_out")
    dyb = mm_nt(cfg, dz0_bf, w_out[0], "l0_dy")
    dg, dpa, dpx, du_dir, lacc0 = lru_bwd(cfg, dyb, ug0, hs0, hprev0, u0, pa0, px0, vec('l0_b_a'), vec('l0_b_x'), vec('l0_lam'), "l0_lru_b")
    g_a = bd_tn(cfg, u0_bf, dpa, cfg.lru_heads, "l0_dw_a")
    g_x = bd_tn(cfg, u0_bf, dpx, cfg.lru_heads, "l0_dw_x")
    du_a = bd_nt(cfg, dpa, w0_a, "l0_du_a")
    du_x = bd_nt(cfg, dpx, w0_x, "l0_du_x")
    du_pre, cacc = conv_bwd(cfg, du_dir, du_a, du_x, ug0, conv_w4, "l0_conv_b")
    d_in = jnp.concatenate([du_pre, dg], axis=1)
    g_in = mm_tn(cfg, h0_bf, d_in, 4, "l0_dw_in")
    dh = mm_nt(cfg, d_in, w0_in, "l0_dh")
    big_rs(0, [g_in, blocks_first(g_a), blocks_first(g_x), g_out.reshape(4, dq4, d)])
    dh0 = axpy(cfg, dz0, dh, alpha, "dh0")
    grad_x = dh0[nm:nm + seq][None]
    small['l0_lam'], small['l0_b_a'], small['l0_b_x'] = lacc0[0:1], lacc0[1:2], lacc0[2:3]
    small['l0_conv_b'] = cacc[4:5]

    rows = [dh0[i:i + 1] for i in range(nm)] + [cacc[k:k + 1] for k in range(CONV_W)] + [small[n] for n in SMALL_REPL]
    red = allreduce_small(_rows8(rows, d), "reduce_small")
    sh = lax.dynamic_slice(red[:nm + CONV_W], (0, p * dq4), (nm + CONV_W, dq4))
    grads['meta_tokens'] = sh[:nm]
    grads['l0_conv_w'] = sh[nm:].reshape(a['l0_conv_w'].shape)
    for i, n in enumerate(SMALL_REPL):
        grads[n] = red[nm + CONV_W + i, :a[n].shape[0]]

    delta, new_m, new_v = {}, {}, {}
    for n in sum(BIG.values(), []):
        delta[n], new_m[n], new_v[n] = adamw(a[n], grads[n], a['m_' + n], a['v_' + n], f"adamw_{n}")
    pack_s = lambda pre: jnp.concatenate([a[pre + 'meta_tokens'], a[pre + 'l0_conv_w'].reshape(CONV_W, dq4)], axis=0)
    ds_, ms_, vs_ = adamw(pack_s(''), sh, pack_s('m_'), pack_s('v_'), "adamw_small_sharded")
    for out, packed in ((delta, ds_), (new_m, ms_), (new_v, vs_)):
        out['meta_tokens'] = packed[:nm]
        out['l0_conv_w'] = packed[nm:].reshape(a['l0_conv_w'].shape)
    pack_r = lambda pre: jnp.stack([jnp.pad(a[pre + n], (0, d - a[n].shape[0])) for n in SMALL_REPL])
    dr_, mr_, vr_ = adamw(pack_r(''), red[nm + CONV_W:nm + CONV_W + len(SMALL_REPL)], pack_r('m_'), pack_r('v_'), "adamw_small_repl")
    for out, packed in ((delta, dr_), (new_m, mr_), (new_v, vr_)):
        for i, n in enumerate(SMALL_REPL):
            out[n] = packed[i, :a[n].shape[0]]

    return (loss, grad_x, *[grads[n] for n in W_NAMES], *[delta[n] for n in W_NAMES],
            *[new_m[n] for n in W_NAMES], *[new_v[n] for n in W_NAMES])


def kernel(x, meta_tokens, l0_w_in, l0_conv_w, l0_conv_b, l0_w_a, l0_b_a, l0_w_x, l0_b_x, l0_lam, l0_w_out, l0_ln_g, l0_ln_b, l1_w_in, l1_w_grp, l1_scale, l1_w_out, l1_ln_g, l1_ln_b, l2_w_in, l2_q_norm, l2_w_uq, l2_kv_norm, l2_w_ukv, l2_w_out, l2_ln_g, l2_ln_b, l3_w_in, l3_w_out, l3_ln_g, l3_ln_b, loss_target, m_meta_tokens, m_l0_w_in, m_l0_conv_w, m_l0_conv_b, m_l0_w_a, m_l0_b_a, m_l0_w_x, m_l0_b_x, m_l0_lam, m_l0_w_out, m_l0_ln_g, m_l0_ln_b, m_l1_w_in, m_l1_w_grp, m_l1_scale, m_l1_w_out, m_l1_ln_g, m_l1_ln_b, m_l2_w_in, m_l2_q_norm, m_l2_w_uq, m_l2_kv_norm, m_l2_w_ukv, m_l2_w_out, m_l2_ln_g, m_l2_ln_b, m_l3_w_in, m_l3_w_out, m_l3_ln_g, m_l3_ln_b, v_meta_tokens, v_l0_w_in, v_l0_conv_w, v_l0_conv_b, v_l0_w_a, v_l0_b_a, v_l0_w_x, v_l0_b_x, v_l0_lam, v_l0_w_out, v_l0_ln_g, v_l0_ln_b, v_l1_w_in, v_l1_w_grp, v_l1_scale, v_l1_w_out, v_l1_ln_g, v_l1_ln_b, v_l2_w_in, v_l2_q_norm, v_l2_w_uq, v_l2_kv_norm, v_l2_w_ukv, v_l2_w_out, v_l2_ln_g, v_l2_ln_b, v_l3_w_in, v_l3_w_out, v_l3_ln_g, v_l3_ln_b):
    return _step(REAL, dict(locals()))
```

```python
import functools
import math
from typing import NamedTuple

import jax
import jax.numpy as jnp
import numpy as np
from jax import lax
from jax.experimental import pallas as pl
from jax.experimental.pallas import tpu as pltpu

F32 = jnp.float32
BF16 = jnp.bfloat16

LN_EPS = 1e-5
RMS_EPS = 1e-6
ROPE_BASE = 10000.0
LRU_C = 8.0
CONV_W = 4
HIST = 16
ADAM_LR, ADAM_B1, ADAM_B2, ADAM_EPS, ADAM_WD, ADAM_STEP = 0.001, 0.9, 0.999, 1e-08, 0.01, 10
VMEM_LIMIT_V7X = 56 * 1024 * 1024
MESH = pl.DeviceIdType.MESH


class Cfg(NamedTuple):
    d: int
    seq: int
    n_meta: int
    tp: int
    depth: int
    lru_heads: int
    pool_windows: tuple
    mla_heads: int
    nope: int
    rope: int
    q_lora: int
    kv_lora: int
    cw: int
    ret_heads: int
    chunk: int
    tq: int
    tr: int
    tt: int
    tc: int
    tm: int
    tn: int
    la: int


REAL = Cfg(d=4096, seq=4096, n_meta=16, tp=4224, depth=4, lru_heads=16, pool_windows=(2, 4, 8, 16),
           mla_heads=32, nope=128, rope=64, q_lora=1024, kv_lora=512, cw=2048, ret_heads=16, chunk=128,
           tq=384, tr=128, tt=1056, tc=512, tm=1408, tn=1024, la=128)


def _div_tile(n, target, align):
    best = None
    for t in range(align, min(n, target) + 1, align):
        if n % t == 0:
            best = t
    return best or n


def _cparams(*sem):
    return pltpu.CompilerParams(dimension_semantics=sem, vmem_limit_bytes=VMEM_LIMIT_V7X)


def _sig(x):
    return 1.0 / (1.0 + jnp.exp(-x))


def _expm1(x):
    p = x * (1.0 + x * (1.0 / 2) * (1.0 + x * (1.0 / 3) * (1.0 + x * (1.0 / 4) * (1.0 + x * (1.0 / 5) * (1.0 + x * (1.0 / 6))))))
    return jnp.where(x > -0.1, p, jnp.exp(x) - 1.0)


def _colsum8(x):
    r, w = x.shape
    return x.reshape(r // 8, 8, w).sum(axis=0)


def _mm_call(a, b, grid, a_spec, b_spec, o_spec, out_sds, acc_shape, dims, name):
    nk = grid[-1]
    kax = len(grid) - 1

    def body(a_ref, b_ref, o_ref, acc_ref):
        k = pl.program_id(kax)

        @pl.when(k == 0)
        def _():
            acc_ref[...] = jnp.zeros_like(acc_ref)

        acc_ref[...] += lax.dot_general(a_ref[...].astype(BF16), b_ref[...].astype(BF16), dims,
                                        preferred_element_type=F32)

        @pl.when(k == nk - 1)
        def _():
            o_ref[...] = acc_ref[...].astype(o_ref.dtype)

    sem = ("parallel",) * kax + ("arbitrary",)
    return pl.pallas_call(body, grid=grid, in_specs=[a_spec, b_spec], out_specs=o_spec, out_shape=out_sds,
                          scratch_shapes=[pltpu.VMEM(acc_shape, F32)], compiler_params=_cparams(*sem), name=name)(a, b)


_NN = (((1,), (0,)), ((), ()))
_NT = (((1,), (1,)), ((), ()))
_TN = (((0,), (0,)), ((), ()))


def mm_nn(cfg, a, w, name, out_dtype=F32, a_col0=0):
    m = a.shape[0]
    g, k, ns = w.shape
    tm, tn, tk = _div_tile(m, cfg.tm, 16), _div_tile(ns, cfg.tn, cfg.la), _div_tile(k, cfg.tn, cfg.la)
    npb, a0 = ns // tn, a_col0 // tk
    assert a_col0 % tk == 0
    return _mm_call(a, w, (m // tm, g * npb, k // tk),
                    pl.BlockSpec((tm, tk), lambda i, j, kk: (i, a0 + kk)),
                    pl.BlockSpec((None, tk, tn), lambda i, j, kk: (j // npb, kk, j % npb)),
                    pl.BlockSpec((tm, tn), lambda i, j, kk: (i, j)),
                    jax.ShapeDtypeStruct((m, g * ns), out_dtype), (tm, tn), _NN, name)


def mm_nt(cfg, dy, w, name, out_dtype=F32):
    m = dy.shape[0]
    g, k, ns = w.shape
    tm, tn, tk = _div_tile(m, cfg.tm, 16), _div_tile(k, cfg.tn, cfg.la), _div_tile(ns, cfg.tn, cfg.la)
    kpb = ns // tk
    return _mm_call(dy, w, (m // tm, k // tn, g * kpb),
                    pl.BlockSpec((tm, tk), lambda i, j, kk: (i, kk)),
                    pl.BlockSpec((None, tn, tk), lambda i, j, kk: (kk // kpb, j, kk % kpb)),
                    pl.BlockSpec((tm, tn), lambda i, j, kk: (i, j)),
                    jax.ShapeDtypeStruct((m, k), out_dtype), (tm, tn), _NT, name)


def mm_tn(cfg, x, dy, g, name, x_col0=0, k=None):
    m = x.shape[0]
    k = x.shape[1] if k is None else k
    ns = dy.shape[1] // g
    tmo, tn, tk = _div_tile(k, cfg.tn, cfg.la), _div_tile(ns, cfg.tn, cfg.la), _div_tile(m, cfg.tm, 16)
    npb, x0 = ns // tn, x_col0 // tmo
    assert x_col0 % tmo == 0
    return _mm_call(x, dy, (k // tmo, g * npb, m // tk),
                    pl.BlockSpec((tk, tmo), lambda i, j, kk: (kk, x0 + i)),
                    pl.BlockSpec((tk, tn), lambda i, j, kk: (kk, j)),
                    pl.BlockSpec((None, tmo, tn), lambda i, j, kk: (j // npb, i, j % npb)),
                    jax.ShapeDtypeStruct((g, k, ns), F32), (tmo, tn), _TN, name)


def bd_nn(cfg, x, w, name, out_dtype=F32):
    m = x.shape[0]
    g, kg, ng = w.shape
    tm, tn = _div_tile(m, cfg.tm, 16), _div_tile(ng, cfg.tn, cfg.la)
    npb = ng // tn
    return _mm_call(x, w, (m // tm, g * npb, 1),
                    pl.BlockSpec((tm, kg), lambda i, j, kk: (i, j // npb)),
                    pl.BlockSpec((None, kg, tn), lambda i, j, kk: (j // npb, 0, j % npb)),
                    pl.BlockSpec((tm, tn), lambda i, j, kk: (i, j)),
                    jax.ShapeDtypeStruct((m, g * ng), out_dtype), (tm, tn), _NN, name)


def bd_nt(cfg, dy, w, name, out_dtype=F32):
    m = dy.shape[0]
    g, kg, ng = w.shape
    tm, tn = _div_tile(m, cfg.tm, 16), _div_tile(kg, cfg.tn, cfg.la)
    npb = kg // tn
    return _mm_call(dy, w, (m // tm, g * npb, 1),
                    pl.BlockSpec((tm, ng), lambda i, j, kk: (i, j // npb)),
                    pl.BlockSpec((None, tn, ng), lambda i, j, kk: (j // npb, j % npb, 0)),
                    pl.BlockSpec((tm, tn), lambda i, j, kk: (i, j)),
                    jax.ShapeDtypeStruct((m, g * kg), out_dtype), (tm, tn), _NT, name)


def bd_tn(cfg, x, dy, g, name):
    m = x.shape[0]
    kg, ng = x.shape[1] // g, dy.shape[1] // g
    tmo, tn, tk = _div_tile(kg, cfg.tn, cfg.la), _div_tile(ng, cfg.tn, cfg.la), _div_tile(m, cfg.tm, 16)
    mpb, npb = kg // tmo, ng // tn
    return _mm_call(x, dy, (g * mpb, npb, m // tk),
                    pl.BlockSpec((tk, tmo), lambda i, j, kk: (kk, i)),
                    pl.BlockSpec((tk, tn), lambda i, j, kk: (kk, (i // mpb) * npb + j)),
                    pl.BlockSpec((None, tmo, tn), lambda i, j, kk: (i // mpb, i % mpb, j)),
                    jax.ShapeDtypeStruct((g, kg, ng), F32), (tmo, tn), _TN, name)


def _rowwise(cfg, body, ins, outs, name, n_acc=0):
    tr = cfg.tr
    n = cfg.tp // tr
    in_specs, args = [], []
    for spec in ins:
        arr = spec[0]
        if spec[1] is None:
            in_specs.append(pl.BlockSpec(arr.shape, lambda i, nd=arr.ndim: (0,) * nd))
        else:
            in_specs.append(pl.BlockSpec((tr, spec[1]), lambda i, cb=spec[2]: (i, cb)))
        args.append(arr)
    out_specs, out_shape = [], []
    for o in outs:
        if len(o) == 2:
            out_specs.append(pl.BlockSpec((tr, o[0]), lambda i: (i, 0)))
            out_shape.append(jax.ShapeDtypeStruct((cfg.tp, o[0]), o[1]))
        else:
            out_specs.append(pl.BlockSpec((8, o[0]), lambda i: (0, 0)))
            out_shape.append(jax.ShapeDtypeStruct((8, o[0]), F32))
    return pl.pallas_call(body, grid=(n,), in_specs=in_specs, out_specs=out_specs, out_shape=out_shape,
                          compiler_params=_cparams("arbitrary" if n_acc else "parallel"), name=name)(*args)


def ln_fwd(cfg, h, o, g, b, alpha, name):
    d = cfg.d

    def body(h_ref, o_ref, g_ref, b_ref, y_ref, yb_ref, xh_ref, rs_ref):
        z = alpha * h_ref[...] + o_ref[...]
        mu = jnp.mean(z, axis=-1, keepdims=True)
        zc = z - mu
        var = jnp.mean(zc * zc, axis=-1, keepdims=True)
        rstd = lax.rsqrt(var + LN_EPS)
        xh = zc * rstd
        y = xh * g_ref[...] + b_ref[...]
        y_ref[...] = y
        yb_ref[...] = y.astype(BF16)
        xh_ref[...] = xh
        rs_ref[...] = rstd

    return _rowwise(cfg, body, [(h, d, 0), (o, d, 0), (g, None), (b, None)], [(d, F32), (d, BF16), (d, F32), (1, F32)], name)


def ln_bwd(cfg, d_res, d_mm, xhat, rstd, g, alpha, name):
    d = cfg.d
    two = d_res is not None

    def body(*refs):
        if two:
            dr_ref, dm_ref, xh_ref, rs_ref, g_ref, dz_ref, dzb_ref, dg_ref, db_ref = refs
            dy = alpha * dr_ref[...] + dm_ref[...]
        else:
            dm_ref, xh_ref, rs_ref, g_ref, dz_ref, dzb_ref, dg_ref, db_ref = refs
            dy = dm_ref[...]
        xh = xh_ref[...]

        @pl.when(pl.program_id(0) == 0)
        def _():
            dg_ref[...] = jnp.zeros_like(dg_ref)
            db_ref[...] = jnp.zeros_like(db_ref)

        dg_ref[...] += _colsum8(dy * xh)
        db_ref[...] += _colsum8(dy)
        dxh = dy * g_ref[...]
        m1 = jnp.mean(dxh, axis=-1, keepdims=True)
        m2 = jnp.mean(dxh * xh, axis=-1, keepdims=True)
        dz = rs_ref[...] * (dxh - m1 - xh * m2)
        dz_ref[...] = dz
        dzb_ref[...] = dz.astype(BF16)

    ins = ([(d_res, d, 0)] if two else []) + [(d_mm, d, 0), (xhat, d, 0), (rstd, 1, 0), (g, None)]
    return _rowwise(cfg, body, ins, [(d, F32), (d, BF16), (d,), (d,)], name, n_acc=2)


def loss_grad(cfg, y, tgt, name):
    d, tr = cfg.d, cfg.tr
    lo, hi = cfg.n_meta, cfg.n_meta + cfg.seq

    def body(y_ref, t_ref, dy_ref, acc_ref):
        i = pl.program_id(0)

        @pl.when(i == 0)
        def _():
            acc_ref[...] = jnp.zeros_like(acc_ref)

        row = i * tr + lax.broadcasted_iota(jnp.int32, (tr, 1), 0)
        err = jnp.where((row >= lo) & (row < hi), y_ref[...] - t_ref[...], 0.0)
        dy_ref[...] = err * (1.0 / d)
        acc_ref[...] += _colsum8(err * err)

    return _rowwise(cfg, body, [(y, d, 0), (tgt, d, 0)], [(d, F32), (d,)], name, n_acc=1)


def axpy(cfg, a, b, alpha, name):
    d = cfg.d

    def body(a_ref, b_ref, o_ref):
        o_ref[...] = alpha * a_ref[...] + b_ref[...]

    return _rowwise(cfg, body, [(a, d, 0), (b, d, 0)], [(d, F32)], name)[0]


def _time_call(cfg, body, ins, outs, accs, scratch, name, reverse=False, groups=1):
    d = cfg.d
    tc = _div_tile(d // groups, cfg.tc, cfg.la)
    tt = _div_tile(cfg.tp, cfg.tt, 16)
    nc, nt = d // tc, cfg.tp // tt
    tmap = (lambda t: nt - 1 - t) if reverse else (lambda t: t)
    in_specs, args = [], []
    for spec in ins:
        arr = spec[0]
        if len(spec) == 2:
            in_specs.append(pl.BlockSpec((tt, tc), lambda c, t, off=spec[1] * nc: (tmap(t), off + c)))
        else:
            in_specs.append(pl.BlockSpec((arr.shape[0], tc), lambda c, t: (0, c)))
        args.append(arr)
    out_specs = [pl.BlockSpec((tt, tc), lambda c, t: (tmap(t), c)) for _ in outs]
    out_shape = [jax.ShapeDtypeStruct((cfg.tp, d), dt) for dt in outs]
    for _ in range(accs):
        out_specs.append(pl.BlockSpec((8, tc), lambda c, t: (0, c)))
        out_shape.append(jax.ShapeDtypeStruct((8, d), F32))
    return pl.pallas_call(functools.partial(body, tt=tt, tc=tc, nt=nt), grid=(nc, nt), in_specs=in_specs,
                          out_specs=out_specs, out_shape=out_shape,
                          scratch_shapes=[pltpu.VMEM(s, F32) for s in scratch(tt, tc)],
                          compiler_params=_cparams("parallel", "arbitrary"), name=name)(*args)


def _push_history(s_ref, new, t, tt):
    @pl.when(t == 0)
    def _():
        s_ref[pl.ds(0, HIST), :] = jnp.zeros((HIST, s_ref.shape[1]), F32)

    @pl.when(t > 0)
    def _():
        s_ref[pl.ds(0, HIST), :] = s_ref[pl.ds(tt, HIST), :]

    s_ref[pl.ds(HIST, tt), :] = new


def _push_future(s_ref, new, t, tt):
    @pl.when(t == 0)
    def _():
        s_ref[pl.ds(tt, HIST), :] = jnp.zeros((HIST, s_ref.shape[1]), F32)

    @pl.when(t > 0)
    def _():
        s_ref[pl.ds(tt, HIST), :] = s_ref[pl.ds(0, HIST), :]

    s_ref[pl.ds(0, tt), :] = new


def conv_fwd(cfg, ug, w4, cb, name):
    def body(x_ref, w_ref, b_ref, u_ref, ub_ref, s_ref, *, tt, tc, nt):
        t = pl.program_id(1)
        _push_history(s_ref, x_ref[...], t, tt)
        acc = b_ref[...] + w_ref[pl.ds(0, 1), :] * s_ref[pl.ds(HIST - 3, tt), :]
        for k in range(1, CONV_W):
            acc = acc + w_ref[pl.ds(k, 1), :] * s_ref[pl.ds(HIST - 3 + k, tt), :]
        u_ref[...] = acc
        ub_ref[...] = acc.astype(BF16)

    return _time_call(cfg, body, [(ug, 0), (w4,), (cb,)], [F32, BF16], 0, lambda tt, tc: [(tt + HIST, tc)], name)


def conv_bwd(cfg, du_a, du_b, du_c, ug, w4, name):
    def body(a_ref, b_ref, c_ref, x_ref, w_ref, dx_ref, acc_ref, s_ref, *, tt, tc, nt):
        t = pl.program_id(1)

        @pl.when(t == 0)
        def _():
            acc_ref[...] = jnp.zeros_like(acc_ref)

        du = a_ref[...] + b_ref[...] + c_ref[...]
        _push_future(s_ref, du, t, tt)
        x = x_ref[...]
        acc_ref[pl.ds(4, 1), :] += jnp.sum(du, axis=0, keepdims=True)
        dx = None
        for k in range(CONV_W):
            sh = s_ref[pl.ds(3 - k, tt), :]
            term = w_ref[pl.ds(k, 1), :] * sh
            dx = term if dx is None else dx + term
            acc_ref[pl.ds(k, 1), :] += jnp.sum(x * sh, axis=0, keepdims=True)
        dx_ref[...] = dx.astype(BF16)

    return _time_call(cfg, body, [(du_a, 0), (du_b, 0), (du_c, 0), (ug, 0), (w4,)], [BF16], 1,
                      lambda tt, tc: [(tt + HIST, tc)], name, reverse=True)


def _scan8(a, b, rows):
    for k in (1, 2, 4):
        ar = jnp.where(rows >= k, pltpu.roll(a, k, 0), 1.0)
        br = jnp.where(rows >= k, pltpu.roll(b, k, 0), 0.0)
        b = a * br + b
        a = a * ar
    return a, b


def _rscan8(a, b, rows):
    for k in (1, 2, 4):
        ar = jnp.where(rows < 8 - k, pltpu.roll(a, 8 - k, 0), 1.0)
        br = jnp.where(rows < 8 - k, pltpu.roll(b, 8 - k, 0), 0.0)
        b = b + a * br
        a = a * ar
    return a, b


def _lru_gates(u, pa, px, ba, bx, c_lam):
    r = _sig(pa + ba)
    i = _sig(px + bx)
    la = LRU_C * r * c_lam
    em = _expm1(2.0 * la)
    return r, i, jnp.exp(la), em, jnp.sqrt(-em)


def _neg_softplus_neg(lam):
    e = jnp.exp(-jnp.abs(lam))
    u = 1.0 + e
    l1p = jnp.where(u == 1.0, e, jnp.log(u) * (e / jnp.where(u == 1.0, 1.0, u - 1.0)))
    return -(jnp.maximum(-lam, 0.0) + l1p)


def lru_fwd(cfg, u, pa, px, ug, ba, bx, lam, name):
    def body(u_ref, pa_ref, px_ref, g_ref, ba_ref, bx_ref, lam_ref, hs_ref, hp_ref, y_ref, car_ref, *, tt, tc, nt):
        @pl.when(pl.program_id(1) == 0)
        def _():
            car_ref[...] = jnp.zeros_like(car_ref)

        rows = lax.broadcasted_iota(jnp.int32, (8, tc), 0)
        ba, bx = ba_ref[...], bx_ref[...]
        c_lam = _neg_softplus_neg(lam_ref[...])

        def step(it, h):
            ys = []
            for half in range(2):
                rs = pl.ds(pl.multiple_of(it * 16 + half * 8, 8), 8)
                uu = u_ref[rs, :]
                r, i, a, em, s = _lru_gates(uu, pa_ref[rs, :], px_ref[rs, :], ba, bx, c_lam)
                acum, b = _scan8(a, uu * i * s, rows)
                hs = acum * h + b
                hs_ref[rs, :] = hs
                hp_ref[rs, :] = jnp.where(rows >= 1, pltpu.roll(hs, 1, 0), h)
                g = g_ref[rs, :]
                ys.append(hs * (g * _sig(g)))
                h = jnp.broadcast_to(hs[7:8, :], (8, tc))
            y_ref[pl.ds(pl.multiple_of(it * 16, 16), 16), :] = jnp.concatenate(ys, axis=0).astype(BF16)
            return h

        car_ref[...] = lax.fori_loop(0, tt // 16, step, car_ref[...])

    return _time_call(cfg, body, [(u, 0), (pa, 0), (px, 0), (ug, 1), (ba,), (bx,), (lam,)], [F32, F32, BF16], 0,
                      lambda tt, tc: [(8, tc)], name)


def lru_bwd(cfg, dy, ug, hs, hprev, u, pa, px, ba, bx, lam, name):
    def body(dy_ref, g_ref, hs_ref, hp_ref, u_ref, pa_ref, px_ref, ba_ref, bx_ref, lam_ref,
             dg_ref, dpa_ref, dpx_ref, du_ref, acc_ref, ecar_ref, acar_ref, *, tt, tc, nt):
        @pl.when(pl.program_id(1) == 0)
        def _():
            ecar_ref[...] = jnp.zeros_like(ecar_ref)
            acar_ref[...] = jnp.zeros_like(acar_ref)
            acc_ref[...] = jnp.zeros_like(acc_ref)

        rows = lax.broadcasted_iota(jnp.int32, (8, tc), 0)
        ba, bx = ba_ref[...], bx_ref[...]
        c_lam = _neg_softplus_neg(lam_ref[...])

        def step(it, carry):
            ecar, acar, s_lam, s_ba, s_bx = carry
            jt = tt // 16 - 1 - it
            dgs, dpas, dpxs = [None, None], [None, None], [None, None]
            for half in (1, 0):
                rs = pl.ds(pl.multiple_of(jt * 16 + half * 8, 8), 8)
                uu = u_ref[rs, :]
                r, i, a, em, s = _lru_gates(uu, pa_ref[rs, :], px_ref[rs, :], ba, bx, c_lam)
                g = g_ref[rs, :]
                sg = _sig(g)
                dy = dy_ref[rs, :]
                dgs[half] = dy * hs_ref[rs, :] * (sg * (1.0 + g * (1.0 - sg)))
                a_next = jnp.where(rows < 7, pltpu.roll(a, 7, 0), acar)
                acum, e = _rscan8(a_next, dy * (g * sg), rows)
                e = e + acum * ecar
                ecar = jnp.broadcast_to(e[0:1, :], (8, tc))
                acar = jnp.broadcast_to(a[0:1, :], (8, tc))
                du_ref[rs, :] = e * i * s
                d_la = e * hp_ref[rs, :] * a - (e * uu * i) * ((em + 1.0) / s)
                s_lam = s_lam + d_la * (LRU_C * r)
                d_pa = d_la * (LRU_C * c_lam) * r * (1.0 - r)
                d_px = e * uu * s * i * (1.0 - i)
                s_ba = s_ba + d_pa
                s_bx = s_bx + d_px
                dpas[half], dpxs[half] = d_pa, d_px
            r16 = pl.ds(pl.multiple_of(jt * 16, 16), 16)
            dg_ref[r16, :] = jnp.concatenate(dgs, axis=0).astype(BF16)
            dpa_ref[r16, :] = jnp.concatenate(dpas, axis=0).astype(BF16)
            dpx_ref[r16, :] = jnp.concatenate(dpxs, axis=0).astype(BF16)
            return ecar, acar, s_lam, s_ba, s_bx

        z = jnp.zeros((8, tc), F32)
        ecar, acar, s_lam, s_ba, s_bx = lax.fori_loop(0, tt // 16, step, (ecar_ref[...], acar_ref[...], z, z, z))
        ecar_ref[...] = ecar
        acar_ref[...] = acar
        acc_ref[pl.ds(0, 1), :] += jnp.sum(s_lam, axis=0, keepdims=True) * _sig(-lam_ref[...])
        acc_ref[pl.ds(1, 1), :] += jnp.sum(s_ba, axis=0, keepdims=True)
        acc_ref[pl.ds(2, 1), :] += jnp.sum(s_bx, axis=0, keepdims=True)

    return _time_call(cfg, body, [(dy, 0), (ug, 1), (hs, 0), (hprev, 0), (u, 0), (pa, 0), (px, 0), (ba,), (bx,), (lam,)],
                      [BF16, BF16, BF16, F32], 1, lambda tt, tc: [(8, tc), (8, tc)], name, reverse=True)


def _pool_select(cfg, grp, fn):
    for gi, w in enumerate(cfg.pool_windows):
        @pl.when(grp == gi)
        def _(w=w):
            fn(w)


def pool_fwd(cfg, ug, name):
    ng = len(cfg.pool_windows)

    def body(x_ref, p_ref, s_ref, *, tt, tc, nt):
        t = pl.program_id(1)
        grp = pl.program_id(0) // (cfg.d // ng // tc)
        x = x_ref[...]
        _push_history(s_ref, x, t, tt)
        row1 = (t * tt + 1 + lax.broadcasted_iota(jnp.int32, (tt, 1), 0)).astype(F32)

        def write(w):
            ws = x
            for j in range(1, w):
                ws = ws + s_ref[pl.ds(HIST - j, tt), :]
            p_ref[...] = (ws / jnp.minimum(row1, float(w)) - x).astype(BF16)

        _pool_select(cfg, grp, write)

    return _time_call(cfg, body, [(ug, 0)], [BF16], 0, lambda tt, tc: [(tt + HIST, tc)], name, groups=ng)[0]


def pool_bwd(cfg, dp, name):
    ng = len(cfg.pool_windows)

    def body(dp_ref, du_ref, s_ref, *, tt, tc, nt):
        t = pl.program_id(1)
        grp = pl.program_id(0) // (cfg.d // ng // tc)
        dp = dp_ref[...]
        row1 = ((nt - 1 - t) * tt + 1 + lax.broadcasted_iota(jnp.int32, (tt, 1), 0)).astype(F32)

        def write(w):
            dm = dp / jnp.minimum(row1, float(w))
            _push_future(s_ref, dm, t, tt)
            ws = dm
            for j in range(1, w):
                ws = ws + s_ref[pl.ds(j, tt), :]
            du_ref[...] = (ws - dp).astype(BF16)

        _pool_select(cfg, grp, write)

    return _time_call(cfg, body, [(dp, 0)], [BF16], 0, lambda tt, tc: [(tt + HIST, tc)], name, reverse=True, groups=ng)[0]


def gate_fwd(cfg, v, gsrc, gblk, scale, name):
    d = cfg.d

    def body(*refs):
        if scale is None:
            v_ref, g_ref, y_ref = refs
            v = v_ref[...]
        else:
            v_ref, g_ref, s_ref, y_ref = refs
            v = v_ref[...] * s_ref[...]
        g = g_ref[...]
        y_ref[...] = (v * (g * _sig(g))).astype(BF16)

    ins = [(v, d, 0), (gsrc, d, gblk)] + ([] if scale is None else [(scale, None)])
    return _rowwise(cfg, body, ins, [(d, BF16)], name)[0]


def gate_bwd(cfg, dy, v, gsrc, gblk, scale, dv_dtype, name):
    d = cfg.d

    def body(*refs):
        if scale is None:
            dy_ref, v_ref, g_ref, dv_ref, dg_ref = refs
            vs = v_ref[...]
        else:
            dy_ref, v_ref, g_ref, s_ref, dv_ref, dg_ref, acc_ref = refs
            vs = v_ref[...] * s_ref[...]
        g = g_ref[...]
        sg = _sig(g)
        dy = dy_ref[...]
        dvs = dy * (g * sg)
        dg_ref[...] = (dy * vs * (sg * (1.0 + g * (1.0 - sg)))).astype(BF16)
        if scale is None:
            dv_ref[...] = dvs.astype(dv_dtype)
        else:
            @pl.when(pl.program_id(0) == 0)
            def _():
                acc_ref[...] = jnp.zeros_like(acc_ref)

            acc_ref[...] += _colsum8(dvs * v_ref[...])
            dv_ref[...] = (dvs * s_ref[...]).astype(dv_dtype)

    ins = [(dy, d, 0), (v, d, 0), (gsrc, d, gblk)] + ([] if scale is None else [(scale, None)])
    outs = [(d, dv_dtype), (d, BF16)] + ([] if scale is None else [(d,)])
    return _rowwise(cfg, body, ins, outs, name, n_acc=0 if scale is None else 1)


def _swap_halves(x, lo, half):
    w = x.shape[1]
    lane = lax.broadcasted_iota(jnp.int32, x.shape, 1)
    sw = jnp.where(lane < lo + half, pltpu.roll(x, w - half, 1), pltpu.roll(x, half, 1))
    return jnp.where((lane >= lo) & (lane < lo + 2 * half), sw, 0.0)


def mla_norm(cfg, proj, qn, kvn, name):
    ql, kl = cfg.q_lora, cfg.kv_lora

    def body(c_ref, qn_ref, kn_ref, q_ref, k_ref):
        cq = c_ref[:, 0:ql]
        ck = c_ref[:, ql:ql + kl]
        q_ref[...] = (cq * lax.rsqrt(jnp.mean(cq * cq, axis=-1, keepdims=True) + RMS_EPS) * qn_ref[...]).astype(BF16)
        k_ref[...] = (ck * lax.rsqrt(jnp.mean(ck * ck, axis=-1, keepdims=True) + RMS_EPS) * kn_ref[...]).astype(BF16)

    return _rowwise(cfg, body, [(proj, cfg.cw, cfg.d // cfg.cw), (qn, None), (kvn, None)], [(ql, BF16), (kl, BF16)], name)


def mla_norm_bwd(cfg, proj, d_cqn, d_ckvn, d_kr, qn, kvn, name):
    ql, kl, cw, npe = cfg.q_lora, cfg.kv_lora, cfg.cw, cfg.nope

    def one(x, dy, gamma):
        rstd = lax.rsqrt(jnp.mean(x * x, axis=-1, keepdims=True) + RMS_EPS)
        xn = x * rstd
        dxn = dy * gamma
        return rstd * (dxn - xn * jnp.mean(dxn * xn, axis=-1, keepdims=True)), dy * xn

    def body(c_ref, dq_ref, dk_ref, dkr_ref, qn_ref, kn_ref, dc_ref, acc_ref):
        @pl.when(pl.program_id(0) == 0)
        def _():
            acc_ref[...] = jnp.zeros_like(acc_ref)

        dcq, gq = one(c_ref[:, 0:ql], dq_ref[...], qn_ref[...])
        dck, gk = one(c_ref[:, ql:ql + kl], dk_ref[...], kn_ref[...])
        dc_ref[:, 0:ql] = dcq.astype(BF16)
        dc_ref[:, ql:ql + kl] = dck.astype(BF16)
        dc_ref[:, ql + kl:ql + kl + npe] = dkr_ref[...].astype(BF16)
        rest = cw - (ql + kl + npe)
        if rest:
            dc_ref[:, ql + kl + npe:cw] = jnp.zeros((dc_ref.shape[0], rest), BF16)
        acc_ref[:, 0:ql] += _colsum8(gq)
        acc_ref[:, ql:ql + kl] += _colsum8(gk)

    return _rowwise(cfg, body, [(proj, cw, cfg.d // cw), (d_cqn, ql, 0), (d_ckvn, kl, 0), (d_kr, npe, 0), (qn, None), (kvn, None)],
                    [(cw, BF16), (cw,)], name, n_acc=1)


def _mla_tables(cfg):
    r2, npe = cfg.rope // 2, cfg.nope
    inv = ROPE_BASE ** (-jnp.arange(0, cfg.rope, 2, dtype=F32) / cfg.rope)
    ang = jnp.arange(cfg.tp, dtype=F32)[:, None] * inv[None, :]
    cos, sin = jnp.cos(ang), jnp.sin(ang)
    one = jnp.ones((cfg.tp, npe - cfg.rope), F32)
    ck = jnp.concatenate([cos, cos, one], axis=1)
    sk = jnp.concatenate([-sin, sin, 0.0 * one], axis=1)
    cq = jnp.concatenate([jnp.ones((cfg.tp, npe), F32), ck], axis=1)
    sq = jnp.concatenate([jnp.zeros((cfg.tp, npe), F32), sk], axis=1)
    return cq, sq, ck, sk


def mla_prep(cfg, q_raw, kv_raw, proj, tabs, name):
    h, npe, r2, ql, kl = cfg.mla_heads, cfg.nope, cfg.rope // 2, cfg.q_lora, cfg.kv_lora
    hp = 2 * npe

    def body(q_ref, kv_ref, c_ref, cq_ref, sq_ref, ck_ref, sk_ref, qo_ref, ko_ref, vo_ref):
        kr = c_ref[:, ql + kl:ql + kl + npe]
        kr = (kr * ck_ref[...] + _swap_halves(kr, 0, r2) * sk_ref[...]).astype(BF16)
        cq, sq = cq_ref[...], sq_ref[...]
        for i in range(h):
            q = q_ref[:, i * hp:(i + 1) * hp]
            qo_ref[:, i * hp:(i + 1) * hp] = (q * cq + _swap_halves(q, npe, r2) * sq).astype(BF16)
            ko_ref[:, i * hp:i * hp + npe] = kv_ref[:, i * npe:(i + 1) * npe].astype(BF16)
            ko_ref[:, i * hp + npe:(i + 1) * hp] = kr
        vo_ref[...] = kv_ref[:, h * npe:2 * h * npe].astype(BF16)

    cq, sq, ck, sk = tabs
    return _rowwise(cfg, body, [(q_raw, h * hp, 0), (kv_raw, 2 * h * npe, 0), (proj, cfg.cw, cfg.d // cfg.cw),
                                (cq, hp, 0), (sq, hp, 0), (ck, npe, 0), (sk, npe, 0)],
                    [(h * hp, BF16), (h * hp, BF16), (h * npe, BF16)], name)


def mla_prep_bwd(cfg, dq_full, dk_full, tabs, name):
    h, npe, r2 = cfg.mla_heads, cfg.nope, cfg.rope // 2
    hp = 2 * npe

    def body(dq_ref, dk_ref, cq_ref, sq_ref, ck_ref, sk_ref, dqo_ref, dko_ref, dkr_ref):
        cq, sq = cq_ref[...], sq_ref[...]
        dkr = None
        for i in range(h):
            dq = dq_ref[:, i * hp:(i + 1) * hp]
            dqo_ref[:, i * hp:(i + 1) * hp] = (dq * cq + _swap_halves(dq * sq, npe, r2)).astype(BF16)
            dko_ref[:, i * npe:(i + 1) * npe] = dk_ref[:, i * hp:i * hp + npe].astype(BF16)
            part = dk_ref[:, i * hp + npe:(i + 1) * hp]
            dkr = part if dkr is None else dkr + part
        dkr_ref[...] = dkr * ck_ref[...] + _swap_halves(dkr * sk_ref[...], 0, r2)

    cq, sq, ck, sk = tabs
    return _rowwise(cfg, body, [(dq_full, h * hp, 0), (dk_full, h * hp, 0), (cq, hp, 0), (sq, hp, 0), (ck, npe, 0), (sk, npe, 0)],
                    [(h * hp, BF16), (h * npe, BF16), (npe, F32)], name)


def _attn_scores(cfg, q, k, diagonal):
    s = lax.dot_general(q, k, _NT, preferred_element_type=F32) * ((cfg.nope + cfg.rope) ** -0.5)
    if diagonal:
        row = lax.broadcasted_iota(jnp.int32, s.shape, 0)
        col = lax.broadcasted_iota(jnp.int32, s.shape, 1)
        s = jnp.where(row >= col, s, -1e30)
    return s


def _tile_rows(i, t):
    return pl.ds(pl.multiple_of(i * t, t), t)


def attn_fwd(cfg, q, k, v, name):
    h, npe, tq = cfg.mla_heads, cfg.nope, cfg.tq
    hp, n = 2 * npe, cfg.tp // tq

    def body(q_ref, k_ref, v_ref, o_ref, lse_ref):
        qi = pl.program_id(1)
        q = q_ref[...]

        def tile(ki, carry, diagonal):
            m, l, acc = carry
            s = _attn_scores(cfg, q, k_ref[_tile_rows(ki, tq), :], diagonal)
            m_new = jnp.maximum(m, jnp.max(s, axis=-1, keepdims=True))
            alpha = jnp.exp(m - m_new)
            p = jnp.exp(s - m_new)
            l = alpha * l + jnp.sum(p, axis=-1, keepdims=True)
            acc = alpha * acc + lax.dot_general(p.astype(BF16), v_ref[_tile_rows(ki, tq), :], _NN, preferred_element_type=F32)
            return m_new, l, acc

        init = (jnp.full((tq, 1), -1e30, F32), jnp.zeros((tq, 1), F32), jnp.zeros((tq, npe), F32))
        carry = lax.fori_loop(0, qi, lambda ki, c: tile(ki, c, False), init)
        m, l, acc = tile(qi, carry, True)
        o_ref[...] = acc / l
        lse_ref[...] = m + jnp.log(l)

    return pl.pallas_call(
        body, grid=(h, n),
        in_specs=[pl.BlockSpec((tq, hp), lambda hh, qi: (qi, hh)),
                  pl.BlockSpec((cfg.tp, hp), lambda hh, qi: (0, hh)),
                  pl.BlockSpec((cfg.tp, npe), lambda hh, qi: (0, hh))],
        out_specs=[pl.BlockSpec((tq, npe), lambda hh, qi: (qi, hh)),
                   pl.BlockSpec((None, tq, 1), lambda hh, qi: (hh, qi, 0))],
        out_shape=[jax.ShapeDtypeStruct((cfg.tp, h * npe), F32), jax.ShapeDtypeStruct((h, cfg.tp, 1), F32)],
        compiler_params=_cparams("parallel", "arbitrary"), name=name)(q, k, v)


def attn_bwd(cfg, q, k, v, do, o, lse, name):
    h, npe, tq = cfg.mla_heads, cfg.nope, cfg.tq
    hp, n = 2 * npe, cfg.tp // tq
    sc = (cfg.nope + cfg.rope) ** -0.5

    def body(q_ref, k_ref, v_ref, do_ref, o_ref, lse_ref, dq_ref, dk_ref, dv_ref, dv_acc):
        dk_ref[...] = jnp.zeros_like(dk_ref)
        dv_acc[...] = jnp.zeros_like(dv_acc)

        def q_tile(qi, _):
            rq = _tile_rows(qi, tq)
            q, do, lse = q_ref[rq, :], do_ref[rq, :], lse_ref[rq, :]
            delta = jnp.sum(do.astype(F32) * o_ref[rq, :], axis=-1, keepdims=True)

            def tile(ki, dq, diagonal):
                rk = _tile_rows(ki, tq)
                kk = k_ref[rk, :]
                p = jnp.exp(_attn_scores(cfg, q, kk, diagonal) - lse)
                dp = lax.dot_general(do, v_ref[rk, :], _NT, preferred_element_type=F32)
                ds = (p * (dp - delta) * sc).astype(BF16)
                dv_acc[rk, :] += lax.dot_general(p.astype(BF16), do, _TN, preferred_element_type=F32)
                dk_ref[rk, :] += lax.dot_general(ds, q, _TN, preferred_element_type=F32)
                return dq + lax.dot_general(ds, kk, _NN, preferred_element_type=F32)

            dq = lax.fori_loop(0, qi, lambda ki, acc: tile(ki, acc, False), jnp.zeros((tq, hp), F32))
            dq_ref[rq, :] = tile(qi, dq, True)
            return 0

        lax.fori_loop(0, n, q_tile, 0)
        dv_ref[...] = dv_acc[...].astype(BF16)

    wide = pl.BlockSpec((cfg.tp, hp), lambda hh: (0, hh))
    narrow = pl.BlockSpec((cfg.tp, npe), lambda hh: (0, hh))
    return pl.pallas_call(
        body, grid=(h,),
        in_specs=[wide, wide, narrow, narrow, narrow, pl.BlockSpec((None, cfg.tp, 1), lambda hh: (hh, 0, 0))],
        out_specs=[wide, wide, narrow],
        out_shape=[jax.ShapeDtypeStruct((cfg.tp, h * hp), F32), jax.ShapeDtypeStruct((cfg.tp, h * hp), F32),
                   jax.ShapeDtypeStruct((cfg.tp, h * npe), BF16)],
        scratch_shapes=[pltpu.VMEM((cfg.tp, npe), F32)],
        compiler_params=_cparams("parallel"), name=name)(q, k, v, do, o, lse)


def _ret_tables(cfg):
    c = cfg.chunk
    lg = jnp.log(1.0 - 2.0 ** (-5.0 - jnp.arange(cfg.ret_heads, dtype=F32)))[:, None, None]
    j = jnp.arange(c, dtype=F32)
    diff = j[:, None] - j[None, :]
    dm = jnp.where(diff >= 0, jnp.exp(jnp.maximum(diff, 0.0)[None] * lg), 0.0)
    lq = jnp.exp((j + 1.0)[None, :, None] * lg)
    lk = jnp.exp((c - 1.0 - j)[None, :, None] * lg)
    gc = jnp.exp(c * lg)
    return dm, lq, lk, gc


def _ret_rope_tables(cfg):
    dk = cfg.d // cfg.ret_heads
    inv = ROPE_BASE ** (-jnp.arange(0, dk, 2, dtype=F32) / dk)
    ang = jnp.arange(cfg.tp, dtype=F32)[:, None] * inv[None, :]
    return jnp.cos(ang), jnp.sin(ang)


def ret_prep(cfg, qkvg, cos, sin, name):
    d, h = cfg.d, cfg.ret_heads
    dk = d // h
    hd = dk // 2
    ksc = dk ** -0.5

    def body(q_ref, k_ref, v_ref, c_ref, s_ref, qo_ref, ko_ref, vo_ref):
        c, s = c_ref[...], s_ref[...]
        for i in range(h):
            for src, dst, f in ((q_ref, qo_ref, 1.0), (k_ref, ko_ref, ksc)):
                x1 = src[:, i * dk:i * dk + hd]
                x2 = src[:, i * dk + hd:(i + 1) * dk]
                dst[:, i * dk:i * dk + hd] = ((x1 * c - x2 * s) * f).astype(BF16)
                dst[:, i * dk + hd:(i + 1) * dk] = ((x2 * c + x1 * s) * f).astype(BF16)
        vo_ref[...] = v_ref[...].astype(BF16)

    return _rowwise(cfg, body, [(qkvg, d, 0), (qkvg, d, 1), (qkvg, d, 2), (cos, hd, 0), (sin, hd, 0)],
                    [(d, BF16), (d, BF16), (d, BF16)], name)


def ret_prep_bwd(cfg, dq, dk_, dv, dg, cos, sin, name):
    d, h = cfg.d, cfg.ret_heads
    dk = d // h
    hd = dk // 2
    ksc = dk ** -0.5

    def body(dq_ref, dk_ref, dv_ref, dg_ref, c_ref, s_ref, o_ref):
        c, s = c_ref[...], s_ref[...]
        for i in range(h):
            for src, off, f in ((dq_ref, 0, 1.0), (dk_ref, d, ksc)):
                y1 = src[:, i * dk:i * dk + hd]
                y2 = src[:, i * dk + hd:(i + 1) * dk]
                o_ref[:, off + i * dk:off + i * dk + hd] = ((y1 * c + y2 * s) * f).astype(BF16)
                o_ref[:, off + i * dk + hd:off + (i + 1) * dk] = ((y2 * c - y1 * s) * f).astype(BF16)
        o_ref[:, 2 * d:3 * d] = dv_ref[...]
        o_ref[:, 3 * d:4 * d] = dg_ref[...]

    return _rowwise(cfg, body, [(dq, d, 0), (dk_, d, 0), (dv, d, 0), (dg, d, 0), (cos, hd, 0), (sin, hd, 0)],
                    [(4 * d, BF16)], name)[0]


def _ret_specs(cfg, reverse):
    c, h = cfg.chunk, cfg.ret_heads
    dk = cfg.d // h
    n = cfg.tp // c
    tm = (lambda t: n - 1 - t) if reverse else (lambda t: t)
    blk = pl.BlockSpec((c, dk), lambda hh, t: (tm(t), hh))
    row = pl.BlockSpec((None, c, 1), lambda hh, t: (hh, tm(t), 0))
    tabs = [pl.BlockSpec((None, c, c), lambda hh, t: (hh, 0, 0)), pl.BlockSpec((None, c, 1), lambda hh, t: (hh, 0, 0)),
            pl.BlockSpec((None, c, 1), lambda hh, t: (hh, 0, 0)), pl.BlockSpec((None, 1, 1), lambda hh, t: (hh, 0, 0))]
    return blk, row, tabs, n, dk


def _ret_state_update(s_ref, k, v, lk, gc):
    kw = (k.astype(F32) * lk).astype(BF16)
    s_ref[...] = gc * s_ref[...] + lax.dot_general(kw, v, _TN, preferred_element_type=F32)


def ret_fwd(cfg, q, k, v, tabs, name):
    blk, row, tspecs, n, dk = _ret_specs(cfg, False)

    def body(q_ref, k_ref, v_ref, dm_ref, lq_ref, lk_ref, gc_ref, on_ref, rs_ref, s_ref):
        @pl.when(pl.program_id(1) == 0)
        def _():
            s_ref[...] = jnp.zeros_like(s_ref)

        qq, kk, vv = q_ref[...], k_ref[...], v_ref[...]
        a = lax.dot_general(qq, kk, _NT, preferred_element_type=F32) * dm_ref[...]
        o = lax.dot_general(a.astype(BF16), vv, _NN, preferred_element_type=F32)
        o = o + lax.dot_general(qq, s_ref[...].astype(BF16), _NN, preferred_element_type=F32) * lq_ref[...]
        _ret_state_update(s_ref, kk, vv, lk_ref[...], gc_ref[...])
        mu = jnp.mean(o, axis=-1, keepdims=True)
        oc = o - mu
        rstd = lax.rsqrt(jnp.mean(oc * oc, axis=-1, keepdims=True) + LN_EPS)
        on_ref[...] = oc * rstd
        rs_ref[...] = rstd

    return pl.pallas_call(body, grid=(cfg.ret_heads, n), in_specs=[blk, blk, blk] + tspecs, out_specs=[blk, row],
                          out_shape=[jax.ShapeDtypeStruct((cfg.tp, cfg.d), F32), jax.ShapeDtypeStruct((cfg.ret_heads, cfg.tp, 1), F32)],
                          scratch_shapes=[pltpu.VMEM((dk, dk), F32)], compiler_params=_cparams("parallel", "arbitrary"),
                          name=name)(q, k, v, *tabs)


def ret_bwd_q(cfg, q, k, v, d_on, on, rstd, tabs, name):
    blk, row, tspecs, n, dk = _ret_specs(cfg, False)

    def body(q_ref, k_ref, v_ref, don_ref, on_ref, rs_ref, dm_ref, lq_ref, lk_ref, gc_ref, do_ref, dq_ref, s_ref):
        @pl.when(pl.program_id(1) == 0)
        def _():
            s_ref[...] = jnp.zeros_like(s_ref)

        kk, vv = k_ref[...], v_ref[...]
        don, on = don_ref[...], on_ref[...]
        do = rs_ref[...] * (don - jnp.mean(don, axis=-1, keepdims=True) - on * jnp.mean(don * on, axis=-1, keepdims=True))
        dob = do.astype(BF16)
        do_ref[...] = dob
        ds = lax.dot_general(dob, vv, _NT, preferred_element_type=F32) * dm_ref[...]
        dq = lax.dot_general(ds.astype(BF16), kk, _NN, preferred_element_type=F32)
        dq_ref[...] = dq + lax.dot_general(dob, s_ref[...].astype(BF16), _NT, preferred_element_type=F32) * lq_ref[...]
        _ret_state_update(s_ref, kk, vv, lk_ref[...], gc_ref[...])

    return pl.pallas_call(body, grid=(cfg.ret_heads, n), in_specs=[blk, blk, blk, blk, blk, row] + tspecs, out_specs=[blk, blk],
                          out_shape=[jax.ShapeDtypeStruct((cfg.tp, cfg.d), BF16), jax.ShapeDtypeStruct((cfg.tp, cfg.d), F32)],
                          scratch_shapes=[pltpu.VMEM((dk, dk), F32)], compiler_params=_cparams("parallel", "arbitrary"),
                          name=name)(q, k, v, d_on, on, rstd, *tabs)


def ret_bwd_kv(cfg, q, k, v, do, tabs, name):
    blk, row, tspecs, n, dk = _ret_specs(cfg, True)

    def body(q_ref, k_ref, v_ref, do_ref, dm_ref, lq_ref, lk_ref, gc_ref, dk_ref, dv_ref, g_ref):
        @pl.when(pl.program_id(1) == 0)
        def _():
            g_ref[...] = jnp.zeros_like(g_ref)

        qq, kk, vv, dob = q_ref[...], k_ref[...], v_ref[...], do_ref[...]
        dm, lk = dm_ref[...], lk_ref[...]
        gb = g_ref[...].astype(BF16)
        a = lax.dot_general(qq, kk, _NT, preferred_element_type=F32) * dm
        ds = lax.dot_general(dob, vv, _NT, preferred_element_type=F32) * dm
        dkk = lax.dot_general(ds.astype(BF16), qq, _TN, preferred_element_type=F32)
        dk_ref[...] = dkk + lax.dot_general(vv, gb, _NT, preferred_element_type=F32) * lk
        kw = (kk.astype(F32) * lk).astype(BF16)
        dvv = lax.dot_general(a.astype(BF16), dob, _TN, preferred_element_type=F32)
        dv_ref[...] = (dvv + lax.dot_general(kw, gb, _NN, preferred_element_type=F32)).astype(BF16)
        qw = (qq.astype(F32) * lq_ref[...]).astype(BF16)
        g_ref[...] = gc_ref[...] * g_ref[...] + lax.dot_general(qw, dob, _TN, preferred_element_type=F32)

    return pl.pallas_call(body, grid=(cfg.ret_heads, n), in_specs=[blk, blk, blk, blk] + tspecs, out_specs=[blk, blk],
                          out_shape=[jax.ShapeDtypeStruct((cfg.tp, cfg.d), F32), jax.ShapeDtypeStruct((cfg.tp, cfg.d), BF16)],
                          scratch_shapes=[pltpu.VMEM((dk, dk), F32)], compiler_params=_cparams("parallel", "arbitrary"),
                          name=name)(q, k, v, do, *tabs)


def _flat2(a):
    return a.reshape(-1, a.shape[-1])


def _ew_call(body, ins, out_dtypes, name, lead=None):
    r, c = ins[-1].shape[-2:]
    tr = _div_tile(r, max(16, (1 << 18) // c), 16)
    specs = []
    for a in ins:
        if a.ndim == 3:
            specs.append(pl.BlockSpec((a.shape[0], tr, c), lambda i: (0, i, 0)))
        else:
            specs.append(pl.BlockSpec((tr, c), lambda i: (i, 0)))
    return pl.pallas_call(body, grid=(r // tr,), in_specs=specs,
                          out_specs=[pl.BlockSpec((tr, c), lambda i: (i, 0)) for _ in out_dtypes],
                          out_shape=[jax.ShapeDtypeStruct((r, c), dt) for dt in out_dtypes],
                          compiler_params=_cparams("parallel"), name=name)(*ins)


def adamw(w, g, m, v, name):
    c1 = 1.0 - ADAM_B1 ** ADAM_STEP
    c2 = 1.0 - ADAM_B2 ** ADAM_STEP

    def body(w_ref, g_ref, m_ref, v_ref, d_ref, mo_ref, vo_ref):
        gg = g_ref[...]
        mn = ADAM_B1 * m_ref[...] + (1.0 - ADAM_B1) * gg
        vn = ADAM_B2 * v_ref[...] + (1.0 - ADAM_B2) * (gg * gg)
        d_ref[...] = -ADAM_LR * ((mn / c1) / (jnp.sqrt(vn / c2) + ADAM_EPS) + ADAM_WD * w_ref[...])
        mo_ref[...] = mn
        vo_ref[...] = vn

    outs = _ew_call(body, [_flat2(w), _flat2(g), _flat2(m), _flat2(v)], [F32, F32, F32], name)
    return [o.reshape(w.shape) for o in outs]


def add_pair_bf16(g, theirs, core, name):
    c = g.shape[-1]
    g4, t3 = g.reshape(4, 2, -1, c), theirs.reshape(4, -1, c)
    rows = t3.shape[1]
    tr = _div_tile(rows, max(16, (1 << 18) // c), 16)

    def body(core_ref, g_ref, t_ref, o_ref):
        o_ref[...] = (g_ref[...] + t_ref[...]).astype(BF16)

    spec = pl.BlockSpec((None, tr, c), lambda q, i, core_ref: (q, i, 0))
    grid_spec = pltpu.PrefetchScalarGridSpec(
        num_scalar_prefetch=1, grid=(4, rows // tr),
        in_specs=[pl.BlockSpec((None, None, tr, c), lambda q, i, core_ref: (q, core_ref[0], i, 0)), spec], out_specs=spec)
    out = pl.pallas_call(body, grid_spec=grid_spec, out_shape=jax.ShapeDtypeStruct((4, rows, c), BF16),
                         compiler_params=_cparams("parallel", "parallel"), name=name)(core, g4, t3)
    return out.reshape(theirs.shape)


def sum_slots(x, name):
    x3 = x.reshape(x.shape[0], -1, x.shape[-1])

    def body(x_ref, o_ref):
        acc = x_ref[0].astype(F32)
        for s in range(1, x3.shape[0]):
            acc = acc + x_ref[s].astype(F32)
        o_ref[...] = acc

    return _ew_call(body, [x3], [F32], name)[0].reshape(x.shape[1:])


def _coords():
    return lax.axis_index("x"), lax.axis_index("y"), lax.axis_index("c")


def _chip_peer(x, y, k):
    return (x ^ (k >> 1), y ^ (k & 1))


_ANY = pl.BlockSpec(memory_space=pl.ANY)
DMA_PIECE_BYTES = 2 << 20
DMA_MAX_PIECES = 32


def _piece_plan(shape, itemsize):
    want = max(1, min(DMA_MAX_PIECES, math.prod(shape) * itemsize // DMA_PIECE_BYTES))
    plan = []
    for ax, n in enumerate(shape[:-1]):
        if want <= 1:
            break
        rows_tiled = ax == len(shape) - 2
        k = max([1] + [c for c in range(2, min(n, want) + 1) if n % c == 0 and (not rows_tiled or (n // c) % 16 == 0)])
        if k > 1:
            plan.append((ax, k))
            want = -(-want // k)
    return plan


class _Copy:
    def __init__(self, src, dst, send_sem, recv_sem=None, device_id=None):
        self.src, self.dst, self.send_sem, self.recv_sem, self.device_id = src, dst, send_sem, recv_sem, device_id

    def _dma(self, src, dst):
        if self.device_id is None:
            return pltpu.make_async_copy(src, dst, self.send_sem)
        return pltpu.make_async_remote_copy(src_ref=src, dst_ref=dst, send_sem=self.send_sem, recv_sem=self.recv_sem,
                                            device_id=self.device_id, device_id_type=MESH)

    def start(self):
        shape = self.src.shape
        plan = _piece_plan(shape, jnp.dtype(self.src.dtype).itemsize)
        for pick in np.ndindex(*[k for _, k in plan]):
            idx = [slice(None)] * len(shape)
            for (ax, k), i in zip(plan, pick):
                step = shape[ax] // k
                idx[ax] = pl.ds(i * step, step)
            idx = tuple(idx)
            self._dma(self.src.at[idx], self.dst.at[idx]).start()

    def wait(self):
        self._dma(self.src, self.dst).wait()

    def wait_send(self):
        self._dma(self.src, self.dst).wait_send()

    def wait_recv(self):
        self._dma(self.src, self.dst).wait_recv()


def allreduce_small(x, name):
    r, _, w = x.shape

    def body(x_ref, o_ref, mine_ref, gat_ref, send_sems, recv_sems):
        mx, my, mc = _coords()
        me = 4 * mx + 2 * my + mc
        mine_ref[...] = jnp.sum(x_ref[...], axis=1)
        gat_ref[me] = mine_ref[...]
        copies = []
        for k in range(1, 8):
            peer = (mx ^ (k >> 2), my ^ ((k >> 1) & 1), mc ^ (k & 1))
            cp = pltpu.make_async_remote_copy(src_ref=mine_ref, dst_ref=gat_ref.at[me], send_sem=send_sems.at[k - 1],
                                              recv_sem=recv_sems.at[k - 1], device_id=peer, device_id_type=MESH)
            cp.start()
            copies.append(cp)
        for k in range(1, 8):
            pltpu.make_async_remote_copy(src_ref=mine_ref, dst_ref=gat_ref.at[me ^ k], send_sem=send_sems.at[k - 1],
                                         recv_sem=recv_sems.at[k - 1], device_id=(mx, my, mc), device_id_type=MESH).wait_recv()
        for cp in copies:
            cp.wait_send()
        acc = gat_ref[0]
        for s in range(1, 8):
            acc = acc + gat_ref[s]
        o_ref[...] = acc

    return pl.pallas_call(body, out_shape=jax.ShapeDtypeStruct((r, w), F32),
                          in_specs=[pl.BlockSpec(memory_space=pltpu.VMEM)], out_specs=pl.BlockSpec(memory_space=pltpu.VMEM),
                          scratch_shapes=[pltpu.VMEM((r, w), F32), pltpu.VMEM((8, r, w), F32),
                                          pltpu.SemaphoreType.DMA((7,)), pltpu.SemaphoreType.DMA((7,))],
                          compiler_params=pltpu.CompilerParams(vmem_limit_bytes=VMEM_LIMIT_V7X), name=name)(x)


def allgather_chips(shards, name):
    n = len(shards)

    def body(*refs):
        ins, outs = refs[:n], refs[n:2 * n]
        loc_sems, s1_send, s1_recv, s2_send, s2_recv = refs[2 * n:]
        mx, my, mc = _coords()
        p = 2 * mx + my
        sib = (mx, my, 1 - mc)
        local, sends = [], []
        for a in range(n):
            h0 = ins[a].shape[0] // 2
            half = pl.ds(mc * h0, h0)
            loc = _Copy(ins[a], outs[a].at[p], loc_sems.at[a])
            loc.start()
            local.append(loc)
            for k in (1, 2, 3):
                qx, qy = _chip_peer(mx, my, k)
                cp = _Copy(ins[a].at[half], outs[a].at[p, half], s1_send.at[a, k - 1], s1_recv.at[a, k - 1], (qx, qy, mc))
                cp.start()
                sends.append(cp)
        for a in range(n):
            h0 = ins[a].shape[0] // 2
            half = pl.ds(mc * h0, h0)
            for k in (1, 2, 3):
                landed = outs[a].at[p ^ k, half]
                _Copy(landed, landed, s1_send.at[a, k - 1], s1_recv.at[a, k - 1], sib).wait_recv()
                fw = _Copy(landed, landed, s2_send.at[a, k - 1], s2_recv.at[a, k - 1], sib)
                fw.start()
                sends.append(fw)
        for a in range(n):
            h0 = ins[a].shape[0] // 2
            other = pl.ds((1 - mc) * h0, h0)
            for k in (1, 2, 3):
                theirs = outs[a].at[p ^ k, other]
                _Copy(theirs, theirs, s2_send.at[a, k - 1], s2_recv.at[a, k - 1], sib).wait_recv()
        for cp in sends:
            cp.wait_send()
        for loc in local:
            loc.wait()

    return pl.pallas_call(body, out_shape=[jax.ShapeDtypeStruct((4,) + s.shape, s.dtype) for s in shards],
                          in_specs=[_ANY] * n, out_specs=[_ANY] * n,
                          scratch_shapes=[pltpu.SemaphoreType.DMA((n,)), pltpu.SemaphoreType.DMA((n, 3)), pltpu.SemaphoreType.DMA((n, 3)),
                                          pltpu.SemaphoreType.DMA((n, 3)), pltpu.SemaphoreType.DMA((n, 3))],
                          name=name)(*shards)


def exchange_halves(grads, name):
    n = len(grads)

    def body(*refs):
        ins, theirs = refs[:n], refs[n:2 * n]
        send_sems, recv_sems = refs[2 * n:]
        mx, my, mc = _coords()
        pend = []
        for a in range(n):
            cp = _Copy(ins[a].at[:, pl.ds(1 - mc, 1)], theirs[a], send_sems.at[a], recv_sems.at[a], (mx, my, 1 - mc))
            cp.start()
            pend.append(cp)
        for cp in pend:
            cp.wait()

    shp = [jax.ShapeDtypeStruct((4, 1) + g.shape[2:], g.dtype) for g in grads]
    return pl.pallas_call(body, out_shape=shp, in_specs=[_ANY] * n, out_specs=[_ANY] * n,
                          scratch_shapes=[pltpu.SemaphoreType.DMA((n,)), pltpu.SemaphoreType.DMA((n,))], name=name)(*grads)


def scatter_chips(parts, name):
    n = len(parts)

    def body(*refs):
        ins, outs = refs[:n], refs[n:2 * n]
        loc_sems, send_sems, recv_sems = refs[2 * n:]
        mx, my, mc = _coords()
        p = 2 * mx + my
        pend = []
        for a in range(n):
            loc = _Copy(ins[a].at[p], outs[a].at[0], loc_sems.at[a])
            loc.start()
            pend.append(loc)
            for k in (1, 2, 3):
                qx, qy = _chip_peer(mx, my, k)
                cp = _Copy(ins[a].at[p ^ k], outs[a].at[k], send_sems.at[a, k - 1], recv_sems.at[a, k - 1], (qx, qy, mc))
                cp.start()
                pend.append(cp)
        for cp in pend:
            cp.wait()

    return pl.pallas_call(body, out_shape=[jax.ShapeDtypeStruct(g.shape, g.dtype) for g in parts], in_specs=[_ANY] * n,
                          out_specs=[_ANY] * n,
                          scratch_shapes=[pltpu.SemaphoreType.DMA((n,)), pltpu.SemaphoreType.DMA((n, 3)), pltpu.SemaphoreType.DMA((n, 3))],
                          name=name)(*parts)


def join_halves(halves, name):
    n = len(halves)

    def body(*refs):
        ins, outs = refs[:n], refs[n:2 * n]
        loc_sems, send_sems, recv_sems = refs[2 * n:]
        mx, my, mc = _coords()
        sib = (mx, my, 1 - mc)
        pend = []
        for a in range(n):
            dst = outs[a].at[pl.ds(mc, 1)]
            loc = _Copy(ins[a], dst, loc_sems.at[a])
            loc.start()
            cp = _Copy(ins[a], dst, send_sems.at[a], recv_sems.at[a], sib)
            cp.start()
            pend.append((loc, cp, outs[a].at[pl.ds(1 - mc, 1)]))
        for a, (loc, cp, theirs) in enumerate(pend):
            cp.wait_send()
            _Copy(ins[a], theirs, send_sems.at[a], recv_sems.at[a], sib).wait_recv()
            loc.wait()

    return pl.pallas_call(body, out_shape=[jax.ShapeDtypeStruct((2,) + g.shape[1:], g.dtype) for g in halves],
                          in_specs=[_ANY] * n, out_specs=[_ANY] * n,
                          scratch_shapes=[pltpu.SemaphoreType.DMA((n,)), pltpu.SemaphoreType.DMA((n,)), pltpu.SemaphoreType.DMA((n,))],
                          name=name)(*halves)


def reduce_scatter_grads(grads, tag):
    theirs = exchange_halves(grads, f"rs_pair_{tag}")
    core = lax.axis_index("c").astype(jnp.int32).reshape(1)
    parts = [add_pair_bf16(g, t, core, f"rs_add_{tag}_{i}") for i, (g, t) in enumerate(zip(grads, theirs))]
    landed = scatter_chips(parts, f"rs_chips_{tag}")
    halves = [sum_slots(x, f"rs_sum_{tag}_{i}") for i, x in enumerate(landed)]
    return join_halves(halves, f"rs_join_{tag}")


W_NAMES = ['meta_tokens', 'l0_w_in', 'l0_conv_w', 'l0_conv_b', 'l0_w_a', 'l0_b_a', 'l0_w_x', 'l0_b_x', 'l0_lam', 'l0_w_out',
           'l0_ln_g', 'l0_ln_b', 'l1_w_in', 'l1_w_grp', 'l1_scale', 'l1_w_out', 'l1_ln_g', 'l1_ln_b', 'l2_w_in', 'l2_q_norm',
           'l2_w_uq', 'l2_kv_norm', 'l2_w_ukv', 'l2_w_out', 'l2_ln_g', 'l2_ln_b', 'l3_w_in', 'l3_w_out', 'l3_ln_g', 'l3_ln_b']
BIG = {0: ['l0_w_in', 'l0_w_a', 'l0_w_x', 'l0_w_out'], 1: ['l1_w_in', 'l1_w_grp', 'l1_w_out'],
       2: ['l2_w_in', 'l2_w_uq', 'l2_w_ukv', 'l2_w_out'], 3: ['l3_w_in', 'l3_w_out']}
SMALL_SHARDED = ['meta_tokens', 'l0_conv_w']
SMALL_REPL = ['l0_conv_b', 'l0_b_a', 'l0_b_x', 'l0_lam', 'l0_ln_g', 'l0_ln_b', 'l1_scale', 'l1_ln_g', 'l1_ln_b',
              'l2_q_norm', 'l2_kv_norm', 'l2_ln_g', 'l2_ln_b', 'l3_ln_g', 'l3_ln_b']


def _rows8(rows, width):
    out = []
    for r in rows:
        r = r.reshape(-1, r.shape[-1])
        out.append(jnp.pad(r, ((0, 8 - r.shape[0]), (0, width - r.shape[1]))))
    return jnp.stack(out)


def _unblock(g4, axis):
    return jnp.concatenate([g4[i] for i in range(4)], axis=axis)


def _block(full, axis):
    return jnp.stack(jnp.split(full, 4, axis=axis))


def _step(cfg, a):
    d, tp, nm, seq = cfg.d, cfg.tp, cfg.n_meta, cfg.seq
    alpha = (2.0 * cfg.depth) ** 0.25
    mx, my, mc = _coords()
    p = 2 * mx + my
    dq4 = d // 4
    hm, npe, ql, kl = cfg.mla_heads, cfg.nope, cfg.q_lora, cfg.kv_lora
    hp = 2 * npe
    qk = npe + cfg.rope
    vec = lambda name: a[name].reshape(1, -1)

    sm = jnp.concatenate([a['meta_tokens'], a['l0_conv_w'].reshape(CONV_W, dq4)], axis=0)
    placed = lax.dynamic_update_slice(jnp.zeros((nm + CONV_W, d), F32), sm * (mc == 0).astype(F32), (0, p * dq4))
    gathered = allreduce_small(_rows8([placed[i:i + 1] for i in range(nm + CONV_W)], d), "gather_small")
    meta_full, conv_w4 = gathered[:nm], gathered[nm:]

    wg = {}
    for layer, names in BIG.items():
        got = allgather_chips([a[n].astype(BF16) for n in names], f"ag_l{layer}")
        wg.update(dict(zip(names, got)))
    heads_first = lambda w: jnp.moveaxis(w, 0, 1).reshape(w.shape[1], w.shape[0] * w.shape[2], w.shape[3])
    w0_in, w0_a, w0_x = wg['l0_w_in'], heads_first(wg['l0_w_a']), heads_first(wg['l0_w_x'])
    w1_in, w1_grp = wg['l1_w_in'], heads_first(wg['l1_w_grp'])
    w_out = [wg[f'l{i}_w_out'].reshape(1, d, d) for i in range(4)]
    w2_in = jnp.pad(_unblock(wg['l2_w_in'], 1), ((0, 0), (0, cfg.cw - (ql + kl + cfg.rope))))[None]
    w2_uq = jnp.pad(_unblock(wg['l2_w_uq'], 1).reshape(ql, hm, qk), ((0, 0), (0, 0), (0, hp - qk))).reshape(1, ql, hm * hp)
    w2_ukv = _unblock(wg['l2_w_ukv'], 1).reshape(kl, hm, 2, npe).transpose(0, 2, 1, 3).reshape(1, kl, 2 * hm * npe)
    w3_in = wg['l3_w_in']

    zpad = jnp.zeros((tp - nm - seq, d), F32)
    h0 = jnp.concatenate([meta_full, a['x'][0], zpad], axis=0)
    h0_bf = h0.astype(BF16)

    ug0 = mm_nn(cfg, h0_bf, w0_in, "l0_in")
    u0, u0_bf = conv_fwd(cfg, ug0, conv_w4, vec('l0_conv_b'), "l0_conv")
    pa0 = bd_nn(cfg, u0_bf, w0_a, "l0_gate_a")
    px0 = bd_nn(cfg, u0_bf, w0_x, "l0_gate_x")
    hs0, hprev0, y0 = lru_fwd(cfg, u0, pa0, px0, ug0, vec('l0_b_a'), vec('l0_b_x'), vec('l0_lam'), "l0_lru")
    o0 = mm_nn(cfg, y0, w_out[0], "l0_out")
    h1, h1_bf, xh0, rs0 = ln_fwd(cfg, h0, o0, vec('l0_ln_g'), vec('l0_ln_b'), alpha, "l0_ln")

    ug1 = mm_nn(cfg, h1_bf, w1_in, "l1_in")
    p1 = pool_fwd(cfg, ug1, "l1_pool")
    mm1 = bd_nn(cfg, p1, w1_grp, "l1_grp")
    y1 = gate_fwd(cfg, mm1, ug1, 1, vec('l1_scale'), "l1_gate")
    o1 = mm_nn(cfg, y1, w_out[1], "l1_out")
    h2, h2_bf, xh1, rs1 = ln_fwd(cfg, h1, o1, vec('l1_ln_g'), vec('l1_ln_b'), alpha, "l1_ln")

    proj = mm_nn(cfg, h2_bf, w2_in, "l2_in")
    cqn, ckvn = mla_norm(cfg, proj, vec('l2_q_norm'), vec('l2_kv_norm'), "l2_norm")
    q_raw = mm_nn(cfg, cqn, w2_uq, "l2_uq")
    kv_raw = mm_nn(cfg, ckvn, w2_ukv, "l2_ukv")
    tabs = _mla_tables(cfg)
    qf, kf, vb = mla_prep(cfg, q_raw, kv_raw, proj, tabs, "l2_prep")
    o_att, lse = attn_fwd(cfg, qf, kf, vb, "l2_attn")
    y2 = gate_fwd(cfg, o_att, proj, 0, None, "l2_gate")
    o2 = mm_nn(cfg, y2, w_out[2], "l2_out")
    h3, h3_bf, xh2, rs2 = ln_fwd(cfg, h2, o2, vec('l2_ln_g'), vec('l2_ln_b'), alpha, "l2_ln")

    qkvg = mm_nn(cfg, h3_bf, w3_in, "l3_in")
    cos, sin = _ret_rope_tables(cfg)
    qr, kr, vr = ret_prep(cfg, qkvg, cos, sin, "l3_prep")
    rt = _ret_tables(cfg)
    on, rsr = ret_fwd(cfg, qr, kr, vr, rt, "l3_ret")
    y3 = gate_fwd(cfg, on, qkvg, 3, None, "l3_gate")
    o3 = mm_nn(cfg, y3, w_out[3], "l3_out")
    h4, _, xh3, rs3 = ln_fwd(cfg, h3, o3, vec('l3_ln_g'), vec('l3_ln_b'), alpha, "l3_ln")

    tgt = jnp.concatenate([jnp.zeros((nm, d), F32), a['loss_target'][0], zpad], axis=0)
    dy4, lacc = loss_grad(cfg, h4, tgt, "loss")
    loss = lax.psum(0.5 * jnp.sum(lacc) / d, ("x", "y", "c"))

    grads, small = {}, {}

    def big_rs(layer, full):
        names = BIG[layer]
        got = reduce_scatter_grads([g.reshape((4, 2, g.shape[1] // 2) + g.shape[2:]) for g in full], f"l{layer}")
        for n, g in zip(names, got):
            grads[n] = g.reshape(a[n].shape)

    dz3, dz3_bf, small['l3_ln_g'], small['l3_ln_b'] = ln_bwd(cfg, None, dy4, xh3, rs3, vec('l3_ln_g'), alpha, "l3_ln_b")
    g_out = mm_tn(cfg, y3, dz3_bf, 1, "l3_dw_out")
    dyb = mm_nt(cfg, dz3_bf, w_out[3], "l3_dy")
    d_on, dg = gate_bwd(cfg, dyb, on, qkvg, 3, None, F32, "l3_gate_b")
    do_bf, dqr = ret_bwd_q(cfg, qr, kr, vr, d_on, on, rsr, rt, "l3_ret_bq")
    dkr, dvr = ret_bwd_kv(cfg, qr, kr, vr, do_bf, rt, "l3_ret_bkv")
    d_in = ret_prep_bwd(cfg, dqr, dkr, dvr, dg, cos, sin, "l3_prep_b")
    g_in = mm_tn(cfg, h3_bf, d_in, 4, "l3_dw_in")
    dh = mm_nt(cfg, d_in, w3_in, "l3_dh")
    big_rs(3, [g_in, g_out.reshape(4, dq4, d)])

    dz2, dz2_bf, small['l2_ln_g'], small['l2_ln_b'] = ln_bwd(cfg, dz3, dh, xh2, rs2, vec('l2_ln_g'), alpha, "l2_ln_b")
    g_out = mm_tn(cfg, y2, dz2_bf, 1, "l2_dw_out")
    dyb = mm_nt(cfg, dz2_bf, w_out[2], "l2_dy")
    do_bf, dg = gate_bwd(cfg, dyb, o_att, proj, 0, None, BF16, "l2_gate_b")
    dq_full, dk_full, dv_bf = attn_bwd(cfg, qf, kf, vb, do_bf, o_att, lse, "l2_attn_b")
    dq_raw, dk_nope, d_kr = mla_prep_bwd(cfg, dq_full, dk_full, tabs, "l2_prep_b")
    dkv_raw = jnp.concatenate([dk_nope, dv_bf], axis=1)
    g_uq = mm_tn(cfg, cqn, dq_raw, 1, "l2_dw_uq")
    d_cqn = mm_nt(cfg, dq_raw, w2_uq, "l2_dcq")
    g_ukv = mm_tn(cfg, ckvn, dkv_raw, 1, "l2_dw_ukv")
    d_ckvn = mm_nt(cfg, dkv_raw, w2_ukv, "l2_dckv")
    d_c, nacc = mla_norm_bwd(cfg, proj, d_cqn, d_ckvn, d_kr, vec('l2_q_norm'), vec('l2_kv_norm'), "l2_norm_b")
    small['l2_q_norm'], small['l2_kv_norm'] = nacc[:, :ql], nacc[:, ql:ql + kl]
    d_in = jnp.concatenate([dg, d_c], axis=1)
    g_in = mm_tn(cfg, h2_bf, d_in, 1, "l2_dw_in")
    dh = mm_nt(cfg, d_in, w2_in, "l2_dh")
    g_in = _block(g_in[0][:, :d + ql + kl + cfg.rope], 1)
    g_uq = _block(g_uq.reshape(ql, hm, hp)[:, :, :qk].reshape(ql, hm * qk), 1)
    g_ukv = _block(g_ukv.reshape(kl, 2, hm, npe).transpose(0, 2, 1, 3).reshape(kl, 2 * hm * npe), 1)
    big_rs(2, [g_in, g_uq, g_ukv, g_out.reshape(4, dq4, d)])

    dz1, dz1_bf, small['l1_ln_g'], small['l1_ln_b'] = ln_bwd(cfg, dz2, dh, xh1, rs1, vec('l1_ln_g'), alpha, "l1_ln_b")
    g_out = mm_tn(cfg, y1, dz1_bf, 1, "l1_dw_out")
    dyb = mm_nt(cfg, dz1_bf, w_out[1], "l1_dy")
    d_mm1, dg, small['l1_scale'] = gate_bwd(cfg, dyb, mm1, ug1, 1, vec('l1_scale'), BF16, "l1_gate_b")
    g_grp = bd_tn(cfg, p1, d_mm1, len(cfg.pool_windows), "l1_dw_grp")
    dp = bd_nt(cfg, d_mm1, w1_grp, "l1_dp")
    du = pool_bwd(cfg, dp, "l1_pool_b")
    d_in = jnp.concatenate([du, dg], axis=1)
    g_in = mm_tn(cfg, h1_bf, d_in, 4, "l1_dw_in")
    dh = mm_nt(cfg, d_in, w1_in, "l1_dh")
    blocks_first = lambda g: jnp.moveaxis(g.reshape(g.shape[0], 4, g.shape[1] // 4, g.shape[2]), 1, 0)
    big_rs(1, [g_in, blocks_first(g_grp), g_out.reshape(4, dq4, d)])

    dz0, dz0_bf, small['l0_ln_g'], small['l0_ln_b'] = ln_bwd(cfg, dz1, dh, xh0, rs0, vec('l0_ln_g'), alpha, "l0_ln_b")
    g_out = mm_tn(cfg, y0, dz0_bf, 1, "l0_dw_out")
    dyb = mm_nt(cfg, dz0_bf, w_out[0], "l0_dy")
    dg, dpa, dpx, du_dir, lacc0 = lru_bwd(cfg, dyb, ug0, hs0, hprev0, u0, pa0, px0, vec('l0_b_a'), vec('l0_b_x'), vec('l0_lam'), "l0_lru_b")
    g_a = bd_tn(cfg, u0_bf, dpa, cfg.lru_heads, "l0_dw_a")
    g_x = bd_tn(cfg, u0_bf, dpx, cfg.lru_heads, "l0_dw_x")
    du_a = bd_nt(cfg, dpa, w0_a, "l0_du_a")
    du_x = bd_nt(cfg, dpx, w0_x, "l0_du_x")
    du_pre, cacc = conv_bwd(cfg, du_dir, du_a, du_x, ug0, conv_w4, "l0_conv_b")
    d_in = jnp.concatenate([du_pre, dg], axis=1)
    g_in = mm_tn(cfg, h0_bf, d_in, 4, "l0_dw_in")
    dh = mm_nt(cfg, d_in, w0_in, "l0_dh")
    big_rs(0, [g_in, blocks_first(g_a), blocks_first(g_x), g_out.reshape(4, dq4, d)])
    dh0 = axpy(cfg, dz0, dh, alpha, "dh0")
    grad_x = dh0[nm:nm + seq][None]
    small['l0_lam'], small['l0_b_a'], small['l0_b_x'] = lacc0[0:1], lacc0[1:2], lacc0[2:3]
    small['l0_conv_b'] = cacc[4:5]

    rows = [dh0[i:i + 1] for i in range(nm)] + [cacc[k:k + 1] for k in range(CONV_W)] + [small[n] for n in SMALL_REPL]
    red = allreduce_small(_rows8(rows, d), "reduce_small")
    sh = lax.dynamic_slice(red[:nm + CONV_W], (0, p * dq4), (nm + CONV_W, dq4))
    grads['meta_tokens'] = sh[:nm]
    grads['l0_conv_w'] = sh[nm:].reshape(a['l0_conv_w'].shape)
    for i, n in enumerate(SMALL_REPL):
        grads[n] = red[nm + CONV_W + i, :a[n].shape[0]]

    delta, new_m, new_v = {}, {}, {}
    for n in sum(BIG.values(), []):
        delta[n], new_m[n], new_v[n] = adamw(a[n], grads[n], a['m_' + n], a['v_' + n], f"adamw_{n}")
    pack_s = lambda pre: jnp.concatenate([a[pre + 'meta_tokens'], a[pre + 'l0_conv_w'].reshape(CONV_W, dq4)], axis=0)
    ds_, ms_, vs_ = adamw(pack_s(''), sh, pack_s('m_'), pack_s('v_'), "adamw_small_sharded")
    for out, packed in ((delta, ds_), (new_m, ms_), (new_v, vs_)):
        out['meta_tokens'] = packed[:nm]
        out['l0_conv_w'] = packed[nm:].reshape(a['l0_conv_w'].shape)
    pack_r = lambda pre: jnp.stack([jnp.pad(a[pre + n], (0, d - a[n].shape[0])) for n in SMALL_REPL])
    dr_, mr_, vr_ = adamw(pack_r(''), red[nm + CONV_W:nm + CONV_W + len(SMALL_REPL)], pack_r('m_'), pack_r('v_'), "adamw_small_repl")
    for out, packed in ((delta, dr_), (new_m, mr_), (new_v, vr_)):
        for i, n in enumerate(SMALL_REPL):
            out[n] = packed[i, :a[n].shape[0]]

    return (loss, grad_x, *[grads[n] for n in W_NAMES], *[delta[n] for n in W_NAMES],
            *[new_m[n] for n in W_NAMES], *[new_v[n] for n in W_NAMES])


def kernel(x, meta_tokens, l0_w_in, l0_conv_w, l0_conv_b, l0_w_a, l0_b_a, l0_w_x, l0_b_x, l0_lam, l0_w_out, l0_ln_g, l0_ln_b, l1_w_in, l1_w_grp, l1_scale, l1_w_out, l1_ln_g, l1_ln_b, l2_w_in, l2_q_norm, l2_w_uq, l2_kv_norm, l2_w_ukv, l2_w_out, l2_ln_g, l2_ln_b, l3_w_in, l3_w_out, l3_ln_g, l3_ln_b, loss_target, m_meta_tokens, m_l0_w_in, m_l0_conv_w, m_l0_conv_b, m_l0_w_a, m_l0_b_a, m_l0_w_x, m_l0_b_x, m_l0_lam, m_l0_w_out, m_l0_ln_g, m_l0_ln_b, m_l1_w_in, m_l1_w_grp, m_l1_scale, m_l1_w_out, m_l1_ln_g, m_l1_ln_b, m_l2_w_in, m_l2_q_norm, m_l2_w_uq, m_l2_kv_norm, m_l2_w_ukv, m_l2_w_out, m_l2_ln_g, m_l2_ln_b, m_l3_w_in, m_l3_w_out, m_l3_ln_g, m_l3_ln_b, v_meta_tokens, v_l0_w_in, v_l0_conv_w, v_l0_conv_b, v_l0_w_a, v_l0_b_a, v_l0_w_x, v_l0_b_x, v_l0_lam, v_l0_w_out, v_l0_ln_g, v_l0_ln_b, v_l1_w_in, v_l1_w_grp, v_l1_scale, v_l1_w_out, v_l1_ln_g, v_l1_ln_b, v_l2_w_in, v_l2_q_norm, v_l2_w_uq, v_l2_kv_norm, v_l2_w_ukv, v_l2_w_out, v_l2_ln_g, v_l2_ln_b, v_l3_w_in, v_l3_w_out, v_l3_ln_g, v_l3_ln_b):
    return _step(REAL, dict(locals()))
```

```python
import functools
import math
from typing import NamedTuple

import jax
import jax.numpy as jnp
import numpy as np
from jax import lax
from jax.experimental import pallas as pl
from jax.experimental.pallas import tpu as pltpu

F32 = jnp.float32
BF16 = jnp.bfloat16

LN_EPS = 1e-5
RMS_EPS = 1e-6
ROPE_BASE = 10000.0
LRU_C = 8.0
CONV_W = 4
HIST = 16
ADAM_LR, ADAM_B1, ADAM_B2, ADAM_EPS, ADAM_WD, ADAM_STEP = 0.001, 0.9, 0.999, 1e-08, 0.01, 10
VMEM_LIMIT_V7X = 56 * 1024 * 1024
MESH = pl.DeviceIdType.MESH


class Cfg(NamedTuple):
    d: int
    seq: int
    n_meta: int
    tp: int
    depth: int
    lru_heads: int
    pool_windows: tuple
    mla_heads: int
    nope: int
    rope: int
    q_lora: int
    kv_lora: int
    cw: int
    ret_heads: int
    chunk: int
    tq: int
    tr: int
    tt: int
    tc: int
    tm: int
    tn: int
    la: int


REAL = Cfg(d=4096, seq=4096, n_meta=16, tp=4224, depth=4, lru_heads=16, pool_windows=(2, 4, 8, 16),
           mla_heads=32, nope=128, rope=64, q_lora=1024, kv_lora=512, cw=2048, ret_heads=16, chunk=384,
           tq=384, tr=128, tt=1056, tc=512, tm=1408, tn=1024, la=128)


def _div_tile(n, target, align):
    best = None
    for t in range(align, min(n, target) + 1, align):
        if n % t == 0:
            best = t
    return best or n


def _cparams(*sem):
    return pltpu.CompilerParams(dimension_semantics=sem, vmem_limit_bytes=VMEM_LIMIT_V7X)


def _sig(x):
    return 1.0 / (1.0 + jnp.exp(-x))


def _expm1(x):
    p = x * (1.0 + x * (1.0 / 2) * (1.0 + x * (1.0 / 3) * (1.0 + x * (1.0 / 4) * (1.0 + x * (1.0 / 5) * (1.0 + x * (1.0 / 6))))))
    return jnp.where(x > -0.1, p, jnp.exp(x) - 1.0)


def _colsum8(x):
    r, w = x.shape
    return x.reshape(r // 8, 8, w).sum(axis=0)


def _mm_call(a, b, grid, a_spec, b_spec, o_spec, out_sds, acc_shape, dims, name):
    nk = grid[-1]
    kax = len(grid) - 1

    def body(a_ref, b_ref, o_ref, acc_ref):
        k = pl.program_id(kax)

        @pl.when(k == 0)
        def _():
            acc_ref[...] = jnp.zeros_like(acc_ref)

        acc_ref[...] += lax.dot_general(a_ref[...].astype(BF16), b_ref[...].astype(BF16), dims,
                                        preferred_element_type=F32)

        @pl.when(k == nk - 1)
        def _():
            o_ref[...] = acc_ref[...].astype(o_ref.dtype)

    sem = ("parallel",) * kax + ("arbitrary",)
    return pl.pallas_call(body, grid=grid, in_specs=[a_spec, b_spec], out_specs=o_spec, out_shape=out_sds,
                          scratch_shapes=[pltpu.VMEM(acc_shape, F32)], compiler_params=_cparams(*sem), name=name)(a, b)


_NN = (((1,), (0,)), ((), ()))
_NT = (((1,), (1,)), ((), ()))
_TN = (((0,), (0,)), ((), ()))


def mm_nn(cfg, a, w, name, out_dtype=F32, a_col0=0):
    m = a.shape[0]
    g, k, ns = w.shape
    tm, tn, tk = _div_tile(m, cfg.tm, 16), _div_tile(ns, cfg.tn, cfg.la), _div_tile(k, cfg.tn, cfg.la)
    npb, a0 = ns // tn, a_col0 // tk
    assert a_col0 % tk == 0
    return _mm_call(a, w, (m // tm, g * npb, k // tk),
                    pl.BlockSpec((tm, tk), lambda i, j, kk: (i, a0 + kk)),
                    pl.BlockSpec((None, tk, tn), lambda i, j, kk: (j // npb, kk, j % npb)),
                    pl.BlockSpec((tm, tn), lambda i, j, kk: (i, j)),
                    jax.ShapeDtypeStruct((m, g * ns), out_dtype), (tm, tn), _NN, name)


def mm_nt(cfg, dy, w, name, out_dtype=F32):
    m = dy.shape[0]
    g, k, ns = w.shape
    tm, tn, tk = _div_tile(m, cfg.tm, 16), _div_tile(k, cfg.tn, cfg.la), _div_tile(ns, cfg.tn, cfg.la)
    kpb = ns // tk
    return _mm_call(dy, w, (m // tm, k // tn, g * kpb),
                    pl.BlockSpec((tm, tk), lambda i, j, kk: (i, kk)),
                    pl.BlockSpec((None, tn, tk), lambda i, j, kk: (kk // kpb, j, kk % kpb)),
                    pl.BlockSpec((tm, tn), lambda i, j, kk: (i, j)),
                    jax.ShapeDtypeStruct((m, k), out_dtype), (tm, tn), _NT, name)


def mm_tn(cfg, x, dy, g, name, x_col0=0, k=None):
    m = x.shape[0]
    k = x.shape[1] if k is None else k
    ns = dy.shape[1] // g
    tmo, tn, tk = _div_tile(k, cfg.tn, cfg.la), _div_tile(ns, cfg.tn, cfg.la), _div_tile(m, cfg.tm, 16)
    npb, x0 = ns // tn, x_col0 // tmo
    assert x_col0 % tmo == 0
    return _mm_call(x, dy, (k // tmo, g * npb, m // tk),
                    pl.BlockSpec((tk, tmo), lambda i, j, kk: (kk, x0 + i)),
                    pl.BlockSpec((tk, tn), lambda i, j, kk: (kk, j)),
                    pl.BlockSpec((None, tmo, tn), lambda i, j, kk: (j // npb, i, j % npb)),
                    jax.ShapeDtypeStruct((g, k, ns), F32), (tmo, tn), _TN, name)


def bd_nn(cfg, x, w, name, out_dtype=F32):
    m = x.shape[0]
    g, kg, ng = w.shape
    tm, tn = _div_tile(m, cfg.tm, 16), _div_tile(ng, cfg.tn, cfg.la)
    npb = ng // tn
    return _mm_call(x, w, (m // tm, g * npb, 1),
                    pl.BlockSpec((tm, kg), lambda i, j, kk: (i, j // npb)),
                    pl.BlockSpec((None, kg, tn), lambda i, j, kk: (j // npb, 0, j % npb)),
                    pl.BlockSpec((tm, tn), lambda i, j, kk: (i, j)),
                    jax.ShapeDtypeStruct((m, g * ng), out_dtype), (tm, tn), _NN, name)


def bd_nt(cfg, dy, w, name, out_dtype=F32):
    m = dy.shape[0]
    g, kg, ng = w.shape
    tm, tn = _div_tile(m, cfg.tm, 16), _div_tile(kg, cfg.tn, cfg.la)
    npb = kg // tn
    return _mm_call(dy, w, (m // tm, g * npb, 1),
                    pl.BlockSpec((tm, ng), lambda i, j, kk: (i, j // npb)),
                    pl.BlockSpec((None, tn, ng), lambda i, j, kk: (j // npb, j % npb, 0)),
                    pl.BlockSpec((tm, tn), lambda i, j, kk: (i, j)),
                    jax.ShapeDtypeStruct((m, g * kg), out_dtype), (tm, tn), _NT, name)


def bd_tn(cfg, x, dy, g, name):
    m = x.shape[0]
    kg, ng = x.shape[1] // g, dy.shape[1] // g
    tmo, tn, tk = _div_tile(kg, cfg.tn, cfg.la), _div_tile(ng, cfg.tn, cfg.la), _div_tile(m, cfg.tm, 16)
    mpb, npb = kg // tmo, ng // tn
    return _mm_call(x, dy, (g * mpb, npb, m // tk),
                    pl.BlockSpec((tk, tmo), lambda i, j, kk: (kk, i)),
                    pl.BlockSpec((tk, tn), lambda i, j, kk: (kk, (i // mpb) * npb + j)),
                    pl.BlockSpec((None, tmo, tn), lambda i, j, kk: (i // mpb, i % mpb, j)),
                    jax.ShapeDtypeStruct((g, kg, ng), F32), (tmo, tn), _TN, name)


def _rowwise(cfg, body, ins, outs, name, n_acc=0):
    tr = cfg.tr
    n = cfg.tp // tr
    in_specs, args = [], []
    for spec in ins:
        arr = spec[0]
        if spec[1] is None:
            in_specs.append(pl.BlockSpec(arr.shape, lambda i, nd=arr.ndim: (0,) * nd))
        else:
            in_specs.append(pl.BlockSpec((tr, spec[1]), lambda i, cb=spec[2]: (i, cb)))
        args.append(arr)
    out_specs, out_shape = [], []
    for o in outs:
        if len(o) == 2:
            out_specs.append(pl.BlockSpec((tr, o[0]), lambda i: (i, 0)))
            out_shape.append(jax.ShapeDtypeStruct((cfg.tp, o[0]), o[1]))
        else:
            out_specs.append(pl.BlockSpec((8, o[0]), lambda i: (0, 0)))
            out_shape.append(jax.ShapeDtypeStruct((8, o[0]), F32))
    return pl.pallas_call(body, grid=(n,), in_specs=in_specs, out_specs=out_specs, out_shape=out_shape,
                          compiler_params=_cparams("arbitrary" if n_acc else "parallel"), name=name)(*args)


def ln_fwd(cfg, h, o, g, b, alpha, name):
    d = cfg.d

    def body(h_ref, o_ref, g_ref, b_ref, y_ref, yb_ref, xh_ref, rs_ref):
        z = alpha * h_ref[...] + o_ref[...]
        mu = jnp.mean(z, axis=-1, keepdims=True)
        zc = z - mu
        var = jnp.mean(zc * zc, axis=-1, keepdims=True)
        rstd = lax.rsqrt(var + LN_EPS)
        xh = zc * rstd
        y = xh * g_ref[...] + b_ref[...]
        y_ref[...] = y
        yb_ref[...] = y.astype(BF16)
        xh_ref[...] = xh
        rs_ref[...] = rstd

    return _rowwise(cfg, body, [(h, d, 0), (o, d, 0), (g, None), (b, None)], [(d, F32), (d, BF16), (d, F32), (1, F32)], name)


def ln_bwd(cfg, d_res, d_mm, xhat, rstd, g, alpha, name):
    d = cfg.d
    two = d_res is not None

    def body(*refs):
        if two:
            dr_ref, dm_ref, xh_ref, rs_ref, g_ref, dz_ref, dzb_ref, dg_ref, db_ref = refs
            dy = alpha * dr_ref[...] + dm_ref[...]
        else:
            dm_ref, xh_ref, rs_ref, g_ref, dz_ref, dzb_ref, dg_ref, db_ref = refs
            dy = dm_ref[...]
        xh = xh_ref[...]

        @pl.when(pl.program_id(0) == 0)
        def _():
            dg_ref[...] = jnp.zeros_like(dg_ref)
            db_ref[...] = jnp.zeros_like(db_ref)

        dg_ref[...] += _colsum8(dy * xh)
        db_ref[...] += _colsum8(dy)
        dxh = dy * g_ref[...]
        m1 = jnp.mean(dxh, axis=-1, keepdims=True)
        m2 = jnp.mean(dxh * xh, axis=-1, keepdims=True)
        dz = rs_ref[...] * (dxh - m1 - xh * m2)
        dz_ref[...] = dz
        dzb_ref[...] = dz.astype(BF16)

    ins = ([(d_res, d, 0)] if two else []) + [(d_mm, d, 0), (xhat, d, 0), (rstd, 1, 0), (g, None)]
    return _rowwise(cfg, body, ins, [(d, F32), (d, BF16), (d,), (d,)], name, n_acc=2)


def loss_grad(cfg, y, tgt, name):
    d, tr = cfg.d, cfg.tr
    lo, hi = cfg.n_meta, cfg.n_meta + cfg.seq

    def body(y_ref, t_ref, dy_ref, acc_ref):
        i = pl.program_id(0)

        @pl.when(i == 0)
        def _():
            acc_ref[...] = jnp.zeros_like(acc_ref)

        row = i * tr + lax.broadcasted_iota(jnp.int32, (tr, 1), 0)
        err = jnp.where((row >= lo) & (row < hi), y_ref[...] - t_ref[...], 0.0)
        dy_ref[...] = err * (1.0 / d)
        acc_ref[...] += _colsum8(err * err)

    return _rowwise(cfg, body, [(y, d, 0), (tgt, d, 0)], [(d, F32), (d,)], name, n_acc=1)


def axpy(cfg, a, b, alpha, name):
    d = cfg.d

    def body(a_ref, b_ref, o_ref):
        o_ref[...] = alpha * a_ref[...] + b_ref[...]

    return _rowwise(cfg, body, [(a, d, 0), (b, d, 0)], [(d, F32)], name)[0]


def _time_call(cfg, body, ins, outs, accs, scratch, name, reverse=False, groups=1):
    d = cfg.d
    tc = _div_tile(d // groups, cfg.tc, cfg.la)
    tt = _div_tile(cfg.tp, cfg.tt, 16)
    nc, nt = d // tc, cfg.tp // tt
    tmap = (lambda t: nt - 1 - t) if reverse else (lambda t: t)
    in_specs, args = [], []
    for spec in ins:
        arr = spec[0]
        if len(spec) == 2:
            in_specs.append(pl.BlockSpec((tt, tc), lambda c, t, off=spec[1] * nc: (tmap(t), off + c)))
        else:
            in_specs.append(pl.BlockSpec((arr.shape[0], tc), lambda c, t: (0, c)))
        args.append(arr)
    out_specs = [pl.BlockSpec((tt, tc), lambda c, t: (tmap(t), c)) for _ in outs]
    out_shape = [jax.ShapeDtypeStruct((cfg.tp, d), dt) for dt in outs]
    for _ in range(accs):
        out_specs.append(pl.BlockSpec((8, tc), lambda c, t: (0, c)))
        out_shape.append(jax.ShapeDtypeStruct((8, d), F32))
    return pl.pallas_call(functools.partial(body, tt=tt, tc=tc, nt=nt), grid=(nc, nt), in_specs=in_specs,
                          out_specs=out_specs, out_shape=out_shape,
                          scratch_shapes=[pltpu.VMEM(s, F32) for s in scratch(tt, tc)],
                          compiler_params=_cparams("parallel", "arbitrary"), name=name)(*args)


def _push_history(s_ref, new, t, tt):
    @pl.when(t == 0)
    def _():
        s_ref[pl.ds(0, HIST), :] = jnp.zeros((HIST, s_ref.shape[1]), F32)

    @pl.when(t > 0)
    def _():
        s_ref[pl.ds(0, HIST), :] = s_ref[pl.ds(tt, HIST), :]

    s_ref[pl.ds(HIST, tt), :] = new


def _push_future(s_ref, new, t, tt):
    @pl.when(t == 0)
    def _():
        s_ref[pl.ds(tt, HIST), :] = jnp.zeros((HIST, s_ref.shape[1]), F32)

    @pl.when(t > 0)
    def _():
        s_ref[pl.ds(tt, HIST), :] = s_ref[pl.ds(0, HIST), :]

    s_ref[pl.ds(0, tt), :] = new


def conv_fwd(cfg, ug, w4, cb, name):
    def body(x_ref, w_ref, b_ref, u_ref, ub_ref, s_ref, *, tt, tc, nt):
        t = pl.program_id(1)
        _push_history(s_ref, x_ref[...], t, tt)
        acc = b_ref[...] + w_ref[pl.ds(0, 1), :] * s_ref[pl.ds(HIST - 3, tt), :]
        for k in range(1, CONV_W):
            acc = acc + w_ref[pl.ds(k, 1), :] * s_ref[pl.ds(HIST - 3 + k, tt), :]
        u_ref[...] = acc
        ub_ref[...] = acc.astype(BF16)

    return _time_call(cfg, body, [(ug, 0), (w4,), (cb,)], [F32, BF16], 0, lambda tt, tc: [(tt + HIST, tc)], name)


def conv_bwd(cfg, du_a, du_b, du_c, ug, w4, name):
    def body(a_ref, b_ref, c_ref, x_ref, w_ref, dx_ref, acc_ref, s_ref, *, tt, tc, nt):
        t = pl.program_id(1)

        @pl.when(t == 0)
        def _():
            acc_ref[...] = jnp.zeros_like(acc_ref)

        du = a_ref[...] + b_ref[...] + c_ref[...]
        _push_future(s_ref, du, t, tt)
        x = x_ref[...]
        acc_ref[pl.ds(4, 1), :] += jnp.sum(du, axis=0, keepdims=True)
        dx = None
        for k in range(CONV_W):
            sh = s_ref[pl.ds(3 - k, tt), :]
            term = w_ref[pl.ds(k, 1), :] * sh
            dx = term if dx is None else dx + term
            acc_ref[pl.ds(k, 1), :] += jnp.sum(x * sh, axis=0, keepdims=True)
        dx_ref[...] = dx.astype(BF16)

    return _time_call(cfg, body, [(du_a, 0), (du_b, 0), (du_c, 0), (ug, 0), (w4,)], [BF16], 1,
                      lambda tt, tc: [(tt + HIST, tc)], name, reverse=True)


def _scan8(a, b, rows):
    for k in (1, 2, 4):
        ar = jnp.where(rows >= k, pltpu.roll(a, k, 0), 1.0)
        br = jnp.where(rows >= k, pltpu.roll(b, k, 0), 0.0)
        b = a * br + b
        a = a * ar
    return a, b


def _rscan8(a, b, rows):
    for k in (1, 2, 4):
        ar = jnp.where(rows < 8 - k, pltpu.roll(a, 8 - k, 0), 1.0)
        br = jnp.where(rows < 8 - k, pltpu.roll(b, 8 - k, 0), 0.0)
        b = b + a * br
        a = a * ar
    return a, b


def _lru_gates(u, pa, px, ba, bx, c_lam):
    r = _sig(pa + ba)
    i = _sig(px + bx)
    la = LRU_C * r * c_lam
    em = _expm1(2.0 * la)
    return r, i, jnp.exp(la), em, jnp.sqrt(-em)


def _neg_softplus_neg(lam):
    e = jnp.exp(-jnp.abs(lam))
    u = 1.0 + e
    l1p = jnp.where(u == 1.0, e, jnp.log(u) * (e / jnp.where(u == 1.0, 1.0, u - 1.0)))
    return -(jnp.maximum(-lam, 0.0) + l1p)


def lru_fwd(cfg, u, pa, px, ug, ba, bx, lam, name):
    def body(u_ref, pa_ref, px_ref, g_ref, ba_ref, bx_ref, lam_ref, hs_ref, hp_ref, y_ref, car_ref, *, tt, tc, nt):
        @pl.when(pl.program_id(1) == 0)
        def _():
            car_ref[...] = jnp.zeros_like(car_ref)

        rows = lax.broadcasted_iota(jnp.int32, (8, tc), 0)
        ba, bx = ba_ref[...], bx_ref[...]
        c_lam = _neg_softplus_neg(lam_ref[...])

        def step(it, h):
            ys = []
            for half in range(2):
                rs = pl.ds(pl.multiple_of(it * 16 + half * 8, 8), 8)
                uu = u_ref[rs, :]
                r, i, a, em, s = _lru_gates(uu, pa_ref[rs, :], px_ref[rs, :], ba, bx, c_lam)
                acum, b = _scan8(a, uu * i * s, rows)
                hs = acum * h + b
                hs_ref[rs, :] = hs
                hp_ref[rs, :] = jnp.where(rows >= 1, pltpu.roll(hs, 1, 0), h)
                g = g_ref[rs, :]
                ys.append(hs * (g * _sig(g)))
                h = jnp.broadcast_to(hs[7:8, :], (8, tc))
            y_ref[pl.ds(pl.multiple_of(it * 16, 16), 16), :] = jnp.concatenate(ys, axis=0).astype(BF16)
            return h

        car_ref[...] = lax.fori_loop(0, tt // 16, step, car_ref[...])

    return _time_call(cfg, body, [(u, 0), (pa, 0), (px, 0), (ug, 1), (ba,), (bx,), (lam,)], [F32, F32, BF16], 0,
                      lambda tt, tc: [(8, tc)], name)


def lru_bwd(cfg, dy, ug, hs, hprev, u, pa, px, ba, bx, lam, name):
    def body(dy_ref, g_ref, hs_ref, hp_ref, u_ref, pa_ref, px_ref, ba_ref, bx_ref, lam_ref,
             dg_ref, dpa_ref, dpx_ref, du_ref, acc_ref, ecar_ref, acar_ref, *, tt, tc, nt):
        @pl.when(pl.program_id(1) == 0)
        def _():
            ecar_ref[...] = jnp.zeros_like(ecar_ref)
            acar_ref[...] = jnp.zeros_like(acar_ref)
            acc_ref[...] = jnp.zeros_like(acc_ref)

        rows = lax.broadcasted_iota(jnp.int32, (8, tc), 0)
        ba, bx = ba_ref[...], bx_ref[...]
        c_lam = _neg_softplus_neg(lam_ref[...])

        def step(it, carry):
            ecar, acar, s_lam, s_ba, s_bx = carry
            jt = tt // 16 - 1 - it
            dgs, dpas, dpxs = [None, None], [None, None], [None, None]
            for half in (1, 0):
                rs = pl.ds(pl.multiple_of(jt * 16 + half * 8, 8), 8)
                uu = u_ref[rs, :]
                r, i, a, em, s = _lru_gates(uu, pa_ref[rs, :], px_ref[rs, :], ba, bx, c_lam)
                g = g_ref[rs, :]
                sg = _sig(g)
                dy = dy_ref[rs, :]
                dgs[half] = dy * hs_ref[rs, :] * (sg * (1.0 + g * (1.0 - sg)))
                a_next = jnp.where(rows < 7, pltpu.roll(a, 7, 0), acar)
                acum, e = _rscan8(a_next, dy * (g * sg), rows)
                e = e + acum * ecar
                ecar = jnp.broadcast_to(e[0:1, :], (8, tc))
                acar = jnp.broadcast_to(a[0:1, :], (8, tc))
                du_ref[rs, :] = e * i * s
                d_la = e * hp_ref[rs, :] * a - (e * uu * i) * ((em + 1.0) / s)
                s_lam = s_lam + d_la * (LRU_C * r)
                d_pa = d_la * (LRU_C * c_lam) * r * (1.0 - r)
                d_px = e * uu * s * i * (1.0 - i)
                s_ba = s_ba + d_pa
                s_bx = s_bx + d_px
                dpas[half], dpxs[half] = d_pa, d_px
            r16 = pl.ds(pl.multiple_of(jt * 16, 16), 16)
            dg_ref[r16, :] = jnp.concatenate(dgs, axis=0).astype(BF16)
            dpa_ref[r16, :] = jnp.concatenate(dpas, axis=0).astype(BF16)
            dpx_ref[r16, :] = jnp.concatenate(dpxs, axis=0).astype(BF16)
            return ecar, acar, s_lam, s_ba, s_bx

        z = jnp.zeros((8, tc), F32)
        ecar, acar, s_lam, s_ba, s_bx = lax.fori_loop(0, tt // 16, step, (ecar_ref[...], acar_ref[...], z, z, z))
        ecar_ref[...] = ecar
        acar_ref[...] = acar
        acc_ref[pl.ds(0, 1), :] += jnp.sum(s_lam, axis=0, keepdims=True) * _sig(-lam_ref[...])
        acc_ref[pl.ds(1, 1), :] += jnp.sum(s_ba, axis=0, keepdims=True)
        acc_ref[pl.ds(2, 1), :] += jnp.sum(s_bx, axis=0, keepdims=True)

    return _time_call(cfg, body, [(dy, 0), (ug, 1), (hs, 0), (hprev, 0), (u, 0), (pa, 0), (px, 0), (ba,), (bx,), (lam,)],
                      [BF16, BF16, BF16, F32], 1, lambda tt, tc: [(8, tc), (8, tc)], name, reverse=True)


def _pool_select(cfg, grp, fn):
    for gi, w in enumerate(cfg.pool_windows):
        @pl.when(grp == gi)
        def _(w=w):
            fn(w)


def pool_fwd(cfg, ug, name):
    ng = len(cfg.pool_windows)

    def body(x_ref, p_ref, s_ref, *, tt, tc, nt):
        t = pl.program_id(1)
        grp = pl.program_id(0) // (cfg.d // ng // tc)
        x = x_ref[...]
        _push_history(s_ref, x, t, tt)
        row1 = (t * tt + 1 + lax.broadcasted_iota(jnp.int32, (tt, 1), 0)).astype(F32)

        def write(w):
            ws = x
            for j in range(1, w):
                ws = ws + s_ref[pl.ds(HIST - j, tt), :]
            p_ref[...] = (ws / jnp.minimum(row1, float(w)) - x).astype(BF16)

        _pool_select(cfg, grp, write)

    return _time_call(cfg, body, [(ug, 0)], [BF16], 0, lambda tt, tc: [(tt + HIST, tc)], name, groups=ng)[0]


def pool_bwd(cfg, dp, name):
    ng = len(cfg.pool_windows)

    def body(dp_ref, du_ref, s_ref, *, tt, tc, nt):
        t = pl.program_id(1)
        grp = pl.program_id(0) // (cfg.d // ng // tc)
        dp = dp_ref[...]
        row1 = ((nt - 1 - t) * tt + 1 + lax.broadcasted_iota(jnp.int32, (tt, 1), 0)).astype(F32)

        def write(w):
            dm = dp / jnp.minimum(row1, float(w))
            _push_future(s_ref, dm, t, tt)
            ws = dm
            for j in range(1, w):
                ws = ws + s_ref[pl.ds(j, tt), :]
            du_ref[...] = (ws - dp).astype(BF16)

        _pool_select(cfg, grp, write)

    return _time_call(cfg, body, [(dp, 0)], [BF16], 0, lambda tt, tc: [(tt + HIST, tc)], name, reverse=True, groups=ng)[0]


def gate_fwd(cfg, v, gsrc, gblk, scale, name):
    d = cfg.d

    def body(*refs):
        if scale is None:
            v_ref, g_ref, y_ref = refs
            v = v_ref[...]
        else:
            v_ref, g_ref, s_ref, y_ref = refs
            v = v_ref[...] * s_ref[...]
        g = g_ref[...]
        y_ref[...] = (v * (g * _sig(g))).astype(BF16)

    ins = [(v, d, 0), (gsrc, d, gblk)] + ([] if scale is None else [(scale, None)])
    return _rowwise(cfg, body, ins, [(d, BF16)], name)[0]


def gate_bwd(cfg, dy, v, gsrc, gblk, scale, dv_dtype, name):
    d = cfg.d

    def body(*refs):
        if scale is None:
            dy_ref, v_ref, g_ref, dv_ref, dg_ref = refs
            vs = v_ref[...]
        else:
            dy_ref, v_ref, g_ref, s_ref, dv_ref, dg_ref, acc_ref = refs
            vs = v_ref[...] * s_ref[...]
        g = g_ref[...]
        sg = _sig(g)
        dy = dy_ref[...]
        dvs = dy * (g * sg)
        dg_ref[...] = (dy * vs * (sg * (1.0 + g * (1.0 - sg)))).astype(BF16)
        if scale is None:
            dv_ref[...] = dvs.astype(dv_dtype)
        else:
            @pl.when(pl.program_id(0) == 0)
            def _():
                acc_ref[...] = jnp.zeros_like(acc_ref)

            acc_ref[...] += _colsum8(dvs * v_ref[...])
            dv_ref[...] = (dvs * s_ref[...]).astype(dv_dtype)

    ins = [(dy, d, 0), (v, d, 0), (gsrc, d, gblk)] + ([] if scale is None else [(scale, None)])
    outs = [(d, dv_dtype), (d, BF16)] + ([] if scale is None else [(d,)])
    return _rowwise(cfg, body, ins, outs, name, n_acc=0 if scale is None else 1)


def _swap_halves(x, lo, half):
    w = x.shape[1]
    lane = lax.broadcasted_iota(jnp.int32, x.shape, 1)
    sw = jnp.where(lane < lo + half, pltpu.roll(x, w - half, 1), pltpu.roll(x, half, 1))
    return jnp.where((lane >= lo) & (lane < lo + 2 * half), sw, 0.0)


def mla_norm(cfg, proj, qn, kvn, name):
    ql, kl = cfg.q_lora, cfg.kv_lora

    def body(c_ref, qn_ref, kn_ref, q_ref, k_ref):
        cq = c_ref[:, 0:ql]
        ck = c_ref[:, ql:ql + kl]
        q_ref[...] = (cq * lax.rsqrt(jnp.mean(cq * cq, axis=-1, keepdims=True) + RMS_EPS) * qn_ref[...]).astype(BF16)
        k_ref[...] = (ck * lax.rsqrt(jnp.mean(ck * ck, axis=-1, keepdims=True) + RMS_EPS) * kn_ref[...]).astype(BF16)

    return _rowwise(cfg, body, [(proj, cfg.cw, cfg.d // cfg.cw), (qn, None), (kvn, None)], [(ql, BF16), (kl, BF16)], name)


def mla_norm_bwd(cfg, proj, d_cqn, d_ckvn, d_kr, qn, kvn, name):
    ql, kl, cw, npe = cfg.q_lora, cfg.kv_lora, cfg.cw, cfg.nope

    def one(x, dy, gamma):
        rstd = lax.rsqrt(jnp.mean(x * x, axis=-1, keepdims=True) + RMS_EPS)
        xn = x * rstd
        dxn = dy * gamma
        return rstd * (dxn - xn * jnp.mean(dxn * xn, axis=-1, keepdims=True)), dy * xn

    def body(c_ref, dq_ref, dk_ref, dkr_ref, qn_ref, kn_ref, dc_ref, acc_ref):
        @pl.when(pl.program_id(0) == 0)
        def _():
            acc_ref[...] = jnp.zeros_like(acc_ref)

        dcq, gq = one(c_ref[:, 0:ql], dq_ref[...], qn_ref[...])
        dck, gk = one(c_ref[:, ql:ql + kl], dk_ref[...], kn_ref[...])
        dc_ref[:, 0:ql] = dcq.astype(BF16)
        dc_ref[:, ql:ql + kl] = dck.astype(BF16)
        dc_ref[:, ql + kl:ql + kl + npe] = dkr_ref[...].astype(BF16)
        rest = cw - (ql + kl + npe)
        if rest:
            dc_ref[:, ql + kl + npe:cw] = jnp.zeros((dc_ref.shape[0], rest), BF16)
        acc_ref[:, 0:ql] += _colsum8(gq)
        acc_ref[:, ql:ql + kl] += _colsum8(gk)

    return _rowwise(cfg, body, [(proj, cw, cfg.d // cw), (d_cqn, ql, 0), (d_ckvn, kl, 0), (d_kr, npe, 0), (qn, None), (kvn, None)],
                    [(cw, BF16), (cw,)], name, n_acc=1)


def _mla_tables(cfg):
    r2, npe = cfg.rope // 2, cfg.nope
    inv = ROPE_BASE ** (-jnp.arange(0, cfg.rope, 2, dtype=F32) / cfg.rope)
    ang = jnp.arange(cfg.tp, dtype=F32)[:, None] * inv[None, :]
    cos, sin = jnp.cos(ang), jnp.sin(ang)
    one = jnp.ones((cfg.tp, npe - cfg.rope), F32)
    ck = jnp.concatenate([cos, cos, one], axis=1)
    sk = jnp.concatenate([-sin, sin, 0.0 * one], axis=1)
    cq = jnp.concatenate([jnp.ones((cfg.tp, npe), F32), ck], axis=1)
    sq = jnp.concatenate([jnp.zeros((cfg.tp, npe), F32), sk], axis=1)
    return cq, sq, ck, sk


def mla_prep(cfg, q_raw, kv_raw, proj, tabs, name):
    h, npe, r2, ql, kl = cfg.mla_heads, cfg.nope, cfg.rope // 2, cfg.q_lora, cfg.kv_lora
    hp = 2 * npe

    def body(q_ref, kv_ref, c_ref, cq_ref, sq_ref, ck_ref, sk_ref, qo_ref, ko_ref, vo_ref):
        kr = c_ref[:, ql + kl:ql + kl + npe]
        kr = (kr * ck_ref[...] + _swap_halves(kr, 0, r2) * sk_ref[...]).astype(BF16)
        cq, sq = cq_ref[...], sq_ref[...]
        for i in range(h):
            q = q_ref[:, i * hp:(i + 1) * hp]
            qo_ref[:, i * hp:(i + 1) * hp] = (q * cq + _swap_halves(q, npe, r2) * sq).astype(BF16)
            ko_ref[:, i * hp:i * hp + npe] = kv_ref[:, i * npe:(i + 1) * npe].astype(BF16)
            ko_ref[:, i * hp + npe:(i + 1) * hp] = kr
        vo_ref[...] = kv_ref[:, h * npe:2 * h * npe].astype(BF16)

    cq, sq, ck, sk = tabs
    return _rowwise(cfg, body, [(q_raw, h * hp, 0), (kv_raw, 2 * h * npe, 0), (proj, cfg.cw, cfg.d // cfg.cw),
                                (cq, hp, 0), (sq, hp, 0), (ck, npe, 0), (sk, npe, 0)],
                    [(h * hp, BF16), (h * hp, BF16), (h * npe, BF16)], name)


def mla_prep_bwd(cfg, dq_full, dk_full, tabs, name):
    h, npe, r2 = cfg.mla_heads, cfg.nope, cfg.rope // 2
    hp = 2 * npe

    def body(dq_ref, dk_ref, cq_ref, sq_ref, ck_ref, sk_ref, dqo_ref, dko_ref, dkr_ref):
        cq, sq = cq_ref[...], sq_ref[...]
        dkr = None
        for i in range(h):
            dq = dq_ref[:, i * hp:(i + 1) * hp]
            dqo_ref[:, i * hp:(i + 1) * hp] = (dq * cq + _swap_halves(dq * sq, npe, r2)).astype(BF16)
            dko_ref[:, i * npe:(i + 1) * npe] = dk_ref[:, i * hp:i * hp + npe].astype(BF16)
            part = dk_ref[:, i * hp + npe:(i + 1) * hp]
            dkr = part if dkr is None else dkr + part
        dkr_ref[...] = dkr * ck_ref[...] + _swap_halves(dkr * sk_ref[...], 0, r2)

    cq, sq, ck, sk = tabs
    return _rowwise(cfg, body, [(dq_full, h * hp, 0), (dk_full, h * hp, 0), (cq, hp, 0), (sq, hp, 0), (ck, npe, 0), (sk, npe, 0)],
                    [(h * hp, BF16), (h * npe, BF16), (npe, F32)], name)


def _attn_scores(cfg, q, k, diagonal):
    s = lax.dot_general(q, k, _NT, preferred_element_type=F32) * ((cfg.nope + cfg.rope) ** -0.5)
    if diagonal:
        row = lax.broadcasted_iota(jnp.int32, s.shape, 0)
        col = lax.broadcasted_iota(jnp.int32, s.shape, 1)
        s = jnp.where(row >= col, s, -1e30)
    return s


def _tile_rows(i, t):
    return pl.ds(pl.multiple_of(i * t, t), t)


def attn_fwd(cfg, q, k, v, name):
    h, npe, tq = cfg.mla_heads, cfg.nope, cfg.tq
    hp, n = 2 * npe, cfg.tp // tq

    def body(q_ref, k_ref, v_ref, o_ref, lse_ref):
        qi = pl.program_id(1)
        q = q_ref[...]

        def tile(ki, carry, diagonal):
            m, l, acc = carry
            s = _attn_scores(cfg, q, k_ref[_tile_rows(ki, tq), :], diagonal)
            m_new = jnp.maximum(m, jnp.max(s, axis=-1, keepdims=True))
            alpha = jnp.exp(m - m_new)
            p = jnp.exp(s - m_new)
            l = alpha * l + jnp.sum(p, axis=-1, keepdims=True)
            acc = alpha * acc + lax.dot_general(p.astype(BF16), v_ref[_tile_rows(ki, tq), :], _NN, preferred_element_type=F32)
            return m_new, l, acc

        init = (jnp.full((tq, 1), -1e30, F32), jnp.zeros((tq, 1), F32), jnp.zeros((tq, npe), F32))
        carry = lax.fori_loop(0, qi, lambda ki, c: tile(ki, c, False), init)
        m, l, acc = tile(qi, carry, True)
        o_ref[...] = acc / l
        lse_ref[...] = m + jnp.log(l)

    return pl.pallas_call(
        body, grid=(h, n),
        in_specs=[pl.BlockSpec((tq, hp), lambda hh, qi: (qi, hh)),
                  pl.BlockSpec((cfg.tp, hp), lambda hh, qi: (0, hh)),
                  pl.BlockSpec((cfg.tp, npe), lambda hh, qi: (0, hh))],
        out_specs=[pl.BlockSpec((tq, npe), lambda hh, qi: (qi, hh)),
                   pl.BlockSpec((None, tq, 1), lambda hh, qi: (hh, qi, 0))],
        out_shape=[jax.ShapeDtypeStruct((cfg.tp, h * npe), F32), jax.ShapeDtypeStruct((h, cfg.tp, 1), F32)],
        compiler_params=_cparams("parallel", "arbitrary"), name=name)(q, k, v)


def attn_bwd(cfg, q, k, v, do, o, lse, name):
    h, npe, tq = cfg.mla_heads, cfg.nope, cfg.tq
    hp, n = 2 * npe, cfg.tp // tq
    sc = (cfg.nope + cfg.rope) ** -0.5

    def body(q_ref, k_ref, v_ref, do_ref, o_ref, lse_ref, dq_ref, dk_ref, dv_ref, dv_acc):
        dk_ref[...] = jnp.zeros_like(dk_ref)
        dv_acc[...] = jnp.zeros_like(dv_acc)

        def q_tile(qi, _):
            rq = _tile_rows(qi, tq)
            q, do, lse = q_ref[rq, :], do_ref[rq, :], lse_ref[rq, :]
            delta = jnp.sum(do.astype(F32) * o_ref[rq, :], axis=-1, keepdims=True)

            def tile(ki, dq, diagonal):
                rk = _tile_rows(ki, tq)
                kk = k_ref[rk, :]
                p = jnp.exp(_attn_scores(cfg, q, kk, diagonal) - lse)
                dp = lax.dot_general(do, v_ref[rk, :], _NT, preferred_element_type=F32)
                ds = (p * (dp - delta) * sc).astype(BF16)
                dv_acc[rk, :] += lax.dot_general(p.astype(BF16), do, _TN, preferred_element_type=F32)
                dk_ref[rk, :] += lax.dot_general(ds, q, _TN, preferred_element_type=F32)
                return dq + lax.dot_general(ds, kk, _NN, preferred_element_type=F32)

            dq = lax.fori_loop(0, qi, lambda ki, acc: tile(ki, acc, False), jnp.zeros((tq, hp), F32))
            dq_ref[rq, :] = tile(qi, dq, True)
            return 0

        lax.fori_loop(0, n, q_tile, 0)
        dv_ref[...] = dv_acc[...].astype(BF16)

    wide = pl.BlockSpec((cfg.tp, hp), lambda hh: (0, hh))
    narrow = pl.BlockSpec((cfg.tp, npe), lambda hh: (0, hh))
    return pl.pallas_call(
        body, grid=(h,),
        in_specs=[wide, wide, narrow, narrow, narrow, pl.BlockSpec((None, cfg.tp, 1), lambda hh: (hh, 0, 0))],
        out_specs=[wide, wide, narrow],
        out_shape=[jax.ShapeDtypeStruct((cfg.tp, h * hp), F32), jax.ShapeDtypeStruct((cfg.tp, h * hp), F32),
                   jax.ShapeDtypeStruct((cfg.tp, h * npe), BF16)],
        scratch_shapes=[pltpu.VMEM((cfg.tp, npe), F32)],
        compiler_params=_cparams("parallel"), name=name)(q, k, v, do, o, lse)


def _ret_tables(cfg):
    c = cfg.chunk
    lg = jnp.log(1.0 - 2.0 ** (-5.0 - jnp.arange(cfg.ret_heads, dtype=F32)))[:, None, None]
    j = jnp.arange(c, dtype=F32)
    diff = j[:, None] - j[None, :]
    dm = jnp.where(diff >= 0, jnp.exp(jnp.maximum(diff, 0.0)[None] * lg), 0.0)
    lq = jnp.exp((j + 1.0)[None, :, None] * lg)
    lk = jnp.exp((c - 1.0 - j)[None, :, None] * lg)
    gc = jnp.exp(c * lg)
    return dm, lq, lk, gc


def _ret_rope_tables(cfg):
    dk = cfg.d // cfg.ret_heads
    inv = ROPE_BASE ** (-jnp.arange(0, dk, 2, dtype=F32) / dk)
    ang = jnp.arange(cfg.tp, dtype=F32)[:, None] * inv[None, :]
    return jnp.cos(ang), jnp.sin(ang)


def ret_prep(cfg, qkvg, cos, sin, name):
    d, h = cfg.d, cfg.ret_heads
    dk = d // h
    hd = dk // 2
    ksc = dk ** -0.5

    def body(q_ref, k_ref, v_ref, c_ref, s_ref, qo_ref, ko_ref, vo_ref):
        c, s = c_ref[...], s_ref[...]
        for i in range(h):
            for src, dst, f in ((q_ref, qo_ref, 1.0), (k_ref, ko_ref, ksc)):
                x1 = src[:, i * dk:i * dk + hd]
                x2 = src[:, i * dk + hd:(i + 1) * dk]
                dst[:, i * dk:i * dk + hd] = ((x1 * c - x2 * s) * f).astype(BF16)
                dst[:, i * dk + hd:(i + 1) * dk] = ((x2 * c + x1 * s) * f).astype(BF16)
        vo_ref[...] = v_ref[...].astype(BF16)

    return _rowwise(cfg, body, [(qkvg, d, 0), (qkvg, d, 1), (qkvg, d, 2), (cos, hd, 0), (sin, hd, 0)],
                    [(d, BF16), (d, BF16), (d, BF16)], name)


def ret_prep_bwd(cfg, dq, dk_, dv, dg, cos, sin, name):
    d, h = cfg.d, cfg.ret_heads
    dk = d // h
    hd = dk // 2
    ksc = dk ** -0.5

    def body(dq_ref, dk_ref, dv_ref, dg_ref, c_ref, s_ref, o_ref):
        c, s = c_ref[...], s_ref[...]
        for i in range(h):
            for src, off, f in ((dq_ref, 0, 1.0), (dk_ref, d, ksc)):
                y1 = src[:, i * dk:i * dk + hd]
                y2 = src[:, i * dk + hd:(i + 1) * dk]
                o_ref[:, off + i * dk:off + i * dk + hd] = ((y1 * c + y2 * s) * f).astype(BF16)
                o_ref[:, off + i * dk + hd:off + (i + 1) * dk] = ((y2 * c - y1 * s) * f).astype(BF16)
        o_ref[:, 2 * d:3 * d] = dv_ref[...]
        o_ref[:, 3 * d:4 * d] = dg_ref[...]

    return _rowwise(cfg, body, [(dq, d, 0), (dk_, d, 0), (dv, d, 0), (dg, d, 0), (cos, hd, 0), (sin, hd, 0)],
                    [(4 * d, BF16)], name)[0]


def _ret_specs(cfg, reverse):
    c, h = cfg.chunk, cfg.ret_heads
    dk = cfg.d // h
    n = cfg.tp // c
    tm = (lambda t: n - 1 - t) if reverse else (lambda t: t)
    blk = pl.BlockSpec((c, dk), lambda hh, t: (tm(t), hh))
    row = pl.BlockSpec((None, c, 1), lambda hh, t: (hh, tm(t), 0))
    tabs = [pl.BlockSpec((None, c, c), lambda hh, t: (hh, 0, 0)), pl.BlockSpec((None, c, 1), lambda hh, t: (hh, 0, 0)),
            pl.BlockSpec((None, c, 1), lambda hh, t: (hh, 0, 0)), pl.BlockSpec((None, 1, 1), lambda hh, t: (hh, 0, 0))]
    return blk, row, tabs, n, dk


def _ret_state_update(s_ref, k, v, lk, gc):
    kw = (k.astype(F32) * lk).astype(BF16)
    s_ref[...] = gc * s_ref[...] + lax.dot_general(kw, v, _TN, preferred_element_type=F32)


def ret_fwd(cfg, q, k, v, tabs, name):
    blk, row, tspecs, n, dk = _ret_specs(cfg, False)

    def body(q_ref, k_ref, v_ref, dm_ref, lq_ref, lk_ref, gc_ref, on_ref, rs_ref, s_ref):
        @pl.when(pl.program_id(1) == 0)
        def _():
            s_ref[...] = jnp.zeros_like(s_ref)

        qq, kk, vv = q_ref[...], k_ref[...], v_ref[...]
        a = lax.dot_general(qq, kk, _NT, preferred_element_type=F32) * dm_ref[...]
        o = lax.dot_general(a.astype(BF16), vv, _NN, preferred_element_type=F32)
        o = o + lax.dot_general(qq, s_ref[...].astype(BF16), _NN, preferred_element_type=F32) * lq_ref[...]
        _ret_state_update(s_ref, kk, vv, lk_ref[...], gc_ref[...])
        mu = jnp.mean(o, axis=-1, keepdims=True)
        oc = o - mu
        rstd = lax.rsqrt(jnp.mean(oc * oc, axis=-1, keepdims=True) + LN_EPS)
        on_ref[...] = oc * rstd
        rs_ref[...] = rstd

    return pl.pallas_call(body, grid=(cfg.ret_heads, n), in_specs=[blk, blk, blk] + tspecs, out_specs=[blk, row],
                          out_shape=[jax.ShapeDtypeStruct((cfg.tp, cfg.d), F32), jax.ShapeDtypeStruct((cfg.ret_heads, cfg.tp, 1), F32)],
                          scratch_shapes=[pltpu.VMEM((dk, dk), F32)], compiler_params=_cparams("parallel", "arbitrary"),
                          name=name)(q, k, v, *tabs)


def ret_bwd_q(cfg, q, k, v, d_on, on, rstd, tabs, name):
    blk, row, tspecs, n, dk = _ret_specs(cfg, False)

    def body(q_ref, k_ref, v_ref, don_ref, on_ref, rs_ref, dm_ref, lq_ref, lk_ref, gc_ref, do_ref, dq_ref, s_ref):
        @pl.when(pl.program_id(1) == 0)
        def _():
            s_ref[...] = jnp.zeros_like(s_ref)

        kk, vv = k_ref[...], v_ref[...]
        don, on = don_ref[...], on_ref[...]
        do = rs_ref[...] * (don - jnp.mean(don, axis=-1, keepdims=True) - on * jnp.mean(don * on, axis=-1, keepdims=True))
        dob = do.astype(BF16)
        do_ref[...] = dob
        ds = lax.dot_general(dob, vv, _NT, preferred_element_type=F32) * dm_ref[...]
        dq = lax.dot_general(ds.astype(BF16), kk, _NN, preferred_element_type=F32)
        dq_ref[...] = dq + lax.dot_general(dob, s_ref[...].astype(BF16), _NT, preferred_element_type=F32) * lq_ref[...]
        _ret_state_update(s_ref, kk, vv, lk_ref[...], gc_ref[...])

    return pl.pallas_call(body, grid=(cfg.ret_heads, n), in_specs=[blk, blk, blk, blk, blk, row] + tspecs, out_specs=[blk, blk],
                          out_shape=[jax.ShapeDtypeStruct((cfg.tp, cfg.d), BF16), jax.ShapeDtypeStruct((cfg.tp, cfg.d), F32)],
                          scratch_shapes=[pltpu.VMEM((dk, dk), F32)], compiler_params=_cparams("parallel", "arbitrary"),
                          name=name)(q, k, v, d_on, on, rstd, *tabs)


def ret_bwd_kv(cfg, q, k, v, do, tabs, name):
    blk, row, tspecs, n, dk = _ret_specs(cfg, True)

    def body(q_ref, k_ref, v_ref, do_ref, dm_ref, lq_ref, lk_ref, gc_ref, dk_ref, dv_ref, g_ref):
        @pl.when(pl.program_id(1) == 0)
        def _():
            g_ref[...] = jnp.zeros_like(g_ref)

        qq, kk, vv, dob = q_ref[...], k_ref[...], v_ref[...], do_ref[...]
        dm, lk = dm_ref[...], lk_ref[...]
        gb = g_ref[...].astype(BF16)
        a = lax.dot_general(qq, kk, _NT, preferred_element_type=F32) * dm
        ds = lax.dot_general(dob, vv, _NT, preferred_element_type=F32) * dm
        dkk = lax.dot_general(ds.astype(BF16), qq, _TN, preferred_element_type=F32)
        dk_ref[...] = dkk + lax.dot_general(vv, gb, _NT, preferred_element_type=F32) * lk
        kw = (kk.astype(F32) * lk).astype(BF16)
        dvv = lax.dot_general(a.astype(BF16), dob, _TN, preferred_element_type=F32)
        dv_ref[...] = (dvv + lax.dot_general(kw, gb, _NN, preferred_element_type=F32)).astype(BF16)
        qw = (qq.astype(F32) * lq_ref[...]).astype(BF16)
        g_ref[...] = gc_ref[...] * g_ref[...] + lax.dot_general(qw, dob, _TN, preferred_element_type=F32)

    return pl.pallas_call(body, grid=(cfg.ret_heads, n), in_specs=[blk, blk, blk, blk] + tspecs, out_specs=[blk, blk],
                          out_shape=[jax.ShapeDtypeStruct((cfg.tp, cfg.d), F32), jax.ShapeDtypeStruct((cfg.tp, cfg.d), BF16)],
                          scratch_shapes=[pltpu.VMEM((dk, dk), F32)], compiler_params=_cparams("parallel", "arbitrary"),
                          name=name)(q, k, v, do, *tabs)


def _flat2(a):
    return a.reshape(-1, a.shape[-1])


def _ew_call(body, ins, out_dtypes, name, lead=None):
    r, c = ins[-1].shape[-2:]
    tr = _div_tile(r, max(16, (1 << 18) // c), 16)
    specs = []
    for a in ins:
        if a.ndim == 3:
            specs.append(pl.BlockSpec((a.shape[0], tr, c), lambda i: (0, i, 0)))
        else:
            specs.append(pl.BlockSpec((tr, c), lambda i: (i, 0)))
    return pl.pallas_call(body, grid=(r // tr,), in_specs=specs,
                          out_specs=[pl.BlockSpec((tr, c), lambda i: (i, 0)) for _ in out_dtypes],
                          out_shape=[jax.ShapeDtypeStruct((r, c), dt) for dt in out_dtypes],
                          compiler_params=_cparams("parallel"), name=name)(*ins)


def adamw(w, g, m, v, name):
    c1 = 1.0 - ADAM_B1 ** ADAM_STEP
    c2 = 1.0 - ADAM_B2 ** ADAM_STEP

    def body(w_ref, g_ref, m_ref, v_ref, d_ref, mo_ref, vo_ref):
        gg = g_ref[...]
        mn = ADAM_B1 * m_ref[...] + (1.0 - ADAM_B1) * gg
        vn = ADAM_B2 * v_ref[...] + (1.0 - ADAM_B2) * (gg * gg)
        d_ref[...] = -ADAM_LR * ((mn / c1) / (jnp.sqrt(vn / c2) + ADAM_EPS) + ADAM_WD * w_ref[...])
        mo_ref[...] = mn
        vo_ref[...] = vn

    outs = _ew_call(body, [_flat2(w), _flat2(g), _flat2(m), _flat2(v)], [F32, F32, F32], name)
    return [o.reshape(w.shape) for o in outs]


def add_pair_bf16(g, theirs, core, name):
    c = g.shape[-1]
    g4, t3 = g.reshape(4, 2, -1, c), theirs.reshape(4, -1, c)
    rows = t3.shape[1]
    tr = _div_tile(rows, max(16, (1 << 18) // c), 16)

    def body(core_ref, g_ref, t_ref, o_ref):
        o_ref[...] = (g_ref[...] + t_ref[...]).astype(BF16)

    spec = pl.BlockSpec((None, tr, c), lambda q, i, core_ref: (q, i, 0))
    grid_spec = pltpu.PrefetchScalarGridSpec(
        num_scalar_prefetch=1, grid=(4, rows // tr),
        in_specs=[pl.BlockSpec((None, None, tr, c), lambda q, i, core_ref: (q, core_ref[0], i, 0)), spec], out_specs=spec)
    out = pl.pallas_call(body, grid_spec=grid_spec, out_shape=jax.ShapeDtypeStruct((4, rows, c), BF16),
                         compiler_params=_cparams("parallel", "parallel"), name=name)(core, g4, t3)
    return out.reshape(theirs.shape)


def sum_slots(parts, landed, chip_core, name):
    c = parts.shape[-1]
    p3, l3 = parts.reshape(4, -1, c), landed.reshape(3, -1, c)
    rows = p3.shape[1]
    tr = _div_tile(rows, max(16, (1 << 18) // c), 16)

    def body(cc_ref, p_ref, l_ref, o_ref):
        acc = p_ref[...].astype(F32)
        for s in range(3):
            acc = acc + l_ref[s].astype(F32)
        o_ref[...] = acc

    grid_spec = pltpu.PrefetchScalarGridSpec(
        num_scalar_prefetch=1, grid=(rows // tr,),
        in_specs=[pl.BlockSpec((None, tr, c), lambda i, cc: (cc[0], i, 0)), pl.BlockSpec((3, tr, c), lambda i, cc: (0, i, 0))],
        out_specs=pl.BlockSpec((None, tr, c), lambda i, cc: (cc[1], i, 0)))
    out = pl.pallas_call(body, grid_spec=grid_spec, out_shape=jax.ShapeDtypeStruct((2, rows, c), F32),
                         compiler_params=_cparams("parallel"), name=name)(chip_core, p3, l3)
    return out.reshape((2,) + parts.shape[2:])


def cast_place(w, chip, name):
    c = w.shape[-1]
    w2 = w.reshape(-1, c)
    rows = w2.shape[0]
    tr = _div_tile(rows, max(16, (1 << 18) // c), 16)

    def body(chip_ref, w_ref, o_ref):
        o_ref[...] = w_ref[...].astype(BF16)

    grid_spec = pltpu.PrefetchScalarGridSpec(
        num_scalar_prefetch=1, grid=(rows // tr,),
        in_specs=[pl.BlockSpec((tr, c), lambda i, chip_ref: (i, 0))],
        out_specs=pl.BlockSpec((None, tr, c), lambda i, chip_ref: (chip_ref[0], i, 0)))
    out = pl.pallas_call(body, grid_spec=grid_spec, out_shape=jax.ShapeDtypeStruct((4, rows, c), BF16),
                         compiler_params=_cparams("parallel"), name=name)(chip, w2)
    return out.reshape((4,) + w.shape)


def _coords():
    return lax.axis_index("x"), lax.axis_index("y"), lax.axis_index("c")


def _chip_peer(x, y, k):
    return (x ^ (k >> 1), y ^ (k & 1))


_ANY = pl.BlockSpec(memory_space=pl.ANY)
DMA_PIECE_BYTES = 2 << 20
DMA_MAX_PIECES = 32


def _piece_plan(shape, itemsize):
    want = max(1, min(DMA_MAX_PIECES, math.prod(shape) * itemsize // DMA_PIECE_BYTES))
    plan = []
    for ax, n in enumerate(shape[:-1]):
        if want <= 1:
            break
        rows_tiled = ax == len(shape) - 2
        k = max([1] + [c for c in range(2, min(n, want) + 1) if n % c == 0 and (not rows_tiled or (n // c) % 16 == 0)])
        if k > 1:
            plan.append((ax, k))
            want = -(-want // k)
    return plan


class _Copy:
    def __init__(self, src, dst, send_sem, recv_sem=None, device_id=None):
        self.src, self.dst, self.send_sem, self.recv_sem, self.device_id = src, dst, send_sem, recv_sem, device_id

    def _dma(self, src, dst):
        if self.device_id is None:
            return pltpu.make_async_copy(src, dst, self.send_sem)
        return pltpu.make_async_remote_copy(src_ref=src, dst_ref=dst, send_sem=self.send_sem, recv_sem=self.recv_sem,
                                            device_id=self.device_id, device_id_type=MESH)

    def start(self):
        shape = self.src.shape
        plan = _piece_plan(shape, jnp.dtype(self.src.dtype).itemsize)
        for pick in np.ndindex(*[k for _, k in plan]):
            idx = [slice(None)] * len(shape)
            for (ax, k), i in zip(plan, pick):
                step = shape[ax] // k
                idx[ax] = pl.ds(i * step, step)
            idx = tuple(idx)
            self._dma(self.src.at[idx], self.dst.at[idx]).start()

    def wait(self):
        self._dma(self.src, self.dst).wait()

    def wait_send(self):
        self._dma(self.src, self.dst).wait_send()

    def wait_recv(self):
        self._dma(self.src, self.dst).wait_recv()


def allreduce_small(x, name):
    r, _, w = x.shape

    def body(x_ref, o_ref, mine_ref, gat_ref, send_sems, recv_sems):
        mx, my, mc = _coords()
        me = 4 * mx + 2 * my + mc
        mine_ref[...] = jnp.sum(x_ref[...], axis=1)
        gat_ref[me] = mine_ref[...]
        copies = []
        for k in range(1, 8):
            peer = (mx ^ (k >> 2), my ^ ((k >> 1) & 1), mc ^ (k & 1))
            cp = pltpu.make_async_remote_copy(src_ref=mine_ref, dst_ref=gat_ref.at[me], send_sem=send_sems.at[k - 1],
                                              recv_sem=recv_sems.at[k - 1], device_id=peer, device_id_type=MESH)
            cp.start()
            copies.append(cp)
        for k in range(1, 8):
            pltpu.make_async_remote_copy(src_ref=mine_ref, dst_ref=gat_ref.at[me ^ k], send_sem=send_sems.at[k - 1],
                                         recv_sem=recv_sems.at[k - 1], device_id=(mx, my, mc), device_id_type=MESH).wait_recv()
        for cp in copies:
            cp.wait_send()
        acc = gat_ref[0]
        for s in range(1, 8):
            acc = acc + gat_ref[s]
        o_ref[...] = acc

    return pl.pallas_call(body, out_shape=jax.ShapeDtypeStruct((r, w), F32),
                          in_specs=[pl.BlockSpec(memory_space=pltpu.VMEM)], out_specs=pl.BlockSpec(memory_space=pltpu.VMEM),
                          scratch_shapes=[pltpu.VMEM((r, w), F32), pltpu.VMEM((8, r, w), F32),
                                          pltpu.SemaphoreType.DMA((7,)), pltpu.SemaphoreType.DMA((7,))],
                          compiler_params=pltpu.CompilerParams(vmem_limit_bytes=VMEM_LIMIT_V7X), name=name)(x)


def allgather_chips(shards, name):
    n = len(shards)

    def body(*refs):
        outs = refs[n:2 * n]
        s1_send, s1_recv, s2_send, s2_recv = refs[2 * n:]
        mx, my, mc = _coords()
        p = 2 * mx + my
        sib = (mx, my, 1 - mc)
        sends = []
        for a in range(n):
            h0 = outs[a].shape[1] // 2
            mine = outs[a].at[p, pl.ds(mc * h0, h0)]
            for k in (1, 2, 3):
                qx, qy = _chip_peer(mx, my, k)
                cp = _Copy(mine, mine, s1_send.at[a, k - 1], s1_recv.at[a, k - 1], (qx, qy, mc))
                cp.start()
                sends.append(cp)
        for a in range(n):
            h0 = outs[a].shape[1] // 2
            half = pl.ds(mc * h0, h0)
            for k in (1, 2, 3):
                landed = outs[a].at[p ^ k, half]
                _Copy(landed, landed, s1_send.at[a, k - 1], s1_recv.at[a, k - 1], sib).wait_recv()
                fw = _Copy(landed, landed, s2_send.at[a, k - 1], s2_recv.at[a, k - 1], sib)
                fw.start()
                sends.append(fw)
        for a in range(n):
            h0 = outs[a].shape[1] // 2
            other = pl.ds((1 - mc) * h0, h0)
            for k in (1, 2, 3):
                theirs = outs[a].at[p ^ k, other]
                _Copy(theirs, theirs, s2_send.at[a, k - 1], s2_recv.at[a, k - 1], sib).wait_recv()
        for cp in sends:
            cp.wait_send()

    return pl.pallas_call(body, out_shape=[jax.ShapeDtypeStruct(s.shape, s.dtype) for s in shards],
                          in_specs=[_ANY] * n, out_specs=[_ANY] * n, input_output_aliases={a: a for a in range(n)},
                          scratch_shapes=[pltpu.SemaphoreType.DMA((n, 3)), pltpu.SemaphoreType.DMA((n, 3)),
                                          pltpu.SemaphoreType.DMA((n, 3)), pltpu.SemaphoreType.DMA((n, 3))],
                          name=name)(*shards)


def exchange_halves(grads, name):
    n = len(grads)

    def body(*refs):
        ins, theirs = refs[:n], refs[n:2 * n]
        send_sems, recv_sems = refs[2 * n:]
        mx, my, mc = _coords()
        pend = []
        for a in range(n):
            cp = _Copy(ins[a].at[:, pl.ds(1 - mc, 1)], theirs[a], send_sems.at[a], recv_sems.at[a], (mx, my, 1 - mc))
            cp.start()
            pend.append(cp)
        for cp in pend:
            cp.wait()

    shp = [jax.ShapeDtypeStruct((4, 1) + g.shape[2:], g.dtype) for g in grads]
    return pl.pallas_call(body, out_shape=shp, in_specs=[_ANY] * n, out_specs=[_ANY] * n,
                          scratch_shapes=[pltpu.SemaphoreType.DMA((n,)), pltpu.SemaphoreType.DMA((n,))], name=name)(*grads)


def scatter_chips(parts, name):
    n = len(parts)

    def body(*refs):
        ins, outs = refs[:n], refs[n:2 * n]
        send_sems, recv_sems = refs[2 * n:]
        mx, my, mc = _coords()
        p = 2 * mx + my
        pend = []
        for a in range(n):
            for k in (1, 2, 3):
                qx, qy = _chip_peer(mx, my, k)
                cp = _Copy(ins[a].at[p ^ k], outs[a].at[k - 1], send_sems.at[a, k - 1], recv_sems.at[a, k - 1], (qx, qy, mc))
                cp.start()
                pend.append(cp)
        for cp in pend:
            cp.wait()

    return pl.pallas_call(body, out_shape=[jax.ShapeDtypeStruct((3,) + g.shape[1:], g.dtype) for g in parts], in_specs=[_ANY] * n,
                          out_specs=[_ANY] * n,
                          scratch_shapes=[pltpu.SemaphoreType.DMA((n, 3)), pltpu.SemaphoreType.DMA((n, 3))], name=name)(*parts)


def join_halves(halves, name):
    n = len(halves)

    def body(*refs):
        outs = refs[n:2 * n]
        send_sems, recv_sems = refs[2 * n:]
        mx, my, mc = _coords()
        pend = []
        for a in range(n):
            mine = outs[a].at[pl.ds(mc, 1)]
            cp = _Copy(mine, mine, send_sems.at[a], recv_sems.at[a], (mx, my, 1 - mc))
            cp.start()
            pend.append((cp, outs[a].at[pl.ds(1 - mc, 1)]))
        for a, (cp, theirs) in enumerate(pend):
            cp.wait_send()
            _Copy(theirs, theirs, send_sems.at[a], recv_sems.at[a], (mx, my, 1 - mc)).wait_recv()

    return pl.pallas_call(body, out_shape=[jax.ShapeDtypeStruct(g.shape, g.dtype) for g in halves],
                          in_specs=[_ANY] * n, out_specs=[_ANY] * n, input_output_aliases={a: a for a in range(n)},
                          scratch_shapes=[pltpu.SemaphoreType.DMA((n,)), pltpu.SemaphoreType.DMA((n,))], name=name)(*halves)


def reduce_scatter_grads(grads, tag):
    theirs = exchange_halves(grads, f"rs_pair_{tag}")
    core = lax.axis_index("c").astype(jnp.int32).reshape(1)
    parts = [add_pair_bf16(g, t, core, f"rs_add_{tag}_{i}") for i, (g, t) in enumerate(zip(grads, theirs))]
    landed = scatter_chips(parts, f"rs_chips_{tag}")
    chip_core = jnp.stack([2 * lax.axis_index("x") + lax.axis_index("y"), lax.axis_index("c")]).astype(jnp.int32)
    halves = [sum_slots(pt, ld, chip_core, f"rs_sum_{tag}_{i}") for i, (pt, ld) in enumerate(zip(parts, landed))]
    return join_halves(halves, f"rs_join_{tag}")


W_NAMES = ['meta_tokens', 'l0_w_in', 'l0_conv_w', 'l0_conv_b', 'l0_w_a', 'l0_b_a', 'l0_w_x', 'l0_b_x', 'l0_lam', 'l0_w_out',
           'l0_ln_g', 'l0_ln_b', 'l1_w_in', 'l1_w_grp', 'l1_scale', 'l1_w_out', 'l1_ln_g', 'l1_ln_b', 'l2_w_in', 'l2_q_norm',
           'l2_w_uq', 'l2_kv_norm', 'l2_w_ukv', 'l2_w_out', 'l2_ln_g', 'l2_ln_b', 'l3_w_in', 'l3_w_out', 'l3_ln_g', 'l3_ln_b']
BIG = {0: ['l0_w_in', 'l0_w_a', 'l0_w_x', 'l0_w_out'], 1: ['l1_w_in', 'l1_w_grp', 'l1_w_out'],
       2: ['l2_w_in', 'l2_w_uq', 'l2_w_ukv', 'l2_w_out'], 3: ['l3_w_in', 'l3_w_out']}
SMALL_SHARDED = ['meta_tokens', 'l0_conv_w']
SMALL_REPL = ['l0_conv_b', 'l0_b_a', 'l0_b_x', 'l0_lam', 'l0_ln_g', 'l0_ln_b', 'l1_scale', 'l1_ln_g', 'l1_ln_b',
              'l2_q_norm', 'l2_kv_norm', 'l2_ln_g', 'l2_ln_b', 'l3_ln_g', 'l3_ln_b']


def _rows8(rows, width):
    out = []
    for r in rows:
        r = r.reshape(-1, r.shape[-1])
        out.append(jnp.pad(r, ((0, 8 - r.shape[0]), (0, width - r.shape[1]))))
    return jnp.stack(out)


def _unblock(g4, axis):
    return jnp.concatenate([g4[i] for i in range(4)], axis=axis)


def _block(full, axis):
    return jnp.stack(jnp.split(full, 4, axis=axis))


def _step(cfg, a):
    d, tp, nm, seq = cfg.d, cfg.tp, cfg.n_meta, cfg.seq
    alpha = (2.0 * cfg.depth) ** 0.25
    mx, my, mc = _coords()
    p = 2 * mx + my
    dq4 = d // 4
    hm, npe, ql, kl = cfg.mla_heads, cfg.nope, cfg.q_lora, cfg.kv_lora
    hp = 2 * npe
    qk = npe + cfg.rope
    vec = lambda name: a[name].reshape(1, -1)

    sm = jnp.concatenate([a['meta_tokens'], a['l0_conv_w'].reshape(CONV_W, dq4)], axis=0)
    placed = lax.dynamic_update_slice(jnp.zeros((nm + CONV_W, d), F32), sm * (mc == 0).astype(F32), (0, p * dq4))
    gathered = allreduce_small(_rows8([placed[i:i + 1] for i in range(nm + CONV_W)], d), "gather_small")
    meta_full, conv_w4 = gathered[:nm], gathered[nm:]

    wg = {}
    for layer, names in BIG.items():
        chip = p.astype(jnp.int32).reshape(1)
        got = allgather_chips([cast_place(a[n], chip, f"place_{n}") for n in names], f"ag_l{layer}")
        wg.update(dict(zip(names, got)))
    heads_first = lambda w: jnp.moveaxis(w, 0, 1).reshape(w.shape[1], w.shape[0] * w.shape[2], w.shape[3])
    w0_in, w0_a, w0_x = wg['l0_w_in'], heads_first(wg['l0_w_a']), heads_first(wg['l0_w_x'])
    w1_in, w1_grp = wg['l1_w_in'], heads_first(wg['l1_w_grp'])
    w_out = [wg[f'l{i}_w_out'].reshape(1, d, d) for i in range(4)]
    w2_in = jnp.pad(_unblock(wg['l2_w_in'], 1), ((0, 0), (0, cfg.cw - (ql + kl + cfg.rope))))[None]
    w2_uq = jnp.pad(_unblock(wg['l2_w_uq'], 1).reshape(ql, hm, qk), ((0, 0), (0, 0), (0, hp - qk))).reshape(1, ql, hm * hp)
    w2_ukv = _unblock(wg['l2_w_ukv'], 1).reshape(kl, hm, 2, npe).transpose(0, 2, 1, 3).reshape(1, kl, 2 * hm * npe)
    w3_in = wg['l3_w_in']

    zpad = jnp.zeros((tp - nm - seq, d), F32)
    h0 = jnp.concatenate([meta_full, a['x'][0], zpad], axis=0)
    h0_bf = h0.astype(BF16)

    ug0 = mm_nn(cfg, h0_bf, w0_in, "l0_in")
    u0, u0_bf = conv_fwd(cfg, ug0, conv_w4, vec('l0_conv_b'), "l0_conv")
    pa0 = bd_nn(cfg, u0_bf, w0_a, "l0_gate_a")
    px0 = bd_nn(cfg, u0_bf, w0_x, "l0_gate_x")
    hs0, hprev0, y0 = lru_fwd(cfg, u0, pa0, px0, ug0, vec('l0_b_a'), vec('l0_b_x'), vec('l0_lam'), "l0_lru")
    o0 = mm_nn(cfg, y0, w_out[0], "l0_out")
    h1, h1_bf, xh0, rs0 = ln_fwd(cfg, h0, o0, vec('l0_ln_g'), vec('l0_ln_b'), alpha, "l0_ln")

    ug1 = mm_nn(cfg, h1_bf, w1_in, "l1_in")
    p1 = pool_fwd(cfg, ug1, "l1_pool")
    mm1 = bd_nn(cfg, p1, w1_grp, "l1_grp")
    y1 = gate_fwd(cfg, mm1, ug1, 1, vec('l1_scale'), "l1_gate")
    o1 = mm_nn(cfg, y1, w_out[1], "l1_out")
    h2, h2_bf, xh1, rs1 = ln_fwd(cfg, h1, o1, vec('l1_ln_g'), vec('l1_ln_b'), alpha, "l1_ln")

    proj = mm_nn(cfg, h2_bf, w2_in, "l2_in")
    cqn, ckvn = mla_norm(cfg, proj, vec('l2_q_norm'), vec('l2_kv_norm'), "l2_norm")
    q_raw = mm_nn(cfg, cqn, w2_uq, "l2_uq")
    kv_raw = mm_nn(cfg, ckvn, w2_ukv, "l2_ukv")
    tabs = _mla_tables(cfg)
    qf, kf, vb = mla_prep(cfg, q_raw, kv_raw, proj, tabs, "l2_prep")
    o_att, lse = attn_fwd(cfg, qf, kf, vb, "l2_attn")
    y2 = gate_fwd(cfg, o_att, proj, 0, None, "l2_gate")
    o2 = mm_nn(cfg, y2, w_out[2], "l2_out")
    h3, h3_bf, xh2, rs2 = ln_fwd(cfg, h2, o2, vec('l2_ln_g'), vec('l2_ln_b'), alpha, "l2_ln")

    qkvg = mm_nn(cfg, h3_bf, w3_in, "l3_in")
    cos, sin = _ret_rope_tables(cfg)
    qr, kr, vr = ret_prep(cfg, qkvg, cos, sin, "l3_prep")
    rt = _ret_tables(cfg)
    on, rsr = ret_fwd(cfg, qr, kr, vr, rt, "l3_ret")
    y3 = gate_fwd(cfg, on, qkvg, 3, None, "l3_gate")
    o3 = mm_nn(cfg, y3, w_out[3], "l3_out")
    h4, _, xh3, rs3 = ln_fwd(cfg, h3, o3, vec('l3_ln_g'), vec('l3_ln_b'), alpha, "l3_ln")

    tgt = jnp.concatenate([jnp.zeros((nm, d), F32), a['loss_target'][0], zpad], axis=0)
    dy4, lacc = loss_grad(cfg, h4, tgt, "loss")
    loss = lax.psum(0.5 * jnp.sum(lacc) / d, ("x", "y", "c"))

    grads, small = {}, {}

    def big_rs(layer, full):
        names = BIG[layer]
        got = reduce_scatter_grads([g.reshape((4, 2, g.shape[1] // 2) + g.shape[2:]) for g in full], f"l{layer}")
        for n, g in zip(names, got):
            grads[n] = g.reshape(a[n].shape)

    dz3, dz3_bf, small['l3_ln_g'], small['l3_ln_b'] = ln_bwd(cfg, None, dy4, xh3, rs3, vec('l3_ln_g'), alpha, "l3_ln_b")
    g_out = mm_tn(cfg, y3, dz3_bf, 1, "l3_dw_out")
    dyb = mm_nt(cfg, dz3_bf, w_out[3], "l3_dy")
    d_on, dg = gate_bwd(cfg, dyb, on, qkvg, 3, None, F32, "l3_gate_b")
    do_bf, dqr = ret_bwd_q(cfg, qr, kr, vr, d_on, on, rsr, rt, "l3_ret_bq")
    dkr, dvr = ret_bwd_kv(cfg, qr, kr, vr, do_bf, rt, "l3_ret_bkv")
    d_in = ret_prep_bwd(cfg, dqr, dkr, dvr, dg, cos, sin, "l3_prep_b")
    g_in = mm_tn(cfg, h3_bf, d_in, 4, "l3_dw_in")
    dh = mm_nt(cfg, d_in, w3_in, "l3_dh")
    big_rs(3, [g_in, g_out.reshape(4, dq4, d)])

    dz2, dz2_bf, small['l2_ln_g'], small['l2_ln_b'] = ln_bwd(cfg, dz3, dh, xh2, rs2, vec('l2_ln_g'), alpha, "l2_ln_b")
    g_out = mm_tn(cfg, y2, dz2_bf, 1, "l2_dw_out")
    dyb = mm_nt(cfg, dz2_bf, w_out[2], "l2_dy")
    do_bf, dg = gate_bwd(cfg, dyb, o_att, proj, 0, None, BF16, "l2_gate_b")
    dq_full, dk_full, dv_bf = attn_bwd(cfg, qf, kf, vb, do_bf, o_att, lse, "l2_attn_b")
    dq_raw, dk_nope, d_kr = mla_prep_bwd(cfg, dq_full, dk_full, tabs, "l2_prep_b")
    dkv_raw = jnp.concatenate([dk_nope, dv_bf], axis=1)
    g_uq = mm_tn(cfg, cqn, dq_raw, 1, "l2_dw_uq")
    d_cqn = mm_nt(cfg, dq_raw, w2_uq, "l2_dcq")
    g_ukv = mm_tn(cfg, ckvn, dkv_raw, 1, "l2_dw_ukv")
    d_ckvn = mm_nt(cfg, dkv_raw, w2_ukv, "l2_dckv")
    d_c, nacc = mla_norm_bwd(cfg, proj, d_cqn, d_ckvn, d_kr, vec('l2_q_norm'), vec('l2_kv_norm'), "l2_norm_b")
    small['l2_q_norm'], small['l2_kv_norm'] = nacc[:, :ql], nacc[:, ql:ql + kl]
    d_in = jnp.concatenate([dg, d_c], axis=1)
    g_in = mm_tn(cfg, h2_bf, d_in, 1, "l2_dw_in")
    dh = mm_nt(cfg, d_in, w2_in, "l2_dh")
    g_in = _block(g_in[0][:, :d + ql + kl + cfg.rope], 1)
    g_uq = _block(g_uq.reshape(ql, hm, hp)[:, :, :qk].reshape(ql, hm * qk), 1)
    g_ukv = _block(g_ukv.reshape(kl, 2, hm, npe).transpose(0, 2, 1, 3).reshape(kl, 2 * hm * npe), 1)
    big_rs(2, [g_in, g_uq, g_ukv, g_out.reshape(4, dq4, d)])

    dz1, dz1_bf, small['l1_ln_g'], small['l1_ln_b'] = ln_bwd(cfg, dz2, dh, xh1, rs1, vec('l1_ln_g'), alpha, "l1_ln_b")
    g_out = mm_tn(cfg, y1, dz1_bf, 1, "l1_dw_out")
    dyb = mm_nt(cfg, dz1_bf, w_out[1], "l1_dy")
    d_mm1, dg, small['l1_scale'] = gate_bwd(cfg, dyb, mm1, ug1, 1, vec('l1_scale'), BF16, "l1_gate_b")
    g_grp = bd_tn(cfg, p1, d_mm1, len(cfg.pool_windows), "l1_dw_grp")
    dp = bd_nt(cfg, d_mm1, w1_grp, "l1_dp")
    du = pool_bwd(cfg, dp, "l1_pool_b")
    d_in = jnp.concatenate([du, dg], axis=1)
    g_in = mm_tn(cfg, h1_bf, d_in, 4, "l1_dw_in")
    dh = mm_nt(cfg, d_in, w1_in, "l1_dh")
    blocks_first = lambda g: jnp.moveaxis(g.reshape(g.shape[0], 4, g.shape[1] // 4, g.shape[2]), 1, 0)
    big_rs(1, [g_in, blocks_first(g_grp), g_out.reshape(4, dq4, d)])

    dz0, dz0_bf, small['l0_ln_g'], small['l0_ln_b'] = ln_bwd(cfg, dz1, dh, xh0, rs0, vec('l0_ln_g'), alpha, "l0_ln_b")
    g_out = mm_tn(cfg, y0, dz0_bf, 1, "l0_dw_out")
    dyb = mm_nt(cfg, dz0_bf, w_out[0], "l0_dy")
    dg, dpa, dpx, du_dir, lacc0 = lru_bwd(cfg, dyb, ug0, hs0, hprev0, u0, pa0, px0, vec('l0_b_a'), vec('l0_b_x'), vec('l0_lam'), "l0_lru_b")
    g_a = bd_tn(cfg, u0_bf, dpa, cfg.lru_heads, "l0_dw_a")
    g_x = bd_tn(cfg, u0_bf, dpx, cfg.lru_heads, "l0_dw_x")
    du_a = bd_nt(cfg, dpa, w0_a, "l0_du_a")
    du_x = bd_nt(cfg, dpx, w0_x, "l0_du_x")
    du_pre, cacc = conv_bwd(cfg, du_dir, du_a, du_x, ug0, conv_w4, "l0_conv_b")
    d_in = jnp.concatenate([du_pre, dg], axis=1)
    g_in = mm_tn(cfg, h0_bf, d_in, 4, "l0_dw_in")
    dh = mm_nt(cfg, d_in, w0_in, "l0_dh")
    big_rs(0, [g_in, blocks_first(g_a), blocks_first(g_x), g_out.reshape(4, dq4, d)])
    dh0 = axpy(cfg, dz0, dh, alpha, "dh0")
    grad_x = dh0[nm:nm + seq][None]
    small['l0_lam'], small['l0_b_a'], small['l0_b_x'] = lacc0[0:1], lacc0[1:2], lacc0[2:3]
    small['l0_conv_b'] = cacc[4:5]

    rows = [dh0[i:i + 1] for i in range(nm)] + [cacc[k:k + 1] for k in range(CONV_W)] + [small[n] for n in SMALL_REPL]
    red = allreduce_small(_rows8(rows, d), "reduce_small")
    sh = lax.dynamic_slice(red[:nm + CONV_W], (0, p * dq4), (nm + CONV_W, dq4))
    grads['meta_tokens'] = sh[:nm]
    grads['l0_conv_w'] = sh[nm:].reshape(a['l0_conv_w'].shape)
    for i, n in enumerate(SMALL_REPL):
        grads[n] = red[nm + CONV_W + i, :a[n].shape[0]]

    delta, new_m, new_v = {}, {}, {}
    for n in sum(BIG.values(), []):
        delta[n], new_m[n], new_v[n] = adamw(a[n], grads[n], a['m_' + n], a['v_' + n], f"adamw_{n}")
    pack_s = lambda pre: jnp.concatenate([a[pre + 'meta_tokens'], a[pre + 'l0_conv_w'].reshape(CONV_W, dq4)], axis=0)
    ds_, ms_, vs_ = adamw(pack_s(''), sh, pack_s('m_'), pack_s('v_'), "adamw_small_sharded")
    for out, packed in ((delta, ds_), (new_m, ms_), (new_v, vs_)):
        out['meta_tokens'] = packed[:nm]
        out['l0_conv_w'] = packed[nm:].reshape(a['l0_conv_w'].shape)
    pack_r = lambda pre: jnp.stack([jnp.pad(a[pre + n], (0, d - a[n].shape[0])) for n in SMALL_REPL])
    dr_, mr_, vr_ = adamw(pack_r(''), red[nm + CONV_W:nm + CONV_W + len(SMALL_REPL)], pack_r('m_'), pack_r('v_'), "adamw_small_repl")
    for out, packed in ((delta, dr_), (new_m, mr_), (new_v, vr_)):
        for i, n in enumerate(SMALL_REPL):
            out[n] = packed[i, :a[n].shape[0]]

    return (loss, grad_x, *[grads[n] for n in W_NAMES], *[delta[n] for n in W_NAMES],
            *[new_m[n] for n in W_NAMES], *[new_v[n] for n in W_NAMES])


def kernel(x, meta_tokens, l0_w_in, l0_conv_w, l0_conv_b, l0_w_a, l0_b_a, l0_w_x, l0_b_x, l0_lam, l0_w_out, l0_ln_g, l0_ln_b, l1_w_in, l1_w_grp, l1_scale, l1_w_out, l1_ln_g, l1_ln_b, l2_w_in, l2_q_norm, l2_w_uq, l2_kv_norm, l2_w_ukv, l2_w_out, l2_ln_g, l2_ln_b, l3_w_in, l3_w_out, l3_ln_g, l3_ln_b, loss_target, m_meta_tokens, m_l0_w_in, m_l0_conv_w, m_l0_conv_b, m_l0_w_a, m_l0_b_a, m_l0_w_x, m_l0_b_x, m_l0_lam, m_l0_w_out, m_l0_ln_g, m_l0_ln_b, m_l1_w_in, m_l1_w_grp, m_l1_scale, m_l1_w_out, m_l1_ln_g, m_l1_ln_b, m_l2_w_in, m_l2_q_norm, m_l2_w_uq, m_l2_kv_norm, m_l2_w_ukv, m_l2_w_out, m_l2_ln_g, m_l2_ln_b, m_l3_w_in, m_l3_w_out, m_l3_ln_g, m_l3_ln_b, v_meta_tokens, v_l0_w_in, v_l0_conv_w, v_l0_conv_b, v_l0_w_a, v_l0_b_a, v_l0_w_x, v_l0_b_x, v_l0_lam, v_l0_w_out, v_l0_ln_g, v_l0_ln_b, v_l1_w_in, v_l1_w_grp, v_l1_scale, v_l1_w_out, v_l1_ln_g, v_l1_ln_b, v_l2_w_in, v_l2_q_norm, v_l2_w_uq, v_l2_kv_norm, v_l2_w_ukv, v_l2_w_out, v_l2_ln_g, v_l2_ln_b, v_l3_w_in, v_l3_w_out, v_l3_ln_g, v_l3_ln_b):
    return _step(REAL, dict(locals()))
```

```python
import functools
import math
from typing import NamedTuple

import jax
import jax.numpy as jnp
import numpy as np
from jax import lax
from jax.experimental import pallas as pl
from jax.experimental.pallas import tpu as pltpu

F32 = jnp.float32
BF16 = jnp.bfloat16

LN_EPS = 1e-5
RMS_EPS = 1e-6
ROPE_BASE = 10000.0
LRU_C = 8.0
CONV_W = 4
HIST = 16
ADAM_LR, ADAM_B1, ADAM_B2, ADAM_EPS, ADAM_WD, ADAM_STEP = 0.001, 0.9, 0.999, 1e-08, 0.01, 10
VMEM_LIMIT_V7X = 56 * 1024 * 1024
MESH = pl.DeviceIdType.MESH


class Cfg(NamedTuple):
    d: int
    seq: int
    n_meta: int
    tp: int
    depth: int
    lru_heads: int
    pool_windows: tuple
    mla_heads: int
    nope: int
    rope: int
    q_lora: int
    kv_lora: int
    cw: int
    ret_heads: int
    chunk: int
    tq: int
    tkw: int
    tr: int
    tt: int
    tc: int
    tm: int
    tn: int
    la: int


REAL = Cfg(d=4096, seq=4096, n_meta=16, tp=4224, depth=4, lru_heads=16, pool_windows=(2, 4, 8, 16),
           mla_heads=32, nope=128, rope=64, q_lora=1024, kv_lora=512, cw=2048, ret_heads=16, chunk=384,
           tq=384, tkw=2, tr=128, tt=1056, tc=512, tm=1408, tn=1024, la=128)


def _div_tile(n, target, align):
    best = None
    for t in range(align, min(n, target) + 1, align):
        if n % t == 0:
            best = t
    return best or n


def _cparams(*sem):
    return pltpu.CompilerParams(dimension_semantics=sem, vmem_limit_bytes=VMEM_LIMIT_V7X)


def _sig(x):
    return 1.0 / (1.0 + jnp.exp(-x))


def _expm1(x):
    p = x * (1.0 + x * (1.0 / 2) * (1.0 + x * (1.0 / 3) * (1.0 + x * (1.0 / 4) * (1.0 + x * (1.0 / 5) * (1.0 + x * (1.0 / 6))))))
    return jnp.where(x > -0.1, p, jnp.exp(x) - 1.0)


def _colsum8(x):
    r, w = x.shape
    return x.reshape(r // 8, 8, w).sum(axis=0)


class Side(NamedTuple):
    ins: list
    inplace: list
    outs: list
    sems: list
    start: object
    finish: object


def _host_call(body, grid, in_specs, out_specs, out_shape, scratch_shapes, semantics, name, args, side=None):
    if side is None:
        return pl.pallas_call(body, grid=grid, in_specs=in_specs, out_specs=out_specs, out_shape=out_shape,
                              scratch_shapes=scratch_shapes, compiler_params=_cparams(*semantics), name=name)(*args), None
    n_in, n_out, n_scr = len(in_specs), len(out_specs), len(scratch_shapes)
    n_si, n_sp, n_so = len(side.ins), len(side.inplace), len(side.outs)

    def wrapped(*refs):
        it = iter(refs)
        take = lambda k: [next(it) for _ in range(k)]
        ins, s_ins, s_inplace_in, outs, s_inplace, s_outs, scr, sems = (take(n_in), take(n_si), take(n_sp), take(n_out),
                                                                        take(n_sp), take(n_so), take(n_scr), take(len(side.sems)))
        ids = [pl.program_id(i) for i in range(len(grid))]
        first = functools.reduce(jnp.logical_and, [i == 0 for i in ids])
        last = functools.reduce(jnp.logical_and, [i == g - 1 for i, g in zip(ids, grid)])

        @pl.when(first)
        def _():
            side.start(s_ins, s_inplace, s_outs, sems)

        body(*ins, *outs, *scr)

        @pl.when(last)
        def _():
            side.finish(s_ins, s_inplace, s_outs, sems)

    any_spec = pl.BlockSpec(memory_space=pl.ANY)
    out = pl.pallas_call(
        wrapped, grid=grid, in_specs=list(in_specs) + [any_spec] * (n_si + n_sp),
        out_specs=list(out_specs) + [any_spec] * (n_sp + n_so),
        out_shape=list(out_shape) + [jax.ShapeDtypeStruct(x.shape, x.dtype) for x in side.inplace] + list(side.outs),
        scratch_shapes=list(scratch_shapes) + list(side.sems),
        input_output_aliases={n_in + n_si + i: n_out + i for i in range(n_sp)},
        compiler_params=_cparams(*(("arbitrary",) * len(grid))), name=name)(*args, *side.ins, *side.inplace)
    return out[:n_out], out[n_out:]


def _mm_call(a, b, grid, a_spec, b_spec, o_spec, out_sds, acc_shape, dims, name, side=None):
    nk = grid[-1]
    kax = len(grid) - 1

    def body(a_ref, b_ref, o_ref, acc_ref):
        k = pl.program_id(kax)

        @pl.when(k == 0)
        def _():
            acc_ref[...] = jnp.zeros_like(acc_ref)

        acc_ref[...] += lax.dot_general(a_ref[...].astype(BF16), b_ref[...].astype(BF16), dims,
                                        preferred_element_type=F32)

        @pl.when(k == nk - 1)
        def _():
            o_ref[...] = acc_ref[...].astype(o_ref.dtype)

    sem = ("parallel",) * kax + ("arbitrary",)
    outs, side_outs = _host_call(body, grid, [a_spec, b_spec], [o_spec], [out_sds], [pltpu.VMEM(acc_shape, F32)], sem, name,
                                 (a, b), side)
    return outs[0] if side is None else (outs[0], side_outs)


_NN = (((1,), (0,)), ((), ()))
_NT = (((1,), (1,)), ((), ()))
_TN = (((0,), (0,)), ((), ()))


def mm_nn(cfg, a, w, name, out_dtype=F32, a_col0=0, side=None):
    m = a.shape[0]
    g, k, ns = w.shape
    tm, tn, tk = _div_tile(m, cfg.tm, 16), _div_tile(ns, cfg.tn, cfg.la), _div_tile(k, cfg.tn, cfg.la)
    npb, a0 = ns // tn, a_col0 // tk
    assert a_col0 % tk == 0
    return _mm_call(a, w, (m // tm, g * npb, k // tk),
                    pl.BlockSpec((tm, tk), lambda i, j, kk: (i, a0 + kk)),
                    pl.BlockSpec((None, tk, tn), lambda i, j, kk: (j // npb, kk, j % npb)),
                    pl.BlockSpec((tm, tn), lambda i, j, kk: (i, j)),
                    jax.ShapeDtypeStruct((m, g * ns), out_dtype), (tm, tn), _NN, name, side)


def mm_nt(cfg, dy, w, name, out_dtype=F32, side=None):
    m = dy.shape[0]
    g, k, ns = w.shape
    tm, tn, tk = _div_tile(m, cfg.tm, 16), _div_tile(k, cfg.tn, cfg.la), _div_tile(ns, cfg.tn, cfg.la)
    kpb = ns // tk
    return _mm_call(dy, w, (m // tm, k // tn, g * kpb),
                    pl.BlockSpec((tm, tk), lambda i, j, kk: (i, kk)),
                    pl.BlockSpec((None, tn, tk), lambda i, j, kk: (kk // kpb, j, kk % kpb)),
                    pl.BlockSpec((tm, tn), lambda i, j, kk: (i, j)),
                    jax.ShapeDtypeStruct((m, k), out_dtype), (tm, tn), _NT, name, side)


def mm_tn(cfg, x, dy, g, name, x_col0=0, k=None, side=None):
    m = x.shape[0]
    k = x.shape[1] if k is None else k
    ns = dy.shape[1] // g
    tmo, tn, tk = _div_tile(k, cfg.tn, cfg.la), _div_tile(ns, cfg.tn, cfg.la), _div_tile(m, cfg.tm, 16)
    npb, x0 = ns // tn, x_col0 // tmo
    assert x_col0 % tmo == 0
    return _mm_call(x, dy, (k // tmo, g * npb, m // tk),
                    pl.BlockSpec((tk, tmo), lambda i, j, kk: (kk, x0 + i)),
                    pl.BlockSpec((tk, tn), lambda i, j, kk: (kk, j)),
                    pl.BlockSpec((None, tmo, tn), lambda i, j, kk: (j // npb, i, j % npb)),
                    jax.ShapeDtypeStruct((g, k, ns), F32), (tmo, tn), _TN, name, side)


def bd_nn(cfg, x, w, name, out_dtype=F32):
    m = x.shape[0]
    g, kg, ng = w.shape
    tm, tn = _div_tile(m, cfg.tm, 16), _div_tile(ng, cfg.tn, cfg.la)
    npb = ng // tn
    return _mm_call(x, w, (m // tm, g * npb, 1),
                    pl.BlockSpec((tm, kg), lambda i, j, kk: (i, j // npb)),
                    pl.BlockSpec((None, kg, tn), lambda i, j, kk: (j // npb, 0, j % npb)),
                    pl.BlockSpec((tm, tn), lambda i, j, kk: (i, j)),
                    jax.ShapeDtypeStruct((m, g * ng), out_dtype), (tm, tn), _NN, name)


def bd_nt(cfg, dy, w, name, out_dtype=F32):
    m = dy.shape[0]
    g, kg, ng = w.shape
    tm, tn = _div_tile(m, cfg.tm, 16), _div_tile(kg, cfg.tn, cfg.la)
    npb = kg // tn
    return _mm_call(dy, w, (m // tm, g * npb, 1),
                    pl.BlockSpec((tm, ng), lambda i, j, kk: (i, j // npb)),
                    pl.BlockSpec((None, tn, ng), lambda i, j, kk: (j // npb, j % npb, 0)),
                    pl.BlockSpec((tm, tn), lambda i, j, kk: (i, j)),
                    jax.ShapeDtypeStruct((m, g * kg), out_dtype), (tm, tn), _NT, name)


def bd_tn(cfg, x, dy, g, name):
    m = x.shape[0]
    kg, ng = x.shape[1] // g, dy.shape[1] // g
    tmo, tn, tk = _div_tile(kg, cfg.tn, cfg.la), _div_tile(ng, cfg.tn, cfg.la), _div_tile(m, cfg.tm, 16)
    mpb, npb = kg // tmo, ng // tn
    return _mm_call(x, dy, (g * mpb, npb, m // tk),
                    pl.BlockSpec((tk, tmo), lambda i, j, kk: (kk, i)),
                    pl.BlockSpec((tk, tn), lambda i, j, kk: (kk, (i // mpb) * npb + j)),
                    pl.BlockSpec((None, tmo, tn), lambda i, j, kk: (i // mpb, i % mpb, j)),
                    jax.ShapeDtypeStruct((g, kg, ng), F32), (tmo, tn), _TN, name)


def _rowwise(cfg, body, ins, outs, name, n_acc=0):
    tr = cfg.tr
    n = cfg.tp // tr
    in_specs, args = [], []
    for spec in ins:
        arr = spec[0]
        if spec[1] is None:
            in_specs.append(pl.BlockSpec(arr.shape, lambda i, nd=arr.ndim: (0,) * nd))
        else:
            in_specs.append(pl.BlockSpec((tr, spec[1]), lambda i, cb=spec[2]: (i, cb)))
        args.append(arr)
    out_specs, out_shape = [], []
    for o in outs:
        if len(o) == 2:
            out_specs.append(pl.BlockSpec((tr, o[0]), lambda i: (i, 0)))
            out_shape.append(jax.ShapeDtypeStruct((cfg.tp, o[0]), o[1]))
        else:
            out_specs.append(pl.BlockSpec((8, o[0]), lambda i: (0, 0)))
            out_shape.append(jax.ShapeDtypeStruct((8, o[0]), F32))
    return pl.pallas_call(body, grid=(n,), in_specs=in_specs, out_specs=out_specs, out_shape=out_shape,
                          compiler_params=_cparams("arbitrary" if n_acc else "parallel"), name=name)(*args)


def ln_fwd(cfg, h, o, g, b, alpha, name):
    d = cfg.d

    def body(h_ref, o_ref, g_ref, b_ref, y_ref, yb_ref, xh_ref, rs_ref):
        z = alpha * h_ref[...] + o_ref[...]
        mu = jnp.mean(z, axis=-1, keepdims=True)
        zc = z - mu
        var = jnp.mean(zc * zc, axis=-1, keepdims=True)
        rstd = lax.rsqrt(var + LN_EPS)
        xh = zc * rstd
        y = xh * g_ref[...] + b_ref[...]
        y_ref[...] = y
        yb_ref[...] = y.astype(BF16)
        xh_ref[...] = xh
        rs_ref[...] = rstd

    return _rowwise(cfg, body, [(h, d, 0), (o, d, 0), (g, None), (b, None)], [(d, F32), (d, BF16), (d, F32), (1, F32)], name)


def ln_bwd(cfg, d_res, d_mm, xhat, rstd, g, alpha, name):
    d = cfg.d
    two = d_res is not None

    def body(*refs):
        if two:
            dr_ref, dm_ref, xh_ref, rs_ref, g_ref, dz_ref, dzb_ref, dg_ref, db_ref = refs
            dy = alpha * dr_ref[...] + dm_ref[...]
        else:
            dm_ref, xh_ref, rs_ref, g_ref, dz_ref, dzb_ref, dg_ref, db_ref = refs
            dy = dm_ref[...]
        xh = xh_ref[...]

        @pl.when(pl.program_id(0) == 0)
        def _():
            dg_ref[...] = jnp.zeros_like(dg_ref)
            db_ref[...] = jnp.zeros_like(db_ref)

        dg_ref[...] += _colsum8(dy * xh)
        db_ref[...] += _colsum8(dy)
        dxh = dy * g_ref[...]
        m1 = jnp.mean(dxh, axis=-1, keepdims=True)
        m2 = jnp.mean(dxh * xh, axis=-1, keepdims=True)
        dz = rs_ref[...] * (dxh - m1 - xh * m2)
        dz_ref[...] = dz
        dzb_ref[...] = dz.astype(BF16)

    ins = ([(d_res, d, 0)] if two else []) + [(d_mm, d, 0), (xhat, d, 0), (rstd, 1, 0), (g, None)]
    return _rowwise(cfg, body, ins, [(d, F32), (d, BF16), (d,), (d,)], name, n_acc=2)


def loss_grad(cfg, y, tgt, name):
    d, tr = cfg.d, cfg.tr
    lo, hi = cfg.n_meta, cfg.n_meta + cfg.seq

    def body(y_ref, t_ref, dy_ref, acc_ref):
        i = pl.program_id(0)

        @pl.when(i == 0)
        def _():
            acc_ref[...] = jnp.zeros_like(acc_ref)

        row = i * tr + lax.broadcasted_iota(jnp.int32, (tr, 1), 0)
        err = jnp.where((row >= lo) & (row < hi), y_ref[...] - t_ref[...], 0.0)
        dy_ref[...] = err * (1.0 / d)
        acc_ref[...] += _colsum8(err * err)

    return _rowwise(cfg, body, [(y, d, 0), (tgt, d, 0)], [(d, F32), (d,)], name, n_acc=1)


def axpy(cfg, a, b, alpha, name):
    d = cfg.d

    def body(a_ref, b_ref, o_ref):
        o_ref[...] = alpha * a_ref[...] + b_ref[...]

    return _rowwise(cfg, body, [(a, d, 0), (b, d, 0)], [(d, F32)], name)[0]


def _time_call(cfg, body, ins, outs, accs, scratch, name, reverse=False, groups=1):
    d = cfg.d
    tc = _div_tile(d // groups, cfg.tc, cfg.la)
    tt = _div_tile(cfg.tp, cfg.tt, 16)
    nc, nt = d // tc, cfg.tp // tt
    tmap = (lambda t: nt - 1 - t) if reverse else (lambda t: t)
    in_specs, args = [], []
    for spec in ins:
        arr = spec[0]
        if len(spec) == 2:
            in_specs.append(pl.BlockSpec((tt, tc), lambda c, t, off=spec[1] * nc: (tmap(t), off + c)))
        else:
            in_specs.append(pl.BlockSpec((arr.shape[0], tc), lambda c, t: (0, c)))
        args.append(arr)
    out_specs = [pl.BlockSpec((tt, tc), lambda c, t: (tmap(t), c)) for _ in outs]
    out_shape = [jax.ShapeDtypeStruct((cfg.tp, d), dt) for dt in outs]
    for _ in range(accs):
        out_specs.append(pl.BlockSpec((8, tc), lambda c, t: (0, c)))
        out_shape.append(jax.ShapeDtypeStruct((8, d), F32))
    return pl.pallas_call(functools.partial(body, tt=tt, tc=tc, nt=nt), grid=(nc, nt), in_specs=in_specs,
                          out_specs=out_specs, out_shape=out_shape,
                          scratch_shapes=[pltpu.VMEM(s, F32) for s in scratch(tt, tc)],
                          compiler_params=_cparams("parallel", "arbitrary"), name=name)(*args)


def _push_history(s_ref, new, t, tt):
    @pl.when(t == 0)
    def _():
        s_ref[pl.ds(0, HIST), :] = jnp.zeros((HIST, s_ref.shape[1]), F32)

    @pl.when(t > 0)
    def _():
        s_ref[pl.ds(0, HIST), :] = s_ref[pl.ds(tt, HIST), :]

    s_ref[pl.ds(HIST, tt), :] = new


def _push_future(s_ref, new, t, tt):
    @pl.when(t == 0)
    def _():
        s_ref[pl.ds(tt, HIST), :] = jnp.zeros((HIST, s_ref.shape[1]), F32)

    @pl.when(t > 0)
    def _():
        s_ref[pl.ds(tt, HIST), :] = s_ref[pl.ds(0, HIST), :]

    s_ref[pl.ds(0, tt), :] = new


def conv_fwd(cfg, ug, w4, cb, name):
    def body(x_ref, w_ref, b_ref, u_ref, ub_ref, s_ref, *, tt, tc, nt):
        t = pl.program_id(1)
        _push_history(s_ref, x_ref[...], t, tt)
        acc = b_ref[...] + w_ref[pl.ds(0, 1), :] * s_ref[pl.ds(HIST - 3, tt), :]
        for k in range(1, CONV_W):
            acc = acc + w_ref[pl.ds(k, 1), :] * s_ref[pl.ds(HIST - 3 + k, tt), :]
        u_ref[...] = acc
        ub_ref[...] = acc.astype(BF16)

    return _time_call(cfg, body, [(ug, 0), (w4,), (cb,)], [F32, BF16], 0, lambda tt, tc: [(tt + HIST, tc)], name)


def conv_bwd(cfg, du_a, du_b, du_c, ug, w4, name):
    def body(a_ref, b_ref, c_ref, x_ref, w_ref, dx_ref, acc_ref, s_ref, *, tt, tc, nt):
        t = pl.program_id(1)

        @pl.when(t == 0)
        def _():
            acc_ref[...] = jnp.zeros_like(acc_ref)

        du = a_ref[...] + b_ref[...] + c_ref[...]
        _push_future(s_ref, du, t, tt)
        x = x_ref[...]
        acc_ref[pl.ds(4, 1), :] += jnp.sum(du, axis=0, keepdims=True)
        dx = None
        for k in range(CONV_W):
            sh = s_ref[pl.ds(3 - k, tt), :]
            term = w_ref[pl.ds(k, 1), :] * sh
            dx = term if dx is None else dx + term
            acc_ref[pl.ds(k, 1), :] += jnp.sum(x * sh, axis=0, keepdims=True)
        dx_ref[...] = dx.astype(BF16)

    return _time_call(cfg, body, [(du_a, 0), (du_b, 0), (du_c, 0), (ug, 0), (w4,)], [BF16], 1,
                      lambda tt, tc: [(tt + HIST, tc)], name, reverse=True)


def _scan8(a, b, rows):
    for k in (1, 2, 4):
        ar = jnp.where(rows >= k, pltpu.roll(a, k, 0), 1.0)
        br = jnp.where(rows >= k, pltpu.roll(b, k, 0), 0.0)
        b = a * br + b
        a = a * ar
    return a, b


def _rscan8(a, b, rows):
    for k in (1, 2, 4):
        ar = jnp.where(rows < 8 - k, pltpu.roll(a, 8 - k, 0), 1.0)
        br = jnp.where(rows < 8 - k, pltpu.roll(b, 8 - k, 0), 0.0)
        b = b + a * br
        a = a * ar
    return a, b


def _lru_gates(u, pa, px, ba, bx, c_lam):
    r = _sig(pa + ba)
    i = _sig(px + bx)
    la = LRU_C * r * c_lam
    em = _expm1(2.0 * la)
    return r, i, jnp.exp(la), em, jnp.sqrt(-em)


def _neg_softplus_neg(lam):
    e = jnp.exp(-jnp.abs(lam))
    u = 1.0 + e
    l1p = jnp.where(u == 1.0, e, jnp.log(u) * (e / jnp.where(u == 1.0, 1.0, u - 1.0)))
    return -(jnp.maximum(-lam, 0.0) + l1p)


def lru_fwd(cfg, u, pa, px, ug, ba, bx, lam, name):
    def body(u_ref, pa_ref, px_ref, g_ref, ba_ref, bx_ref, lam_ref, hs_ref, hp_ref, y_ref, car_ref, *, tt, tc, nt):
        @pl.when(pl.program_id(1) == 0)
        def _():
            car_ref[...] = jnp.zeros_like(car_ref)

        rows = lax.broadcasted_iota(jnp.int32, (8, tc), 0)
        ba, bx = ba_ref[...], bx_ref[...]
        c_lam = _neg_softplus_neg(lam_ref[...])

        def step(it, h):
            ys = []
            for half in range(2):
                rs = pl.ds(pl.multiple_of(it * 16 + half * 8, 8), 8)
                uu = u_ref[rs, :]
                r, i, a, em, s = _lru_gates(uu, pa_ref[rs, :], px_ref[rs, :], ba, bx, c_lam)
                acum, b = _scan8(a, uu * i * s, rows)
                hs = acum * h + b
                hs_ref[rs, :] = hs
                hp_ref[rs, :] = jnp.where(rows >= 1, pltpu.roll(hs, 1, 0), h)
                g = g_ref[rs, :]
                ys.append(hs * (g * _sig(g)))
                h = jnp.broadcast_to(hs[7:8, :], (8, tc))
            y_ref[pl.ds(pl.multiple_of(it * 16, 16), 16), :] = jnp.concatenate(ys, axis=0).astype(BF16)
            return h

        car_ref[...] = lax.fori_loop(0, tt // 16, step, car_ref[...])

    return _time_call(cfg, body, [(u, 0), (pa, 0), (px, 0), (ug, 1), (ba,), (bx,), (lam,)], [F32, F32, BF16], 0,
                      lambda tt, tc: [(8, tc)], name)


def lru_bwd(cfg, dy, ug, hs, hprev, u, pa, px, ba, bx, lam, name):
    def body(dy_ref, g_ref, hs_ref, hp_ref, u_ref, pa_ref, px_ref, ba_ref, bx_ref, lam_ref,
             dg_ref, dpa_ref, dpx_ref, du_ref, acc_ref, ecar_ref, acar_ref, *, tt, tc, nt):
        @pl.when(pl.program_id(1) == 0)
        def _():
            ecar_ref[...] = jnp.zeros_like(ecar_ref)
            acar_ref[...] = jnp.zeros_like(acar_ref)
            acc_ref[...] = jnp.zeros_like(acc_ref)

        rows = lax.broadcasted_iota(jnp.int32, (8, tc), 0)
        ba, bx = ba_ref[...], bx_ref[...]
        c_lam = _neg_softplus_neg(lam_ref[...])

        def step(it, carry):
            ecar, acar, s_lam, s_ba, s_bx = carry
            jt = tt // 16 - 1 - it
            dgs, dpas, dpxs = [None, None], [None, None], [None, None]
            for half in (1, 0):
                rs = pl.ds(pl.multiple_of(jt * 16 + half * 8, 8), 8)
                uu = u_ref[rs, :]
                r, i, a, em, s = _lru_gates(uu, pa_ref[rs, :], px_ref[rs, :], ba, bx, c_lam)
                g = g_ref[rs, :]
                sg = _sig(g)
                dy = dy_ref[rs, :]
                dgs[half] = dy * hs_ref[rs, :] * (sg * (1.0 + g * (1.0 - sg)))
                a_next = jnp.where(rows < 7, pltpu.roll(a, 7, 0), acar)
                acum, e = _rscan8(a_next, dy * (g * sg), rows)
                e = e + acum * ecar
                ecar = jnp.broadcast_to(e[0:1, :], (8, tc))
                acar = jnp.broadcast_to(a[0:1, :], (8, tc))
                du_ref[rs, :] = e * i * s
                d_la = e * hp_ref[rs, :] * a - (e * uu * i) * ((em + 1.0) / s)
                s_lam = s_lam + d_la * (LRU_C * r)
                d_pa = d_la * (LRU_C * c_lam) * r * (1.0 - r)
                d_px = e * uu * s * i * (1.0 - i)
                s_ba = s_ba + d_pa
                s_bx = s_bx + d_px
                dpas[half], dpxs[half] = d_pa, d_px
            r16 = pl.ds(pl.multiple_of(jt * 16, 16), 16)
            dg_ref[r16, :] = jnp.concatenate(dgs, axis=0).astype(BF16)
            dpa_ref[r16, :] = jnp.concatenate(dpas, axis=0).astype(BF16)
            dpx_ref[r16, :] = jnp.concatenate(dpxs, axis=0).astype(BF16)
            return ecar, acar, s_lam, s_ba, s_bx

        z = jnp.zeros((8, tc), F32)
        ecar, acar, s_lam, s_ba, s_bx = lax.fori_loop(0, tt // 16, step, (ecar_ref[...], acar_ref[...], z, z, z))
        ecar_ref[...] = ecar
        acar_ref[...] = acar
        acc_ref[pl.ds(0, 1), :] += jnp.sum(s_lam, axis=0, keepdims=True) * _sig(-lam_ref[...])
        acc_ref[pl.ds(1, 1), :] += jnp.sum(s_ba, axis=0, keepdims=True)
        acc_ref[pl.ds(2, 1), :] += jnp.sum(s_bx, axis=0, keepdims=True)

    return _time_call(cfg, body, [(dy, 0), (ug, 1), (hs, 0), (hprev, 0), (u, 0), (pa, 0), (px, 0), (ba,), (bx,), (lam,)],
                      [BF16, BF16, BF16, F32], 1, lambda tt, tc: [(8, tc), (8, tc)], name, reverse=True)


def _pool_select(cfg, grp, fn):
    for gi, w in enumerate(cfg.pool_windows):
        @pl.when(grp == gi)
        def _(w=w):
            fn(w)


def pool_fwd(cfg, ug, name):
    ng = len(cfg.pool_windows)

    def body(x_ref, p_ref, s_ref, *, tt, tc, nt):
        t = pl.program_id(1)
        grp = pl.program_id(0) // (cfg.d // ng // tc)
        x = x_ref[...]
        _push_history(s_ref, x, t, tt)
        row1 = (t * tt + 1 + lax.broadcasted_iota(jnp.int32, (tt, 1), 0)).astype(F32)

        def write(w):
            ws = x
            for j in range(1, w):
                ws = ws + s_ref[pl.ds(HIST - j, tt), :]
            p_ref[...] = (ws / jnp.minimum(row1, float(w)) - x).astype(BF16)

        _pool_select(cfg, grp, write)

    return _time_call(cfg, body, [(ug, 0)], [BF16], 0, lambda tt, tc: [(tt + HIST, tc)], name, groups=ng)[0]


def pool_bwd(cfg, dp, name):
    ng = len(cfg.pool_windows)

    def body(dp_ref, du_ref, s_ref, *, tt, tc, nt):
        t = pl.program_id(1)
        grp = pl.program_id(0) // (cfg.d // ng // tc)
        dp = dp_ref[...]
        row1 = ((nt - 1 - t) * tt + 1 + lax.broadcasted_iota(jnp.int32, (tt, 1), 0)).astype(F32)

        def write(w):
            dm = dp / jnp.minimum(row1, float(w))
            _push_future(s_ref, dm, t, tt)
            ws = dm
            for j in range(1, w):
                ws = ws + s_ref[pl.ds(j, tt), :]
            du_ref[...] = (ws - dp).astype(BF16)

        _pool_select(cfg, grp, write)

    return _time_call(cfg, body, [(dp, 0)], [BF16], 0, lambda tt, tc: [(tt + HIST, tc)], name, reverse=True, groups=ng)[0]


def gate_fwd(cfg, v, gsrc, gblk, scale, name):
    d = cfg.d

    def body(*refs):
        if scale is None:
            v_ref, g_ref, y_ref = refs
            v = v_ref[...]
        else:
            v_ref, g_ref, s_ref, y_ref = refs
            v = v_ref[...] * s_ref[...]
        g = g_ref[...]
        y_ref[...] = (v * (g * _sig(g))).astype(BF16)

    ins = [(v, d, 0), (gsrc, d, gblk)] + ([] if scale is None else [(scale, None)])
    return _rowwise(cfg, body, ins, [(d, BF16)], name)[0]


def gate_bwd(cfg, dy, v, gsrc, gblk, scale, dv_dtype, name):
    d = cfg.d

    def body(*refs):
        if scale is None:
            dy_ref, v_ref, g_ref, dv_ref, dg_ref = refs
            vs = v_ref[...]
        else:
            dy_ref, v_ref, g_ref, s_ref, dv_ref, dg_ref, acc_ref = refs
            vs = v_ref[...] * s_ref[...]
        g = g_ref[...]
        sg = _sig(g)
        dy = dy_ref[...]
        dvs = dy * (g * sg)
        dg_ref[...] = (dy * vs * (sg * (1.0 + g * (1.0 - sg)))).astype(BF16)
        if scale is None:
            dv_ref[...] = dvs.astype(dv_dtype)
        else:
            @pl.when(pl.program_id(0) == 0)
            def _():
                acc_ref[...] = jnp.zeros_like(acc_ref)

            acc_ref[...] += _colsum8(dvs * v_ref[...])
            dv_ref[...] = (dvs * s_ref[...]).astype(dv_dtype)

    ins = [(dy, d, 0), (v, d, 0), (gsrc, d, gblk)] + ([] if scale is None else [(scale, None)])
    outs = [(d, dv_dtype), (d, BF16)] + ([] if scale is None else [(d,)])
    return _rowwise(cfg, body, ins, outs, name, n_acc=0 if scale is None else 1)


def _swap_halves(x, lo, half):
    w = x.shape[1]
    lane = lax.broadcasted_iota(jnp.int32, x.shape, 1)
    sw = jnp.where(lane < lo + half, pltpu.roll(x, w - half, 1), pltpu.roll(x, half, 1))
    return jnp.where((lane >= lo) & (lane < lo + 2 * half), sw, 0.0)


def mla_norm(cfg, proj, qn, kvn, name):
    ql, kl = cfg.q_lora, cfg.kv_lora

    def body(c_ref, qn_ref, kn_ref, q_ref, k_ref):
        cq = c_ref[:, 0:ql]
        ck = c_ref[:, ql:ql + kl]
        q_ref[...] = (cq * lax.rsqrt(jnp.mean(cq * cq, axis=-1, keepdims=True) + RMS_EPS) * qn_ref[...]).astype(BF16)
        k_ref[...] = (ck * lax.rsqrt(jnp.mean(ck * ck, axis=-1, keepdims=True) + RMS_EPS) * kn_ref[...]).astype(BF16)

    return _rowwise(cfg, body, [(proj, cfg.cw, cfg.d // cfg.cw), (qn, None), (kvn, None)], [(ql, BF16), (kl, BF16)], name)


def mla_norm_bwd(cfg, proj, d_cqn, d_ckvn, d_kr, qn, kvn, name):
    ql, kl, cw, npe = cfg.q_lora, cfg.kv_lora, cfg.cw, cfg.nope

    def one(x, dy, gamma):
        rstd = lax.rsqrt(jnp.mean(x * x, axis=-1, keepdims=True) + RMS_EPS)
        xn = x * rstd
        dxn = dy * gamma
        return rstd * (dxn - xn * jnp.mean(dxn * xn, axis=-1, keepdims=True)), dy * xn

    def body(c_ref, dq_ref, dk_ref, dkr_ref, qn_ref, kn_ref, dc_ref, acc_ref):
        @pl.when(pl.program_id(0) == 0)
        def _():
            acc_ref[...] = jnp.zeros_like(acc_ref)

        dcq, gq = one(c_ref[:, 0:ql], dq_ref[...], qn_ref[...])
        dck, gk = one(c_ref[:, ql:ql + kl], dk_ref[...], kn_ref[...])
        dc_ref[:, 0:ql] = dcq.astype(BF16)
        dc_ref[:, ql:ql + kl] = dck.astype(BF16)
        dc_ref[:, ql + kl:ql + kl + npe] = dkr_ref[...].astype(BF16)
        rest = cw - (ql + kl + npe)
        if rest:
            dc_ref[:, ql + kl + npe:cw] = jnp.zeros((dc_ref.shape[0], rest), BF16)
        acc_ref[:, 0:ql] += _colsum8(gq)
        acc_ref[:, ql:ql + kl] += _colsum8(gk)

    return _rowwise(cfg, body, [(proj, cw, cfg.d // cw), (d_cqn, ql, 0), (d_ckvn, kl, 0), (d_kr, npe, 0), (qn, None), (kvn, None)],
                    [(cw, BF16), (cw,)], name, n_acc=1)


def _mla_tables(cfg):
    r2, npe = cfg.rope // 2, cfg.nope
    inv = ROPE_BASE ** (-jnp.arange(0, cfg.rope, 2, dtype=F32) / cfg.rope)
    ang = jnp.arange(cfg.tp, dtype=F32)[:, None] * inv[None, :]
    cos, sin = jnp.cos(ang), jnp.sin(ang)
    one = jnp.ones((cfg.tp, npe - cfg.rope), F32)
    ck = jnp.concatenate([cos, cos, one], axis=1)
    sk = jnp.concatenate([-sin, sin, 0.0 * one], axis=1)
    cq = jnp.concatenate([jnp.ones((cfg.tp, npe), F32), ck], axis=1)
    sq = jnp.concatenate([jnp.zeros((cfg.tp, npe), F32), sk], axis=1)
    return cq, sq, ck, sk


def mla_prep(cfg, q_raw, kv_raw, proj, tabs, name):
    h, npe, r2, ql, kl = cfg.mla_heads, cfg.nope, cfg.rope // 2, cfg.q_lora, cfg.kv_lora
    hp = 2 * npe

    def body(q_ref, kv_ref, c_ref, cq_ref, sq_ref, ck_ref, sk_ref, qo_ref, ko_ref, vo_ref):
        kr = c_ref[:, ql + kl:ql + kl + npe]
        kr = (kr * ck_ref[...] + _swap_halves(kr, 0, r2) * sk_ref[...]).astype(BF16)
        cq, sq = cq_ref[...], sq_ref[...]
        for i in range(h):
            q = q_ref[:, i * hp:(i + 1) * hp]
            qo_ref[:, i * hp:(i + 1) * hp] = (q * cq + _swap_halves(q, npe, r2) * sq).astype(BF16)
            ko_ref[:, i * hp:i * hp + npe] = kv_ref[:, i * npe:(i + 1) * npe].astype(BF16)
            ko_ref[:, i * hp + npe:(i + 1) * hp] = kr
        vo_ref[...] = kv_ref[:, h * npe:2 * h * npe].astype(BF16)

    cq, sq, ck, sk = tabs
    return _rowwise(cfg, body, [(q_raw, h * hp, 0), (kv_raw, 2 * h * npe, 0), (proj, cfg.cw, cfg.d // cfg.cw),
                                (cq, hp, 0), (sq, hp, 0), (ck, npe, 0), (sk, npe, 0)],
                    [(h * hp, BF16), (h * hp, BF16), (h * npe, BF16)], name)


def mla_prep_bwd(cfg, dq_full, dk_full, tabs, name):
    h, npe, r2 = cfg.mla_heads, cfg.nope, cfg.rope // 2
    hp = 2 * npe

    def body(dq_ref, dk_ref, cq_ref, sq_ref, ck_ref, sk_ref, dqo_ref, dko_ref, dkr_ref):
        cq, sq = cq_ref[...], sq_ref[...]
        dkr = None
        for i in range(h):
            dq = dq_ref[:, i * hp:(i + 1) * hp]
            dqo_ref[:, i * hp:(i + 1) * hp] = (dq * cq + _swap_halves(dq * sq, npe, r2)).astype(BF16)
            dko_ref[:, i * npe:(i + 1) * npe] = dk_ref[:, i * hp:i * hp + npe].astype(BF16)
            part = dk_ref[:, i * hp + npe:(i + 1) * hp]
            dkr = part if dkr is None else dkr + part
        dkr_ref[...] = dkr * ck_ref[...] + _swap_halves(dkr * sk_ref[...], 0, r2)

    cq, sq, ck, sk = tabs
    return _rowwise(cfg, body, [(dq_full, h * hp, 0), (dk_full, h * hp, 0), (cq, hp, 0), (sq, hp, 0), (ck, npe, 0), (sk, npe, 0)],
                    [(h * hp, BF16), (h * npe, BF16), (npe, F32)], name)


def _attn_scores(cfg, q, k, diagonal):
    s = lax.dot_general(q, k, _NT, preferred_element_type=F32) * ((cfg.nope + cfg.rope) ** -0.5)
    if diagonal:
        row = lax.broadcasted_iota(jnp.int32, s.shape, 0)
        col = lax.broadcasted_iota(jnp.int32, s.shape, 1)
        s = jnp.where(row >= col, s, -1e30)
    return s


def _tile_rows(i, t):
    return pl.ds(pl.multiple_of(i * t, t), t)


def attn_fwd(cfg, q, k, v, name, side=None):
    h, npe, tq, wide = cfg.mla_heads, cfg.nope, cfg.tq, cfg.tkw
    hp, n = 2 * npe, cfg.tp // tq

    def body(q_ref, k_ref, v_ref, o_ref, lse_ref):
        qi = pl.program_id(1)
        q = q_ref[...]

        def tile(ki, carry, width, diagonal):
            m, l, acc = carry
            rk = _tile_rows(ki, width)
            s = _attn_scores(cfg, q, k_ref[rk, :], diagonal)
            m_new = jnp.maximum(m, jnp.max(s, axis=-1, keepdims=True))
            alpha = jnp.exp(m - m_new)
            p = jnp.exp(s - m_new)
            l = alpha * l + jnp.sum(p, axis=-1, keepdims=True)
            acc = alpha * acc + lax.dot_general(p.astype(BF16), v_ref[rk, :], _NN, preferred_element_type=F32)
            return m_new, l, acc

        init = (jnp.full((tq, 1), -1e30, F32), jnp.zeros((tq, 1), F32), jnp.zeros((tq, npe), F32))
        n_wide = qi // wide
        carry = lax.fori_loop(0, n_wide, lambda j, c: tile(j, c, wide * tq, False), init)
        carry = lax.fori_loop(n_wide * wide, qi, lambda ki, c: tile(ki, c, tq, False), carry)
        m, l, acc = tile(qi, carry, tq, True)
        o_ref[...] = acc / l
        lse_ref[...] = m + jnp.log(l)

    outs, side_outs = _host_call(
        body, (h, n),
        [pl.BlockSpec((tq, hp), lambda hh, qi: (qi, hh)),
         pl.BlockSpec((cfg.tp, hp), lambda hh, qi: (0, hh)),
         pl.BlockSpec((cfg.tp, npe), lambda hh, qi: (0, hh))],
        [pl.BlockSpec((tq, npe), lambda hh, qi: (qi, hh)),
         pl.BlockSpec((None, tq, 1), lambda hh, qi: (hh, qi, 0))],
        [jax.ShapeDtypeStruct((cfg.tp, h * npe), F32), jax.ShapeDtypeStruct((h, cfg.tp, 1), F32)],
        [], ("parallel", "arbitrary"), name, (q, k, v), side)
    return outs if side is None else (outs, side_outs)


def attn_bwd(cfg, q, k, v, do, o, lse, name, side=None):
    h, npe, tq, kw = cfg.mla_heads, cfg.nope, cfg.tq, cfg.tkw
    hp, n = 2 * npe, cfg.tp // tq
    sc = (cfg.nope + cfg.rope) ** -0.5

    def body(q_ref, k_ref, v_ref, do_ref, o_ref, lse_ref, dq_ref, dk_ref, dv_ref, dv_acc):
        dk_ref[...] = jnp.zeros_like(dk_ref)
        dv_acc[...] = jnp.zeros_like(dv_acc)

        def q_tile(qi, _):
            rq = _tile_rows(qi, tq)
            q, do, lse = q_ref[rq, :], do_ref[rq, :], lse_ref[rq, :]
            delta = jnp.sum(do.astype(F32) * o_ref[rq, :], axis=-1, keepdims=True)

            def tile(ki, dq, width, diagonal):
                rk = _tile_rows(ki, width)
                kk = k_ref[rk, :]
                p = jnp.exp(_attn_scores(cfg, q, kk, diagonal) - lse)
                dp = lax.dot_general(do, v_ref[rk, :], _NT, preferred_element_type=F32)
                ds = (p * (dp - delta) * sc).astype(BF16)
                dv_acc[rk, :] += lax.dot_general(p.astype(BF16), do, _TN, preferred_element_type=F32)
                dk_ref[rk, :] += lax.dot_general(ds, q, _TN, preferred_element_type=F32)
                return dq + lax.dot_general(ds, kk, _NN, preferred_element_type=F32)

            n_wide = qi // kw
            dq = lax.fori_loop(0, n_wide, lambda j, acc: tile(j, acc, kw * tq, False), jnp.zeros((tq, hp), F32))
            dq = lax.fori_loop(n_wide * kw, qi, lambda ki, acc: tile(ki, acc, tq, False), dq)
            dq_ref[rq, :] = tile(qi, dq, tq, True)
            return 0

        lax.fori_loop(0, n, q_tile, 0)
        dv_ref[...] = dv_acc[...].astype(BF16)

    wide = pl.BlockSpec((cfg.tp, hp), lambda hh: (0, hh))
    narrow = pl.BlockSpec((cfg.tp, npe), lambda hh: (0, hh))
    outs, side_outs = _host_call(
        body, (h,), [wide, wide, narrow, narrow, narrow, pl.BlockSpec((None, cfg.tp, 1), lambda hh: (hh, 0, 0))],
        [wide, wide, narrow],
        [jax.ShapeDtypeStruct((cfg.tp, h * hp), F32), jax.ShapeDtypeStruct((cfg.tp, h * hp), F32),
         jax.ShapeDtypeStruct((cfg.tp, h * npe), BF16)],
        [pltpu.VMEM((cfg.tp, npe), F32)], ("parallel",), name, (q, k, v, do, o, lse), side)
    return outs if side is None else (outs, side_outs)


def _ret_tables(cfg):
    c = cfg.chunk
    lg = jnp.log(1.0 - 2.0 ** (-5.0 - jnp.arange(cfg.ret_heads, dtype=F32)))[:, None, None]
    j = jnp.arange(c, dtype=F32)
    diff = j[:, None] - j[None, :]
    dm = jnp.where(diff >= 0, jnp.exp(jnp.maximum(diff, 0.0)[None] * lg), 0.0)
    lq = jnp.exp((j + 1.0)[None, :, None] * lg)
    lk = jnp.exp((c - 1.0 - j)[None, :, None] * lg)
    gc = jnp.exp(c * lg)
    return dm, lq, lk, gc


def _ret_rope_tables(cfg):
    dk = cfg.d // cfg.ret_heads
    inv = ROPE_BASE ** (-jnp.arange(0, dk, 2, dtype=F32) / dk)
    ang = jnp.arange(cfg.tp, dtype=F32)[:, None] * inv[None, :]
    return jnp.cos(ang), jnp.sin(ang)


def ret_prep(cfg, qkvg, cos, sin, name):
    d, h = cfg.d, cfg.ret_heads
    dk = d // h
    hd = dk // 2
    ksc = dk ** -0.5

    def body(q_ref, k_ref, v_ref, c_ref, s_ref, qo_ref, ko_ref, vo_ref):
        c, s = c_ref[...], s_ref[...]
        for i in range(h):
            for src, dst, f in ((q_ref, qo_ref, 1.0), (k_ref, ko_ref, ksc)):
                x1 = src[:, i * dk:i * dk + hd]
                x2 = src[:, i * dk + hd:(i + 1) * dk]
                dst[:, i * dk:i * dk + hd] = ((x1 * c - x2 * s) * f).astype(BF16)
                dst[:, i * dk + hd:(i + 1) * dk] = ((x2 * c + x1 * s) * f).astype(BF16)
        vo_ref[...] = v_ref[...].astype(BF16)

    return _rowwise(cfg, body, [(qkvg, d, 0), (qkvg, d, 1), (qkvg, d, 2), (cos, hd, 0), (sin, hd, 0)],
                    [(d, BF16), (d, BF16), (d, BF16)], name)


def ret_prep_bwd(cfg, dq, dk_, dv, dg, cos, sin, name):
    d, h = cfg.d, cfg.ret_heads
    dk = d // h
    hd = dk // 2
    ksc = dk ** -0.5

    def body(dq_ref, dk_ref, dv_ref, dg_ref, c_ref, s_ref, o_ref):
        c, s = c_ref[...], s_ref[...]
        for i in range(h):
            for src, off, f in ((dq_ref, 0, 1.0), (dk_ref, d, ksc)):
                y1 = src[:, i * dk:i * dk + hd]
                y2 = src[:, i * dk + hd:(i + 1) * dk]
                o_ref[:, off + i * dk:off + i * dk + hd] = ((y1 * c + y2 * s) * f).astype(BF16)
                o_ref[:, off + i * dk + hd:off + (i + 1) * dk] = ((y2 * c - y1 * s) * f).astype(BF16)
        o_ref[:, 2 * d:3 * d] = dv_ref[...]
        o_ref[:, 3 * d:4 * d] = dg_ref[...]

    return _rowwise(cfg, body, [(dq, d, 0), (dk_, d, 0), (dv, d, 0), (dg, d, 0), (cos, hd, 0), (sin, hd, 0)],
                    [(4 * d, BF16)], name)[0]


def _ret_specs(cfg, reverse):
    c, h = cfg.chunk, cfg.ret_heads
    dk = cfg.d // h
    n = cfg.tp // c
    tm = (lambda t: n - 1 - t) if reverse else (lambda t: t)
    blk = pl.BlockSpec((c, dk), lambda hh, t: (tm(t), hh))
    row = pl.BlockSpec((None, c, 1), lambda hh, t: (hh, tm(t), 0))
    tabs = [pl.BlockSpec((None, c, c), lambda hh, t: (hh, 0, 0)), pl.BlockSpec((None, c, 1), lambda hh, t: (hh, 0, 0)),
            pl.BlockSpec((None, c, 1), lambda hh, t: (hh, 0, 0)), pl.BlockSpec((None, 1, 1), lambda hh, t: (hh, 0, 0))]
    return blk, row, tabs, n, dk


def _ret_state_update(s_ref, k, v, lk, gc):
    kw = (k.astype(F32) * lk).astype(BF16)
    s_ref[...] = gc * s_ref[...] + lax.dot_general(kw, v, _TN, preferred_element_type=F32)


def ret_fwd(cfg, q, k, v, tabs, name):
    blk, row, tspecs, n, dk = _ret_specs(cfg, False)

    def body(q_ref, k_ref, v_ref, dm_ref, lq_ref, lk_ref, gc_ref, on_ref, rs_ref, s_ref):
        @pl.when(pl.program_id(1) == 0)
        def _():
            s_ref[...] = jnp.zeros_like(s_ref)

        qq, kk, vv = q_ref[...], k_ref[...], v_ref[...]
        a = lax.dot_general(qq, kk, _NT, preferred_element_type=F32) * dm_ref[...]
        o = lax.dot_general(a.astype(BF16), vv, _NN, preferred_element_type=F32)
        o = o + lax.dot_general(qq, s_ref[...].astype(BF16), _NN, preferred_element_type=F32) * lq_ref[...]
        _ret_state_update(s_ref, kk, vv, lk_ref[...], gc_ref[...])
        mu = jnp.mean(o, axis=-1, keepdims=True)
        oc = o - mu
        rstd = lax.rsqrt(jnp.mean(oc * oc, axis=-1, keepdims=True) + LN_EPS)
        on_ref[...] = oc * rstd
        rs_ref[...] = rstd

    return pl.pallas_call(body, grid=(cfg.ret_heads, n), in_specs=[blk, blk, blk] + tspecs, out_specs=[blk, row],
                          out_shape=[jax.ShapeDtypeStruct((cfg.tp, cfg.d), F32), jax.ShapeDtypeStruct((cfg.ret_heads, cfg.tp, 1), F32)],
                          scratch_shapes=[pltpu.VMEM((dk, dk), F32)], compiler_params=_cparams("parallel", "arbitrary"),
                          name=name)(q, k, v, *tabs)


def ret_bwd_q(cfg, q, k, v, d_on, on, rstd, tabs, name):
    blk, row, tspecs, n, dk = _ret_specs(cfg, False)

    def body(q_ref, k_ref, v_ref, don_ref, on_ref, rs_ref, dm_ref, lq_ref, lk_ref, gc_ref, do_ref, dq_ref, s_ref):
        @pl.when(pl.program_id(1) == 0)
        def _():
            s_ref[...] = jnp.zeros_like(s_ref)

        kk, vv = k_ref[...], v_ref[...]
        don, on = don_ref[...], on_ref[...]
        do = rs_ref[...] * (don - jnp.mean(don, axis=-1, keepdims=True) - on * jnp.mean(don * on, axis=-1, keepdims=True))
        dob = do.astype(BF16)
        do_ref[...] = dob
        ds = lax.dot_general(dob, vv, _NT, preferred_element_type=F32) * dm_ref[...]
        dq = lax.dot_general(ds.astype(BF16), kk, _NN, preferred_element_type=F32)
        dq_ref[...] = dq + lax.dot_general(dob, s_ref[...].astype(BF16), _NT, preferred_element_type=F32) * lq_ref[...]
        _ret_state_update(s_ref, kk, vv, lk_ref[...], gc_ref[...])

    return pl.pallas_call(body, grid=(cfg.ret_heads, n), in_specs=[blk, blk, blk, blk, blk, row] + tspecs, out_specs=[blk, blk],
                          out_shape=[jax.ShapeDtypeStruct((cfg.tp, cfg.d), BF16), jax.ShapeDtypeStruct((cfg.tp, cfg.d), F32)],
                          scratch_shapes=[pltpu.VMEM((dk, dk), F32)], compiler_params=_cparams("parallel", "arbitrary"),
                          name=name)(q, k, v, d_on, on, rstd, *tabs)


def ret_bwd_kv(cfg, q, k, v, do, tabs, name):
    blk, row, tspecs, n, dk = _ret_specs(cfg, True)

    def body(q_ref, k_ref, v_ref, do_ref, dm_ref, lq_ref, lk_ref, gc_ref, dk_ref, dv_ref, g_ref):
        @pl.when(pl.program_id(1) == 0)
        def _():
            g_ref[...] = jnp.zeros_like(g_ref)

        qq, kk, vv, dob = q_ref[...], k_ref[...], v_ref[...], do_ref[...]
        dm, lk = dm_ref[...], lk_ref[...]
        gb = g_ref[...].astype(BF16)
        a = lax.dot_general(qq, kk, _NT, preferred_element_type=F32) * dm
        ds = lax.dot_general(dob, vv, _NT, preferred_element_type=F32) * dm
        dkk = lax.dot_general(ds.astype(BF16), qq, _TN, preferred_element_type=F32)
        dk_ref[...] = dkk + lax.dot_general(vv, gb, _NT, preferred_element_type=F32) * lk
        kw = (kk.astype(F32) * lk).astype(BF16)
        dvv = lax.dot_general(a.astype(BF16), dob, _TN, preferred_element_type=F32)
        dv_ref[...] = (dvv + lax.dot_general(kw, gb, _NN, preferred_element_type=F32)).astype(BF16)
        qw = (qq.astype(F32) * lq_ref[...]).astype(BF16)
        g_ref[...] = gc_ref[...] * g_ref[...] + lax.dot_general(qw, dob, _TN, preferred_element_type=F32)

    return pl.pallas_call(body, grid=(cfg.ret_heads, n), in_specs=[blk, blk, blk, blk] + tspecs, out_specs=[blk, blk],
                          out_shape=[jax.ShapeDtypeStruct((cfg.tp, cfg.d), F32), jax.ShapeDtypeStruct((cfg.tp, cfg.d), BF16)],
                          scratch_shapes=[pltpu.VMEM((dk, dk), F32)], compiler_params=_cparams("parallel", "arbitrary"),
                          name=name)(q, k, v, do, *tabs)


def _flat2(a):
    return a.reshape(-1, a.shape[-1])


def _ew_call(body, ins, out_dtypes, name, lead=None):
    r, c = ins[-1].shape[-2:]
    tr = _div_tile(r, max(16, (1 << 18) // c), 16)
    specs = []
    for a in ins:
        if a.ndim == 3:
            specs.append(pl.BlockSpec((a.shape[0], tr, c), lambda i: (0, i, 0)))
        else:
            specs.append(pl.BlockSpec((tr, c), lambda i: (i, 0)))
    return pl.pallas_call(body, grid=(r // tr,), in_specs=specs,
                          out_specs=[pl.BlockSpec((tr, c), lambda i: (i, 0)) for _ in out_dtypes],
                          out_shape=[jax.ShapeDtypeStruct((r, c), dt) for dt in out_dtypes],
                          compiler_params=_cparams("parallel"), name=name)(*ins)


def adamw(w, g, m, v, name):
    c1 = 1.0 - ADAM_B1 ** ADAM_STEP
    c2 = 1.0 - ADAM_B2 ** ADAM_STEP

    def body(w_ref, g_ref, m_ref, v_ref, d_ref, mo_ref, vo_ref):
        gg = g_ref[...]
        mn = ADAM_B1 * m_ref[...] + (1.0 - ADAM_B1) * gg
        vn = ADAM_B2 * v_ref[...] + (1.0 - ADAM_B2) * (gg * gg)
        d_ref[...] = -ADAM_LR * ((mn / c1) / (jnp.sqrt(vn / c2) + ADAM_EPS) + ADAM_WD * w_ref[...])
        mo_ref[...] = mn
        vo_ref[...] = vn

    outs = _ew_call(body, [_flat2(w), _flat2(g), _flat2(m), _flat2(v)], [F32, F32, F32], name)
    return [o.reshape(w.shape) for o in outs]


def add_pair_bf16(g, theirs, core, name):
    c = g.shape[-1]
    g4, t3 = g.reshape(4, 2, -1, c), theirs.reshape(4, -1, c)
    rows = t3.shape[1]
    tr = _div_tile(rows, max(16, (1 << 18) // c), 16)

    def body(core_ref, g_ref, t_ref, o_ref):
        o_ref[...] = (g_ref[...] + t_ref[...]).astype(BF16)

    spec = pl.BlockSpec((None, tr, c), lambda q, i, core_ref: (q, i, 0))
    grid_spec = pltpu.PrefetchScalarGridSpec(
        num_scalar_prefetch=1, grid=(4, rows // tr),
        in_specs=[pl.BlockSpec((None, None, tr, c), lambda q, i, core_ref: (q, core_ref[0], i, 0)), spec], out_specs=spec)
    out = pl.pallas_call(body, grid_spec=grid_spec, out_shape=jax.ShapeDtypeStruct((4, rows, c), BF16),
                         compiler_params=_cparams("parallel", "parallel"), name=name)(core, g4, t3)
    return out.reshape(theirs.shape)


def sum_slots(parts, landed, chip_core, name):
    c = parts.shape[-1]
    p3, l3 = parts.reshape(4, -1, c), landed.reshape(3, -1, c)
    rows = p3.shape[1]
    tr = _div_tile(rows, max(16, (1 << 18) // c), 16)

    def body(cc_ref, p_ref, l_ref, o_ref):
        acc = p_ref[...].astype(F32)
        for s in range(3):
            acc = acc + l_ref[s].astype(F32)
        o_ref[...] = acc

    grid_spec = pltpu.PrefetchScalarGridSpec(
        num_scalar_prefetch=1, grid=(rows // tr,),
        in_specs=[pl.BlockSpec((None, tr, c), lambda i, cc: (cc[0], i, 0)), pl.BlockSpec((3, tr, c), lambda i, cc: (0, i, 0))],
        out_specs=pl.BlockSpec((None, tr, c), lambda i, cc: (cc[1], i, 0)))
    out = pl.pallas_call(body, grid_spec=grid_spec, out_shape=jax.ShapeDtypeStruct((2, rows, c), F32),
                         compiler_params=_cparams("parallel"), name=name)(chip_core, p3, l3)
    return out.reshape((2,) + parts.shape[2:])


def cast_place(w, chip, name):
    c = w.shape[-1]
    w2 = w.reshape(-1, c)
    rows = w2.shape[0]
    tr = _div_tile(rows, max(16, (1 << 18) // c), 16)

    def body(chip_ref, w_ref, o_ref):
        o_ref[...] = w_ref[...].astype(BF16)

    grid_spec = pltpu.PrefetchScalarGridSpec(
        num_scalar_prefetch=1, grid=(rows // tr,),
        in_specs=[pl.BlockSpec((tr, c), lambda i, chip_ref: (i, 0))],
        out_specs=pl.BlockSpec((None, tr, c), lambda i, chip_ref: (chip_ref[0], i, 0)))
    out = pl.pallas_call(body, grid_spec=grid_spec, out_shape=jax.ShapeDtypeStruct((4, rows, c), BF16),
                         compiler_params=_cparams("parallel"), name=name)(chip, w2)
    return out.reshape((4,) + w.shape)


def _coords():
    return lax.axis_index("x"), lax.axis_index("y"), lax.axis_index("c")


def _chip_peer(x, y, k):
    return (x ^ (k >> 1), y ^ (k & 1))


_ANY = pl.BlockSpec(memory_space=pl.ANY)
DMA_PIECE_BYTES = 2 << 20
DMA_MAX_PIECES = 32


def _piece_plan(shape, itemsize):
    want = max(1, min(DMA_MAX_PIECES, math.prod(shape) * itemsize // DMA_PIECE_BYTES))
    plan = []
    for ax, n in enumerate(shape[:-1]):
        if want <= 1:
            break
        rows_tiled = ax == len(shape) - 2
        k = max([1] + [c for c in range(2, min(n, want) + 1) if n % c == 0 and (not rows_tiled or (n // c) % 16 == 0)])
        if k > 1:
            plan.append((ax, k))
            want = -(-want // k)
    return plan


class _Copy:
    def __init__(self, src, dst, send_sem, recv_sem=None, device_id=None):
        self.src, self.dst, self.send_sem, self.recv_sem, self.device_id = src, dst, send_sem, recv_sem, device_id

    def _dma(self, src, dst):
        if self.device_id is None:
            return pltpu.make_async_copy(src, dst, self.send_sem)
        return pltpu.make_async_remote_copy(src_ref=src, dst_ref=dst, send_sem=self.send_sem, recv_sem=self.recv_sem,
                                            device_id=self.device_id, device_id_type=MESH)

    def start(self):
        shape = self.src.shape
        plan = _piece_plan(shape, jnp.dtype(self.src.dtype).itemsize)
        for pick in np.ndindex(*[k for _, k in plan]):
            idx = [slice(None)] * len(shape)
            for (ax, k), i in zip(plan, pick):
                step = shape[ax] // k
                idx[ax] = pl.ds(i * step, step)
            idx = tuple(idx)
            self._dma(self.src.at[idx], self.dst.at[idx]).start()

    def wait(self):
        self._dma(self.src, self.dst).wait()

    def wait_send(self):
        self._dma(self.src, self.dst).wait_send()

    def wait_recv(self):
        self._dma(self.src, self.dst).wait_recv()


def allreduce_small(x, name):
    r, _, w = x.shape

    def body(x_ref, o_ref, mine_ref, gat_ref, send_sems, recv_sems):
        mx, my, mc = _coords()
        me = 4 * mx + 2 * my + mc
        mine_ref[...] = jnp.sum(x_ref[...], axis=1)
        gat_ref[me] = mine_ref[...]
        copies = []
        for k in range(1, 8):
            peer = (mx ^ (k >> 2), my ^ ((k >> 1) & 1), mc ^ (k & 1))
            cp = pltpu.make_async_remote_copy(src_ref=mine_ref, dst_ref=gat_ref.at[me], send_sem=send_sems.at[k - 1],
                                              recv_sem=recv_sems.at[k - 1], device_id=peer, device_id_type=MESH)
            cp.start()
            copies.append(cp)
        for k in range(1, 8):
            pltpu.make_async_remote_copy(src_ref=mine_ref, dst_ref=gat_ref.at[me ^ k], send_sem=send_sems.at[k - 1],
                                         recv_sem=recv_sems.at[k - 1], device_id=(mx, my, mc), device_id_type=MESH).wait_recv()
        for cp in copies:
            cp.wait_send()
        acc = gat_ref[0]
        for s in range(1, 8):
            acc = acc + gat_ref[s]
        o_ref[...] = acc

    return pl.pallas_call(body, out_shape=jax.ShapeDtypeStruct((r, w), F32),
                          in_specs=[pl.BlockSpec(memory_space=pltpu.VMEM)], out_specs=pl.BlockSpec(memory_space=pltpu.VMEM),
                          scratch_shapes=[pltpu.VMEM((r, w), F32), pltpu.VMEM((8, r, w), F32),
                                          pltpu.SemaphoreType.DMA((7,)), pltpu.SemaphoreType.DMA((7,))],
                          compiler_params=pltpu.CompilerParams(vmem_limit_bytes=VMEM_LIMIT_V7X), name=name)(x)


def allgather_chips(shards, name):
    n = len(shards)

    def body(*refs):
        outs = refs[n:2 * n]
        s1_send, s1_recv, s2_send, s2_recv = refs[2 * n:]
        mx, my, mc = _coords()
        p = 2 * mx + my
        sib = (mx, my, 1 - mc)
        sends = []
        for a in range(n):
            h0 = outs[a].shape[1] // 2
            mine = outs[a].at[p, pl.ds(mc * h0, h0)]
            for k in (1, 2, 3):
                qx, qy = _chip_peer(mx, my, k)
                cp = _Copy(mine, mine, s1_send.at[a, k - 1], s1_recv.at[a, k - 1], (qx, qy, mc))
                cp.start()
                sends.append(cp)
        for a in range(n):
            h0 = outs[a].shape[1] // 2
            half = pl.ds(mc * h0, h0)
            for k in (1, 2, 3):
                landed = outs[a].at[p ^ k, half]
                _Copy(landed, landed, s1_send.at[a, k - 1], s1_recv.at[a, k - 1], sib).wait_recv()
                fw = _Copy(landed, landed, s2_send.at[a, k - 1], s2_recv.at[a, k - 1], sib)
                fw.start()
                sends.append(fw)
        for a in range(n):
            h0 = outs[a].shape[1] // 2
            other = pl.ds((1 - mc) * h0, h0)
            for k in (1, 2, 3):
                theirs = outs[a].at[p ^ k, other]
                _Copy(theirs, theirs, s2_send.at[a, k - 1], s2_recv.at[a, k - 1], sib).wait_recv()
        for cp in sends:
            cp.wait_send()

    return pl.pallas_call(body, out_shape=[jax.ShapeDtypeStruct(s.shape, s.dtype) for s in shards],
                          in_specs=[_ANY] * n, out_specs=[_ANY] * n, input_output_aliases={a: a for a in range(n)},
                          scratch_shapes=[pltpu.SemaphoreType.DMA((n, 3)), pltpu.SemaphoreType.DMA((n, 3)),
                                          pltpu.SemaphoreType.DMA((n, 3)), pltpu.SemaphoreType.DMA((n, 3))],
                          name=name)(*shards)


def ag_chips_side(bufs):
    n = len(bufs)

    def each(inplace, sems, act):
        s_send, s_recv = sems
        mx, my, mc = _coords()
        p = 2 * mx + my
        for a in range(n):
            h0 = inplace[a].shape[1] // 2
            half = pl.ds(mc * h0, h0)
            mine = inplace[a].at[p, half]
            for k in (1, 2, 3):
                qx, qy = _chip_peer(mx, my, k)
                act(_Copy(mine, mine, s_send.at[a, k - 1], s_recv.at[a, k - 1], (qx, qy, mc)),
                    _Copy(inplace[a].at[p ^ k, half], inplace[a].at[p ^ k, half], s_send.at[a, k - 1], s_recv.at[a, k - 1], (qx, qy, mc)))

    def start(ins, inplace, outs, sems):
        each(inplace, sems, lambda send, landed: send.start())

    def finish(ins, inplace, outs, sems):
        each(inplace, sems, lambda send, landed: (landed.wait_recv(), send.wait_send()))

    return Side([], list(bufs), [], [pltpu.SemaphoreType.DMA((n, 3)), pltpu.SemaphoreType.DMA((n, 3))], start, finish)


def ag_forward(bufs, name):
    n = len(bufs)

    def body(*refs):
        outs = refs[n:2 * n]
        s_send, s_recv = refs[2 * n:]
        mx, my, mc = _coords()
        p = 2 * mx + my
        sib = (mx, my, 1 - mc)
        sends = []
        for a in range(n):
            h0 = outs[a].shape[1] // 2
            for k in (1, 2, 3):
                landed = outs[a].at[p ^ k, pl.ds(mc * h0, h0)]
                fw = _Copy(landed, landed, s_send.at[a, k - 1], s_recv.at[a, k - 1], sib)
                fw.start()
                sends.append(fw)
        for a in range(n):
            h0 = outs[a].shape[1] // 2
            for k in (1, 2, 3):
                theirs = outs[a].at[p ^ k, pl.ds((1 - mc) * h0, h0)]
                _Copy(theirs, theirs, s_send.at[a, k - 1], s_recv.at[a, k - 1], sib).wait_recv()
        for fw in sends:
            fw.wait_send()

    return pl.pallas_call(body, out_shape=[jax.ShapeDtypeStruct(s.shape, s.dtype) for s in bufs],
                          in_specs=[_ANY] * n, out_specs=[_ANY] * n, input_output_aliases={a: a for a in range(n)},
                          scratch_shapes=[pltpu.SemaphoreType.DMA((n, 3)), pltpu.SemaphoreType.DMA((n, 3))], name=name)(*bufs)


def rs_chips_side(parts):
    n = len(parts)

    def copies(ins, outs, sems):
        send_sems, recv_sems = sems
        mx, my, mc = _coords()
        p = 2 * mx + my
        return [_Copy(ins[a].at[p ^ k], outs[a].at[k - 1], send_sems.at[a, k - 1], recv_sems.at[a, k - 1],
                      (*_chip_peer(mx, my, k), mc)) for a in range(n) for k in (1, 2, 3)]

    def start(ins, inplace, outs, sems):
        for cp in copies(ins, outs, sems):
            cp.start()

    def finish(ins, inplace, outs, sems):
        for cp in copies(ins, outs, sems):
            cp.wait()

    return Side(list(parts), [], [jax.ShapeDtypeStruct((3,) + g.shape[1:], g.dtype) for g in parts],
                [pltpu.SemaphoreType.DMA((n, 3)), pltpu.SemaphoreType.DMA((n, 3))], start, finish)


def exchange_halves(grads, name):
    n = len(grads)

    def body(*refs):
        ins, theirs = refs[:n], refs[n:2 * n]
        send_sems, recv_sems = refs[2 * n:]
        mx, my, mc = _coords()
        pend = []
        for a in range(n):
            cp = _Copy(ins[a].at[:, pl.ds(1 - mc, 1)], theirs[a], send_sems.at[a], recv_sems.at[a], (mx, my, 1 - mc))
            cp.start()
            pend.append(cp)
        for cp in pend:
            cp.wait()

    shp = [jax.ShapeDtypeStruct((4, 1) + g.shape[2:], g.dtype) for g in grads]
    return pl.pallas_call(body, out_shape=shp, in_specs=[_ANY] * n, out_specs=[_ANY] * n,
                          scratch_shapes=[pltpu.SemaphoreType.DMA((n,)), pltpu.SemaphoreType.DMA((n,))], name=name)(*grads)


def scatter_chips(parts, name):
    n = len(parts)
    side = rs_chips_side(parts)

    def body(*refs):
        ins, outs, sems = refs[:n], refs[n:2 * n], refs[2 * n:]
        side.start(ins, [], outs, sems)
        side.finish(ins, [], outs, sems)

    return pl.pallas_call(body, out_shape=side.outs, in_specs=[_ANY] * n, out_specs=[_ANY] * n, scratch_shapes=side.sems,
                          name=name)(*parts)


def join_halves(halves, name):
    n = len(halves)

    def body(*refs):
        outs = refs[n:2 * n]
        send_sems, recv_sems = refs[2 * n:]
        mx, my, mc = _coords()
        pend = []
        for a in range(n):
            mine = outs[a].at[pl.ds(mc, 1)]
            cp = _Copy(mine, mine, send_sems.at[a], recv_sems.at[a], (mx, my, 1 - mc))
            cp.start()
            pend.append((cp, outs[a].at[pl.ds(1 - mc, 1)]))
        for a, (cp, theirs) in enumerate(pend):
            cp.wait_send()
            _Copy(theirs, theirs, send_sems.at[a], recv_sems.at[a], (mx, my, 1 - mc)).wait_recv()

    return pl.pallas_call(body, out_shape=[jax.ShapeDtypeStruct(g.shape, g.dtype) for g in halves],
                          in_specs=[_ANY] * n, out_specs=[_ANY] * n, input_output_aliases={a: a for a in range(n)},
                          scratch_shapes=[pltpu.SemaphoreType.DMA((n,)), pltpu.SemaphoreType.DMA((n,))], name=name)(*halves)


def rs_pair_sums(grads, tag):
    theirs = exchange_halves(grads, f"rs_pair_{tag}")
    core = lax.axis_index("c").astype(jnp.int32).reshape(1)
    return [add_pair_bf16(g, t, core, f"rs_add_{tag}_{i}") for i, (g, t) in enumerate(zip(grads, theirs))]


def rs_finish(parts, landed, tag):
    chip_core = jnp.stack([2 * lax.axis_index("x") + lax.axis_index("y"), lax.axis_index("c")]).astype(jnp.int32)
    halves = [sum_slots(pt, ld, chip_core, f"rs_sum_{tag}_{i}") for i, (pt, ld) in enumerate(zip(parts, landed))]
    return join_halves(halves, f"rs_join_{tag}")


W_NAMES = ['meta_tokens', 'l0_w_in', 'l0_conv_w', 'l0_conv_b', 'l0_w_a', 'l0_b_a', 'l0_w_x', 'l0_b_x', 'l0_lam', 'l0_w_out',
           'l0_ln_g', 'l0_ln_b', 'l1_w_in', 'l1_w_grp', 'l1_scale', 'l1_w_out', 'l1_ln_g', 'l1_ln_b', 'l2_w_in', 'l2_q_norm',
           'l2_w_uq', 'l2_kv_norm', 'l2_w_ukv', 'l2_w_out', 'l2_ln_g', 'l2_ln_b', 'l3_w_in', 'l3_w_out', 'l3_ln_g', 'l3_ln_b']
BIG = {0: ['l0_w_in', 'l0_w_a', 'l0_w_x', 'l0_w_out'], 1: ['l1_w_in', 'l1_w_grp', 'l1_w_out'],
       2: ['l2_w_in', 'l2_w_uq', 'l2_w_ukv', 'l2_w_out'], 3: ['l3_w_in', 'l3_w_out']}
SMALL_SHARDED = ['meta_tokens', 'l0_conv_w']
SMALL_REPL = ['l0_conv_b', 'l0_b_a', 'l0_b_x', 'l0_lam', 'l0_ln_g', 'l0_ln_b', 'l1_scale', 'l1_ln_g', 'l1_ln_b',
              'l2_q_norm', 'l2_kv_norm', 'l2_ln_g', 'l2_ln_b', 'l3_ln_g', 'l3_ln_b']


def _rows8(rows, width):
    out = []
    for r in rows:
        r = r.reshape(-1, r.shape[-1])
        out.append(jnp.pad(r, ((0, 8 - r.shape[0]), (0, width - r.shape[1]))))
    return jnp.stack(out)


def _unblock(g4, axis):
    return jnp.concatenate([g4[i] for i in range(4)], axis=axis)


def _block(full, axis):
    return jnp.stack(jnp.split(full, 4, axis=axis))


def _step(cfg, a):
    d, tp, nm, seq = cfg.d, cfg.tp, cfg.n_meta, cfg.seq
    alpha = (2.0 * cfg.depth) ** 0.25
    mx, my, mc = _coords()
    p = 2 * mx + my
    dq4 = d // 4
    hm, npe, ql, kl = cfg.mla_heads, cfg.nope, cfg.q_lora, cfg.kv_lora
    hp = 2 * npe
    qk = npe + cfg.rope
    vec = lambda name: a[name].reshape(1, -1)

    sm = jnp.concatenate([a['meta_tokens'], a['l0_conv_w'].reshape(CONV_W, dq4)], axis=0)
    placed = lax.dynamic_update_slice(jnp.zeros((nm + CONV_W, d), F32), sm * (mc == 0).astype(F32), (0, p * dq4))
    gathered = allreduce_small(_rows8([placed[i:i + 1] for i in range(nm + CONV_W)], d), "gather_small")
    meta_full, conv_w4 = gathered[:nm], gathered[nm:]

    chip = p.astype(jnp.int32).reshape(1)
    wg = {n: cast_place(a[n], chip, f"place_{n}") for names in BIG.values() for n in names}
    wg.update(zip(BIG[0], allgather_chips([wg[n] for n in BIG[0]], "ag_l0")))

    def hosted(call, names):
        out, got = call(ag_chips_side([wg[n] for n in names]))
        wg.update(zip(names, got))
        return out

    def gathered_layer(layer):
        wg.update(zip(BIG[layer], ag_forward([wg[n] for n in BIG[layer]], f"ag_fwd_l{layer}")))

    heads_first = lambda w: jnp.moveaxis(w, 0, 1).reshape(w.shape[1], w.shape[0] * w.shape[2], w.shape[3])
    w_out = lambda i: wg[f'l{i}_w_out'].reshape(1, d, d)

    zpad = jnp.zeros((tp - nm - seq, d), F32)
    h0 = jnp.concatenate([meta_full, a['x'][0], zpad], axis=0)
    h0_bf = h0.astype(BF16)

    w0_in, w0_a, w0_x = wg['l0_w_in'], heads_first(wg['l0_w_a']), heads_first(wg['l0_w_x'])
    ug0 = hosted(lambda side: mm_nn(cfg, h0_bf, w0_in, "l0_in", side=side), ['l1_w_in'])
    u0, u0_bf = conv_fwd(cfg, ug0, conv_w4, vec('l0_conv_b'), "l0_conv")
    pa0 = bd_nn(cfg, u0_bf, w0_a, "l0_gate_a")
    px0 = bd_nn(cfg, u0_bf, w0_x, "l0_gate_x")
    hs0, hprev0, y0 = lru_fwd(cfg, u0, pa0, px0, ug0, vec('l0_b_a'), vec('l0_b_x'), vec('l0_lam'), "l0_lru")
    o0 = hosted(lambda side: mm_nn(cfg, y0, w_out(0), "l0_out", side=side), ['l1_w_grp', 'l1_w_out'])
    h1, h1_bf, xh0, rs0 = ln_fwd(cfg, h0, o0, vec('l0_ln_g'), vec('l0_ln_b'), alpha, "l0_ln")

    gathered_layer(1)
    w1_in, w1_grp = wg['l1_w_in'], heads_first(wg['l1_w_grp'])
    ug1 = hosted(lambda side: mm_nn(cfg, h1_bf, w1_in, "l1_in", side=side), ['l2_w_in'])
    p1 = pool_fwd(cfg, ug1, "l1_pool")
    mm1 = bd_nn(cfg, p1, w1_grp, "l1_grp")
    y1 = gate_fwd(cfg, mm1, ug1, 1, vec('l1_scale'), "l1_gate")
    o1 = hosted(lambda side: mm_nn(cfg, y1, w_out(1), "l1_out", side=side), ['l2_w_uq', 'l2_w_ukv', 'l2_w_out'])
    h2, h2_bf, xh1, rs1 = ln_fwd(cfg, h1, o1, vec('l1_ln_g'), vec('l1_ln_b'), alpha, "l1_ln")

    gathered_layer(2)
    w2_in = jnp.pad(_unblock(wg['l2_w_in'], 1), ((0, 0), (0, cfg.cw - (ql + kl + cfg.rope))))[None]
    w2_uq = jnp.pad(_unblock(wg['l2_w_uq'], 1).reshape(ql, hm, qk), ((0, 0), (0, 0), (0, hp - qk))).reshape(1, ql, hm * hp)
    w2_ukv = _unblock(wg['l2_w_ukv'], 1).reshape(kl, hm, 2, npe).transpose(0, 2, 1, 3).reshape(1, kl, 2 * hm * npe)
    proj = mm_nn(cfg, h2_bf, w2_in, "l2_in")
    cqn, ckvn = mla_norm(cfg, proj, vec('l2_q_norm'), vec('l2_kv_norm'), "l2_norm")
    q_raw = mm_nn(cfg, cqn, w2_uq, "l2_uq")
    kv_raw = mm_nn(cfg, ckvn, w2_ukv, "l2_ukv")
    tabs = _mla_tables(cfg)
    qf, kf, vb = mla_prep(cfg, q_raw, kv_raw, proj, tabs, "l2_prep")
    o_att, lse = hosted(lambda side: attn_fwd(cfg, qf, kf, vb, "l2_attn", side=side), ['l3_w_in', 'l3_w_out'])
    y2 = gate_fwd(cfg, o_att, proj, 0, None, "l2_gate")
    o2 = mm_nn(cfg, y2, w_out(2), "l2_out")
    h3, h3_bf, xh2, rs2 = ln_fwd(cfg, h2, o2, vec('l2_ln_g'), vec('l2_ln_b'), alpha, "l2_ln")

    gathered_layer(3)
    w3_in = wg['l3_w_in']
    qkvg = mm_nn(cfg, h3_bf, w3_in, "l3_in")
    cos, sin = _ret_rope_tables(cfg)
    qr, kr, vr = ret_prep(cfg, qkvg, cos, sin, "l3_prep")
    rt = _ret_tables(cfg)
    on, rsr = ret_fwd(cfg, qr, kr, vr, rt, "l3_ret")
    y3 = gate_fwd(cfg, on, qkvg, 3, None, "l3_gate")
    o3 = mm_nn(cfg, y3, w_out(3), "l3_out")
    h4, _, xh3, rs3 = ln_fwd(cfg, h3, o3, vec('l3_ln_g'), vec('l3_ln_b'), alpha, "l3_ln")

    tgt = jnp.concatenate([jnp.zeros((nm, d), F32), a['loss_target'][0], zpad], axis=0)
    dy4, lacc = loss_grad(cfg, h4, tgt, "loss")
    loss = lax.psum(0.5 * jnp.sum(lacc) / d, ("x", "y", "c"))

    grads, small, parts, landed = {}, {}, {}, {}

    def rs_begin(layer, full):
        got = rs_pair_sums([g.reshape((4, 2, g.shape[1] // 2) + g.shape[2:]) for g in full], f"l{layer}")
        parts.update(zip(BIG[layer], got))

    def carried(call, names):
        out, got = call(rs_chips_side([parts[n] for n in names]))
        landed.update(zip(names, got))
        return out

    def rs_end(layer):
        names = BIG[layer]
        got = rs_finish([parts[n] for n in names], [landed[n] for n in names], f"l{layer}")
        for n, g in zip(names, got):
            grads[n] = g.reshape(a[n].shape)

    dz3, dz3_bf, small['l3_ln_g'], small['l3_ln_b'] = ln_bwd(cfg, None, dy4, xh3, rs3, vec('l3_ln_g'), alpha, "l3_ln_b")
    g_out = mm_tn(cfg, y3, dz3_bf, 1, "l3_dw_out")
    dyb = mm_nt(cfg, dz3_bf, w_out(3), "l3_dy")
    d_on, dg = gate_bwd(cfg, dyb, on, qkvg, 3, None, F32, "l3_gate_b")
    do_bf, dqr = ret_bwd_q(cfg, qr, kr, vr, d_on, on, rsr, rt, "l3_ret_bq")
    dkr, dvr = ret_bwd_kv(cfg, qr, kr, vr, do_bf, rt, "l3_ret_bkv")
    d_in = ret_prep_bwd(cfg, dqr, dkr, dvr, dg, cos, sin, "l3_prep_b")
    g_in = mm_tn(cfg, h3_bf, d_in, 4, "l3_dw_in")
    dh = mm_nt(cfg, d_in, w3_in, "l3_dh")
    rs_begin(3, [g_in, g_out.reshape(4, dq4, d)])

    dz2, dz2_bf, small['l2_ln_g'], small['l2_ln_b'] = ln_bwd(cfg, dz3, dh, xh2, rs2, vec('l2_ln_g'), alpha, "l2_ln_b")
    g_out = mm_tn(cfg, y2, dz2_bf, 1, "l2_dw_out")
    dyb = mm_nt(cfg, dz2_bf, w_out(2), "l2_dy")
    do_bf, dg = gate_bwd(cfg, dyb, o_att, proj, 0, None, BF16, "l2_gate_b")
    dq_full, dk_full, dv_bf = carried(lambda side: attn_bwd(cfg, qf, kf, vb, do_bf, o_att, lse, "l2_attn_b", side=side), BIG[3])
    rs_end(3)
    dq_raw, dk_nope, d_kr = mla_prep_bwd(cfg, dq_full, dk_full, tabs, "l2_prep_b")
    dkv_raw = jnp.concatenate([dk_nope, dv_bf], axis=1)
    g_uq = mm_tn(cfg, cqn, dq_raw, 1, "l2_dw_uq")
    d_cqn = mm_nt(cfg, dq_raw, w2_uq, "l2_dcq")
    g_ukv = mm_tn(cfg, ckvn, dkv_raw, 1, "l2_dw_ukv")
    d_ckvn = mm_nt(cfg, dkv_raw, w2_ukv, "l2_dckv")
    d_c, nacc = mla_norm_bwd(cfg, proj, d_cqn, d_ckvn, d_kr, vec('l2_q_norm'), vec('l2_kv_norm'), "l2_norm_b")
    small['l2_q_norm'], small['l2_kv_norm'] = nacc[:, :ql], nacc[:, ql:ql + kl]
    d_in = jnp.concatenate([dg, d_c], axis=1)
    g_in = mm_tn(cfg, h2_bf, d_in, 1, "l2_dw_in")
    dh = mm_nt(cfg, d_in, w2_in, "l2_dh")
    g_in = _block(g_in[0][:, :d + ql + kl + cfg.rope], 1)
    g_uq = _block(g_uq.reshape(ql, hm, hp)[:, :, :qk].reshape(ql, hm * qk), 1)
    g_ukv = _block(g_ukv.reshape(kl, 2, hm, npe).transpose(0, 2, 1, 3).reshape(kl, 2 * hm * npe), 1)
    rs_begin(2, [g_in, g_uq, g_ukv, g_out.reshape(4, dq4, d)])

    dz1, dz1_bf, small['l1_ln_g'], small['l1_ln_b'] = ln_bwd(cfg, dz2, dh, xh1, rs1, vec('l1_ln_g'), alpha, "l1_ln_b")
    g_out = mm_tn(cfg, y1, dz1_bf, 1, "l1_dw_out")
    dyb = mm_nt(cfg, dz1_bf, w_out(1), "l1_dy")
    d_mm1, dg, small['l1_scale'] = gate_bwd(cfg, dyb, mm1, ug1, 1, vec('l1_scale'), BF16, "l1_gate_b")
    g_grp = bd_tn(cfg, p1, d_mm1, len(cfg.pool_windows), "l1_dw_grp")
    dp = bd_nt(cfg, d_mm1, w1_grp, "l1_dp")
    du = pool_bwd(cfg, dp, "l1_pool_b")
    d_in = jnp.concatenate([du, dg], axis=1)
    g_in = carried(lambda side: mm_tn(cfg, h1_bf, d_in, 4, "l1_dw_in", side=side), ['l2_w_in', 'l2_w_uq'])
    dh = carried(lambda side: mm_nt(cfg, d_in, w1_in, "l1_dh", side=side), ['l2_w_ukv', 'l2_w_out'])
    rs_end(2)
    blocks_first = lambda g: jnp.moveaxis(g.reshape(g.shape[0], 4, g.shape[1] // 4, g.shape[2]), 1, 0)
    rs_begin(1, [g_in, blocks_first(g_grp), g_out.reshape(4, dq4, d)])

    dz0, dz0_bf, small['l0_ln_g'], small['l0_ln_b'] = ln_bwd(cfg, dz1, dh, xh0, rs0, vec('l0_ln_g'), alpha, "l0_ln_b")
    g_out = mm_tn(cfg, y0, dz0_bf, 1, "l0_dw_out")
    dyb = mm_nt(cfg, dz0_bf, w_out(0), "l0_dy")
    dg, dpa, dpx, du_dir, lacc0 = lru_bwd(cfg, dyb, ug0, hs0, hprev0, u0, pa0, px0, vec('l0_b_a'), vec('l0_b_x'), vec('l0_lam'), "l0_lru_b")
    g_a = bd_tn(cfg, u0_bf, dpa, cfg.lru_heads, "l0_dw_a")
    g_x = bd_tn(cfg, u0_bf, dpx, cfg.lru_heads, "l0_dw_x")
    du_a = bd_nt(cfg, dpa, w0_a, "l0_du_a")
    du_x = bd_nt(cfg, dpx, w0_x, "l0_du_x")
    du_pre, cacc = conv_bwd(cfg, du_dir, du_a, du_x, ug0, conv_w4, "l0_conv_b")
    d_in = jnp.concatenate([du_pre, dg], axis=1)
    g_in = carried(lambda side: mm_tn(cfg, h0_bf, d_in, 4, "l0_dw_in", side=side), ['l1_w_in'])
    dh = carried(lambda side: mm_nt(cfg, d_in, w0_in, "l0_dh", side=side), ['l1_w_grp', 'l1_w_out'])
    rs_end(1)
    rs_begin(0, [g_in, blocks_first(g_a), blocks_first(g_x), g_out.reshape(4, dq4, d)])
    landed.update(zip(BIG[0], scatter_chips([parts[n] for n in BIG[0]], "rs_chips_l0")))
    rs_end(0)
    dh0 = axpy(cfg, dz0, dh, alpha, "dh0")
    grad_x = dh0[nm:nm + seq][None]
    small['l0_lam'], small['l0_b_a'], small['l0_b_x'] = lacc0[0:1], lacc0[1:2], lacc0[2:3]
    small['l0_conv_b'] = cacc[4:5]

    rows = [dh0[i:i + 1] for i in range(nm)] + [cacc[k:k + 1] for k in range(CONV_W)] + [small[n] for n in SMALL_REPL]
    red = allreduce_small(_rows8(rows, d), "reduce_small")
    sh = lax.dynamic_slice(red[:nm + CONV_W], (0, p * dq4), (nm + CONV_W, dq4))
    grads['meta_tokens'] = sh[:nm]
    grads['l0_conv_w'] = sh[nm:].reshape(a['l0_conv_w'].shape)
    for i, n in enumerate(SMALL_REPL):
        grads[n] = red[nm + CONV_W + i, :a[n].shape[0]]

    delta, new_m, new_v = {}, {}, {}
    for n in sum(BIG.values(), []):
        delta[n], new_m[n], new_v[n] = adamw(a[n], grads[n], a['m_' + n], a['v_' + n], f"adamw_{n}")
    pack_s = lambda pre: jnp.concatenate([a[pre + 'meta_tokens'], a[pre + 'l0_conv_w'].reshape(CONV_W, dq4)], axis=0)
    ds_, ms_, vs_ = adamw(pack_s(''), sh, pack_s('m_'), pack_s('v_'), "adamw_small_sharded")
    for out, packed in ((delta, ds_), (new_m, ms_), (new_v, vs_)):
        out['meta_tokens'] = packed[:nm]
        out['l0_conv_w'] = packed[nm:].reshape(a['l0_conv_w'].shape)
    pack_r = lambda pre: jnp.stack([jnp.pad(a[pre + n], (0, d - a[n].shape[0])) for n in SMALL_REPL])
    dr_, mr_, vr_ = adamw(pack_r(''), red[nm + CONV_W:nm + CONV_W + len(SMALL_REPL)], pack_r('m_'), pack_r('v_'), "adamw_small_repl")
    for out, packed in ((delta, dr_), (new_m, mr_), (new_v, vr_)):
        for i, n in enumerate(SMALL_REPL):
            out[n] = packed[i, :a[n].shape[0]]

    return (loss, grad_x, *[grads[n] for n in W_NAMES], *[delta[n] for n in W_NAMES],
            *[new_m[n] for n in W_NAMES], *[new_v[n] for n in W_NAMES])


def kernel(x, meta_tokens, l0_w_in, l0_conv_w, l0_conv_b, l0_w_a, l0_b_a, l0_w_x, l0_b_x, l0_lam, l0_w_out, l0_ln_g, l0_ln_b, l1_w_in, l1_w_grp, l1_scale, l1_w_out, l1_ln_g, l1_ln_b, l2_w_in, l2_q_norm, l2_w_uq, l2_kv_norm, l2_w_ukv, l2_w_out, l2_ln_g, l2_ln_b, l3_w_in, l3_w_out, l3_ln_g, l3_ln_b, loss_target, m_meta_tokens, m_l0_w_in, m_l0_conv_w, m_l0_conv_b, m_l0_w_a, m_l0_b_a, m_l0_w_x, m_l0_b_x, m_l0_lam, m_l0_w_out, m_l0_ln_g, m_l0_ln_b, m_l1_w_in, m_l1_w_grp, m_l1_scale, m_l1_w_out, m_l1_ln_g, m_l1_ln_b, m_l2_w_in, m_l2_q_norm, m_l2_w_uq, m_l2_kv_norm, m_l2_w_ukv, m_l2_w_out, m_l2_ln_g, m_l2_ln_b, m_l3_w_in, m_l3_w_out, m_l3_ln_g, m_l3_ln_b, v_meta_tokens, v_l0_w_in, v_l0_conv_w, v_l0_conv_b, v_l0_w_a, v_l0_b_a, v_l0_w_x, v_l0_b_x, v_l0_lam, v_l0_w_out, v_l0_ln_g, v_l0_ln_b, v_l1_w_in, v_l1_w_grp, v_l1_scale, v_l1_w_out, v_l1_ln_g, v_l1_ln_b, v_l2_w_in, v_l2_q_norm, v_l2_w_uq, v_l2_kv_norm, v_l2_w_ukv, v_l2_w_out, v_l2_ln_g, v_l2_ln_b, v_l3_w_in, v_l3_w_out, v_l3_ln_g, v_l3_ln_b):
    return _step(REAL, dict(locals()))
```

```python
import functools
import math
from typing import NamedTuple

import jax
import jax.numpy as jnp
import numpy as np
from jax import lax
from jax.experimental import pallas as pl
from jax.experimental.pallas import tpu as pltpu

F32 = jnp.float32
BF16 = jnp.bfloat16

LN_EPS = 1e-5
RMS_EPS = 1e-6
ROPE_BASE = 10000.0
LRU_C = 8.0
CONV_W = 4
HIST = 16
ADAM_LR, ADAM_B1, ADAM_B2, ADAM_EPS, ADAM_WD, ADAM_STEP = 0.001, 0.9, 0.999, 1e-08, 0.01, 10
VMEM_LIMIT_V7X = 56 * 1024 * 1024
MESH = pl.DeviceIdType.MESH


class Cfg(NamedTuple):
    d: int
    seq: int
    n_meta: int
    tp: int
    depth: int
    lru_heads: int
    pool_windows: tuple
    mla_heads: int
    nope: int
    rope: int
    q_lora: int
    kv_lora: int
    cw: int
    ret_heads: int
    chunk: int
    tq: int
    tkw: int
    tr: int
    tt: int
    tc: int
    tm: int
    tn: int
    tkc: int
    tkm: int
    la: int


REAL = Cfg(d=4096, seq=4096, n_meta=16, tp=4224, depth=4, lru_heads=16, pool_windows=(2, 4, 8, 16),
           mla_heads=32, nope=128, rope=64, q_lora=1024, kv_lora=512, cw=2048, ret_heads=16, chunk=384,
           tq=384, tkw=2, tr=128, tt=1056, tc=512, tm=1408, tn=1024, tkc=4096, tkm=4224, la=128)


def _div_tile(n, target, align):
    best = None
    for t in range(align, min(n, target) + 1, align):
        if n % t == 0:
            best = t
    return best or n


def _cparams(*sem):
    return pltpu.CompilerParams(dimension_semantics=sem, vmem_limit_bytes=VMEM_LIMIT_V7X)


def _sig(x):
    return 1.0 / (1.0 + jnp.exp(-x))


def _expm1(x):
    p = x * (1.0 + x * (1.0 / 2) * (1.0 + x * (1.0 / 3) * (1.0 + x * (1.0 / 4) * (1.0 + x * (1.0 / 5) * (1.0 + x * (1.0 / 6))))))
    return jnp.where(x > -0.1, p, jnp.exp(x) - 1.0)


def _colsum8(x):
    r, w = x.shape
    return x.reshape(r // 8, 8, w).sum(axis=0)


class Side(NamedTuple):
    ins: list
    inplace: list
    outs: list
    sems: list
    start: object
    finish: object


def _host_call(body, grid, in_specs, out_specs, out_shape, scratch_shapes, semantics, name, args, side=None):
    if side is None:
        return pl.pallas_call(body, grid=grid, in_specs=in_specs, out_specs=out_specs, out_shape=out_shape,
                              scratch_shapes=scratch_shapes, compiler_params=_cparams(*semantics), name=name)(*args), None
    n_in, n_out, n_scr = len(in_specs), len(out_specs), len(scratch_shapes)
    n_si, n_sp, n_so = len(side.ins), len(side.inplace), len(side.outs)

    def wrapped(*refs):
        it = iter(refs)
        take = lambda k: [next(it) for _ in range(k)]
        ins, s_ins, s_inplace_in, outs, s_inplace, s_outs, scr, sems = (take(n_in), take(n_si), take(n_sp), take(n_out),
                                                                        take(n_sp), take(n_so), take(n_scr), take(len(side.sems)))
        ids = [pl.program_id(i) for i in range(len(grid))]
        first = functools.reduce(jnp.logical_and, [i == 0 for i in ids])
        last = functools.reduce(jnp.logical_and, [i == g - 1 for i, g in zip(ids, grid)])

        @pl.when(first)
        def _():
            side.start(s_ins, s_inplace, s_outs, sems)

        body(*ins, *outs, *scr)

        @pl.when(last)
        def _():
            side.finish(s_ins, s_inplace, s_outs, sems)

    any_spec = pl.BlockSpec(memory_space=pl.ANY)
    out = pl.pallas_call(
        wrapped, grid=grid, in_specs=list(in_specs) + [any_spec] * (n_si + n_sp),
        out_specs=list(out_specs) + [any_spec] * (n_sp + n_so),
        out_shape=list(out_shape) + [jax.ShapeDtypeStruct(x.shape, x.dtype) for x in side.inplace] + list(side.outs),
        scratch_shapes=list(scratch_shapes) + list(side.sems),
        input_output_aliases={n_in + n_si + i: n_out + i for i in range(n_sp)},
        compiler_params=_cparams(*(("arbitrary",) * len(grid))), name=name)(*args, *side.ins, *side.inplace)
    return out[:n_out], out[n_out:]


def _mm_call(a, b, grid, a_spec, b_spec, o_spec, out_sds, acc_shape, dims, name, side=None):
    nk = grid[-1]
    kax = len(grid) - 1

    def product(a_ref, b_ref):
        return lax.dot_general(a_ref[...].astype(BF16), b_ref[...].astype(BF16), dims, preferred_element_type=F32)

    def body_one(a_ref, b_ref, o_ref):
        o_ref[...] = product(a_ref, b_ref).astype(o_ref.dtype)

    def body_acc(a_ref, b_ref, o_ref, acc_ref):
        k = pl.program_id(kax)

        @pl.when(k == 0)
        def _():
            acc_ref[...] = jnp.zeros_like(acc_ref)

        acc_ref[...] += product(a_ref, b_ref)

        @pl.when(k == nk - 1)
        def _():
            o_ref[...] = acc_ref[...].astype(o_ref.dtype)

    sem = ("parallel",) * kax + ("arbitrary",)
    body, scratch = (body_one, []) if nk == 1 else (body_acc, [pltpu.VMEM(acc_shape, F32)])
    outs, side_outs = _host_call(body, grid, [a_spec, b_spec], [o_spec], [out_sds], scratch, sem, name, (a, b), side)
    return outs[0] if side is None else (outs[0], side_outs)


def _out_tile(cfg, tk):
    return cfg.tn if tk <= cfg.tn else cfg.tn // 2


_NN = (((1,), (0,)), ((), ()))
_NT = (((1,), (1,)), ((), ()))
_TN = (((0,), (0,)), ((), ()))


def mm_nn(cfg, a, w, name, out_dtype=F32, a_col0=0, side=None):
    m = a.shape[0]
    g, k, ns = w.shape
    tm, tk = _div_tile(m, cfg.tm, 16), _div_tile(k, cfg.tkc, cfg.la)
    tn = _div_tile(ns, _out_tile(cfg, tk), cfg.la)
    npb, a0 = ns // tn, a_col0 // tk
    assert a_col0 % tk == 0
    return _mm_call(a, w, (m // tm, g * npb, k // tk),
                    pl.BlockSpec((tm, tk), lambda i, j, kk: (i, a0 + kk)),
                    pl.BlockSpec((None, tk, tn), lambda i, j, kk: (j // npb, kk, j % npb)),
                    pl.BlockSpec((tm, tn), lambda i, j, kk: (i, j)),
                    jax.ShapeDtypeStruct((m, g * ns), out_dtype), (tm, tn), _NN, name, side)


def mm_nt(cfg, dy, w, name, out_dtype=F32, side=None):
    m = dy.shape[0]
    g, k, ns = w.shape
    tm, tk = _div_tile(m, cfg.tm, 16), _div_tile(ns, cfg.tkc, cfg.la)
    tn = _div_tile(k, _out_tile(cfg, tk), cfg.la)
    kpb = ns // tk
    return _mm_call(dy, w, (m // tm, k // tn, g * kpb),
                    pl.BlockSpec((tm, tk), lambda i, j, kk: (i, kk)),
                    pl.BlockSpec((None, tn, tk), lambda i, j, kk: (kk // kpb, j, kk % kpb)),
                    pl.BlockSpec((tm, tn), lambda i, j, kk: (i, j)),
                    jax.ShapeDtypeStruct((m, k), out_dtype), (tm, tn), _NT, name, side)


def mm_tn(cfg, x, dy, g, name, x_col0=0, k=None, side=None):
    m = x.shape[0]
    k = x.shape[1] if k is None else k
    ns = dy.shape[1] // g
    tk = _div_tile(m, cfg.tkm, 16)
    tmo, tn = _div_tile(k, _out_tile(cfg, tk), cfg.la), _div_tile(ns, cfg.tn, cfg.la)
    npb, x0 = ns // tn, x_col0 // tmo
    assert x_col0 % tmo == 0
    return _mm_call(x, dy, (k // tmo, g * npb, m // tk),
                    pl.BlockSpec((tk, tmo), lambda i, j, kk: (kk, x0 + i)),
                    pl.BlockSpec((tk, tn), lambda i, j, kk: (kk, j)),
                    pl.BlockSpec((None, tmo, tn), lambda i, j, kk: (j // npb, i, j % npb)),
                    jax.ShapeDtypeStruct((g, k, ns), F32), (tmo, tn), _TN, name, side)


def bd_nn(cfg, x, w, name, out_dtype=F32):
    m = x.shape[0]
    g, kg, ng = w.shape
    tm, tn = _div_tile(m, cfg.tm, 16), _div_tile(ng, cfg.tn, cfg.la)
    npb = ng // tn
    return _mm_call(x, w, (m // tm, g * npb, 1),
                    pl.BlockSpec((tm, kg), lambda i, j, kk: (i, j // npb)),
                    pl.BlockSpec((None, kg, tn), lambda i, j, kk: (j // npb, 0, j % npb)),
                    pl.BlockSpec((tm, tn), lambda i, j, kk: (i, j)),
                    jax.ShapeDtypeStruct((m, g * ng), out_dtype), (tm, tn), _NN, name)


def bd_nt(cfg, dy, w, name, out_dtype=F32):
    m = dy.shape[0]
    g, kg, ng = w.shape
    tm, tn = _div_tile(m, cfg.tm, 16), _div_tile(kg, cfg.tn, cfg.la)
    npb = kg // tn
    return _mm_call(dy, w, (m // tm, g * npb, 1),
                    pl.BlockSpec((tm, ng), lambda i, j, kk: (i, j // npb)),
                    pl.BlockSpec((None, tn, ng), lambda i, j, kk: (j // npb, j % npb, 0)),
                    pl.BlockSpec((tm, tn), lambda i, j, kk: (i, j)),
                    jax.ShapeDtypeStruct((m, g * kg), out_dtype), (tm, tn), _NT, name)


def bd_tn(cfg, x, dy, g, name):
    m = x.shape[0]
    kg, ng = x.shape[1] // g, dy.shape[1] // g
    tmo, tn, tk = _div_tile(kg, cfg.tn, cfg.la), _div_tile(ng, cfg.tn, cfg.la), _div_tile(m, cfg.tm, 16)
    mpb, npb = kg // tmo, ng // tn
    return _mm_call(x, dy, (g * mpb, npb, m // tk),
                    pl.BlockSpec((tk, tmo), lambda i, j, kk: (kk, i)),
                    pl.BlockSpec((tk, tn), lambda i, j, kk: (kk, (i // mpb) * npb + j)),
                    pl.BlockSpec((None, tmo, tn), lambda i, j, kk: (i // mpb, i % mpb, j)),
                    jax.ShapeDtypeStruct((g, kg, ng), F32), (tmo, tn), _TN, name)


def _rowwise(cfg, body, ins, outs, name, n_acc=0):
    tr = cfg.tr
    n = cfg.tp // tr
    in_specs, args = [], []
    for spec in ins:
        arr = spec[0]
        if spec[1] is None:
            in_specs.append(pl.BlockSpec(arr.shape, lambda i, nd=arr.ndim: (0,) * nd))
        else:
            in_specs.append(pl.BlockSpec((tr, spec[1]), lambda i, cb=spec[2]: (i, cb)))
        args.append(arr)
    out_specs, out_shape = [], []
    for o in outs:
        if len(o) == 2:
            out_specs.append(pl.BlockSpec((tr, o[0]), lambda i: (i, 0)))
            out_shape.append(jax.ShapeDtypeStruct((cfg.tp, o[0]), o[1]))
        else:
            out_specs.append(pl.BlockSpec((8, o[0]), lambda i: (0, 0)))
            out_shape.append(jax.ShapeDtypeStruct((8, o[0]), F32))
    return pl.pallas_call(body, grid=(n,), in_specs=in_specs, out_specs=out_specs, out_shape=out_shape,
                          compiler_params=_cparams("arbitrary" if n_acc else "parallel"), name=name)(*args)


def ln_fwd(cfg, h, o, g, b, alpha, name):
    d = cfg.d

    def body(h_ref, o_ref, g_ref, b_ref, y_ref, yb_ref, xh_ref, rs_ref):
        z = alpha * h_ref[...] + o_ref[...]
        mu = jnp.mean(z, axis=-1, keepdims=True)
        zc = z - mu
        var = jnp.mean(zc * zc, axis=-1, keepdims=True)
        rstd = lax.rsqrt(var + LN_EPS)
        xh = zc * rstd
        y = xh * g_ref[...] + b_ref[...]
        y_ref[...] = y
        yb_ref[...] = y.astype(BF16)
        xh_ref[...] = xh
        rs_ref[...] = rstd

    return _rowwise(cfg, body, [(h, d, 0), (o, d, 0), (g, None), (b, None)], [(d, F32), (d, BF16), (d, F32), (1, F32)], name)


def ln_bwd(cfg, d_res, d_mm, xhat, rstd, g, alpha, name):
    d = cfg.d
    two = d_res is not None

    def body(*refs):
        if two:
            dr_ref, dm_ref, xh_ref, rs_ref, g_ref, dz_ref, dzb_ref, dg_ref, db_ref = refs
            dy = alpha * dr_ref[...] + dm_ref[...]
        else:
            dm_ref, xh_ref, rs_ref, g_ref, dz_ref, dzb_ref, dg_ref, db_ref = refs
            dy = dm_ref[...]
        xh = xh_ref[...]

        @pl.when(pl.program_id(0) == 0)
        def _():
            dg_ref[...] = jnp.zeros_like(dg_ref)
            db_ref[...] = jnp.zeros_like(db_ref)

        dg_ref[...] += _colsum8(dy * xh)
        db_ref[...] += _colsum8(dy)
        dxh = dy * g_ref[...]
        m1 = jnp.mean(dxh, axis=-1, keepdims=True)
        m2 = jnp.mean(dxh * xh, axis=-1, keepdims=True)
        dz = rs_ref[...] * (dxh - m1 - xh * m2)
        dz_ref[...] = dz
        dzb_ref[...] = dz.astype(BF16)

    ins = ([(d_res, d, 0)] if two else []) + [(d_mm, d, 0), (xhat, d, 0), (rstd, 1, 0), (g, None)]
    return _rowwise(cfg, body, ins, [(d, F32), (d, BF16), (d,), (d,)], name, n_acc=2)


def loss_grad(cfg, y, tgt, name):
    d, tr = cfg.d, cfg.tr
    lo, hi = cfg.n_meta, cfg.n_meta + cfg.seq

    def body(y_ref, t_ref, dy_ref, acc_ref):
        i = pl.program_id(0)

        @pl.when(i == 0)
        def _():
            acc_ref[...] = jnp.zeros_like(acc_ref)

        row = i * tr + lax.broadcasted_iota(jnp.int32, (tr, 1), 0)
        err = jnp.where((row >= lo) & (row < hi), y_ref[...] - t_ref[...], 0.0)
        dy_ref[...] = err * (1.0 / d)
        acc_ref[...] += _colsum8(err * err)

    return _rowwise(cfg, body, [(y, d, 0), (tgt, d, 0)], [(d, F32), (d,)], name, n_acc=1)


def axpy(cfg, a, b, alpha, name):
    d = cfg.d

    def body(a_ref, b_ref, o_ref):
        o_ref[...] = alpha * a_ref[...] + b_ref[...]

    return _rowwise(cfg, body, [(a, d, 0), (b, d, 0)], [(d, F32)], name)[0]


def _time_call(cfg, body, ins, outs, accs, scratch, name, reverse=False, groups=1):
    d = cfg.d
    tc = _div_tile(d // groups, cfg.tc, cfg.la)
    tt = _div_tile(cfg.tp, cfg.tt, 16)
    nc, nt = d // tc, cfg.tp // tt
    tmap = (lambda t: nt - 1 - t) if reverse else (lambda t: t)
    in_specs, args = [], []
    for spec in ins:
        arr = spec[0]
        if len(spec) == 2:
            in_specs.append(pl.BlockSpec((tt, tc), lambda c, t, off=spec[1] * nc: (tmap(t), off + c)))
        else:
            in_specs.append(pl.BlockSpec((arr.shape[0], tc), lambda c, t: (0, c)))
        args.append(arr)
    out_specs = [pl.BlockSpec((tt, tc), lambda c, t: (tmap(t), c)) for _ in outs]
    out_shape = [jax.ShapeDtypeStruct((cfg.tp, d), dt) for dt in outs]
    for _ in range(accs):
        out_specs.append(pl.BlockSpec((8, tc), lambda c, t: (0, c)))
        out_shape.append(jax.ShapeDtypeStruct((8, d), F32))
    return pl.pallas_call(functools.partial(body, tt=tt, tc=tc, nt=nt), grid=(nc, nt), in_specs=in_specs,
                          out_specs=out_specs, out_shape=out_shape,
                          scratch_shapes=[pltpu.VMEM(s, F32) for s in scratch(tt, tc)],
                          compiler_params=_cparams("parallel", "arbitrary"), name=name)(*args)


def _push_history(s_ref, new, t, tt):
    @pl.when(t == 0)
    def _():
        s_ref[pl.ds(0, HIST), :] = jnp.zeros((HIST, s_ref.shape[1]), F32)

    @pl.when(t > 0)
    def _():
        s_ref[pl.ds(0, HIST), :] = s_ref[pl.ds(tt, HIST), :]

    s_ref[pl.ds(HIST, tt), :] = new


def _push_future(s_ref, new, t, tt):
    @pl.when(t == 0)
    def _():
        s_ref[pl.ds(tt, HIST), :] = jnp.zeros((HIST, s_ref.shape[1]), F32)

    @pl.when(t > 0)
    def _():
        s_ref[pl.ds(tt, HIST), :] = s_ref[pl.ds(0, HIST), :]

    s_ref[pl.ds(0, tt), :] = new


def conv_fwd(cfg, ug, w4, cb, name):
    def body(x_ref, w_ref, b_ref, u_ref, ub_ref, s_ref, *, tt, tc, nt):
        t = pl.program_id(1)
        _push_history(s_ref, x_ref[...], t, tt)
        acc = b_ref[...] + w_ref[pl.ds(0, 1), :] * s_ref[pl.ds(HIST - 3, tt), :]
        for k in range(1, CONV_W):
            acc = acc + w_ref[pl.ds(k, 1), :] * s_ref[pl.ds(HIST - 3 + k, tt), :]
        u_ref[...] = acc
        ub_ref[...] = acc.astype(BF16)

    return _time_call(cfg, body, [(ug, 0), (w4,), (cb,)], [F32, BF16], 0, lambda tt, tc: [(tt + HIST, tc)], name)


def conv_bwd(cfg, du_a, du_b, du_c, ug, w4, name):
    def body(a_ref, b_ref, c_ref, x_ref, w_ref, dx_ref, acc_ref, s_ref, *, tt, tc, nt):
        t = pl.program_id(1)

        @pl.when(t == 0)
        def _():
            acc_ref[...] = jnp.zeros_like(acc_ref)

        du = a_ref[...] + b_ref[...] + c_ref[...]
        _push_future(s_ref, du, t, tt)
        x = x_ref[...]
        acc_ref[pl.ds(4, 1), :] += jnp.sum(du, axis=0, keepdims=True)
        dx = None
        for k in range(CONV_W):
            sh = s_ref[pl.ds(3 - k, tt), :]
            term = w_ref[pl.ds(k, 1), :] * sh
            dx = term if dx is None else dx + term
            acc_ref[pl.ds(k, 1), :] += jnp.sum(x * sh, axis=0, keepdims=True)
        dx_ref[...] = dx.astype(BF16)

    return _time_call(cfg, body, [(du_a, 0), (du_b, 0), (du_c, 0), (ug, 0), (w4,)], [BF16], 1,
                      lambda tt, tc: [(tt + HIST, tc)], name, reverse=True)


def _scan8(a, b, rows):
    for k in (1, 2, 4):
        ar = jnp.where(rows >= k, pltpu.roll(a, k, 0), 1.0)
        br = jnp.where(rows >= k, pltpu.roll(b, k, 0), 0.0)
        b = a * br + b
        a = a * ar
    return a, b


def _rscan8(a, b, rows):
    for k in (1, 2, 4):
        ar = jnp.where(rows < 8 - k, pltpu.roll(a, 8 - k, 0), 1.0)
        br = jnp.where(rows < 8 - k, pltpu.roll(b, 8 - k, 0), 0.0)
        b = b + a * br
        a = a * ar
    return a, b


def _lru_gates(u, pa, px, ba, bx, c_lam):
    r = _sig(pa + ba)
    i = _sig(px + bx)
    la = LRU_C * r * c_lam
    em = _expm1(2.0 * la)
    return r, i, jnp.exp(la), em, jnp.sqrt(-em)


def _neg_softplus_neg(lam):
    e = jnp.exp(-jnp.abs(lam))
    u = 1.0 + e
    l1p = jnp.where(u == 1.0, e, jnp.log(u) * (e / jnp.where(u == 1.0, 1.0, u - 1.0)))
    return -(jnp.maximum(-lam, 0.0) + l1p)


def lru_fwd(cfg, u, pa, px, ug, ba, bx, lam, name):
    def body(u_ref, pa_ref, px_ref, g_ref, ba_ref, bx_ref, lam_ref, hs_ref, hp_ref, y_ref, car_ref, *, tt, tc, nt):
        @pl.when(pl.program_id(1) == 0)
        def _():
            car_ref[...] = jnp.zeros_like(car_ref)

        rows = lax.broadcasted_iota(jnp.int32, (8, tc), 0)
        ba, bx = ba_ref[...], bx_ref[...]
        c_lam = _neg_softplus_neg(lam_ref[...])

        def step(it, h):
            ys = []
            for half in range(2):
                rs = pl.ds(pl.multiple_of(it * 16 + half * 8, 8), 8)
                uu = u_ref[rs, :]
                r, i, a, em, s = _lru_gates(uu, pa_ref[rs, :], px_ref[rs, :], ba, bx, c_lam)
                acum, b = _scan8(a, uu * i * s, rows)
                hs = acum * h + b
                hs_ref[rs, :] = hs
                hp_ref[rs, :] = jnp.where(rows >= 1, pltpu.roll(hs, 1, 0), h)
                g = g_ref[rs, :]
                ys.append(hs * (g * _sig(g)))
                h = jnp.broadcast_to(hs[7:8, :], (8, tc))
            y_ref[pl.ds(pl.multiple_of(it * 16, 16), 16), :] = jnp.concatenate(ys, axis=0).astype(BF16)
            return h

        car_ref[...] = lax.fori_loop(0, tt // 16, step, car_ref[...])

    return _time_call(cfg, body, [(u, 0), (pa, 0), (px, 0), (ug, 1), (ba,), (bx,), (lam,)], [F32, F32, BF16], 0,
                      lambda tt, tc: [(8, tc)], name)


def lru_bwd(cfg, dy, ug, hs, hprev, u, pa, px, ba, bx, lam, name):
    def body(dy_ref, g_ref, hs_ref, hp_ref, u_ref, pa_ref, px_ref, ba_ref, bx_ref, lam_ref,
             dg_ref, dpa_ref, dpx_ref, du_ref, acc_ref, ecar_ref, acar_ref, *, tt, tc, nt):
        @pl.when(pl.program_id(1) == 0)
        def _():
            ecar_ref[...] = jnp.zeros_like(ecar_ref)
            acar_ref[...] = jnp.zeros_like(acar_ref)
            acc_ref[...] = jnp.zeros_like(acc_ref)

        rows = lax.broadcasted_iota(jnp.int32, (8, tc), 0)
        ba, bx = ba_ref[...], bx_ref[...]
        c_lam = _neg_softplus_neg(lam_ref[...])

        def step(it, carry):
            ecar, acar, s_lam, s_ba, s_bx = carry
            jt = tt // 16 - 1 - it
            dgs, dpas, dpxs = [None, None], [None, None], [None, None]
            for half in (1, 0):
                rs = pl.ds(pl.multiple_of(jt * 16 + half * 8, 8), 8)
                uu = u_ref[rs, :]
                r, i, a, em, s = _lru_gates(uu, pa_ref[rs, :], px_ref[rs, :], ba, bx, c_lam)
                g = g_ref[rs, :]
                sg = _sig(g)
                dy = dy_ref[rs, :]
                dgs[half] = dy * hs_ref[rs, :] * (sg * (1.0 + g * (1.0 - sg)))
                a_next = jnp.where(rows < 7, pltpu.roll(a, 7, 0), acar)
                acum, e = _rscan8(a_next, dy * (g * sg), rows)
                e = e + acum * ecar
                ecar = jnp.broadcast_to(e[0:1, :], (8, tc))
                acar = jnp.broadcast_to(a[0:1, :], (8, tc))
                du_ref[rs, :] = e * i * s
                d_la = e * hp_ref[rs, :] * a - (e * uu * i) * ((em + 1.0) / s)
                s_lam = s_lam + d_la * (LRU_C * r)
                d_pa = d_la * (LRU_C * c_lam) * r * (1.0 - r)
                d_px = e * uu * s * i * (1.0 - i)
                s_ba = s_ba + d_pa
                s_bx = s_bx + d_px
                dpas[half], dpxs[half] = d_pa, d_px
            r16 = pl.ds(pl.multiple_of(jt * 16, 16), 16)
            dg_ref[r16, :] = jnp.concatenate(dgs, axis=0).astype(BF16)
            dpa_ref[r16, :] = jnp.concatenate(dpas, axis=0).astype(BF16)
            dpx_ref[r16, :] = jnp.concatenate(dpxs, axis=0).astype(BF16)
            return ecar, acar, s_lam, s_ba, s_bx

        z = jnp.zeros((8, tc), F32)
        ecar, acar, s_lam, s_ba, s_bx = lax.fori_loop(0, tt // 16, step, (ecar_ref[...], acar_ref[...], z, z, z))
        ecar_ref[...] = ecar
        acar_ref[...] = acar
        acc_ref[pl.ds(0, 1), :] += jnp.sum(s_lam, axis=0, keepdims=True) * _sig(-lam_ref[...])
        acc_ref[pl.ds(1, 1), :] += jnp.sum(s_ba, axis=0, keepdims=True)
        acc_ref[pl.ds(2, 1), :] += jnp.sum(s_bx, axis=0, keepdims=True)

    return _time_call(cfg, body, [(dy, 0), (ug, 1), (hs, 0), (hprev, 0), (u, 0), (pa, 0), (px, 0), (ba,), (bx,), (lam,)],
                      [BF16, BF16, BF16, F32], 1, lambda tt, tc: [(8, tc), (8, tc)], name, reverse=True)


def _pool_select(cfg, grp, fn):
    for gi, w in enumerate(cfg.pool_windows):
        @pl.when(grp == gi)
        def _(w=w):
            fn(w)


def pool_fwd(cfg, ug, name):
    ng = len(cfg.pool_windows)

    def body(x_ref, p_ref, s_ref, *, tt, tc, nt):
        t = pl.program_id(1)
        grp = pl.program_id(0) // (cfg.d // ng // tc)
        x = x_ref[...]
        _push_history(s_ref, x, t, tt)
        row1 = (t * tt + 1 + lax.broadcasted_iota(jnp.int32, (tt, 1), 0)).astype(F32)

        def write(w):
            ws = x
            for j in range(1, w):
                ws = ws + s_ref[pl.ds(HIST - j, tt), :]
            p_ref[...] = (ws / jnp.minimum(row1, float(w)) - x).astype(BF16)

        _pool_select(cfg, grp, write)

    return _time_call(cfg, body, [(ug, 0)], [BF16], 0, lambda tt, tc: [(tt + HIST, tc)], name, groups=ng)[0]


def pool_bwd(cfg, dp, name):
    ng = len(cfg.pool_windows)

    def body(dp_ref, du_ref, s_ref, *, tt, tc, nt):
        t = pl.program_id(1)
        grp = pl.program_id(0) // (cfg.d // ng // tc)
        dp = dp_ref[...]
        row1 = ((nt - 1 - t) * tt + 1 + lax.broadcasted_iota(jnp.int32, (tt, 1), 0)).astype(F32)

        def write(w):
            dm = dp / jnp.minimum(row1, float(w))
            _push_future(s_ref, dm, t, tt)
            ws = dm
            for j in range(1, w):
                ws = ws + s_ref[pl.ds(j, tt), :]
            du_ref[...] = (ws - dp).astype(BF16)

        _pool_select(cfg, grp, write)

    return _time_call(cfg, body, [(dp, 0)], [BF16], 0, lambda tt, tc: [(tt + HIST, tc)], name, reverse=True, groups=ng)[0]


def gate_fwd(cfg, v, gsrc, gblk, scale, name):
    d = cfg.d

    def body(*refs):
        if scale is None:
            v_ref, g_ref, y_ref = refs
            v = v_ref[...]
        else:
            v_ref, g_ref, s_ref, y_ref = refs
            v = v_ref[...] * s_ref[...]
        g = g_ref[...]
        y_ref[...] = (v * (g * _sig(g))).astype(BF16)

    ins = [(v, d, 0), (gsrc, d, gblk)] + ([] if scale is None else [(scale, None)])
    return _rowwise(cfg, body, ins, [(d, BF16)], name)[0]


def gate_bwd(cfg, dy, v, gsrc, gblk, scale, dv_dtype, name):
    d = cfg.d

    def body(*refs):
        if scale is None:
            dy_ref, v_ref, g_ref, dv_ref, dg_ref = refs
            vs = v_ref[...]
        else:
            dy_ref, v_ref, g_ref, s_ref, dv_ref, dg_ref, acc_ref = refs
            vs = v_ref[...] * s_ref[...]
        g = g_ref[...]
        sg = _sig(g)
        dy = dy_ref[...]
        dvs = dy * (g * sg)
        dg_ref[...] = (dy * vs * (sg * (1.0 + g * (1.0 - sg)))).astype(BF16)
        if scale is None:
            dv_ref[...] = dvs.astype(dv_dtype)
        else:
            @pl.when(pl.program_id(0) == 0)
            def _():
                acc_ref[...] = jnp.zeros_like(acc_ref)

            acc_ref[...] += _colsum8(dvs * v_ref[...])
            dv_ref[...] = (dvs * s_ref[...]).astype(dv_dtype)

    ins = [(dy, d, 0), (v, d, 0), (gsrc, d, gblk)] + ([] if scale is None else [(scale, None)])
    outs = [(d, dv_dtype), (d, BF16)] + ([] if scale is None else [(d,)])
    return _rowwise(cfg, body, ins, outs, name, n_acc=0 if scale is None else 1)


def _swap_halves(x, lo, half):
    w = x.shape[1]
    lane = lax.broadcasted_iota(jnp.int32, x.shape, 1)
    sw = jnp.where(lane < lo + half, pltpu.roll(x, w - half, 1), pltpu.roll(x, half, 1))
    return jnp.where((lane >= lo) & (lane < lo + 2 * half), sw, 0.0)


def mla_norm(cfg, proj, qn, kvn, name):
    ql, kl = cfg.q_lora, cfg.kv_lora

    def body(c_ref, qn_ref, kn_ref, q_ref, k_ref):
        cq = c_ref[:, 0:ql]
        ck = c_ref[:, ql:ql + kl]
        q_ref[...] = (cq * lax.rsqrt(jnp.mean(cq * cq, axis=-1, keepdims=True) + RMS_EPS) * qn_ref[...]).astype(BF16)
        k_ref[...] = (ck * lax.rsqrt(jnp.mean(ck * ck, axis=-1, keepdims=True) + RMS_EPS) * kn_ref[...]).astype(BF16)

    return _rowwise(cfg, body, [(proj, cfg.cw, cfg.d // cfg.cw), (qn, None), (kvn, None)], [(ql, BF16), (kl, BF16)], name)


def mla_norm_bwd(cfg, proj, d_cqn, d_ckvn, d_kr, qn, kvn, name):
    ql, kl, cw, npe = cfg.q_lora, cfg.kv_lora, cfg.cw, cfg.nope

    def one(x, dy, gamma):
        rstd = lax.rsqrt(jnp.mean(x * x, axis=-1, keepdims=True) + RMS_EPS)
        xn = x * rstd
        dxn = dy * gamma
        return rstd * (dxn - xn * jnp.mean(dxn * xn, axis=-1, keepdims=True)), dy * xn

    def body(c_ref, dq_ref, dk_ref, dkr_ref, qn_ref, kn_ref, dc_ref, acc_ref):
        @pl.when(pl.program_id(0) == 0)
        def _():
            acc_ref[...] = jnp.zeros_like(acc_ref)

        dcq, gq = one(c_ref[:, 0:ql], dq_ref[...], qn_ref[...])
        dck, gk = one(c_ref[:, ql:ql + kl], dk_ref[...], kn_ref[...])
        dc_ref[:, 0:ql] = dcq.astype(BF16)
        dc_ref[:, ql:ql + kl] = dck.astype(BF16)
        dc_ref[:, ql + kl:ql + kl + npe] = dkr_ref[...].astype(BF16)
        rest = cw - (ql + kl + npe)
        if rest:
            dc_ref[:, ql + kl + npe:cw] = jnp.zeros((dc_ref.shape[0], rest), BF16)
        acc_ref[:, 0:ql] += _colsum8(gq)
        acc_ref[:, ql:ql + kl] += _colsum8(gk)

    return _rowwise(cfg, body, [(proj, cw, cfg.d // cw), (d_cqn, ql, 0), (d_ckvn, kl, 0), (d_kr, npe, 0), (qn, None), (kvn, None)],
                    [(cw, BF16), (cw,)], name, n_acc=1)


def _mla_tables(cfg):
    r2, npe = cfg.rope // 2, cfg.nope
    inv = ROPE_BASE ** (-jnp.arange(0, cfg.rope, 2, dtype=F32) / cfg.rope)
    ang = jnp.arange(cfg.tp, dtype=F32)[:, None] * inv[None, :]
    cos, sin = jnp.cos(ang), jnp.sin(ang)
    one = jnp.ones((cfg.tp, npe - cfg.rope), F32)
    ck = jnp.concatenate([cos, cos, one], axis=1)
    sk = jnp.concatenate([-sin, sin, 0.0 * one], axis=1)
    cq = jnp.concatenate([jnp.ones((cfg.tp, npe), F32), ck], axis=1)
    sq = jnp.concatenate([jnp.zeros((cfg.tp, npe), F32), sk], axis=1)
    return cq, sq, ck, sk


def mla_prep(cfg, q_raw, kv_raw, proj, tabs, name):
    h, npe, r2, ql, kl = cfg.mla_heads, cfg.nope, cfg.rope // 2, cfg.q_lora, cfg.kv_lora
    hp = 2 * npe

    def body(q_ref, kv_ref, c_ref, cq_ref, sq_ref, ck_ref, sk_ref, qo_ref, ko_ref, vo_ref):
        kr = c_ref[:, ql + kl:ql + kl + npe]
        kr = (kr * ck_ref[...] + _swap_halves(kr, 0, r2) * sk_ref[...]).astype(BF16)
        cq, sq = cq_ref[...], sq_ref[...]
        for i in range(h):
            q = q_ref[:, i * hp:(i + 1) * hp]
            qo_ref[:, i * hp:(i + 1) * hp] = (q * cq + _swap_halves(q, npe, r2) * sq).astype(BF16)
            ko_ref[:, i * hp:i * hp + npe] = kv_ref[:, i * npe:(i + 1) * npe].astype(BF16)
            ko_ref[:, i * hp + npe:(i + 1) * hp] = kr
        vo_ref[...] = kv_ref[:, h * npe:2 * h * npe].astype(BF16)

    cq, sq, ck, sk = tabs
    return _rowwise(cfg, body, [(q_raw, h * hp, 0), (kv_raw, 2 * h * npe, 0), (proj, cfg.cw, cfg.d // cfg.cw),
                                (cq, hp, 0), (sq, hp, 0), (ck, npe, 0), (sk, npe, 0)],
                    [(h * hp, BF16), (h * hp, BF16), (h * npe, BF16)], name)


def mla_prep_bwd(cfg, dq_full, dk_full, tabs, name):
    h, npe, r2 = cfg.mla_heads, cfg.nope, cfg.rope // 2
    hp = 2 * npe

    def body(dq_ref, dk_ref, cq_ref, sq_ref, ck_ref, sk_ref, dqo_ref, dko_ref, dkr_ref):
        cq, sq = cq_ref[...], sq_ref[...]
        dkr = None
        for i in range(h):
            dq = dq_ref[:, i * hp:(i + 1) * hp]
            dqo_ref[:, i * hp:(i + 1) * hp] = (dq * cq + _swap_halves(dq * sq, npe, r2)).astype(BF16)
            dko_ref[:, i * npe:(i + 1) * npe] = dk_ref[:, i * hp:i * hp + npe].astype(BF16)
            part = dk_ref[:, i * hp + npe:(i + 1) * hp]
            dkr = part if dkr is None else dkr + part
        dkr_ref[...] = dkr * ck_ref[...] + _swap_halves(dkr * sk_ref[...], 0, r2)

    cq, sq, ck, sk = tabs
    return _rowwise(cfg, body, [(dq_full, h * hp, 0), (dk_full, h * hp, 0), (cq, hp, 0), (sq, hp, 0), (ck, npe, 0), (sk, npe, 0)],
                    [(h * hp, BF16), (h * npe, BF16), (npe, F32)], name)


def _attn_scores(cfg, q, k, diagonal):
    s = lax.dot_general(q, k, _NT, preferred_element_type=F32) * ((cfg.nope + cfg.rope) ** -0.5)
    if diagonal:
        row = lax.broadcasted_iota(jnp.int32, s.shape, 0)
        col = lax.broadcasted_iota(jnp.int32, s.shape, 1)
        s = jnp.where(row >= col, s, -1e30)
    return s


def _tile_rows(i, t):
    return pl.ds(pl.multiple_of(i * t, t), t)


def attn_fwd(cfg, q, k, v, name, side=None):
    h, npe, tq, wide = cfg.mla_heads, cfg.nope, cfg.tq, cfg.tkw
    hp, n = 2 * npe, cfg.tp // tq

    def body(q_ref, k_ref, v_ref, o_ref, lse_ref):
        qi = pl.program_id(1)
        q = q_ref[...]

        def tile(ki, carry, width, diagonal):
            m, l, acc = carry
            rk = _tile_rows(ki, width)
            s = _attn_scores(cfg, q, k_ref[rk, :], diagonal)
            m_new = jnp.maximum(m, jnp.max(s, axis=-1, keepdims=True))
            alpha = jnp.exp(m - m_new)
            p = jnp.exp(s - m_new)
            l = alpha * l + jnp.sum(p, axis=-1, keepdims=True)
            acc = alpha * acc + lax.dot_general(p.astype(BF16), v_ref[rk, :], _NN, preferred_element_type=F32)
            return m_new, l, acc

        init = (jnp.full((tq, 1), -1e30, F32), jnp.zeros((tq, 1), F32), jnp.zeros((tq, npe), F32))
        n_wide = qi // wide
        carry = lax.fori_loop(0, n_wide, lambda j, c: tile(j, c, wide * tq, False), init)
        carry = lax.fori_loop(n_wide * wide, qi, lambda ki, c: tile(ki, c, tq, False), carry)
        m, l, acc = tile(qi, carry, tq, True)
        o_ref[...] = acc / l
        lse_ref[...] = m + jnp.log(l)

    outs, side_outs = _host_call(
        body, (h, n),
        [pl.BlockSpec((tq, hp), lambda hh, qi: (qi, hh)),
         pl.BlockSpec((cfg.tp, hp), lambda hh, qi: (0, hh)),
         pl.BlockSpec((cfg.tp, npe), lambda hh, qi: (0, hh))],
        [pl.BlockSpec((tq, npe), lambda hh, qi: (qi, hh)),
         pl.BlockSpec((None, tq, 1), lambda hh, qi: (hh, qi, 0))],
        [jax.ShapeDtypeStruct((cfg.tp, h * npe), F32), jax.ShapeDtypeStruct((h, cfg.tp, 1), F32)],
        [], ("parallel", "arbitrary"), name, (q, k, v), side)
    return outs if side is None else (outs, side_outs)


def attn_bwd(cfg, q, k, v, do, o, lse, name, side=None):
    h, npe, tq, kw = cfg.mla_heads, cfg.nope, cfg.tq, cfg.tkw
    hp, n = 2 * npe, cfg.tp // tq
    sc = (cfg.nope + cfg.rope) ** -0.5

    def body(q_ref, k_ref, v_ref, do_ref, o_ref, lse_ref, dq_ref, dk_ref, dv_ref, dv_acc):
        dk_ref[...] = jnp.zeros_like(dk_ref)
        dv_acc[...] = jnp.zeros_like(dv_acc)

        def q_tile(qi, _):
            rq = _tile_rows(qi, tq)
            q, do, lse = q_ref[rq, :], do_ref[rq, :], lse_ref[rq, :]
            delta = jnp.sum(do.astype(F32) * o_ref[rq, :], axis=-1, keepdims=True)

            def tile(ki, dq, width, diagonal):
                rk = _tile_rows(ki, width)
                kk = k_ref[rk, :]
                p = jnp.exp(_attn_scores(cfg, q, kk, diagonal) - lse)
                dp = lax.dot_general(do, v_ref[rk, :], _NT, preferred_element_type=F32)
                ds = (p * (dp - delta) * sc).astype(BF16)
                dv_acc[rk, :] += lax.dot_general(p.astype(BF16), do, _TN, preferred_element_type=F32)
                dk_ref[rk, :] += lax.dot_general(ds, q, _TN, preferred_element_type=F32)
                return dq + lax.dot_general(ds, kk, _NN, preferred_element_type=F32)

            n_wide = qi // kw
            dq = lax.fori_loop(0, n_wide, lambda j, acc: tile(j, acc, kw * tq, False), jnp.zeros((tq, hp), F32))
            dq = lax.fori_loop(n_wide * kw, qi, lambda ki, acc: tile(ki, acc, tq, False), dq)
            dq_ref[rq, :] = tile(qi, dq, tq, True)
            return 0

        lax.fori_loop(0, n, q_tile, 0)
        dv_ref[...] = dv_acc[...].astype(BF16)

    wide = pl.BlockSpec((cfg.tp, hp), lambda hh: (0, hh))
    narrow = pl.BlockSpec((cfg.tp, npe), lambda hh: (0, hh))
    outs, side_outs = _host_call(
        body, (h,), [wide, wide, narrow, narrow, narrow, pl.BlockSpec((None, cfg.tp, 1), lambda hh: (hh, 0, 0))],
        [wide, wide, narrow],
        [jax.ShapeDtypeStruct((cfg.tp, h * hp), F32), jax.ShapeDtypeStruct((cfg.tp, h * hp), F32),
         jax.ShapeDtypeStruct((cfg.tp, h * npe), BF16)],
        [pltpu.VMEM((cfg.tp, npe), F32)], ("parallel",), name, (q, k, v, do, o, lse), side)
    return outs if side is None else (outs, side_outs)


def _ret_tables(cfg):
    c = cfg.chunk
    lg = jnp.log(1.0 - 2.0 ** (-5.0 - jnp.arange(cfg.ret_heads, dtype=F32)))[:, None, None]
    j = jnp.arange(c, dtype=F32)
    diff = j[:, None] - j[None, :]
    dm = jnp.where(diff >= 0, jnp.exp(jnp.maximum(diff, 0.0)[None] * lg), 0.0)
    lq = jnp.exp((j + 1.0)[None, :, None] * lg)
    lk = jnp.exp((c - 1.0 - j)[None, :, None] * lg)
    gc = jnp.exp(c * lg)
    return dm, lq, lk, gc


def _ret_rope_tables(cfg):
    dk = cfg.d // cfg.ret_heads
    inv = ROPE_BASE ** (-jnp.arange(0, dk, 2, dtype=F32) / dk)
    ang = jnp.arange(cfg.tp, dtype=F32)[:, None] * inv[None, :]
    return jnp.cos(ang), jnp.sin(ang)


def ret_prep(cfg, qkvg, cos, sin, name):
    d, h = cfg.d, cfg.ret_heads
    dk = d // h
    hd = dk // 2
    ksc = dk ** -0.5

    def body(q_ref, k_ref, v_ref, c_ref, s_ref, qo_ref, ko_ref, vo_ref):
        c, s = c_ref[...], s_ref[...]
        for i in range(h):
            for src, dst, f in ((q_ref, qo_ref, 1.0), (k_ref, ko_ref, ksc)):
                x1 = src[:, i * dk:i * dk + hd]
                x2 = src[:, i * dk + hd:(i + 1) * dk]
                dst[:, i * dk:i * dk + hd] = ((x1 * c - x2 * s) * f).astype(BF16)
                dst[:, i * dk + hd:(i + 1) * dk] = ((x2 * c + x1 * s) * f).astype(BF16)
        vo_ref[...] = v_ref[...].astype(BF16)

    return _rowwise(cfg, body, [(qkvg, d, 0), (qkvg, d, 1), (qkvg, d, 2), (cos, hd, 0), (sin, hd, 0)],
                    [(d, BF16), (d, BF16), (d, BF16)], name)


def ret_prep_bwd(cfg, dq, dk_, dv, dg, cos, sin, name):
    d, h = cfg.d, cfg.ret_heads
    dk = d // h
    hd = dk // 2
    ksc = dk ** -0.5

    def body(dq_ref, dk_ref, dv_ref, dg_ref, c_ref, s_ref, o_ref):
        c, s = c_ref[...], s_ref[...]
        for i in range(h):
            for src, off, f in ((dq_ref, 0, 1.0), (dk_ref, d, ksc)):
                y1 = src[:, i * dk:i * dk + hd]
                y2 = src[:, i * dk + hd:(i + 1) * dk]
                o_ref[:, off + i * dk:off + i * dk + hd] = ((y1 * c + y2 * s) * f).astype(BF16)
                o_ref[:, off + i * dk + hd:off + (i + 1) * dk] = ((y2 * c - y1 * s) * f).astype(BF16)
        o_ref[:, 2 * d:3 * d] = dv_ref[...]
        o_ref[:, 3 * d:4 * d] = dg_ref[...]

    return _rowwise(cfg, body, [(dq, d, 0), (dk_, d, 0), (dv, d, 0), (dg, d, 0), (cos, hd, 0), (sin, hd, 0)],
                    [(4 * d, BF16)], name)[0]


def _ret_specs(cfg, reverse):
    c, h = cfg.chunk, cfg.ret_heads
    dk = cfg.d // h
    n = cfg.tp // c
    tm = (lambda t: n - 1 - t) if reverse else (lambda t: t)
    blk = pl.BlockSpec((c, dk), lambda hh, t: (tm(t), hh))
    row = pl.BlockSpec((None, c, 1), lambda hh, t: (hh, tm(t), 0))
    tabs = [pl.BlockSpec((None, c, c), lambda hh, t: (hh, 0, 0)), pl.BlockSpec((None, c, 1), lambda hh, t: (hh, 0, 0)),
            pl.BlockSpec((None, c, 1), lambda hh, t: (hh, 0, 0)), pl.BlockSpec((None, 1, 1), lambda hh, t: (hh, 0, 0))]
    return blk, row, tabs, n, dk


def _ret_state_update(s_ref, k, v, lk, gc):
    kw = (k.astype(F32) * lk).astype(BF16)
    s_ref[...] = gc * s_ref[...] + lax.dot_general(kw, v, _TN, preferred_element_type=F32)


def ret_fwd(cfg, q, k, v, tabs, name):
    blk, row, tspecs, n, dk = _ret_specs(cfg, False)

    def body(q_ref, k_ref, v_ref, dm_ref, lq_ref, lk_ref, gc_ref, on_ref, rs_ref, s_ref):
        @pl.when(pl.program_id(1) == 0)
        def _():
            s_ref[...] = jnp.zeros_like(s_ref)

        qq, kk, vv = q_ref[...], k_ref[...], v_ref[...]
        a = lax.dot_general(qq, kk, _NT, preferred_element_type=F32) * dm_ref[...]
        o = lax.dot_general(a.astype(BF16), vv, _NN, preferred_element_type=F32)
        o = o + lax.dot_general(qq, s_ref[...].astype(BF16), _NN, preferred_element_type=F32) * lq_ref[...]
        _ret_state_update(s_ref, kk, vv, lk_ref[...], gc_ref[...])
        mu = jnp.mean(o, axis=-1, keepdims=True)
        oc = o - mu
        rstd = lax.rsqrt(jnp.mean(oc * oc, axis=-1, keepdims=True) + LN_EPS)
        on_ref[...] = oc * rstd
        rs_ref[...] = rstd

    return pl.pallas_call(body, grid=(cfg.ret_heads, n), in_specs=[blk, blk, blk] + tspecs, out_specs=[blk, row],
                          out_shape=[jax.ShapeDtypeStruct((cfg.tp, cfg.d), F32), jax.ShapeDtypeStruct((cfg.ret_heads, cfg.tp, 1), F32)],
                          scratch_shapes=[pltpu.VMEM((dk, dk), F32)], compiler_params=_cparams("parallel", "arbitrary"),
                          name=name)(q, k, v, *tabs)


def ret_bwd_q(cfg, q, k, v, d_on, on, rstd, tabs, name):
    blk, row, tspecs, n, dk = _ret_specs(cfg, False)

    def body(q_ref, k_ref, v_ref, don_ref, on_ref, rs_ref, dm_ref, lq_ref, lk_ref, gc_ref, do_ref, dq_ref, s_ref):
        @pl.when(pl.program_id(1) == 0)
        def _():
            s_ref[...] = jnp.zeros_like(s_ref)

        kk, vv = k_ref[...], v_ref[...]
        don, on = don_ref[...], on_ref[...]
        do = rs_ref[...] * (don - jnp.mean(don, axis=-1, keepdims=True) - on * jnp.mean(don * on, axis=-1, keepdims=True))
        dob = do.astype(BF16)
        do_ref[...] = dob
        ds = lax.dot_general(dob, vv, _NT, preferred_element_type=F32) * dm_ref[...]
        dq = lax.dot_general(ds.astype(BF16), kk, _NN, preferred_element_type=F32)
        dq_ref[...] = dq + lax.dot_general(dob, s_ref[...].astype(BF16), _NT, preferred_element_type=F32) * lq_ref[...]
        _ret_state_update(s_ref, kk, vv, lk_ref[...], gc_ref[...])

    return pl.pallas_call(body, grid=(cfg.ret_heads, n), in_specs=[blk, blk, blk, blk, blk, row] + tspecs, out_specs=[blk, blk],
                          out_shape=[jax.ShapeDtypeStruct((cfg.tp, cfg.d), BF16), jax.ShapeDtypeStruct((cfg.tp, cfg.d), F32)],
                          scratch_shapes=[pltpu.VMEM((dk, dk), F32)], compiler_params=_cparams("parallel", "arbitrary"),
                          name=name)(q, k, v, d_on, on, rstd, *tabs)


def ret_bwd_kv(cfg, q, k, v, do, tabs, name):
    blk, row, tspecs, n, dk = _ret_specs(cfg, True)

    def body(q_ref, k_ref, v_ref, do_ref, dm_ref, lq_ref, lk_ref, gc_ref, dk_ref, dv_ref, g_ref):
        @pl.when(pl.program_id(1) == 0)
        def _():
            g_ref[...] = jnp.zeros_like(g_ref)

        qq, kk, vv, dob = q_ref[...], k_ref[...], v_ref[...], do_ref[...]
        dm, lk = dm_ref[...], lk_ref[...]
        gb = g_ref[...].astype(BF16)
        a = lax.dot_general(qq, kk, _NT, preferred_element_type=F32) * dm
        ds = lax.dot_general(dob, vv, _NT, preferred_element_type=F32) * dm
        dkk = lax.dot_general(ds.astype(BF16), qq, _TN, preferred_element_type=F32)
        dk_ref[...] = dkk + lax.dot_general(vv, gb, _NT, preferred_element_type=F32) * lk
        kw = (kk.astype(F32) * lk).astype(BF16)
        dvv = lax.dot_general(a.astype(BF16), dob, _TN, preferred_element_type=F32)
        dv_ref[...] = (dvv + lax.dot_general(kw, gb, _NN, preferred_element_type=F32)).astype(BF16)
        qw = (qq.astype(F32) * lq_ref[...]).astype(BF16)
        g_ref[...] = gc_ref[...] * g_ref[...] + lax.dot_general(qw, dob, _TN, preferred_element_type=F32)

    return pl.pallas_call(body, grid=(cfg.ret_heads, n), in_specs=[blk, blk, blk, blk] + tspecs, out_specs=[blk, blk],
                          out_shape=[jax.ShapeDtypeStruct((cfg.tp, cfg.d), F32), jax.ShapeDtypeStruct((cfg.tp, cfg.d), BF16)],
                          scratch_shapes=[pltpu.VMEM((dk, dk), F32)], compiler_params=_cparams("parallel", "arbitrary"),
                          name=name)(q, k, v, do, *tabs)


def _flat2(a):
    return a.reshape(-1, a.shape[-1])


def _ew_call(body, ins, out_dtypes, name, lead=None):
    r, c = ins[-1].shape[-2:]
    tr = _div_tile(r, max(16, (1 << 18) // c), 16)
    specs = []
    for a in ins:
        if a.ndim == 3:
            specs.append(pl.BlockSpec((a.shape[0], tr, c), lambda i: (0, i, 0)))
        else:
            specs.append(pl.BlockSpec((tr, c), lambda i: (i, 0)))
    return pl.pallas_call(body, grid=(r // tr,), in_specs=specs,
                          out_specs=[pl.BlockSpec((tr, c), lambda i: (i, 0)) for _ in out_dtypes],
                          out_shape=[jax.ShapeDtypeStruct((r, c), dt) for dt in out_dtypes],
                          compiler_params=_cparams("parallel"), name=name)(*ins)


def adamw(w, g, m, v, name):
    c1 = 1.0 - ADAM_B1 ** ADAM_STEP
    c2 = 1.0 - ADAM_B2 ** ADAM_STEP

    def body(w_ref, g_ref, m_ref, v_ref, d_ref, mo_ref, vo_ref):
        gg = g_ref[...]
        mn = ADAM_B1 * m_ref[...] + (1.0 - ADAM_B1) * gg
        vn = ADAM_B2 * v_ref[...] + (1.0 - ADAM_B2) * (gg * gg)
        d_ref[...] = -ADAM_LR * ((mn / c1) / (jnp.sqrt(vn / c2) + ADAM_EPS) + ADAM_WD * w_ref[...])
        mo_ref[...] = mn
        vo_ref[...] = vn

    outs = _ew_call(body, [_flat2(w), _flat2(g), _flat2(m), _flat2(v)], [F32, F32, F32], name)
    return [o.reshape(w.shape) for o in outs]


def add_pair_bf16(g, theirs, core, name):
    c = g.shape[-1]
    g4, t3 = g.reshape(4, 2, -1, c), theirs.reshape(4, -1, c)
    rows = t3.shape[1]
    tr = _div_tile(rows, max(16, (1 << 18) // c), 16)

    def body(core_ref, g_ref, t_ref, o_ref):
        o_ref[...] = (g_ref[...] + t_ref[...]).astype(BF16)

    spec = pl.BlockSpec((None, tr, c), lambda q, i, core_ref: (q, i, 0))
    grid_spec = pltpu.PrefetchScalarGridSpec(
        num_scalar_prefetch=1, grid=(4, rows // tr),
        in_specs=[pl.BlockSpec((None, None, tr, c), lambda q, i, core_ref: (q, core_ref[0], i, 0)), spec], out_specs=spec)
    out = pl.pallas_call(body, grid_spec=grid_spec, out_shape=jax.ShapeDtypeStruct((4, rows, c), BF16),
                         compiler_params=_cparams("parallel", "parallel"), name=name)(core, g4, t3)
    return out.reshape(theirs.shape)


def sum_slots(parts, landed, chip_core, name):
    c = parts.shape[-1]
    p3, l3 = parts.reshape(4, -1, c), landed.reshape(3, -1, c)
    rows = p3.shape[1]
    tr = _div_tile(rows, max(16, (1 << 18) // c), 16)

    def body(cc_ref, p_ref, l_ref, o_ref):
        acc = p_ref[...].astype(F32)
        for s in range(3):
            acc = acc + l_ref[s].astype(F32)
        o_ref[...] = acc

    grid_spec = pltpu.PrefetchScalarGridSpec(
        num_scalar_prefetch=1, grid=(rows // tr,),
        in_specs=[pl.BlockSpec((None, tr, c), lambda i, cc: (cc[0], i, 0)), pl.BlockSpec((3, tr, c), lambda i, cc: (0, i, 0))],
        out_specs=pl.BlockSpec((None, tr, c), lambda i, cc: (cc[1], i, 0)))
    out = pl.pallas_call(body, grid_spec=grid_spec, out_shape=jax.ShapeDtypeStruct((2, rows, c), F32),
                         compiler_params=_cparams("parallel"), name=name)(chip_core, p3, l3)
    return out.reshape((2,) + parts.shape[2:])


def cast_place(w, chip, name):
    c = w.shape[-1]
    w2 = w.reshape(-1, c)
    rows = w2.shape[0]
    tr = _div_tile(rows, max(16, (1 << 18) // c), 16)

    def body(chip_ref, w_ref, o_ref):
        o_ref[...] = w_ref[...].astype(BF16)

    grid_spec = pltpu.PrefetchScalarGridSpec(
        num_scalar_prefetch=1, grid=(rows // tr,),
        in_specs=[pl.BlockSpec((tr, c), lambda i, chip_ref: (i, 0))],
        out_specs=pl.BlockSpec((None, tr, c), lambda i, chip_ref: (chip_ref[0], i, 0)))
    out = pl.pallas_call(body, grid_spec=grid_spec, out_shape=jax.ShapeDtypeStruct((4, rows, c), BF16),
                         compiler_params=_cparams("parallel"), name=name)(chip, w2)
    return out.reshape((4,) + w.shape)


def _coords():
    return lax.axis_index("x"), lax.axis_index("y"), lax.axis_index("c")


def _chip_peer(x, y, k):
    return (x ^ (k >> 1), y ^ (k & 1))


_ANY = pl.BlockSpec(memory_space=pl.ANY)
DMA_PIECE_BYTES = 2 << 20
DMA_MAX_PIECES = 32


def _piece_plan(shape, itemsize):
    want = max(1, min(DMA_MAX_PIECES, math.prod(shape) * itemsize // DMA_PIECE_BYTES))
    plan = []
    for ax, n in enumerate(shape[:-1]):
        if want <= 1:
            break
        rows_tiled = ax == len(shape) - 2
        k = max([1] + [c for c in range(2, min(n, want) + 1) if n % c == 0 and (not rows_tiled or (n // c) % 16 == 0)])
        if k > 1:
            plan.append((ax, k))
            want = -(-want // k)
    return plan


class _Copy:
    def __init__(self, src, dst, send_sem, recv_sem=None, device_id=None):
        self.src, self.dst, self.send_sem, self.recv_sem, self.device_id = src, dst, send_sem, recv_sem, device_id

    def _dma(self, src, dst):
        if self.device_id is None:
            return pltpu.make_async_copy(src, dst, self.send_sem)
        return pltpu.make_async_remote_copy(src_ref=src, dst_ref=dst, send_sem=self.send_sem, recv_sem=self.recv_sem,
                                            device_id=self.device_id, device_id_type=MESH)

    def start(self):
        shape = self.src.shape
        plan = _piece_plan(shape, jnp.dtype(self.src.dtype).itemsize)
        for pick in np.ndindex(*[k for _, k in plan]):
            idx = [slice(None)] * len(shape)
            for (ax, k), i in zip(plan, pick):
                step = shape[ax] // k
                idx[ax] = pl.ds(i * step, step)
            idx = tuple(idx)
            self._dma(self.src.at[idx], self.dst.at[idx]).start()

    def wait(self):
        self._dma(self.src, self.dst).wait()

    def wait_send(self):
        self._dma(self.src, self.dst).wait_send()

    def wait_recv(self):
        self._dma(self.src, self.dst).wait_recv()


def allreduce_small(x, name):
    r, _, w = x.shape

    def body(x_ref, o_ref, mine_ref, gat_ref, send_sems, recv_sems):
        mx, my, mc = _coords()
        me = 4 * mx + 2 * my + mc
        mine_ref[...] = jnp.sum(x_ref[...], axis=1)
        gat_ref[me] = mine_ref[...]
        copies = []
        for k in range(1, 8):
            peer = (mx ^ (k >> 2), my ^ ((k >> 1) & 1), mc ^ (k & 1))
            cp = pltpu.make_async_remote_copy(src_ref=mine_ref, dst_ref=gat_ref.at[me], send_sem=send_sems.at[k - 1],
                                              recv_sem=recv_sems.at[k - 1], device_id=peer, device_id_type=MESH)
            cp.start()
            copies.append(cp)
        for k in range(1, 8):
            pltpu.make_async_remote_copy(src_ref=mine_ref, dst_ref=gat_ref.at[me ^ k], send_sem=send_sems.at[k - 1],
                                         recv_sem=recv_sems.at[k - 1], device_id=(mx, my, mc), device_id_type=MESH).wait_recv()
        for cp in copies:
            cp.wait_send()
        acc = gat_ref[0]
        for s in range(1, 8):
            acc = acc + gat_ref[s]
        o_ref[...] = acc

    return pl.pallas_call(body, out_shape=jax.ShapeDtypeStruct((r, w), F32),
                          in_specs=[pl.BlockSpec(memory_space=pltpu.VMEM)], out_specs=pl.BlockSpec(memory_space=pltpu.VMEM),
                          scratch_shapes=[pltpu.VMEM((r, w), F32), pltpu.VMEM((8, r, w), F32),
                                          pltpu.SemaphoreType.DMA((7,)), pltpu.SemaphoreType.DMA((7,))],
                          compiler_params=pltpu.CompilerParams(vmem_limit_bytes=VMEM_LIMIT_V7X), name=name)(x)


def allgather_chips(shards, name):
    n = len(shards)

    def body(*refs):
        outs = refs[n:2 * n]
        s1_send, s1_recv, s2_send, s2_recv = refs[2 * n:]
        mx, my, mc = _coords()
        p = 2 * mx + my
        sib = (mx, my, 1 - mc)
        sends = []
        for a in range(n):
            h0 = outs[a].shape[1] // 2
            mine = outs[a].at[p, pl.ds(mc * h0, h0)]
            for k in (1, 2, 3):
                qx, qy = _chip_peer(mx, my, k)
                cp = _Copy(mine, mine, s1_send.at[a, k - 1], s1_recv.at[a, k - 1], (qx, qy, mc))
                cp.start()
                sends.append(cp)
        for a in range(n):
            h0 = outs[a].shape[1] // 2
            half = pl.ds(mc * h0, h0)
            for k in (1, 2, 3):
                landed = outs[a].at[p ^ k, half]
                _Copy(landed, landed, s1_send.at[a, k - 1], s1_recv.at[a, k - 1], sib).wait_recv()
                fw = _Copy(landed, landed, s2_send.at[a, k - 1], s2_recv.at[a, k - 1], sib)
                fw.start()
                sends.append(fw)
        for a in range(n):
            h0 = outs[a].shape[1] // 2
            other = pl.ds((1 - mc) * h0, h0)
            for k in (1, 2, 3):
                theirs = outs[a].at[p ^ k, other]
                _Copy(theirs, theirs, s2_send.at[a, k - 1], s2_recv.at[a, k - 1], sib).wait_recv()
        for cp in sends:
            cp.wait_send()

    return pl.pallas_call(body, out_shape=[jax.ShapeDtypeStruct(s.shape, s.dtype) for s in shards],
                          in_specs=[_ANY] * n, out_specs=[_ANY] * n, input_output_aliases={a: a for a in range(n)},
                          scratch_shapes=[pltpu.SemaphoreType.DMA((n, 3)), pltpu.SemaphoreType.DMA((n, 3)),
                                          pltpu.SemaphoreType.DMA((n, 3)), pltpu.SemaphoreType.DMA((n, 3))],
                          name=name)(*shards)


def ag_chips_side(bufs):
    n = len(bufs)

    def each(inplace, sems, act):
        s_send, s_recv = sems
        mx, my, mc = _coords()
        p = 2 * mx + my
        for a in range(n):
            h0 = inplace[a].shape[1] // 2
            half = pl.ds(mc * h0, h0)
            mine = inplace[a].at[p, half]
            for k in (1, 2, 3):
                qx, qy = _chip_peer(mx, my, k)
                act(_Copy(mine, mine, s_send.at[a, k - 1], s_recv.at[a, k - 1], (qx, qy, mc)),
                    _Copy(inplace[a].at[p ^ k, half], inplace[a].at[p ^ k, half], s_send.at[a, k - 1], s_recv.at[a, k - 1], (qx, qy, mc)))

    def start(ins, inplace, outs, sems):
        each(inplace, sems, lambda send, landed: send.start())

    def finish(ins, inplace, outs, sems):
        each(inplace, sems, lambda send, landed: (landed.wait_recv(), send.wait_send()))

    return Side([], list(bufs), [], [pltpu.SemaphoreType.DMA((n, 3)), pltpu.SemaphoreType.DMA((n, 3))], start, finish)


def ag_forward(bufs, name):
    n = len(bufs)

    def body(*refs):
        outs = refs[n:2 * n]
        s_send, s_recv = refs[2 * n:]
        mx, my, mc = _coords()
        p = 2 * mx + my
        sib = (mx, my, 1 - mc)
        sends = []
        for a in range(n):
            h0 = outs[a].shape[1] // 2
            for k in (1, 2, 3):
                landed = outs[a].at[p ^ k, pl.ds(mc * h0, h0)]
                fw = _Copy(landed, landed, s_send.at[a, k - 1], s_recv.at[a, k - 1], sib)
                fw.start()
                sends.append(fw)
        for a in range(n):
            h0 = outs[a].shape[1] // 2
            for k in (1, 2, 3):
                theirs = outs[a].at[p ^ k, pl.ds((1 - mc) * h0, h0)]
                _Copy(theirs, theirs, s_send.at[a, k - 1], s_recv.at[a, k - 1], sib).wait_recv()
        for fw in sends:
            fw.wait_send()

    return pl.pallas_call(body, out_shape=[jax.ShapeDtypeStruct(s.shape, s.dtype) for s in bufs],
                          in_specs=[_ANY] * n, out_specs=[_ANY] * n, input_output_aliases={a: a for a in range(n)},
                          scratch_shapes=[pltpu.SemaphoreType.DMA((n, 3)), pltpu.SemaphoreType.DMA((n, 3))], name=name)(*bufs)


def rs_chips_side(parts):
    n = len(parts)

    def copies(ins, outs, sems):
        send_sems, recv_sems = sems
        mx, my, mc = _coords()
        p = 2 * mx + my
        return [_Copy(ins[a].at[p ^ k], outs[a].at[k - 1], send_sems.at[a, k - 1], recv_sems.at[a, k - 1],
                      (*_chip_peer(mx, my, k), mc)) for a in range(n) for k in (1, 2, 3)]

    def start(ins, inplace, outs, sems):
        for cp in copies(ins, outs, sems):
            cp.start()

    def finish(ins, inplace, outs, sems):
        for cp in copies(ins, outs, sems):
            cp.wait()

    return Side(list(parts), [], [jax.ShapeDtypeStruct((3,) + g.shape[1:], g.dtype) for g in parts],
                [pltpu.SemaphoreType.DMA((n, 3)), pltpu.SemaphoreType.DMA((n, 3))], start, finish)


def rs_pair_side(grads):
    n = len(grads)

    def copies(ins, outs, sems):
        send_sems, recv_sems = sems
        mx, my, mc = _coords()
        return [_Copy(ins[a].at[:, pl.ds(1 - mc, 1)], outs[a], send_sems.at[a], recv_sems.at[a], (mx, my, 1 - mc)) for a in range(n)]

    def start(ins, inplace, outs, sems):
        for cp in copies(ins, outs, sems):
            cp.start()

    def finish(ins, inplace, outs, sems):
        for cp in copies(ins, outs, sems):
            cp.wait()

    return Side(list(grads), [], [jax.ShapeDtypeStruct((4, 1) + g.shape[2:], g.dtype) for g in grads],
                [pltpu.SemaphoreType.DMA((n,)), pltpu.SemaphoreType.DMA((n,))], start, finish)


def exchange_halves(grads, name):
    n = len(grads)
    side = rs_pair_side(grads)

    def body(*refs):
        ins, outs, sems = refs[:n], refs[n:2 * n], refs[2 * n:]
        side.start(ins, [], outs, sems)
        side.finish(ins, [], outs, sems)

    return pl.pallas_call(body, out_shape=side.outs, in_specs=[_ANY] * n, out_specs=[_ANY] * n, scratch_shapes=side.sems,
                          name=name)(*grads)


def scatter_chips(parts, name):
    n = len(parts)
    side = rs_chips_side(parts)

    def body(*refs):
        ins, outs, sems = refs[:n], refs[n:2 * n], refs[2 * n:]
        side.start(ins, [], outs, sems)
        side.finish(ins, [], outs, sems)

    return pl.pallas_call(body, out_shape=side.outs, in_specs=[_ANY] * n, out_specs=[_ANY] * n, scratch_shapes=side.sems,
                          name=name)(*parts)


def join_halves(halves, name):
    n = len(halves)

    def body(*refs):
        outs = refs[n:2 * n]
        send_sems, recv_sems = refs[2 * n:]
        mx, my, mc = _coords()
        pend = []
        for a in range(n):
            mine = outs[a].at[pl.ds(mc, 1)]
            cp = _Copy(mine, mine, send_sems.at[a], recv_sems.at[a], (mx, my, 1 - mc))
            cp.start()
            pend.append((cp, outs[a].at[pl.ds(1 - mc, 1)]))
        for a, (cp, theirs) in enumerate(pend):
            cp.wait_send()
            _Copy(theirs, theirs, send_sems.at[a], recv_sems.at[a], (mx, my, 1 - mc)).wait_recv()

    return pl.pallas_call(body, out_shape=[jax.ShapeDtypeStruct(g.shape, g.dtype) for g in halves],
                          in_specs=[_ANY] * n, out_specs=[_ANY] * n, input_output_aliases={a: a for a in range(n)},
                          scratch_shapes=[pltpu.SemaphoreType.DMA((n,)), pltpu.SemaphoreType.DMA((n,))], name=name)(*halves)


def rs_pair_sums(grads, tag, call=None):
    out, theirs = (None, exchange_halves(grads, f"rs_pair_{tag}")) if call is None else call(rs_pair_side(grads))
    core = lax.axis_index("c").astype(jnp.int32).reshape(1)
    return out, [add_pair_bf16(g, t, core, f"rs_add_{tag}_{i}") for i, (g, t) in enumerate(zip(grads, theirs))]


def rs_finish(parts, landed, tag):
    chip_core = jnp.stack([2 * lax.axis_index("x") + lax.axis_index("y"), lax.axis_index("c")]).astype(jnp.int32)
    halves = [sum_slots(pt, ld, chip_core, f"rs_sum_{tag}_{i}") for i, (pt, ld) in enumerate(zip(parts, landed))]
    return join_halves(halves, f"rs_join_{tag}")


W_NAMES = ['meta_tokens', 'l0_w_in', 'l0_conv_w', 'l0_conv_b', 'l0_w_a', 'l0_b_a', 'l0_w_x', 'l0_b_x', 'l0_lam', 'l0_w_out',
           'l0_ln_g', 'l0_ln_b', 'l1_w_in', 'l1_w_grp', 'l1_scale', 'l1_w_out', 'l1_ln_g', 'l1_ln_b', 'l2_w_in', 'l2_q_norm',
           'l2_w_uq', 'l2_kv_norm', 'l2_w_ukv', 'l2_w_out', 'l2_ln_g', 'l2_ln_b', 'l3_w_in', 'l3_w_out', 'l3_ln_g', 'l3_ln_b']
BIG = {0: ['l0_w_in', 'l0_w_a', 'l0_w_x', 'l0_w_out'], 1: ['l1_w_in', 'l1_w_grp', 'l1_w_out'],
       2: ['l2_w_in', 'l2_w_uq', 'l2_w_ukv', 'l2_w_out'], 3: ['l3_w_in', 'l3_w_out']}
SMALL_SHARDED = ['meta_tokens', 'l0_conv_w']
SMALL_REPL = ['l0_conv_b', 'l0_b_a', 'l0_b_x', 'l0_lam', 'l0_ln_g', 'l0_ln_b', 'l1_scale', 'l1_ln_g', 'l1_ln_b',
              'l2_q_norm', 'l2_kv_norm', 'l2_ln_g', 'l2_ln_b', 'l3_ln_g', 'l3_ln_b']


def _rows8(rows, width):
    out = []
    for r in rows:
        r = r.reshape(-1, r.shape[-1])
        out.append(jnp.pad(r, ((0, 8 - r.shape[0]), (0, width - r.shape[1]))))
    return jnp.stack(out)


def _unblock(g4, axis):
    return jnp.concatenate([g4[i] for i in range(4)], axis=axis)


def _block(full, axis):
    return jnp.stack(jnp.split(full, 4, axis=axis))


def _step(cfg, a):
    d, tp, nm, seq = cfg.d, cfg.tp, cfg.n_meta, cfg.seq
    alpha = (2.0 * cfg.depth) ** 0.25
    mx, my, mc = _coords()
    p = 2 * mx + my
    dq4 = d // 4
    hm, npe, ql, kl = cfg.mla_heads, cfg.nope, cfg.q_lora, cfg.kv_lora
    hp = 2 * npe
    qk = npe + cfg.rope
    vec = lambda name: a[name].reshape(1, -1)

    sm = jnp.concatenate([a['meta_tokens'], a['l0_conv_w'].reshape(CONV_W, dq4)], axis=0)
    placed = lax.dynamic_update_slice(jnp.zeros((nm + CONV_W, d), F32), sm * (mc == 0).astype(F32), (0, p * dq4))
    gathered = allreduce_small(_rows8([placed[i:i + 1] for i in range(nm + CONV_W)], d), "gather_small")
    meta_full, conv_w4 = gathered[:nm], gathered[nm:]

    chip = p.astype(jnp.int32).reshape(1)
    wg = {n: cast_place(a[n], chip, f"place_{n}") for names in BIG.values() for n in names}
    wg.update(zip(BIG[0], allgather_chips([wg[n] for n in BIG[0]], "ag_l0")))

    def hosted(call, names):
        out, got = call(ag_chips_side([wg[n] for n in names]))
        wg.update(zip(names, got))
        return out

    def gathered_layer(layer):
        wg.update(zip(BIG[layer], ag_forward([wg[n] for n in BIG[layer]], f"ag_fwd_l{layer}")))

    heads_first = lambda w: jnp.moveaxis(w, 0, 1).reshape(w.shape[1], w.shape[0] * w.shape[2], w.shape[3])
    w_out = lambda i: wg[f'l{i}_w_out'].reshape(1, d, d)

    zpad = jnp.zeros((tp - nm - seq, d), F32)
    h0 = jnp.concatenate([meta_full, a['x'][0], zpad], axis=0)
    h0_bf = h0.astype(BF16)

    w0_in, w0_a, w0_x = wg['l0_w_in'], heads_first(wg['l0_w_a']), heads_first(wg['l0_w_x'])
    ug0 = hosted(lambda side: mm_nn(cfg, h0_bf, w0_in, "l0_in", side=side), ['l1_w_in'])
    u0, u0_bf = conv_fwd(cfg, ug0, conv_w4, vec('l0_conv_b'), "l0_conv")
    pa0 = bd_nn(cfg, u0_bf, w0_a, "l0_gate_a")
    px0 = bd_nn(cfg, u0_bf, w0_x, "l0_gate_x")
    hs0, hprev0, y0 = lru_fwd(cfg, u0, pa0, px0, ug0, vec('l0_b_a'), vec('l0_b_x'), vec('l0_lam'), "l0_lru")
    o0 = hosted(lambda side: mm_nn(cfg, y0, w_out(0), "l0_out", side=side), ['l1_w_grp', 'l1_w_out'])
    h1, h1_bf, xh0, rs0 = ln_fwd(cfg, h0, o0, vec('l0_ln_g'), vec('l0_ln_b'), alpha, "l0_ln")

    gathered_layer(1)
    w1_in, w1_grp = wg['l1_w_in'], heads_first(wg['l1_w_grp'])
    ug1 = hosted(lambda side: mm_nn(cfg, h1_bf, w1_in, "l1_in", side=side), ['l2_w_in'])
    p1 = pool_fwd(cfg, ug1, "l1_pool")
    mm1 = bd_nn(cfg, p1, w1_grp, "l1_grp")
    y1 = gate_fwd(cfg, mm1, ug1, 1, vec('l1_scale'), "l1_gate")
    o1 = hosted(lambda side: mm_nn(cfg, y1, w_out(1), "l1_out", side=side), ['l2_w_uq', 'l2_w_ukv', 'l2_w_out'])
    h2, h2_bf, xh1, rs1 = ln_fwd(cfg, h1, o1, vec('l1_ln_g'), vec('l1_ln_b'), alpha, "l1_ln")

    gathered_layer(2)
    w2_in = jnp.pad(_unblock(wg['l2_w_in'], 1), ((0, 0), (0, cfg.cw - (ql + kl + cfg.rope))))[None]
    w2_uq = jnp.pad(_unblock(wg['l2_w_uq'], 1).reshape(ql, hm, qk), ((0, 0), (0, 0), (0, hp - qk))).reshape(1, ql, hm * hp)
    w2_ukv = _unblock(wg['l2_w_ukv'], 1).reshape(kl, hm, 2, npe).transpose(0, 2, 1, 3).reshape(1, kl, 2 * hm * npe)
    proj = mm_nn(cfg, h2_bf, w2_in, "l2_in")
    cqn, ckvn = mla_norm(cfg, proj, vec('l2_q_norm'), vec('l2_kv_norm'), "l2_norm")
    q_raw = mm_nn(cfg, cqn, w2_uq, "l2_uq")
    kv_raw = mm_nn(cfg, ckvn, w2_ukv, "l2_ukv")
    tabs = _mla_tables(cfg)
    qf, kf, vb = mla_prep(cfg, q_raw, kv_raw, proj, tabs, "l2_prep")
    o_att, lse = hosted(lambda side: attn_fwd(cfg, qf, kf, vb, "l2_attn", side=side), ['l3_w_in', 'l3_w_out'])
    y2 = gate_fwd(cfg, o_att, proj, 0, None, "l2_gate")
    o2 = mm_nn(cfg, y2, w_out(2), "l2_out")
    h3, h3_bf, xh2, rs2 = ln_fwd(cfg, h2, o2, vec('l2_ln_g'), vec('l2_ln_b'), alpha, "l2_ln")

    gathered_layer(3)
    w3_in = wg['l3_w_in']
    qkvg = mm_nn(cfg, h3_bf, w3_in, "l3_in")
    cos, sin = _ret_rope_tables(cfg)
    qr, kr, vr = ret_prep(cfg, qkvg, cos, sin, "l3_prep")
    rt = _ret_tables(cfg)
    on, rsr = ret_fwd(cfg, qr, kr, vr, rt, "l3_ret")
    y3 = gate_fwd(cfg, on, qkvg, 3, None, "l3_gate")
    o3 = mm_nn(cfg, y3, w_out(3), "l3_out")
    h4, _, xh3, rs3 = ln_fwd(cfg, h3, o3, vec('l3_ln_g'), vec('l3_ln_b'), alpha, "l3_ln")

    tgt = jnp.concatenate([jnp.zeros((nm, d), F32), a['loss_target'][0], zpad], axis=0)
    dy4, lacc = loss_grad(cfg, h4, tgt, "loss")
    loss = lax.psum(0.5 * jnp.sum(lacc) / d, ("x", "y", "c"))

    grads, small, parts, landed = {}, {}, {}, {}

    def rs_begin(layer, full, call=None):
        out, got = rs_pair_sums([g.reshape((4, 2, g.shape[1] // 2) + g.shape[2:]) for g in full], f"l{layer}", call)
        parts.update(zip(BIG[layer], got))
        return out

    def carried(call, names):
        out, got = call(rs_chips_side([parts[n] for n in names]))
        landed.update(zip(names, got))
        return out

    def rs_end(layer):
        names = BIG[layer]
        got = rs_finish([parts[n] for n in names], [landed[n] for n in names], f"l{layer}")
        for n, g in zip(names, got):
            grads[n] = g.reshape(a[n].shape)

    dz3, dz3_bf, small['l3_ln_g'], small['l3_ln_b'] = ln_bwd(cfg, None, dy4, xh3, rs3, vec('l3_ln_g'), alpha, "l3_ln_b")
    g_out = mm_tn(cfg, y3, dz3_bf, 1, "l3_dw_out")
    dyb = mm_nt(cfg, dz3_bf, w_out(3), "l3_dy")
    d_on, dg = gate_bwd(cfg, dyb, on, qkvg, 3, None, F32, "l3_gate_b")
    do_bf, dqr = ret_bwd_q(cfg, qr, kr, vr, d_on, on, rsr, rt, "l3_ret_bq")
    dkr, dvr = ret_bwd_kv(cfg, qr, kr, vr, do_bf, rt, "l3_ret_bkv")
    d_in = ret_prep_bwd(cfg, dqr, dkr, dvr, dg, cos, sin, "l3_prep_b")
    g_in = mm_tn(cfg, h3_bf, d_in, 4, "l3_dw_in")
    dh = mm_nt(cfg, d_in, w3_in, "l3_dh")
    g3 = [g_in, g_out.reshape(4, dq4, d)]

    dz2, dz2_bf, small['l2_ln_g'], small['l2_ln_b'] = ln_bwd(cfg, dz3, dh, xh2, rs2, vec('l2_ln_g'), alpha, "l2_ln_b")
    g_out = rs_begin(3, g3, lambda side: mm_tn(cfg, y2, dz2_bf, 1, "l2_dw_out", side=side))
    dyb = mm_nt(cfg, dz2_bf, w_out(2), "l2_dy")
    do_bf, dg = gate_bwd(cfg, dyb, o_att, proj, 0, None, BF16, "l2_gate_b")
    dq_full, dk_full, dv_bf = carried(lambda side: attn_bwd(cfg, qf, kf, vb, do_bf, o_att, lse, "l2_attn_b", side=side), BIG[3])
    rs_end(3)
    dq_raw, dk_nope, d_kr = mla_prep_bwd(cfg, dq_full, dk_full, tabs, "l2_prep_b")
    dkv_raw = jnp.concatenate([dk_nope, dv_bf], axis=1)
    g_uq = mm_tn(cfg, cqn, dq_raw, 1, "l2_dw_uq")
    d_cqn = mm_nt(cfg, dq_raw, w2_uq, "l2_dcq")
    g_ukv = mm_tn(cfg, ckvn, dkv_raw, 1, "l2_dw_ukv")
    d_ckvn = mm_nt(cfg, dkv_raw, w2_ukv, "l2_dckv")
    d_c, nacc = mla_norm_bwd(cfg, proj, d_cqn, d_ckvn, d_kr, vec('l2_q_norm'), vec('l2_kv_norm'), "l2_norm_b")
    small['l2_q_norm'], small['l2_kv_norm'] = nacc[:, :ql], nacc[:, ql:ql + kl]
    d_in = jnp.concatenate([dg, d_c], axis=1)
    g_in = mm_tn(cfg, h2_bf, d_in, 1, "l2_dw_in")
    dh = mm_nt(cfg, d_in, w2_in, "l2_dh")
    g_in = _block(g_in[0][:, :d + ql + kl + cfg.rope], 1)
    g_uq = _block(g_uq.reshape(ql, hm, hp)[:, :, :qk].reshape(ql, hm * qk), 1)
    g_ukv = _block(g_ukv.reshape(kl, 2, hm, npe).transpose(0, 2, 1, 3).reshape(kl, 2 * hm * npe), 1)
    g2 = [g_in, g_uq, g_ukv, g_out.reshape(4, dq4, d)]

    dz1, dz1_bf, small['l1_ln_g'], small['l1_ln_b'] = ln_bwd(cfg, dz2, dh, xh1, rs1, vec('l1_ln_g'), alpha, "l1_ln_b")
    g_out = rs_begin(2, g2, lambda side: mm_tn(cfg, y1, dz1_bf, 1, "l1_dw_out", side=side))
    dyb = mm_nt(cfg, dz1_bf, w_out(1), "l1_dy")
    d_mm1, dg, small['l1_scale'] = gate_bwd(cfg, dyb, mm1, ug1, 1, vec('l1_scale'), BF16, "l1_gate_b")
    g_grp = bd_tn(cfg, p1, d_mm1, len(cfg.pool_windows), "l1_dw_grp")
    dp = bd_nt(cfg, d_mm1, w1_grp, "l1_dp")
    du = pool_bwd(cfg, dp, "l1_pool_b")
    d_in = jnp.concatenate([du, dg], axis=1)
    g_in = carried(lambda side: mm_tn(cfg, h1_bf, d_in, 4, "l1_dw_in", side=side), ['l2_w_in', 'l2_w_uq'])
    dh = carried(lambda side: mm_nt(cfg, d_in, w1_in, "l1_dh", side=side), ['l2_w_ukv', 'l2_w_out'])
    rs_end(2)
    blocks_first = lambda g: jnp.moveaxis(g.reshape(g.shape[0], 4, g.shape[1] // 4, g.shape[2]), 1, 0)
    g1 = [g_in, blocks_first(g_grp), g_out.reshape(4, dq4, d)]

    dz0, dz0_bf, small['l0_ln_g'], small['l0_ln_b'] = ln_bwd(cfg, dz1, dh, xh0, rs0, vec('l0_ln_g'), alpha, "l0_ln_b")
    g_out = rs_begin(1, g1, lambda side: mm_tn(cfg, y0, dz0_bf, 1, "l0_dw_out", side=side))
    dyb = mm_nt(cfg, dz0_bf, w_out(0), "l0_dy")
    dg, dpa, dpx, du_dir, lacc0 = lru_bwd(cfg, dyb, ug0, hs0, hprev0, u0, pa0, px0, vec('l0_b_a'), vec('l0_b_x'), vec('l0_lam'), "l0_lru_b")
    g_a = bd_tn(cfg, u0_bf, dpa, cfg.lru_heads, "l0_dw_a")
    g_x = bd_tn(cfg, u0_bf, dpx, cfg.lru_heads, "l0_dw_x")
    du_a = bd_nt(cfg, dpa, w0_a, "l0_du_a")
    du_x = bd_nt(cfg, dpx, w0_x, "l0_du_x")
    du_pre, cacc = conv_bwd(cfg, du_dir, du_a, du_x, ug0, conv_w4, "l0_conv_b")
    d_in = jnp.concatenate([du_pre, dg], axis=1)
    g_in = carried(lambda side: mm_tn(cfg, h0_bf, d_in, 4, "l0_dw_in", side=side), ['l1_w_in'])
    dh = carried(lambda side: mm_nt(cfg, d_in, w0_in, "l0_dh", side=side), ['l1_w_grp', 'l1_w_out'])
    rs_end(1)
    rs_begin(0, [g_in, blocks_first(g_a), blocks_first(g_x), g_out.reshape(4, dq4, d)])
    landed.update(zip(BIG[0], scatter_chips([parts[n] for n in BIG[0]], "rs_chips_l0")))
    rs_end(0)
    dh0 = axpy(cfg, dz0, dh, alpha, "dh0")
    grad_x = dh0[nm:nm + seq][None]
    small['l0_lam'], small['l0_b_a'], small['l0_b_x'] = lacc0[0:1], lacc0[1:2], lacc0[2:3]
    small['l0_conv_b'] = cacc[4:5]

    rows = [dh0[i:i + 1] for i in range(nm)] + [cacc[k:k + 1] for k in range(CONV_W)] + [small[n] for n in SMALL_REPL]
    red = allreduce_small(_rows8(rows, d), "reduce_small")
    sh = lax.dynamic_slice(red[:nm + CONV_W], (0, p * dq4), (nm + CONV_W, dq4))
    grads['meta_tokens'] = sh[:nm]
    grads['l0_conv_w'] = sh[nm:].reshape(a['l0_conv_w'].shape)
    for i, n in enumerate(SMALL_REPL):
        grads[n] = red[nm + CONV_W + i, :a[n].shape[0]]

    delta, new_m, new_v = {}, {}, {}
    for n in sum(BIG.values(), []):
        delta[n], new_m[n], new_v[n] = adamw(a[n], grads[n], a['m_' + n], a['v_' + n], f"adamw_{n}")
    pack_s = lambda pre: jnp.concatenate([a[pre + 'meta_tokens'], a[pre + 'l0_conv_w'].reshape(CONV_W, dq4)], axis=0)
    ds_, ms_, vs_ = adamw(pack_s(''), sh, pack_s('m_'), pack_s('v_'), "adamw_small_sharded")
    for out, packed in ((delta, ds_), (new_m, ms_), (new_v, vs_)):
        out['meta_tokens'] = packed[:nm]
        out['l0_conv_w'] = packed[nm:].reshape(a['l0_conv_w'].shape)
    pack_r = lambda pre: jnp.stack([jnp.pad(a[pre + n], (0, d - a[n].shape[0])) for n in SMALL_REPL])
    dr_, mr_, vr_ = adamw(pack_r(''), red[nm + CONV_W:nm + CONV_W + len(SMALL_REPL)], pack_r('m_'), pack_r('v_'), "adamw_small_repl")
    for out, packed in ((delta, dr_), (new_m, mr_), (new_v, vr_)):
        for i, n in enumerate(SMALL_REPL):
            out[n] = packed[i, :a[n].shape[0]]

    return (loss, grad_x, *[grads[n] for n in W_NAMES], *[delta[n] for n in W_NAMES],
            *[new_m[n] for n in W_NAMES], *[new_v[n] for n in W_NAMES])


def kernel(x, meta_tokens, l0_w_in, l0_conv_w, l0_conv_b, l0_w_a, l0_b_a, l0_w_x, l0_b_x, l0_lam, l0_w_out, l0_ln_g, l0_ln_b, l1_w_in, l1_w_grp, l1_scale, l1_w_out, l1_ln_g, l1_ln_b, l2_w_in, l2_q_norm, l2_w_uq, l2_kv_norm, l2_w_ukv, l2_w_out, l2_ln_g, l2_ln_b, l3_w_in, l3_w_out, l3_ln_g, l3_ln_b, loss_target, m_meta_tokens, m_l0_w_in, m_l0_conv_w, m_l0_conv_b, m_l0_w_a, m_l0_b_a, m_l0_w_x, m_l0_b_x, m_l0_lam, m_l0_w_out, m_l0_ln_g, m_l0_ln_b, m_l1_w_in, m_l1_w_grp, m_l1_scale, m_l1_w_out, m_l1_ln_g, m_l1_ln_b, m_l2_w_in, m_l2_q_norm, m_l2_w_uq, m_l2_kv_norm, m_l2_w_ukv, m_l2_w_out, m_l2_ln_g, m_l2_ln_b, m_l3_w_in, m_l3_w_out, m_l3_ln_g, m_l3_ln_b, v_meta_tokens, v_l0_w_in, v_l0_conv_w, v_l0_conv_b, v_l0_w_a, v_l0_b_a, v_l0_w_x, v_l0_b_x, v_l0_lam, v_l0_w_out, v_l0_ln_g, v_l0_ln_b, v_l1_w_in, v_l1_w_grp, v_l1_scale, v_l1_w_out, v_l1_ln_g, v_l1_ln_b, v_l2_w_in, v_l2_q_norm, v_l2_w_uq, v_l2_kv_norm, v_l2_w_ukv, v_l2_w_out, v_l2_ln_g, v_l2_ln_b, v_l3_w_in, v_l3_w_out, v_l3_ln_g, v_l3_ln_b):
    return _step(REAL, dict(locals()))
```

```python
import functools
import math
from typing import NamedTuple

import jax
import jax.numpy as jnp
import numpy as np
from jax import lax
from jax.experimental import pallas as pl
from jax.experimental.pallas import tpu as pltpu

F32 = jnp.float32
BF16 = jnp.bfloat16

LN_EPS = 1e-5
RMS_EPS = 1e-6
ROPE_BASE = 10000.0
LRU_C = 8.0
CONV_W = 4
HIST = 16
ADAM_LR, ADAM_B1, ADAM_B2, ADAM_EPS, ADAM_WD, ADAM_STEP = 0.001, 0.9, 0.999, 1e-08, 0.01, 10
VMEM_LIMIT_V7X = 56 * 1024 * 1024
MESH = pl.DeviceIdType.MESH


class Cfg(NamedTuple):
    d: int
    seq: int
    n_meta: int
    tp: int
    depth: int
    lru_heads: int
    pool_windows: tuple
    mla_heads: int
    nope: int
    rope: int
    q_lora: int
    kv_lora: int
    cw: int
    ret_heads: int
    chunk: int
    tq: int
    tkw: int
    tr: int
    tt: int
    tc: int
    tm: int
    tn: int
    tkc: int
    tkm: int
    la: int


REAL = Cfg(d=4096, seq=4096, n_meta=16, tp=4224, depth=4, lru_heads=16, pool_windows=(2, 4, 8, 16),
           mla_heads=32, nope=128, rope=64, q_lora=1024, kv_lora=512, cw=2048, ret_heads=16, chunk=384,
           tq=384, tkw=2, tr=128, tt=1056, tc=512, tm=1408, tn=1024, tkc=4096, tkm=4224, la=128)


def _div_tile(n, target, align):
    best = None
    for t in range(align, min(n, target) + 1, align):
        if n % t == 0:
            best = t
    return best or n


def _cparams(*sem):
    return pltpu.CompilerParams(dimension_semantics=sem, vmem_limit_bytes=VMEM_LIMIT_V7X)


def _sig(x):
    return 1.0 / (1.0 + jnp.exp(-x))


def _expm1(x):
    p = x * (1.0 + x * (1.0 / 2) * (1.0 + x * (1.0 / 3) * (1.0 + x * (1.0 / 4) * (1.0 + x * (1.0 / 5) * (1.0 + x * (1.0 / 6))))))
    return jnp.where(x > -0.1, p, jnp.exp(x) - 1.0)


def _colsum8(x):
    r, w = x.shape
    return x.reshape(r // 8, 8, w).sum(axis=0)


class Side(NamedTuple):
    ins: list
    inplace: list
    outs: list
    sems: list
    start: object
    finish: object


def _host_call(body, grid, in_specs, out_specs, out_shape, scratch_shapes, semantics, name, args, side=None):
    if side is None:
        return pl.pallas_call(body, grid=grid, in_specs=in_specs, out_specs=out_specs, out_shape=out_shape,
                              scratch_shapes=scratch_shapes, compiler_params=_cparams(*semantics), name=name)(*args), None
    n_in, n_out, n_scr = len(in_specs), len(out_specs), len(scratch_shapes)
    n_si, n_sp, n_so = len(side.ins), len(side.inplace), len(side.outs)

    def wrapped(*refs):
        it = iter(refs)
        take = lambda k: [next(it) for _ in range(k)]
        ins, s_ins, s_inplace_in, outs, s_inplace, s_outs, scr, sems = (take(n_in), take(n_si), take(n_sp), take(n_out),
                                                                        take(n_sp), take(n_so), take(n_scr), take(len(side.sems)))
        ids = [pl.program_id(i) for i in range(len(grid))]
        first = functools.reduce(jnp.logical_and, [i == 0 for i in ids])
        last = functools.reduce(jnp.logical_and, [i == g - 1 for i, g in zip(ids, grid)])

        @pl.when(first)
        def _():
            side.start(s_ins, s_inplace, s_outs, sems)

        body(*ins, *outs, *scr)

        @pl.when(last)
        def _():
            side.finish(s_ins, s_inplace, s_outs, sems)

    any_spec = pl.BlockSpec(memory_space=pl.ANY)
    out = pl.pallas_call(
        wrapped, grid=grid, in_specs=list(in_specs) + [any_spec] * (n_si + n_sp),
        out_specs=list(out_specs) + [any_spec] * (n_sp + n_so),
        out_shape=list(out_shape) + [jax.ShapeDtypeStruct(x.shape, x.dtype) for x in side.inplace] + list(side.outs),
        scratch_shapes=list(scratch_shapes) + list(side.sems),
        input_output_aliases={n_in + n_si + i: n_out + i for i in range(n_sp)},
        compiler_params=_cparams(*(("arbitrary",) * len(grid))), name=name)(*args, *side.ins, *side.inplace)
    return out[:n_out], out[n_out:]


def _mm_call(a, b, grid, a_spec, b_spec, o_spec, out_sds, acc_shape, dims, name, side=None):
    nk = grid[-1]
    kax = len(grid) - 1

    def product(a_ref, b_ref):
        return lax.dot_general(a_ref[...].astype(BF16), b_ref[...].astype(BF16), dims, preferred_element_type=F32)

    def body_one(a_ref, b_ref, o_ref):
        o_ref[...] = product(a_ref, b_ref).astype(o_ref.dtype)

    def body_acc(a_ref, b_ref, o_ref, acc_ref):
        k = pl.program_id(kax)

        @pl.when(k == 0)
        def _():
            acc_ref[...] = jnp.zeros_like(acc_ref)

        acc_ref[...] += product(a_ref, b_ref)

        @pl.when(k == nk - 1)
        def _():
            o_ref[...] = acc_ref[...].astype(o_ref.dtype)

    sem = ("parallel",) * kax + ("arbitrary",)
    body, scratch = (body_one, []) if nk == 1 else (body_acc, [pltpu.VMEM(acc_shape, F32)])
    outs, side_outs = _host_call(body, grid, [a_spec, b_spec], [o_spec], [out_sds], scratch, sem, name, (a, b), side)
    return outs[0] if side is None else (outs[0], side_outs)


def _out_tile(cfg, tk):
    return cfg.tn if tk <= cfg.tn else cfg.tn // 2


_NN = (((1,), (0,)), ((), ()))
_NT = (((1,), (1,)), ((), ()))
_TN = (((0,), (0,)), ((), ()))


def mm_nn(cfg, a, w, name, out_dtype=F32, a_col0=0, side=None):
    m = a.shape[0]
    g, k, ns = w.shape
    tm, tk = _div_tile(m, cfg.tm, 16), _div_tile(k, cfg.tkc, cfg.la)
    tn = _div_tile(ns, _out_tile(cfg, tk), cfg.la)
    npb, a0 = ns // tn, a_col0 // tk
    assert a_col0 % tk == 0
    return _mm_call(a, w, (m // tm, g * npb, k // tk),
                    pl.BlockSpec((tm, tk), lambda i, j, kk: (i, a0 + kk)),
                    pl.BlockSpec((None, tk, tn), lambda i, j, kk: (j // npb, kk, j % npb)),
                    pl.BlockSpec((tm, tn), lambda i, j, kk: (i, j)),
                    jax.ShapeDtypeStruct((m, g * ns), out_dtype), (tm, tn), _NN, name, side)


def mm_nt(cfg, dy, w, name, out_dtype=F32, side=None):
    m = dy.shape[0]
    g, k, ns = w.shape
    tm, tk = _div_tile(m, cfg.tm, 16), _div_tile(ns, cfg.tkc, cfg.la)
    tn = _div_tile(k, _out_tile(cfg, tk), cfg.la)
    kpb = ns // tk
    return _mm_call(dy, w, (m // tm, k // tn, g * kpb),
                    pl.BlockSpec((tm, tk), lambda i, j, kk: (i, kk)),
                    pl.BlockSpec((None, tn, tk), lambda i, j, kk: (kk // kpb, j, kk % kpb)),
                    pl.BlockSpec((tm, tn), lambda i, j, kk: (i, j)),
                    jax.ShapeDtypeStruct((m, k), out_dtype), (tm, tn), _NT, name, side)


def mm_tn(cfg, x, dy, g, name, x_col0=0, k=None, side=None):
    m = x.shape[0]
    k = x.shape[1] if k is None else k
    ns = dy.shape[1] // g
    tk = _div_tile(m, cfg.tkm, 16)
    tmo, tn = _div_tile(k, _out_tile(cfg, tk), cfg.la), _div_tile(ns, cfg.tn, cfg.la)
    npb, x0 = ns // tn, x_col0 // tmo
    assert x_col0 % tmo == 0
    return _mm_call(x, dy, (k // tmo, g * npb, m // tk),
                    pl.BlockSpec((tk, tmo), lambda i, j, kk: (kk, x0 + i)),
                    pl.BlockSpec((tk, tn), lambda i, j, kk: (kk, j)),
                    pl.BlockSpec((None, tmo, tn), lambda i, j, kk: (j // npb, i, j % npb)),
                    jax.ShapeDtypeStruct((g, k, ns), F32), (tmo, tn), _TN, name, side)


def bd_nn(cfg, x, w, name, out_dtype=F32):
    m = x.shape[0]
    g, kg, ng = w.shape
    tm, tn = _div_tile(m, cfg.tm, 16), _div_tile(ng, cfg.tn, cfg.la)
    npb = ng // tn
    return _mm_call(x, w, (m // tm, g * npb, 1),
                    pl.BlockSpec((tm, kg), lambda i, j, kk: (i, j // npb)),
                    pl.BlockSpec((None, kg, tn), lambda i, j, kk: (j // npb, 0, j % npb)),
                    pl.BlockSpec((tm, tn), lambda i, j, kk: (i, j)),
                    jax.ShapeDtypeStruct((m, g * ng), out_dtype), (tm, tn), _NN, name)


def bd_nt(cfg, dy, w, name, out_dtype=F32):
    m = dy.shape[0]
    g, kg, ng = w.shape
    tm, tn = _div_tile(m, cfg.tm, 16), _div_tile(kg, cfg.tn, cfg.la)
    npb = kg // tn
    return _mm_call(dy, w, (m // tm, g * npb, 1),
                    pl.BlockSpec((tm, ng), lambda i, j, kk: (i, j // npb)),
                    pl.BlockSpec((None, tn, ng), lambda i, j, kk: (j // npb, j % npb, 0)),
                    pl.BlockSpec((tm, tn), lambda i, j, kk: (i, j)),
                    jax.ShapeDtypeStruct((m, g * kg), out_dtype), (tm, tn), _NT, name)


def bd_tn(cfg, x, dy, g, name):
    m = x.shape[0]
    kg, ng = x.shape[1] // g, dy.shape[1] // g
    tmo, tn, tk = _div_tile(kg, cfg.tn, cfg.la), _div_tile(ng, cfg.tn, cfg.la), _div_tile(m, cfg.tm, 16)
    mpb, npb = kg // tmo, ng // tn
    return _mm_call(x, dy, (g * mpb, npb, m // tk),
                    pl.BlockSpec((tk, tmo), lambda i, j, kk: (kk, i)),
                    pl.BlockSpec((tk, tn), lambda i, j, kk: (kk, (i // mpb) * npb + j)),
                    pl.BlockSpec((None, tmo, tn), lambda i, j, kk: (i // mpb, i % mpb, j)),
                    jax.ShapeDtypeStruct((g, kg, ng), F32), (tmo, tn), _TN, name)


def _rowwise(cfg, body, ins, outs, name, n_acc=0, side=None):
    tr = cfg.tr
    n = cfg.tp // tr
    in_specs, args = [], []
    for spec in ins:
        arr = spec[0]
        if spec[1] is None:
            in_specs.append(pl.BlockSpec(arr.shape, lambda i, nd=arr.ndim: (0,) * nd))
        else:
            in_specs.append(pl.BlockSpec((tr, spec[1]), lambda i, cb=spec[2]: (i, cb)))
        args.append(arr)
    out_specs, out_shape = [], []
    for o in outs:
        if len(o) == 2:
            out_specs.append(pl.BlockSpec((tr, o[0]), lambda i: (i, 0)))
            out_shape.append(jax.ShapeDtypeStruct((cfg.tp, o[0]), o[1]))
        else:
            out_specs.append(pl.BlockSpec((8, o[0]), lambda i: (0, 0)))
            out_shape.append(jax.ShapeDtypeStruct((8, o[0]), F32))
    outs, side_outs = _host_call(body, (n,), in_specs, out_specs, out_shape, [], ("arbitrary" if n_acc else "parallel",),
                                 name, args, side)
    return outs if side is None else (outs, side_outs)


def ln_fwd(cfg, h, o, g, b, alpha, name, side=None):
    d = cfg.d

    def body(h_ref, o_ref, g_ref, b_ref, y_ref, yb_ref, xh_ref, rs_ref):
        z = alpha * h_ref[...] + o_ref[...]
        mu = jnp.mean(z, axis=-1, keepdims=True)
        zc = z - mu
        var = jnp.mean(zc * zc, axis=-1, keepdims=True)
        rstd = lax.rsqrt(var + LN_EPS)
        xh = zc * rstd
        y = xh * g_ref[...] + b_ref[...]
        y_ref[...] = y
        yb_ref[...] = y.astype(BF16)
        xh_ref[...] = xh
        rs_ref[...] = rstd

    return _rowwise(cfg, body, [(h, d, 0), (o, d, 0), (g, None), (b, None)], [(d, F32), (d, BF16), (d, F32), (1, F32)], name,
                    side=side)


def ln_bwd(cfg, d_res, d_mm, xhat, rstd, g, alpha, name, side=None):
    d = cfg.d
    two = d_res is not None

    def body(*refs):
        if two:
            dr_ref, dm_ref, xh_ref, rs_ref, g_ref, dz_ref, dzb_ref, dg_ref, db_ref = refs
            dy = alpha * dr_ref[...] + dm_ref[...]
        else:
            dm_ref, xh_ref, rs_ref, g_ref, dz_ref, dzb_ref, dg_ref, db_ref = refs
            dy = dm_ref[...]
        xh = xh_ref[...]

        @pl.when(pl.program_id(0) == 0)
        def _():
            dg_ref[...] = jnp.zeros_like(dg_ref)
            db_ref[...] = jnp.zeros_like(db_ref)

        dg_ref[...] += _colsum8(dy * xh)
        db_ref[...] += _colsum8(dy)
        dxh = dy * g_ref[...]
        m1 = jnp.mean(dxh, axis=-1, keepdims=True)
        m2 = jnp.mean(dxh * xh, axis=-1, keepdims=True)
        dz = rs_ref[...] * (dxh - m1 - xh * m2)
        dz_ref[...] = dz
        dzb_ref[...] = dz.astype(BF16)

    ins = ([(d_res, d, 0)] if two else []) + [(d_mm, d, 0), (xhat, d, 0), (rstd, 1, 0), (g, None)]
    return _rowwise(cfg, body, ins, [(d, F32), (d, BF16), (d,), (d,)], name, n_acc=2, side=side)


def loss_grad(cfg, y, tgt, name):
    d, tr = cfg.d, cfg.tr
    lo, hi = cfg.n_meta, cfg.n_meta + cfg.seq

    def body(y_ref, t_ref, dy_ref, acc_ref):
        i = pl.program_id(0)

        @pl.when(i == 0)
        def _():
            acc_ref[...] = jnp.zeros_like(acc_ref)

        row = i * tr + lax.broadcasted_iota(jnp.int32, (tr, 1), 0)
        err = jnp.where((row >= lo) & (row < hi), y_ref[...] - t_ref[...], 0.0)
        dy_ref[...] = err * (1.0 / d)
        acc_ref[...] += _colsum8(err * err)

    return _rowwise(cfg, body, [(y, d, 0), (tgt, d, 0)], [(d, F32), (d,)], name, n_acc=1)


def axpy(cfg, a, b, alpha, name, side=None):
    d = cfg.d

    def body(a_ref, b_ref, o_ref):
        o_ref[...] = alpha * a_ref[...] + b_ref[...]

    return _rowwise(cfg, body, [(a, d, 0), (b, d, 0)], [(d, F32)], name, side=side)


def _time_call(cfg, body, ins, outs, accs, scratch, name, reverse=False, groups=1):
    d = cfg.d
    tc = _div_tile(d // groups, cfg.tc, cfg.la)
    tt = _div_tile(cfg.tp, cfg.tt, 16)
    nc, nt = d // tc, cfg.tp // tt
    tmap = (lambda t: nt - 1 - t) if reverse else (lambda t: t)
    in_specs, args = [], []
    for spec in ins:
        arr = spec[0]
        if len(spec) == 2:
            in_specs.append(pl.BlockSpec((tt, tc), lambda c, t, off=spec[1] * nc: (tmap(t), off + c)))
        else:
            in_specs.append(pl.BlockSpec((arr.shape[0], tc), lambda c, t: (0, c)))
        args.append(arr)
    out_specs = [pl.BlockSpec((tt, tc), lambda c, t: (tmap(t), c)) for _ in outs]
    out_shape = [jax.ShapeDtypeStruct((cfg.tp, d), dt) for dt in outs]
    for _ in range(accs):
        out_specs.append(pl.BlockSpec((8, tc), lambda c, t: (0, c)))
        out_shape.append(jax.ShapeDtypeStruct((8, d), F32))
    return pl.pallas_call(functools.partial(body, tt=tt, tc=tc, nt=nt), grid=(nc, nt), in_specs=in_specs,
                          out_specs=out_specs, out_shape=out_shape,
                          scratch_shapes=[pltpu.VMEM(s, F32) for s in scratch(tt, tc)],
                          compiler_params=_cparams("parallel", "arbitrary"), name=name)(*args)


def _push_history(s_ref, new, t, tt):
    @pl.when(t == 0)
    def _():
        s_ref[pl.ds(0, HIST), :] = jnp.zeros((HIST, s_ref.shape[1]), F32)

    @pl.when(t > 0)
    def _():
        s_ref[pl.ds(0, HIST), :] = s_ref[pl.ds(tt, HIST), :]

    s_ref[pl.ds(HIST, tt), :] = new


def _push_future(s_ref, new, t, tt):
    @pl.when(t == 0)
    def _():
        s_ref[pl.ds(tt, HIST), :] = jnp.zeros((HIST, s_ref.shape[1]), F32)

    @pl.when(t > 0)
    def _():
        s_ref[pl.ds(tt, HIST), :] = s_ref[pl.ds(0, HIST), :]

    s_ref[pl.ds(0, tt), :] = new


def conv_fwd(cfg, ug, w4, cb, name):
    def body(x_ref, w_ref, b_ref, u_ref, ub_ref, s_ref, *, tt, tc, nt):
        t = pl.program_id(1)
        _push_history(s_ref, x_ref[...], t, tt)
        acc = b_ref[...] + w_ref[pl.ds(0, 1), :] * s_ref[pl.ds(HIST - 3, tt), :]
        for k in range(1, CONV_W):
            acc = acc + w_ref[pl.ds(k, 1), :] * s_ref[pl.ds(HIST - 3 + k, tt), :]
        u_ref[...] = acc
        ub_ref[...] = acc.astype(BF16)

    return _time_call(cfg, body, [(ug, 0), (w4,), (cb,)], [F32, BF16], 0, lambda tt, tc: [(tt + HIST, tc)], name)


def conv_bwd(cfg, du_a, du_b, du_c, ug, w4, name):
    def body(a_ref, b_ref, c_ref, x_ref, w_ref, dx_ref, acc_ref, s_ref, *, tt, tc, nt):
        t = pl.program_id(1)

        @pl.when(t == 0)
        def _():
            acc_ref[...] = jnp.zeros_like(acc_ref)

        du = a_ref[...] + b_ref[...] + c_ref[...]
        _push_future(s_ref, du, t, tt)
        x = x_ref[...]
        acc_ref[pl.ds(4, 1), :] += jnp.sum(du, axis=0, keepdims=True)
        dx = None
        for k in range(CONV_W):
            sh = s_ref[pl.ds(3 - k, tt), :]
            term = w_ref[pl.ds(k, 1), :] * sh
            dx = term if dx is None else dx + term
            acc_ref[pl.ds(k, 1), :] += jnp.sum(x * sh, axis=0, keepdims=True)
        dx_ref[...] = dx.astype(BF16)

    return _time_call(cfg, body, [(du_a, 0), (du_b, 0), (du_c, 0), (ug, 0), (w4,)], [BF16], 1,
                      lambda tt, tc: [(tt + HIST, tc)], name, reverse=True)


def _scan8(a, b, rows):
    for k in (1, 2, 4):
        ar = jnp.where(rows >= k, pltpu.roll(a, k, 0), 1.0)
        br = jnp.where(rows >= k, pltpu.roll(b, k, 0), 0.0)
        b = a * br + b
        a = a * ar
    return a, b


def _rscan8(a, b, rows):
    for k in (1, 2, 4):
        ar = jnp.where(rows < 8 - k, pltpu.roll(a, 8 - k, 0), 1.0)
        br = jnp.where(rows < 8 - k, pltpu.roll(b, 8 - k, 0), 0.0)
        b = b + a * br
        a = a * ar
    return a, b


def _lru_gates(u, pa, px, ba, bx, c_lam):
    r = _sig(pa + ba)
    i = _sig(px + bx)
    la = LRU_C * r * c_lam
    em = _expm1(2.0 * la)
    return r, i, jnp.exp(la), em, jnp.sqrt(-em)


def _neg_softplus_neg(lam):
    e = jnp.exp(-jnp.abs(lam))
    u = 1.0 + e
    l1p = jnp.where(u == 1.0, e, jnp.log(u) * (e / jnp.where(u == 1.0, 1.0, u - 1.0)))
    return -(jnp.maximum(-lam, 0.0) + l1p)


def lru_fwd(cfg, u, pa, px, ug, ba, bx, lam, name):
    def body(u_ref, pa_ref, px_ref, g_ref, ba_ref, bx_ref, lam_ref, hs_ref, hp_ref, y_ref, car_ref, *, tt, tc, nt):
        @pl.when(pl.program_id(1) == 0)
        def _():
            car_ref[...] = jnp.zeros_like(car_ref)

        rows = lax.broadcasted_iota(jnp.int32, (8, tc), 0)
        ba, bx = ba_ref[...], bx_ref[...]
        c_lam = _neg_softplus_neg(lam_ref[...])

        def step(it, h):
            ys = []
            for half in range(2):
                rs = pl.ds(pl.multiple_of(it * 16 + half * 8, 8), 8)
                uu = u_ref[rs, :]
                r, i, a, em, s = _lru_gates(uu, pa_ref[rs, :], px_ref[rs, :], ba, bx, c_lam)
                acum, b = _scan8(a, uu * i * s, rows)
                hs = acum * h + b
                hs_ref[rs, :] = hs
                hp_ref[rs, :] = jnp.where(rows >= 1, pltpu.roll(hs, 1, 0), h)
                g = g_ref[rs, :]
                ys.append(hs * (g * _sig(g)))
                h = jnp.broadcast_to(hs[7:8, :], (8, tc))
            y_ref[pl.ds(pl.multiple_of(it * 16, 16), 16), :] = jnp.concatenate(ys, axis=0).astype(BF16)
            return h

        car_ref[...] = lax.fori_loop(0, tt // 16, step, car_ref[...])

    return _time_call(cfg, body, [(u, 0), (pa, 0), (px, 0), (ug, 1), (ba,), (bx,), (lam,)], [F32, F32, BF16], 0,
                      lambda tt, tc: [(8, tc)], name)


def lru_bwd(cfg, dy, ug, hs, hprev, u, pa, px, ba, bx, lam, name):
    def body(dy_ref, g_ref, hs_ref, hp_ref, u_ref, pa_ref, px_ref, ba_ref, bx_ref, lam_ref,
             dg_ref, dpa_ref, dpx_ref, du_ref, acc_ref, ecar_ref, acar_ref, *, tt, tc, nt):
        @pl.when(pl.program_id(1) == 0)
        def _():
            ecar_ref[...] = jnp.zeros_like(ecar_ref)
            acar_ref[...] = jnp.zeros_like(acar_ref)
            acc_ref[...] = jnp.zeros_like(acc_ref)

        rows = lax.broadcasted_iota(jnp.int32, (8, tc), 0)
        ba, bx = ba_ref[...], bx_ref[...]
        c_lam = _neg_softplus_neg(lam_ref[...])

        def step(it, carry):
            ecar, acar, s_lam, s_ba, s_bx = carry
            jt = tt // 16 - 1 - it
            dgs, dpas, dpxs = [None, None], [None, None], [None, None]
            for half in (1, 0):
                rs = pl.ds(pl.multiple_of(jt * 16 + half * 8, 8), 8)
                uu = u_ref[rs, :]
                r, i, a, em, s = _lru_gates(uu, pa_ref[rs, :], px_ref[rs, :], ba, bx, c_lam)
                g = g_ref[rs, :]
                sg = _sig(g)
                dy = dy_ref[rs, :]
                dgs[half] = dy * hs_ref[rs, :] * (sg * (1.0 + g * (1.0 - sg)))
                a_next = jnp.where(rows < 7, pltpu.roll(a, 7, 0), acar)
                acum, e = _rscan8(a_next, dy * (g * sg), rows)
                e = e + acum * ecar
                ecar = jnp.broadcast_to(e[0:1, :], (8, tc))
                acar = jnp.broadcast_to(a[0:1, :], (8, tc))
                du_ref[rs, :] = e * i * s
                d_la = e * hp_ref[rs, :] * a - (e * uu * i) * ((em + 1.0) / s)
                s_lam = s_lam + d_la * (LRU_C * r)
                d_pa = d_la * (LRU_C * c_lam) * r * (1.0 - r)
                d_px = e * uu * s * i * (1.0 - i)
                s_ba = s_ba + d_pa
                s_bx = s_bx + d_px
                dpas[half], dpxs[half] = d_pa, d_px
            r16 = pl.ds(pl.multiple_of(jt * 16, 16), 16)
            dg_ref[r16, :] = jnp.concatenate(dgs, axis=0).astype(BF16)
            dpa_ref[r16, :] = jnp.concatenate(dpas, axis=0).astype(BF16)
            dpx_ref[r16, :] = jnp.concatenate(dpxs, axis=0).astype(BF16)
            return ecar, acar, s_lam, s_ba, s_bx

        z = jnp.zeros((8, tc), F32)
        ecar, acar, s_lam, s_ba, s_bx = lax.fori_loop(0, tt // 16, step, (ecar_ref[...], acar_ref[...], z, z, z))
        ecar_ref[...] = ecar
        acar_ref[...] = acar
        acc_ref[pl.ds(0, 1), :] += jnp.sum(s_lam, axis=0, keepdims=True) * _sig(-lam_ref[...])
        acc_ref[pl.ds(1, 1), :] += jnp.sum(s_ba, axis=0, keepdims=True)
        acc_ref[pl.ds(2, 1), :] += jnp.sum(s_bx, axis=0, keepdims=True)

    return _time_call(cfg, body, [(dy, 0), (ug, 1), (hs, 0), (hprev, 0), (u, 0), (pa, 0), (px, 0), (ba,), (bx,), (lam,)],
                      [BF16, BF16, BF16, F32], 1, lambda tt, tc: [(8, tc), (8, tc)], name, reverse=True)


def _pool_select(cfg, grp, fn):
    for gi, w in enumerate(cfg.pool_windows):
        @pl.when(grp == gi)
        def _(w=w):
            fn(w)


def pool_fwd(cfg, ug, name):
    ng = len(cfg.pool_windows)

    def body(x_ref, p_ref, s_ref, *, tt, tc, nt):
        t = pl.program_id(1)
        grp = pl.program_id(0) // (cfg.d // ng // tc)
        x = x_ref[...]
        _push_history(s_ref, x, t, tt)
        row1 = (t * tt + 1 + lax.broadcasted_iota(jnp.int32, (tt, 1), 0)).astype(F32)

        def write(w):
            ws = x
            for j in range(1, w):
                ws = ws + s_ref[pl.ds(HIST - j, tt), :]
            p_ref[...] = (ws / jnp.minimum(row1, float(w)) - x).astype(BF16)

        _pool_select(cfg, grp, write)

    return _time_call(cfg, body, [(ug, 0)], [BF16], 0, lambda tt, tc: [(tt + HIST, tc)], name, groups=ng)[0]


def pool_bwd(cfg, dp, name):
    ng = len(cfg.pool_windows)

    def body(dp_ref, du_ref, s_ref, *, tt, tc, nt):
        t = pl.program_id(1)
        grp = pl.program_id(0) // (cfg.d // ng // tc)
        dp = dp_ref[...]
        row1 = ((nt - 1 - t) * tt + 1 + lax.broadcasted_iota(jnp.int32, (tt, 1), 0)).astype(F32)

        def write(w):
            dm = dp / jnp.minimum(row1, float(w))
            _push_future(s_ref, dm, t, tt)
            ws = dm
            for j in range(1, w):
                ws = ws + s_ref[pl.ds(j, tt), :]
            du_ref[...] = (ws - dp).astype(BF16)

        _pool_select(cfg, grp, write)

    return _time_call(cfg, body, [(dp, 0)], [BF16], 0, lambda tt, tc: [(tt + HIST, tc)], name, reverse=True, groups=ng)[0]


def gate_fwd(cfg, v, gsrc, gblk, scale, name):
    d = cfg.d

    def body(*refs):
        if scale is None:
            v_ref, g_ref, y_ref = refs
            v = v_ref[...]
        else:
            v_ref, g_ref, s_ref, y_ref = refs
            v = v_ref[...] * s_ref[...]
        g = g_ref[...]
        y_ref[...] = (v * (g * _sig(g))).astype(BF16)

    ins = [(v, d, 0), (gsrc, d, gblk)] + ([] if scale is None else [(scale, None)])
    return _rowwise(cfg, body, ins, [(d, BF16)], name)[0]


def gate_bwd(cfg, dy, v, gsrc, gblk, scale, dv_dtype, name):
    d = cfg.d

    def body(*refs):
        if scale is None:
            dy_ref, v_ref, g_ref, dv_ref, dg_ref = refs
            vs = v_ref[...]
        else:
            dy_ref, v_ref, g_ref, s_ref, dv_ref, dg_ref, acc_ref = refs
            vs = v_ref[...] * s_ref[...]
        g = g_ref[...]
        sg = _sig(g)
        dy = dy_ref[...]
        dvs = dy * (g * sg)
        dg_ref[...] = (dy * vs * (sg * (1.0 + g * (1.0 - sg)))).astype(BF16)
        if scale is None:
            dv_ref[...] = dvs.astype(dv_dtype)
        else:
            @pl.when(pl.program_id(0) == 0)
            def _():
                acc_ref[...] = jnp.zeros_like(acc_ref)

            acc_ref[...] += _colsum8(dvs * v_ref[...])
            dv_ref[...] = (dvs * s_ref[...]).astype(dv_dtype)

    ins = [(dy, d, 0), (v, d, 0), (gsrc, d, gblk)] + ([] if scale is None else [(scale, None)])
    outs = [(d, dv_dtype), (d, BF16)] + ([] if scale is None else [(d,)])
    return _rowwise(cfg, body, ins, outs, name, n_acc=0 if scale is None else 1)


def _swap_halves(x, lo, half):
    w = x.shape[1]
    lane = lax.broadcasted_iota(jnp.int32, x.shape, 1)
    sw = jnp.where(lane < lo + half, pltpu.roll(x, w - half, 1), pltpu.roll(x, half, 1))
    return jnp.where((lane >= lo) & (lane < lo + 2 * half), sw, 0.0)


def mla_norm(cfg, proj, qn, kvn, name):
    ql, kl = cfg.q_lora, cfg.kv_lora

    def body(c_ref, qn_ref, kn_ref, q_ref, k_ref):
        cq = c_ref[:, 0:ql]
        ck = c_ref[:, ql:ql + kl]
        q_ref[...] = (cq * lax.rsqrt(jnp.mean(cq * cq, axis=-1, keepdims=True) + RMS_EPS) * qn_ref[...]).astype(BF16)
        k_ref[...] = (ck * lax.rsqrt(jnp.mean(ck * ck, axis=-1, keepdims=True) + RMS_EPS) * kn_ref[...]).astype(BF16)

    return _rowwise(cfg, body, [(proj, cfg.cw, cfg.d // cfg.cw), (qn, None), (kvn, None)], [(ql, BF16), (kl, BF16)], name)


def mla_norm_bwd(cfg, proj, d_cqn, d_ckvn, d_kr, qn, kvn, name):
    ql, kl, cw, npe = cfg.q_lora, cfg.kv_lora, cfg.cw, cfg.nope

    def one(x, dy, gamma):
        rstd = lax.rsqrt(jnp.mean(x * x, axis=-1, keepdims=True) + RMS_EPS)
        xn = x * rstd
        dxn = dy * gamma
        return rstd * (dxn - xn * jnp.mean(dxn * xn, axis=-1, keepdims=True)), dy * xn

    def body(c_ref, dq_ref, dk_ref, dkr_ref, qn_ref, kn_ref, dc_ref, acc_ref):
        @pl.when(pl.program_id(0) == 0)
        def _():
            acc_ref[...] = jnp.zeros_like(acc_ref)

        dcq, gq = one(c_ref[:, 0:ql], dq_ref[...], qn_ref[...])
        dck, gk = one(c_ref[:, ql:ql + kl], dk_ref[...], kn_ref[...])
        dc_ref[:, 0:ql] = dcq.astype(BF16)
        dc_ref[:, ql:ql + kl] = dck.astype(BF16)
        dc_ref[:, ql + kl:ql + kl + npe] = dkr_ref[...].astype(BF16)
        rest = cw - (ql + kl + npe)
        if rest:
            dc_ref[:, ql + kl + npe:cw] = jnp.zeros((dc_ref.shape[0], rest), BF16)
        acc_ref[:, 0:ql] += _colsum8(gq)
        acc_ref[:, ql:ql + kl] += _colsum8(gk)

    return _rowwise(cfg, body, [(proj, cw, cfg.d // cw), (d_cqn, ql, 0), (d_ckvn, kl, 0), (d_kr, npe, 0), (qn, None), (kvn, None)],
                    [(cw, BF16), (cw,)], name, n_acc=1)


def _mla_tables(cfg):
    r2, npe = cfg.rope // 2, cfg.nope
    inv = ROPE_BASE ** (-jnp.arange(0, cfg.rope, 2, dtype=F32) / cfg.rope)
    ang = jnp.arange(cfg.tp, dtype=F32)[:, None] * inv[None, :]
    cos, sin = jnp.cos(ang), jnp.sin(ang)
    one = jnp.ones((cfg.tp, npe - cfg.rope), F32)
    ck = jnp.concatenate([cos, cos, one], axis=1)
    sk = jnp.concatenate([-sin, sin, 0.0 * one], axis=1)
    cq = jnp.concatenate([jnp.ones((cfg.tp, npe), F32), ck], axis=1)
    sq = jnp.concatenate([jnp.zeros((cfg.tp, npe), F32), sk], axis=1)
    return cq, sq, ck, sk


def mla_prep(cfg, q_raw, kv_raw, proj, tabs, name):
    h, npe, r2, ql, kl = cfg.mla_heads, cfg.nope, cfg.rope // 2, cfg.q_lora, cfg.kv_lora
    hp = 2 * npe

    def body(q_ref, kv_ref, c_ref, cq_ref, sq_ref, ck_ref, sk_ref, qo_ref, ko_ref, vo_ref):
        kr = c_ref[:, ql + kl:ql + kl + npe]
        kr = (kr * ck_ref[...] + _swap_halves(kr, 0, r2) * sk_ref[...]).astype(BF16)
        cq, sq = cq_ref[...], sq_ref[...]
        for i in range(h):
            q = q_ref[:, i * hp:(i + 1) * hp]
            qo_ref[:, i * hp:(i + 1) * hp] = (q * cq + _swap_halves(q, npe, r2) * sq).astype(BF16)
            ko_ref[:, i * hp:i * hp + npe] = kv_ref[:, i * npe:(i + 1) * npe].astype(BF16)
            ko_ref[:, i * hp + npe:(i + 1) * hp] = kr
        vo_ref[...] = kv_ref[:, h * npe:2 * h * npe].astype(BF16)

    cq, sq, ck, sk = tabs
    return _rowwise(cfg, body, [(q_raw, h * hp, 0), (kv_raw, 2 * h * npe, 0), (proj, cfg.cw, cfg.d // cfg.cw),
                                (cq, hp, 0), (sq, hp, 0), (ck, npe, 0), (sk, npe, 0)],
                    [(h * hp, BF16), (h * hp, BF16), (h * npe, BF16)], name)


def mla_prep_bwd(cfg, dq_full, dk_full, tabs, name, side=None):
    h, npe, r2 = cfg.mla_heads, cfg.nope, cfg.rope // 2
    hp = 2 * npe

    def body(dq_ref, dk_ref, cq_ref, sq_ref, ck_ref, sk_ref, dqo_ref, dko_ref, dkr_ref):
        cq, sq = cq_ref[...], sq_ref[...]
        dkr = None
        for i in range(h):
            dq = dq_ref[:, i * hp:(i + 1) * hp]
            dqo_ref[:, i * hp:(i + 1) * hp] = (dq * cq + _swap_halves(dq * sq, npe, r2)).astype(BF16)
            dko_ref[:, i * npe:(i + 1) * npe] = dk_ref[:, i * hp:i * hp + npe].astype(BF16)
            part = dk_ref[:, i * hp + npe:(i + 1) * hp]
            dkr = part if dkr is None else dkr + part
        dkr_ref[...] = dkr * ck_ref[...] + _swap_halves(dkr * sk_ref[...], 0, r2)

    cq, sq, ck, sk = tabs
    return _rowwise(cfg, body, [(dq_full, h * hp, 0), (dk_full, h * hp, 0), (cq, hp, 0), (sq, hp, 0), (ck, npe, 0), (sk, npe, 0)],
                    [(h * hp, BF16), (h * npe, BF16), (npe, F32)], name, side=side)


def _attn_scores(cfg, q, k, diagonal):
    s = lax.dot_general(q, k, _NT, preferred_element_type=F32) * ((cfg.nope + cfg.rope) ** -0.5)
    if diagonal:
        row = lax.broadcasted_iota(jnp.int32, s.shape, 0)
        col = lax.broadcasted_iota(jnp.int32, s.shape, 1)
        s = jnp.where(row >= col, s, -1e30)
    return s


def _tile_rows(i, t):
    return pl.ds(pl.multiple_of(i * t, t), t)


def attn_fwd(cfg, q, k, v, name, side=None):
    h, npe, tq, wide = cfg.mla_heads, cfg.nope, cfg.tq, cfg.tkw
    hp, n = 2 * npe, cfg.tp // tq

    def body(q_ref, k_ref, v_ref, o_ref, lse_ref):
        qi = pl.program_id(1)
        q = q_ref[...]

        def tile(ki, carry, width, diagonal):
            m, l, acc = carry
            rk = _tile_rows(ki, width)
            s = _attn_scores(cfg, q, k_ref[rk, :], diagonal)
            m_new = jnp.maximum(m, jnp.max(s, axis=-1, keepdims=True))
            alpha = jnp.exp(m - m_new)
            p = jnp.exp(s - m_new)
            l = alpha * l + jnp.sum(p, axis=-1, keepdims=True)
            acc = alpha * acc + lax.dot_general(p.astype(BF16), v_ref[rk, :], _NN, preferred_element_type=F32)
            return m_new, l, acc

        init = (jnp.full((tq, 1), -1e30, F32), jnp.zeros((tq, 1), F32), jnp.zeros((tq, npe), F32))
        n_wide = qi // wide
        carry = lax.fori_loop(0, n_wide, lambda j, c: tile(j, c, wide * tq, False), init)
        carry = lax.fori_loop(n_wide * wide, qi, lambda ki, c: tile(ki, c, tq, False), carry)
        m, l, acc = tile(qi, carry, tq, True)
        o_ref[...] = acc / l
        lse_ref[...] = m + jnp.log(l)

    outs, side_outs = _host_call(
        body, (h, n),
        [pl.BlockSpec((tq, hp), lambda hh, qi: (qi, hh)),
         pl.BlockSpec((cfg.tp, hp), lambda hh, qi: (0, hh)),
         pl.BlockSpec((cfg.tp, npe), lambda hh, qi: (0, hh))],
        [pl.BlockSpec((tq, npe), lambda hh, qi: (qi, hh)),
         pl.BlockSpec((None, tq, 1), lambda hh, qi: (hh, qi, 0))],
        [jax.ShapeDtypeStruct((cfg.tp, h * npe), F32), jax.ShapeDtypeStruct((h, cfg.tp, 1), F32)],
        [], ("parallel", "arbitrary"), name, (q, k, v), side)
    return outs if side is None else (outs, side_outs)


def attn_bwd(cfg, q, k, v, do, o, lse, name, side=None):
    h, npe, tq, kw = cfg.mla_heads, cfg.nope, cfg.tq, cfg.tkw
    hp, n = 2 * npe, cfg.tp // tq
    sc = (cfg.nope + cfg.rope) ** -0.5

    def body(q_ref, k_ref, v_ref, do_ref, o_ref, lse_ref, dq_ref, dk_ref, dv_ref, dv_acc):
        dk_ref[...] = jnp.zeros_like(dk_ref)
        dv_acc[...] = jnp.zeros_like(dv_acc)

        def q_tile(qi, _):
            rq = _tile_rows(qi, tq)
            q, do, lse = q_ref[rq, :], do_ref[rq, :], lse_ref[rq, :]
            delta = jnp.sum(do.astype(F32) * o_ref[rq, :], axis=-1, keepdims=True)

            def tile(ki, dq, width, diagonal):
                rk = _tile_rows(ki, width)
                kk = k_ref[rk, :]
                p = jnp.exp(_attn_scores(cfg, q, kk, diagonal) - lse)
                dp = lax.dot_general(do, v_ref[rk, :], _NT, preferred_element_type=F32)
                ds = (p * (dp - delta) * sc).astype(BF16)
                dv_acc[rk, :] += lax.dot_general(p.astype(BF16), do, _TN, preferred_element_type=F32)
                dk_ref[rk, :] += lax.dot_general(ds, q, _TN, preferred_element_type=F32)
                return dq + lax.dot_general(ds, kk, _NN, preferred_element_type=F32)

            n_wide = qi // kw
            dq = lax.fori_loop(0, n_wide, lambda j, acc: tile(j, acc, kw * tq, False), jnp.zeros((tq, hp), F32))
            dq = lax.fori_loop(n_wide * kw, qi, lambda ki, acc: tile(ki, acc, tq, False), dq)
            dq_ref[rq, :] = tile(qi, dq, tq, True)
            return 0

        lax.fori_loop(0, n, q_tile, 0)
        dv_ref[...] = dv_acc[...].astype(BF16)

    wide = pl.BlockSpec((cfg.tp, hp), lambda hh: (0, hh))
    narrow = pl.BlockSpec((cfg.tp, npe), lambda hh: (0, hh))
    outs, side_outs = _host_call(
        body, (h,), [wide, wide, narrow, narrow, narrow, pl.BlockSpec((None, cfg.tp, 1), lambda hh: (hh, 0, 0))],
        [wide, wide, narrow],
        [jax.ShapeDtypeStruct((cfg.tp, h * hp), F32), jax.ShapeDtypeStruct((cfg.tp, h * hp), F32),
         jax.ShapeDtypeStruct((cfg.tp, h * npe), BF16)],
        [pltpu.VMEM((cfg.tp, npe), F32)], ("parallel",), name, (q, k, v, do, o, lse), side)
    return outs if side is None else (outs, side_outs)


def _ret_tables(cfg):
    c = cfg.chunk
    lg = jnp.log(1.0 - 2.0 ** (-5.0 - jnp.arange(cfg.ret_heads, dtype=F32)))[:, None, None]
    j = jnp.arange(c, dtype=F32)
    diff = j[:, None] - j[None, :]
    dm = jnp.where(diff >= 0, jnp.exp(jnp.maximum(diff, 0.0)[None] * lg), 0.0)
    lq = jnp.exp((j + 1.0)[None, :, None] * lg)
    lk = jnp.exp((c - 1.0 - j)[None, :, None] * lg)
    gc = jnp.exp(c * lg)
    return dm, lq, lk, gc


def _ret_rope_tables(cfg):
    dk = cfg.d // cfg.ret_heads
    inv = ROPE_BASE ** (-jnp.arange(0, dk, 2, dtype=F32) / dk)
    ang = jnp.arange(cfg.tp, dtype=F32)[:, None] * inv[None, :]
    return jnp.cos(ang), jnp.sin(ang)


def ret_prep(cfg, qkvg, cos, sin, name):
    d, h = cfg.d, cfg.ret_heads
    dk = d // h
    hd = dk // 2
    ksc = dk ** -0.5

    def body(q_ref, k_ref, v_ref, c_ref, s_ref, qo_ref, ko_ref, vo_ref):
        c, s = c_ref[...], s_ref[...]
        for i in range(h):
            for src, dst, f in ((q_ref, qo_ref, 1.0), (k_ref, ko_ref, ksc)):
                x1 = src[:, i * dk:i * dk + hd]
                x2 = src[:, i * dk + hd:(i + 1) * dk]
                dst[:, i * dk:i * dk + hd] = ((x1 * c - x2 * s) * f).astype(BF16)
                dst[:, i * dk + hd:(i + 1) * dk] = ((x2 * c + x1 * s) * f).astype(BF16)
        vo_ref[...] = v_ref[...].astype(BF16)

    return _rowwise(cfg, body, [(qkvg, d, 0), (qkvg, d, 1), (qkvg, d, 2), (cos, hd, 0), (sin, hd, 0)],
                    [(d, BF16), (d, BF16), (d, BF16)], name)


def ret_prep_bwd(cfg, dq, dk_, dv, dg, cos, sin, name):
    d, h = cfg.d, cfg.ret_heads
    dk = d // h
    hd = dk // 2
    ksc = dk ** -0.5

    def body(dq_ref, dk_ref, dv_ref, dg_ref, c_ref, s_ref, o_ref):
        c, s = c_ref[...], s_ref[...]
        for i in range(h):
            for src, off, f in ((dq_ref, 0, 1.0), (dk_ref, d, ksc)):
                y1 = src[:, i * dk:i * dk + hd]
                y2 = src[:, i * dk + hd:(i + 1) * dk]
                o_ref[:, off + i * dk:off + i * dk + hd] = ((y1 * c + y2 * s) * f).astype(BF16)
                o_ref[:, off + i * dk + hd:off + (i + 1) * dk] = ((y2 * c - y1 * s) * f).astype(BF16)
        o_ref[:, 2 * d:3 * d] = dv_ref[...]
        o_ref[:, 3 * d:4 * d] = dg_ref[...]

    return _rowwise(cfg, body, [(dq, d, 0), (dk_, d, 0), (dv, d, 0), (dg, d, 0), (cos, hd, 0), (sin, hd, 0)],
                    [(4 * d, BF16)], name)[0]


def _ret_specs(cfg, reverse):
    c, h = cfg.chunk, cfg.ret_heads
    dk = cfg.d // h
    n = cfg.tp // c
    tm = (lambda t: n - 1 - t) if reverse else (lambda t: t)
    blk = pl.BlockSpec((c, dk), lambda hh, t: (tm(t), hh))
    row = pl.BlockSpec((None, c, 1), lambda hh, t: (hh, tm(t), 0))
    tabs = [pl.BlockSpec((None, c, c), lambda hh, t: (hh, 0, 0)), pl.BlockSpec((None, c, 1), lambda hh, t: (hh, 0, 0)),
            pl.BlockSpec((None, c, 1), lambda hh, t: (hh, 0, 0)), pl.BlockSpec((None, 1, 1), lambda hh, t: (hh, 0, 0))]
    return blk, row, tabs, n, dk


def _ret_state_update(s_ref, k, v, lk, gc):
    kw = (k.astype(F32) * lk).astype(BF16)
    s_ref[...] = gc * s_ref[...] + lax.dot_general(kw, v, _TN, preferred_element_type=F32)


def ret_fwd(cfg, q, k, v, tabs, name):
    blk, row, tspecs, n, dk = _ret_specs(cfg, False)

    def body(q_ref, k_ref, v_ref, dm_ref, lq_ref, lk_ref, gc_ref, on_ref, rs_ref, s_ref):
        @pl.when(pl.program_id(1) == 0)
        def _():
            s_ref[...] = jnp.zeros_like(s_ref)

        qq, kk, vv = q_ref[...], k_ref[...], v_ref[...]
        a = lax.dot_general(qq, kk, _NT, preferred_element_type=F32) * dm_ref[...]
        o = lax.dot_general(a.astype(BF16), vv, _NN, preferred_element_type=F32)
        o = o + lax.dot_general(qq, s_ref[...].astype(BF16), _NN, preferred_element_type=F32) * lq_ref[...]
        _ret_state_update(s_ref, kk, vv, lk_ref[...], gc_ref[...])
        mu = jnp.mean(o, axis=-1, keepdims=True)
        oc = o - mu
        rstd = lax.rsqrt(jnp.mean(oc * oc, axis=-1, keepdims=True) + LN_EPS)
        on_ref[...] = oc * rstd
        rs_ref[...] = rstd

    return pl.pallas_call(body, grid=(cfg.ret_heads, n), in_specs=[blk, blk, blk] + tspecs, out_specs=[blk, row],
                          out_shape=[jax.ShapeDtypeStruct((cfg.tp, cfg.d), F32), jax.ShapeDtypeStruct((cfg.ret_heads, cfg.tp, 1), F32)],
                          scratch_shapes=[pltpu.VMEM((dk, dk), F32)], compiler_params=_cparams("parallel", "arbitrary"),
                          name=name)(q, k, v, *tabs)


def ret_bwd_q(cfg, q, k, v, d_on, on, rstd, tabs, name):
    blk, row, tspecs, n, dk = _ret_specs(cfg, False)

    def body(q_ref, k_ref, v_ref, don_ref, on_ref, rs_ref, dm_ref, lq_ref, lk_ref, gc_ref, do_ref, dq_ref, s_ref):
        @pl.when(pl.program_id(1) == 0)
        def _():
            s_ref[...] = jnp.zeros_like(s_ref)

        kk, vv = k_ref[...], v_ref[...]
        don, on = don_ref[...], on_ref[...]
        do = rs_ref[...] * (don - jnp.mean(don, axis=-1, keepdims=True) - on * jnp.mean(don * on, axis=-1, keepdims=True))
        dob = do.astype(BF16)
        do_ref[...] = dob
        ds = lax.dot_general(dob, vv, _NT, preferred_element_type=F32) * dm_ref[...]
        dq = lax.dot_general(ds.astype(BF16), kk, _NN, preferred_element_type=F32)
        dq_ref[...] = dq + lax.dot_general(dob, s_ref[...].astype(BF16), _NT, preferred_element_type=F32) * lq_ref[...]
        _ret_state_update(s_ref, kk, vv, lk_ref[...], gc_ref[...])

    return pl.pallas_call(body, grid=(cfg.ret_heads, n), in_specs=[blk, blk, blk, blk, blk, row] + tspecs, out_specs=[blk, blk],
                          out_shape=[jax.ShapeDtypeStruct((cfg.tp, cfg.d), BF16), jax.ShapeDtypeStruct((cfg.tp, cfg.d), F32)],
                          scratch_shapes=[pltpu.VMEM((dk, dk), F32)], compiler_params=_cparams("parallel", "arbitrary"),
                          name=name)(q, k, v, d_on, on, rstd, *tabs)


def ret_bwd_kv(cfg, q, k, v, do, tabs, name):
    blk, row, tspecs, n, dk = _ret_specs(cfg, True)

    def body(q_ref, k_ref, v_ref, do_ref, dm_ref, lq_ref, lk_ref, gc_ref, dk_ref, dv_ref, g_ref):
        @pl.when(pl.program_id(1) == 0)
        def _():
            g_ref[...] = jnp.zeros_like(g_ref)

        qq, kk, vv, dob = q_ref[...], k_ref[...], v_ref[...], do_ref[...]
        dm, lk = dm_ref[...], lk_ref[...]
        gb = g_ref[...].astype(BF16)
        a = lax.dot_general(qq, kk, _NT, preferred_element_type=F32) * dm
        ds = lax.dot_general(dob, vv, _NT, preferred_element_type=F32) * dm
        dkk = lax.dot_general(ds.astype(BF16), qq, _TN, preferred_element_type=F32)
        dk_ref[...] = dkk + lax.dot_general(vv, gb, _NT, preferred_element_type=F32) * lk
        kw = (kk.astype(F32) * lk).astype(BF16)
        dvv = lax.dot_general(a.astype(BF16), dob, _TN, preferred_element_type=F32)
        dv_ref[...] = (dvv + lax.dot_general(kw, gb, _NN, preferred_element_type=F32)).astype(BF16)
        qw = (qq.astype(F32) * lq_ref[...]).astype(BF16)
        g_ref[...] = gc_ref[...] * g_ref[...] + lax.dot_general(qw, dob, _TN, preferred_element_type=F32)

    return pl.pallas_call(body, grid=(cfg.ret_heads, n), in_specs=[blk, blk, blk, blk] + tspecs, out_specs=[blk, blk],
                          out_shape=[jax.ShapeDtypeStruct((cfg.tp, cfg.d), F32), jax.ShapeDtypeStruct((cfg.tp, cfg.d), BF16)],
                          scratch_shapes=[pltpu.VMEM((dk, dk), F32)], compiler_params=_cparams("parallel", "arbitrary"),
                          name=name)(q, k, v, do, *tabs)


def _flat2(a):
    return a.reshape(-1, a.shape[-1])


def _ew_call(body, ins, out_dtypes, name, side=None):
    r, c = ins[-1].shape[-2:]
    tr = _div_tile(r, max(16, (1 << 18) // c), 16)
    specs = []
    for a in ins:
        if a.ndim == 3:
            specs.append(pl.BlockSpec((a.shape[0], tr, c), lambda i: (0, i, 0)))
        else:
            specs.append(pl.BlockSpec((tr, c), lambda i: (i, 0)))
    outs, side_outs = _host_call(body, (r // tr,), specs, [pl.BlockSpec((tr, c), lambda i: (i, 0)) for _ in out_dtypes],
                                 [jax.ShapeDtypeStruct((r, c), dt) for dt in out_dtypes], [], ("parallel",), name, ins, side)
    return outs if side is None else (outs, side_outs)


def adamw(w, g, m, v, name, side=None):
    c1 = 1.0 - ADAM_B1 ** ADAM_STEP
    c2 = 1.0 - ADAM_B2 ** ADAM_STEP

    def body(w_ref, g_ref, m_ref, v_ref, d_ref, mo_ref, vo_ref):
        gg = g_ref[...]
        mn = ADAM_B1 * m_ref[...] + (1.0 - ADAM_B1) * gg
        vn = ADAM_B2 * v_ref[...] + (1.0 - ADAM_B2) * (gg * gg)
        d_ref[...] = -ADAM_LR * ((mn / c1) / (jnp.sqrt(vn / c2) + ADAM_EPS) + ADAM_WD * w_ref[...])
        mo_ref[...] = mn
        vo_ref[...] = vn

    got = _ew_call(body, [_flat2(w), _flat2(g), _flat2(m), _flat2(v)], [F32, F32, F32], name, side)
    outs, side_outs = (got, None) if side is None else got
    outs = [o.reshape(w.shape) for o in outs]
    return outs if side is None else (outs, side_outs)


def add_pair_bf16(g, theirs, core, name):
    c = g.shape[-1]
    g4, t3 = g.reshape(4, 2, -1, c), theirs.reshape(4, -1, c)
    rows = t3.shape[1]
    tr = _div_tile(rows, max(16, (1 << 18) // c), 16)

    def body(core_ref, g_ref, t_ref, o_ref):
        o_ref[...] = (g_ref[...] + t_ref[...]).astype(BF16)

    spec = pl.BlockSpec((None, tr, c), lambda q, i, core_ref: (q, i, 0))
    grid_spec = pltpu.PrefetchScalarGridSpec(
        num_scalar_prefetch=1, grid=(4, rows // tr),
        in_specs=[pl.BlockSpec((None, None, tr, c), lambda q, i, core_ref: (q, core_ref[0], i, 0)), spec], out_specs=spec)
    out = pl.pallas_call(body, grid_spec=grid_spec, out_shape=jax.ShapeDtypeStruct((4, rows, c), BF16),
                         compiler_params=_cparams("parallel", "parallel"), name=name)(core, g4, t3)
    return out.reshape(theirs.shape)


def sum_slots(parts, landed, chip_core, name):
    c = parts.shape[-1]
    p3, l3 = parts.reshape(4, -1, c), landed.reshape(3, -1, c)
    rows = p3.shape[1]
    tr = _div_tile(rows, max(16, (1 << 18) // c), 16)

    def body(cc_ref, p_ref, l_ref, o_ref):
        acc = p_ref[...].astype(F32)
        for s in range(3):
            acc = acc + l_ref[s].astype(F32)
        o_ref[...] = acc

    grid_spec = pltpu.PrefetchScalarGridSpec(
        num_scalar_prefetch=1, grid=(rows // tr,),
        in_specs=[pl.BlockSpec((None, tr, c), lambda i, cc: (cc[0], i, 0)), pl.BlockSpec((3, tr, c), lambda i, cc: (0, i, 0))],
        out_specs=pl.BlockSpec((None, tr, c), lambda i, cc: (cc[1], i, 0)))
    out = pl.pallas_call(body, grid_spec=grid_spec, out_shape=jax.ShapeDtypeStruct((2, rows, c), F32),
                         compiler_params=_cparams("parallel"), name=name)(chip_core, p3, l3)
    return out.reshape((2,) + parts.shape[2:])


def cast_place(w, chip, name):
    c = w.shape[-1]
    w2 = w.reshape(-1, c)
    rows = w2.shape[0]
    tr = _div_tile(rows, max(16, (1 << 18) // c), 16)

    def body(chip_ref, w_ref, o_ref):
        o_ref[...] = w_ref[...].astype(BF16)

    grid_spec = pltpu.PrefetchScalarGridSpec(
        num_scalar_prefetch=1, grid=(rows // tr,),
        in_specs=[pl.BlockSpec((tr, c), lambda i, chip_ref: (i, 0))],
        out_specs=pl.BlockSpec((None, tr, c), lambda i, chip_ref: (chip_ref[0], i, 0)))
    out = pl.pallas_call(body, grid_spec=grid_spec, out_shape=jax.ShapeDtypeStruct((4, rows, c), BF16),
                         compiler_params=_cparams("parallel"), name=name)(chip, w2)
    return out.reshape((4,) + w.shape)


def _coords():
    return lax.axis_index("x"), lax.axis_index("y"), lax.axis_index("c")


def _chip_peer(x, y, k):
    return (x ^ (k >> 1), y ^ (k & 1))


_ANY = pl.BlockSpec(memory_space=pl.ANY)
DMA_PIECE_BYTES = 2 << 20
DMA_MAX_PIECES = 32


def _piece_plan(shape, itemsize):
    want = max(1, min(DMA_MAX_PIECES, math.prod(shape) * itemsize // DMA_PIECE_BYTES))
    plan = []
    for ax, n in enumerate(shape[:-1]):
        if want <= 1:
            break
        rows_tiled = ax == len(shape) - 2
        k = max([1] + [c for c in range(2, min(n, want) + 1) if n % c == 0 and (not rows_tiled or (n // c) % 16 == 0)])
        if k > 1:
            plan.append((ax, k))
            want = -(-want // k)
    return plan


class _Copy:
    def __init__(self, src, dst, send_sem, recv_sem=None, device_id=None):
        self.src, self.dst, self.send_sem, self.recv_sem, self.device_id = src, dst, send_sem, recv_sem, device_id

    def _dma(self, src, dst):
        if self.device_id is None:
            return pltpu.make_async_copy(src, dst, self.send_sem)
        return pltpu.make_async_remote_copy(src_ref=src, dst_ref=dst, send_sem=self.send_sem, recv_sem=self.recv_sem,
                                            device_id=self.device_id, device_id_type=MESH)

    def start(self):
        shape = self.src.shape
        plan = _piece_plan(shape, jnp.dtype(self.src.dtype).itemsize)
        for pick in np.ndindex(*[k for _, k in plan]):
            idx = [slice(None)] * len(shape)
            for (ax, k), i in zip(plan, pick):
                step = shape[ax] // k
                idx[ax] = pl.ds(i * step, step)
            idx = tuple(idx)
            self._dma(self.src.at[idx], self.dst.at[idx]).start()

    def wait(self):
        self._dma(self.src, self.dst).wait()

    def wait_send(self):
        self._dma(self.src, self.dst).wait_send()

    def wait_recv(self):
        self._dma(self.src, self.dst).wait_recv()


def allreduce_small(x, name):
    r, _, w = x.shape

    def body(x_ref, o_ref, mine_ref, gat_ref, send_sems, recv_sems):
        mx, my, mc = _coords()
        me = 4 * mx + 2 * my + mc
        mine_ref[...] = jnp.sum(x_ref[...], axis=1)
        gat_ref[me] = mine_ref[...]
        copies = []
        for k in range(1, 8):
            peer = (mx ^ (k >> 2), my ^ ((k >> 1) & 1), mc ^ (k & 1))
            cp = pltpu.make_async_remote_copy(src_ref=mine_ref, dst_ref=gat_ref.at[me], send_sem=send_sems.at[k - 1],
                                              recv_sem=recv_sems.at[k - 1], device_id=peer, device_id_type=MESH)
            cp.start()
            copies.append(cp)
        for k in range(1, 8):
            pltpu.make_async_remote_copy(src_ref=mine_ref, dst_ref=gat_ref.at[me ^ k], send_sem=send_sems.at[k - 1],
                                         recv_sem=recv_sems.at[k - 1], device_id=(mx, my, mc), device_id_type=MESH).wait_recv()
        for cp in copies:
            cp.wait_send()
        acc = gat_ref[0]
        for s in range(1, 8):
            acc = acc + gat_ref[s]
        o_ref[...] = acc

    return pl.pallas_call(body, out_shape=jax.ShapeDtypeStruct((r, w), F32),
                          in_specs=[pl.BlockSpec(memory_space=pltpu.VMEM)], out_specs=pl.BlockSpec(memory_space=pltpu.VMEM),
                          scratch_shapes=[pltpu.VMEM((r, w), F32), pltpu.VMEM((8, r, w), F32),
                                          pltpu.SemaphoreType.DMA((7,)), pltpu.SemaphoreType.DMA((7,))],
                          compiler_params=pltpu.CompilerParams(vmem_limit_bytes=VMEM_LIMIT_V7X), name=name)(x)


def allgather_chips(shards, name):
    n = len(shards)

    def body(*refs):
        outs = refs[n:2 * n]
        s1_send, s1_recv, s2_send, s2_recv = refs[2 * n:]
        mx, my, mc = _coords()
        p = 2 * mx + my
        sib = (mx, my, 1 - mc)
        sends = []
        for a in range(n):
            h0 = outs[a].shape[1] // 2
            mine = outs[a].at[p, pl.ds(mc * h0, h0)]
            for k in (1, 2, 3):
                qx, qy = _chip_peer(mx, my, k)
                cp = _Copy(mine, mine, s1_send.at[a, k - 1], s1_recv.at[a, k - 1], (qx, qy, mc))
                cp.start()
                sends.append(cp)
        for a in range(n):
            h0 = outs[a].shape[1] // 2
            half = pl.ds(mc * h0, h0)
            for k in (1, 2, 3):
                landed = outs[a].at[p ^ k, half]
                _Copy(landed, landed, s1_send.at[a, k - 1], s1_recv.at[a, k - 1], sib).wait_recv()
                fw = _Copy(landed, landed, s2_send.at[a, k - 1], s2_recv.at[a, k - 1], sib)
                fw.start()
                sends.append(fw)
        for a in range(n):
            h0 = outs[a].shape[1] // 2
            other = pl.ds((1 - mc) * h0, h0)
            for k in (1, 2, 3):
                theirs = outs[a].at[p ^ k, other]
                _Copy(theirs, theirs, s2_send.at[a, k - 1], s2_recv.at[a, k - 1], sib).wait_recv()
        for cp in sends:
            cp.wait_send()

    return pl.pallas_call(body, out_shape=[jax.ShapeDtypeStruct(s.shape, s.dtype) for s in shards],
                          in_specs=[_ANY] * n, out_specs=[_ANY] * n, input_output_aliases={a: a for a in range(n)},
                          scratch_shapes=[pltpu.SemaphoreType.DMA((n, 3)), pltpu.SemaphoreType.DMA((n, 3)),
                                          pltpu.SemaphoreType.DMA((n, 3)), pltpu.SemaphoreType.DMA((n, 3))],
                          name=name)(*shards)


def ag_chips_side(bufs):
    n = len(bufs)

    def each(inplace, sems, act):
        s_send, s_recv = sems
        mx, my, mc = _coords()
        p = 2 * mx + my
        for a in range(n):
            h0 = inplace[a].shape[1] // 2
            half = pl.ds(mc * h0, h0)
            mine = inplace[a].at[p, half]
            for k in (1, 2, 3):
                qx, qy = _chip_peer(mx, my, k)
                act(_Copy(mine, mine, s_send.at[a, k - 1], s_recv.at[a, k - 1], (qx, qy, mc)),
                    _Copy(inplace[a].at[p ^ k, half], inplace[a].at[p ^ k, half], s_send.at[a, k - 1], s_recv.at[a, k - 1], (qx, qy, mc)))

    def start(ins, inplace, outs, sems):
        each(inplace, sems, lambda send, landed: send.start())

    def finish(ins, inplace, outs, sems):
        each(inplace, sems, lambda send, landed: (landed.wait_recv(), send.wait_send()))

    return Side([], list(bufs), [], [pltpu.SemaphoreType.DMA((n, 3)), pltpu.SemaphoreType.DMA((n, 3))], start, finish)


def ag_forward_side(bufs):
    n = len(bufs)

    def each(inplace, sems, act):
        s_send, s_recv = sems
        mx, my, mc = _coords()
        p = 2 * mx + my
        sib = (mx, my, 1 - mc)
        for a in range(n):
            h0 = inplace[a].shape[1] // 2
            for k in (1, 2, 3):
                landed = inplace[a].at[p ^ k, pl.ds(mc * h0, h0)]
                theirs = inplace[a].at[p ^ k, pl.ds((1 - mc) * h0, h0)]
                act(_Copy(landed, landed, s_send.at[a, k - 1], s_recv.at[a, k - 1], sib),
                    _Copy(theirs, theirs, s_send.at[a, k - 1], s_recv.at[a, k - 1], sib))

    def start(ins, inplace, outs, sems):
        each(inplace, sems, lambda send, theirs: send.start())

    def finish(ins, inplace, outs, sems):
        each(inplace, sems, lambda send, theirs: (theirs.wait_recv(), send.wait_send()))

    return Side([], list(bufs), [], [pltpu.SemaphoreType.DMA((n, 3)), pltpu.SemaphoreType.DMA((n, 3))], start, finish)


def rs_chips_side(parts):
    n = len(parts)

    def copies(ins, outs, sems):
        send_sems, recv_sems = sems
        mx, my, mc = _coords()
        p = 2 * mx + my
        return [_Copy(ins[a].at[p ^ k], outs[a].at[k - 1], send_sems.at[a, k - 1], recv_sems.at[a, k - 1],
                      (*_chip_peer(mx, my, k), mc)) for a in range(n) for k in (1, 2, 3)]

    def start(ins, inplace, outs, sems):
        for cp in copies(ins, outs, sems):
            cp.start()

    def finish(ins, inplace, outs, sems):
        for cp in copies(ins, outs, sems):
            cp.wait()

    return Side(list(parts), [], [jax.ShapeDtypeStruct((3,) + g.shape[1:], g.dtype) for g in parts],
                [pltpu.SemaphoreType.DMA((n, 3)), pltpu.SemaphoreType.DMA((n, 3))], start, finish)


def rs_pair_side(grads):
    n = len(grads)

    def copies(ins, outs, sems):
        send_sems, recv_sems = sems
        mx, my, mc = _coords()
        return [_Copy(ins[a].at[:, pl.ds(1 - mc, 1)], outs[a], send_sems.at[a], recv_sems.at[a], (mx, my, 1 - mc)) for a in range(n)]

    def start(ins, inplace, outs, sems):
        for cp in copies(ins, outs, sems):
            cp.start()

    def finish(ins, inplace, outs, sems):
        for cp in copies(ins, outs, sems):
            cp.wait()

    return Side(list(grads), [], [jax.ShapeDtypeStruct((4, 1) + g.shape[2:], g.dtype) for g in grads],
                [pltpu.SemaphoreType.DMA((n,)), pltpu.SemaphoreType.DMA((n,))], start, finish)


def exchange_halves(grads, name):
    n = len(grads)
    side = rs_pair_side(grads)

    def body(*refs):
        ins, outs, sems = refs[:n], refs[n:2 * n], refs[2 * n:]
        side.start(ins, [], outs, sems)
        side.finish(ins, [], outs, sems)

    return pl.pallas_call(body, out_shape=side.outs, in_specs=[_ANY] * n, out_specs=[_ANY] * n, scratch_shapes=side.sems,
                          name=name)(*grads)


def scatter_chips(parts, name):
    n = len(parts)
    side = rs_chips_side(parts)

    def body(*refs):
        ins, outs, sems = refs[:n], refs[n:2 * n], refs[2 * n:]
        side.start(ins, [], outs, sems)
        side.finish(ins, [], outs, sems)

    return pl.pallas_call(body, out_shape=side.outs, in_specs=[_ANY] * n, out_specs=[_ANY] * n, scratch_shapes=side.sems,
                          name=name)(*parts)


def rs_join_side(halves):
    n = len(halves)

    def each(inplace, sems, act):
        send_sems, recv_sems = sems
        mx, my, mc = _coords()
        for a in range(n):
            mine, theirs = inplace[a].at[pl.ds(mc, 1)], inplace[a].at[pl.ds(1 - mc, 1)]
            act(_Copy(mine, mine, send_sems.at[a], recv_sems.at[a], (mx, my, 1 - mc)),
                _Copy(theirs, theirs, send_sems.at[a], recv_sems.at[a], (mx, my, 1 - mc)))

    def start(ins, inplace, outs, sems):
        each(inplace, sems, lambda send, theirs: send.start())

    def finish(ins, inplace, outs, sems):
        each(inplace, sems, lambda send, theirs: (send.wait_send(), theirs.wait_recv()))

    return Side([], list(halves), [], [pltpu.SemaphoreType.DMA((n,)), pltpu.SemaphoreType.DMA((n,))], start, finish)


def join_halves(halves, name):
    n = len(halves)
    side = rs_join_side(halves)

    def body(*refs):
        outs, sems = refs[n:2 * n], refs[2 * n:]
        side.start([], outs, [], sems)
        side.finish([], outs, [], sems)

    return pl.pallas_call(body, out_shape=[jax.ShapeDtypeStruct(g.shape, g.dtype) for g in halves],
                          in_specs=[_ANY] * n, out_specs=[_ANY] * n, input_output_aliases={a: a for a in range(n)},
                          scratch_shapes=side.sems, name=name)(*halves)


def rs_pair_sums(grads, tag, call=None):
    out, theirs = (None, exchange_halves(grads, f"rs_pair_{tag}")) if call is None else call(rs_pair_side(grads))
    core = lax.axis_index("c").astype(jnp.int32).reshape(1)
    return out, [add_pair_bf16(g, t, core, f"rs_add_{tag}_{i}") for i, (g, t) in enumerate(zip(grads, theirs))]


def rs_finish(parts, landed, tag, call=None):
    chip_core = jnp.stack([2 * lax.axis_index("x") + lax.axis_index("y"), lax.axis_index("c")]).astype(jnp.int32)
    halves = [sum_slots(pt, ld, chip_core, f"rs_sum_{tag}_{i}") for i, (pt, ld) in enumerate(zip(parts, landed))]
    return (None, join_halves(halves, f"rs_join_{tag}")) if call is None else call(rs_join_side(halves))


W_NAMES = ['meta_tokens', 'l0_w_in', 'l0_conv_w', 'l0_conv_b', 'l0_w_a', 'l0_b_a', 'l0_w_x', 'l0_b_x', 'l0_lam', 'l0_w_out',
           'l0_ln_g', 'l0_ln_b', 'l1_w_in', 'l1_w_grp', 'l1_scale', 'l1_w_out', 'l1_ln_g', 'l1_ln_b', 'l2_w_in', 'l2_q_norm',
           'l2_w_uq', 'l2_kv_norm', 'l2_w_ukv', 'l2_w_out', 'l2_ln_g', 'l2_ln_b', 'l3_w_in', 'l3_w_out', 'l3_ln_g', 'l3_ln_b']
BIG = {0: ['l0_w_in', 'l0_w_a', 'l0_w_x', 'l0_w_out'], 1: ['l1_w_in', 'l1_w_grp', 'l1_w_out'],
       2: ['l2_w_in', 'l2_w_uq', 'l2_w_ukv', 'l2_w_out'], 3: ['l3_w_in', 'l3_w_out']}
SMALL_SHARDED = ['meta_tokens', 'l0_conv_w']
SMALL_REPL = ['l0_conv_b', 'l0_b_a', 'l0_b_x', 'l0_lam', 'l0_ln_g', 'l0_ln_b', 'l1_scale', 'l1_ln_g', 'l1_ln_b',
              'l2_q_norm', 'l2_kv_norm', 'l2_ln_g', 'l2_ln_b', 'l3_ln_g', 'l3_ln_b']


def _rows8(rows, width):
    out = []
    for r in rows:
        r = r.reshape(-1, r.shape[-1])
        out.append(jnp.pad(r, ((0, 8 - r.shape[0]), (0, width - r.shape[1]))))
    return jnp.stack(out)


def _unblock(g4, axis):
    return jnp.concatenate([g4[i] for i in range(4)], axis=axis)


def _block(full, axis):
    return jnp.stack(jnp.split(full, 4, axis=axis))


def _step(cfg, a):
    d, tp, nm, seq = cfg.d, cfg.tp, cfg.n_meta, cfg.seq
    alpha = (2.0 * cfg.depth) ** 0.25
    mx, my, mc = _coords()
    p = 2 * mx + my
    dq4 = d // 4
    hm, npe, ql, kl = cfg.mla_heads, cfg.nope, cfg.q_lora, cfg.kv_lora
    hp = 2 * npe
    qk = npe + cfg.rope
    vec = lambda name: a[name].reshape(1, -1)

    sm = jnp.concatenate([a['meta_tokens'], a['l0_conv_w'].reshape(CONV_W, dq4)], axis=0)
    placed = lax.dynamic_update_slice(jnp.zeros((nm + CONV_W, d), F32), sm * (mc == 0).astype(F32), (0, p * dq4))
    gathered = allreduce_small(_rows8([placed[i:i + 1] for i in range(nm + CONV_W)], d), "gather_small")
    meta_full, conv_w4 = gathered[:nm], gathered[nm:]

    chip = p.astype(jnp.int32).reshape(1)
    wg = {n: cast_place(a[n], chip, f"place_{n}") for names in BIG.values() for n in names}
    wg.update(zip(BIG[0], allgather_chips([wg[n] for n in BIG[0]], "ag_l0")))

    def hosted(call, names):
        out, got = call(ag_chips_side([wg[n] for n in names]))
        wg.update(zip(names, got))
        return out

    def forwarded(call, layer):
        out, got = call(ag_forward_side([wg[n] for n in BIG[layer]]))
        wg.update(zip(BIG[layer], got))
        return out

    heads_first = lambda w: jnp.moveaxis(w, 0, 1).reshape(w.shape[1], w.shape[0] * w.shape[2], w.shape[3])
    w_out = lambda i: wg[f'l{i}_w_out'].reshape(1, d, d)

    zpad = jnp.zeros((tp - nm - seq, d), F32)
    h0 = jnp.concatenate([meta_full, a['x'][0], zpad], axis=0)
    h0_bf = h0.astype(BF16)

    w0_in, w0_a, w0_x = wg['l0_w_in'], heads_first(wg['l0_w_a']), heads_first(wg['l0_w_x'])
    ug0 = hosted(lambda side: mm_nn(cfg, h0_bf, w0_in, "l0_in", side=side), ['l1_w_in'])
    u0, u0_bf = conv_fwd(cfg, ug0, conv_w4, vec('l0_conv_b'), "l0_conv")
    pa0 = bd_nn(cfg, u0_bf, w0_a, "l0_gate_a")
    px0 = bd_nn(cfg, u0_bf, w0_x, "l0_gate_x")
    hs0, hprev0, y0 = lru_fwd(cfg, u0, pa0, px0, ug0, vec('l0_b_a'), vec('l0_b_x'), vec('l0_lam'), "l0_lru")
    o0 = hosted(lambda side: mm_nn(cfg, y0, w_out(0), "l0_out", side=side), ['l1_w_grp', 'l1_w_out'])
    h1, h1_bf, xh0, rs0 = forwarded(lambda side: ln_fwd(cfg, h0, o0, vec('l0_ln_g'), vec('l0_ln_b'), alpha, "l0_ln", side=side), 1)

    w1_in, w1_grp = wg['l1_w_in'], heads_first(wg['l1_w_grp'])
    ug1 = hosted(lambda side: mm_nn(cfg, h1_bf, w1_in, "l1_in", side=side), ['l2_w_in'])
    p1 = pool_fwd(cfg, ug1, "l1_pool")
    mm1 = bd_nn(cfg, p1, w1_grp, "l1_grp")
    y1 = gate_fwd(cfg, mm1, ug1, 1, vec('l1_scale'), "l1_gate")
    o1 = hosted(lambda side: mm_nn(cfg, y1, w_out(1), "l1_out", side=side), ['l2_w_uq', 'l2_w_ukv', 'l2_w_out'])
    h2, h2_bf, xh1, rs1 = forwarded(lambda side: ln_fwd(cfg, h1, o1, vec('l1_ln_g'), vec('l1_ln_b'), alpha, "l1_ln", side=side), 2)

    w2_in = jnp.pad(_unblock(wg['l2_w_in'], 1), ((0, 0), (0, cfg.cw - (ql + kl + cfg.rope))))[None]
    w2_uq = jnp.pad(_unblock(wg['l2_w_uq'], 1).reshape(ql, hm, qk), ((0, 0), (0, 0), (0, hp - qk))).reshape(1, ql, hm * hp)
    w2_ukv = _unblock(wg['l2_w_ukv'], 1).reshape(kl, hm, 2, npe).transpose(0, 2, 1, 3).reshape(1, kl, 2 * hm * npe)
    proj = mm_nn(cfg, h2_bf, w2_in, "l2_in")
    cqn, ckvn = mla_norm(cfg, proj, vec('l2_q_norm'), vec('l2_kv_norm'), "l2_norm")
    q_raw = mm_nn(cfg, cqn, w2_uq, "l2_uq")
    kv_raw = mm_nn(cfg, ckvn, w2_ukv, "l2_ukv")
    tabs = _mla_tables(cfg)
    qf, kf, vb = mla_prep(cfg, q_raw, kv_raw, proj, tabs, "l2_prep")
    o_att, lse = hosted(lambda side: attn_fwd(cfg, qf, kf, vb, "l2_attn", side=side), ['l3_w_in', 'l3_w_out'])
    y2 = gate_fwd(cfg, o_att, proj, 0, None, "l2_gate")
    w2_out = w_out(2)
    o2 = forwarded(lambda side: mm_nn(cfg, y2, w2_out, "l2_out", side=side), 3)
    h3, h3_bf, xh2, rs2 = ln_fwd(cfg, h2, o2, vec('l2_ln_g'), vec('l2_ln_b'), alpha, "l2_ln")

    w3_in = wg['l3_w_in']
    qkvg = mm_nn(cfg, h3_bf, w3_in, "l3_in")
    cos, sin = _ret_rope_tables(cfg)
    qr, kr, vr = ret_prep(cfg, qkvg, cos, sin, "l3_prep")
    rt = _ret_tables(cfg)
    on, rsr = ret_fwd(cfg, qr, kr, vr, rt, "l3_ret")
    y3 = gate_fwd(cfg, on, qkvg, 3, None, "l3_gate")
    o3 = mm_nn(cfg, y3, w_out(3), "l3_out")
    h4, _, xh3, rs3 = ln_fwd(cfg, h3, o3, vec('l3_ln_g'), vec('l3_ln_b'), alpha, "l3_ln")

    tgt = jnp.concatenate([jnp.zeros((nm, d), F32), a['loss_target'][0], zpad], axis=0)
    dy4, lacc = loss_grad(cfg, h4, tgt, "loss")
    loss = lax.psum(0.5 * jnp.sum(lacc) / d, ("x", "y", "c"))

    grads, small, parts, landed = {}, {}, {}, {}

    def rs_begin(layer, full, call=None):
        out, got = rs_pair_sums([g.reshape((4, 2, g.shape[1] // 2) + g.shape[2:]) for g in full], f"l{layer}", call)
        parts.update(zip(BIG[layer], got))
        return out

    def carried(call, names):
        out, got = call(rs_chips_side([parts[n] for n in names]))
        landed.update(zip(names, got))
        return out

    def rs_end(layer, call=None):
        names = BIG[layer]
        out, got = rs_finish([parts[n] for n in names], [landed[n] for n in names], f"l{layer}", call)
        for n, g in zip(names, got):
            grads[n] = g.reshape(a[n].shape)
        return out

    dz3, dz3_bf, small['l3_ln_g'], small['l3_ln_b'] = ln_bwd(cfg, None, dy4, xh3, rs3, vec('l3_ln_g'), alpha, "l3_ln_b")
    g_out = mm_tn(cfg, y3, dz3_bf, 1, "l3_dw_out")
    dyb = mm_nt(cfg, dz3_bf, w_out(3), "l3_dy")
    d_on, dg = gate_bwd(cfg, dyb, on, qkvg, 3, None, F32, "l3_gate_b")
    do_bf, dqr = ret_bwd_q(cfg, qr, kr, vr, d_on, on, rsr, rt, "l3_ret_bq")
    dkr, dvr = ret_bwd_kv(cfg, qr, kr, vr, do_bf, rt, "l3_ret_bkv")
    d_in = ret_prep_bwd(cfg, dqr, dkr, dvr, dg, cos, sin, "l3_prep_b")
    g_in = mm_tn(cfg, h3_bf, d_in, 4, "l3_dw_in")
    dh = mm_nt(cfg, d_in, w3_in, "l3_dh")
    g3 = [g_in, g_out.reshape(4, dq4, d)]

    dz2, dz2_bf, small['l2_ln_g'], small['l2_ln_b'] = ln_bwd(cfg, dz3, dh, xh2, rs2, vec('l2_ln_g'), alpha, "l2_ln_b")
    g_out = rs_begin(3, g3, lambda side: mm_tn(cfg, y2, dz2_bf, 1, "l2_dw_out", side=side))
    dyb = mm_nt(cfg, dz2_bf, w_out(2), "l2_dy")
    do_bf, dg = gate_bwd(cfg, dyb, o_att, proj, 0, None, BF16, "l2_gate_b")
    dq_full, dk_full, dv_bf = carried(lambda side: attn_bwd(cfg, qf, kf, vb, do_bf, o_att, lse, "l2_attn_b", side=side), BIG[3])
    dq_raw, dk_nope, d_kr = rs_end(3, lambda side: mla_prep_bwd(cfg, dq_full, dk_full, tabs, "l2_prep_b", side=side))
    dkv_raw = jnp.concatenate([dk_nope, dv_bf], axis=1)
    g_uq = mm_tn(cfg, cqn, dq_raw, 1, "l2_dw_uq")
    d_cqn = mm_nt(cfg, dq_raw, w2_uq, "l2_dcq")
    g_ukv = mm_tn(cfg, ckvn, dkv_raw, 1, "l2_dw_ukv")
    d_ckvn = mm_nt(cfg, dkv_raw, w2_ukv, "l2_dckv")
    d_c, nacc = mla_norm_bwd(cfg, proj, d_cqn, d_ckvn, d_kr, vec('l2_q_norm'), vec('l2_kv_norm'), "l2_norm_b")
    small['l2_q_norm'], small['l2_kv_norm'] = nacc[:, :ql], nacc[:, ql:ql + kl]
    d_in = jnp.concatenate([dg, d_c], axis=1)
    g_in = mm_tn(cfg, h2_bf, d_in, 1, "l2_dw_in")
    dh = mm_nt(cfg, d_in, w2_in, "l2_dh")
    g_in = _block(g_in[0][:, :d + ql + kl + cfg.rope], 1)
    g_uq = _block(g_uq.reshape(ql, hm, hp)[:, :, :qk].reshape(ql, hm * qk), 1)
    g_ukv = _block(g_ukv.reshape(kl, 2, hm, npe).transpose(0, 2, 1, 3).reshape(kl, 2 * hm * npe), 1)
    g2 = [g_in, g_uq, g_ukv, g_out.reshape(4, dq4, d)]

    dz1, dz1_bf, small['l1_ln_g'], small['l1_ln_b'] = ln_bwd(cfg, dz2, dh, xh1, rs1, vec('l1_ln_g'), alpha, "l1_ln_b")
    g_out = rs_begin(2, g2, lambda side: mm_tn(cfg, y1, dz1_bf, 1, "l1_dw_out", side=side))
    dyb = mm_nt(cfg, dz1_bf, w_out(1), "l1_dy")
    d_mm1, dg, small['l1_scale'] = gate_bwd(cfg, dyb, mm1, ug1, 1, vec('l1_scale'), BF16, "l1_gate_b")
    g_grp = bd_tn(cfg, p1, d_mm1, len(cfg.pool_windows), "l1_dw_grp")
    dp = bd_nt(cfg, d_mm1, w1_grp, "l1_dp")
    du = pool_bwd(cfg, dp, "l1_pool_b")
    d_in = jnp.concatenate([du, dg], axis=1)
    g_in = carried(lambda side: mm_tn(cfg, h1_bf, d_in, 4, "l1_dw_in", side=side), ['l2_w_in', 'l2_w_uq'])
    dh = carried(lambda side: mm_nt(cfg, d_in, w1_in, "l1_dh", side=side), ['l2_w_ukv', 'l2_w_out'])
    blocks_first = lambda g: jnp.moveaxis(g.reshape(g.shape[0], 4, g.shape[1] // 4, g.shape[2]), 1, 0)
    g1 = [g_in, blocks_first(g_grp), g_out.reshape(4, dq4, d)]

    dh1 = dh
    dz0, dz0_bf, small['l0_ln_g'], small['l0_ln_b'] = rs_end(
        2, lambda side: ln_bwd(cfg, dz1, dh1, xh0, rs0, vec('l0_ln_g'), alpha, "l0_ln_b", side=side))
    g_out = rs_begin(1, g1, lambda side: mm_tn(cfg, y0, dz0_bf, 1, "l0_dw_out", side=side))
    dyb = mm_nt(cfg, dz0_bf, w_out(0), "l0_dy")
    dg, dpa, dpx, du_dir, lacc0 = lru_bwd(cfg, dyb, ug0, hs0, hprev0, u0, pa0, px0, vec('l0_b_a'), vec('l0_b_x'), vec('l0_lam'), "l0_lru_b")
    g_a = bd_tn(cfg, u0_bf, dpa, cfg.lru_heads, "l0_dw_a")
    g_x = bd_tn(cfg, u0_bf, dpx, cfg.lru_heads, "l0_dw_x")
    du_a = bd_nt(cfg, dpa, w0_a, "l0_du_a")
    du_x = bd_nt(cfg, dpx, w0_x, "l0_du_x")
    du_pre, cacc = conv_bwd(cfg, du_dir, du_a, du_x, ug0, conv_w4, "l0_conv_b")
    d_in = jnp.concatenate([du_pre, dg], axis=1)
    g_in = carried(lambda side: mm_tn(cfg, h0_bf, d_in, 4, "l0_dw_in", side=side), ['l1_w_in'])
    dh = carried(lambda side: mm_nt(cfg, d_in, w0_in, "l0_dh", side=side), ['l1_w_grp', 'l1_w_out'])
    dh0 = rs_end(1, lambda side: axpy(cfg, dz0, dh, alpha, "dh0", side=side))[0]
    rs_begin(0, [g_in, blocks_first(g_a), blocks_first(g_x), g_out.reshape(4, dq4, d)])
    grad_x = dh0[nm:nm + seq][None]
    small['l0_lam'], small['l0_b_a'], small['l0_b_x'] = lacc0[0:1], lacc0[1:2], lacc0[2:3]
    small['l0_conv_b'] = cacc[4:5]

    rows = [dh0[i:i + 1] for i in range(nm)] + [cacc[k:k + 1] for k in range(CONV_W)] + [small[n] for n in SMALL_REPL]
    red = allreduce_small(_rows8(rows, d), "reduce_small")
    sh = lax.dynamic_slice(red[:nm + CONV_W], (0, p * dq4), (nm + CONV_W, dq4))
    grads['meta_tokens'] = sh[:nm]
    grads['l0_conv_w'] = sh[nm:].reshape(a['l0_conv_w'].shape)
    for i, n in enumerate(SMALL_REPL):
        grads[n] = red[nm + CONV_W + i, :a[n].shape[0]]

    delta, new_m, new_v = {}, {}, {}
    riders = {'l3_w_in': ['l0_w_in'], 'l1_w_in': ['l0_w_out'], 'l2_w_in': ['l0_w_a', 'l0_w_x']}
    for n in BIG[3] + BIG[2] + BIG[1]:
        update = lambda side=None, n=n: adamw(a[n], grads[n], a['m_' + n], a['v_' + n], f"adamw_{n}", side=side)
        delta[n], new_m[n], new_v[n] = carried(update, riders[n]) if n in riders else update()
    rs_end(0)
    for n in BIG[0]:
        delta[n], new_m[n], new_v[n] = adamw(a[n], grads[n], a['m_' + n], a['v_' + n], f"adamw_{n}")
    pack_s = lambda pre: jnp.concatenate([a[pre + 'meta_tokens'], a[pre + 'l0_conv_w'].reshape(CONV_W, dq4)], axis=0)
    ds_, ms_, vs_ = adamw(pack_s(''), sh, pack_s('m_'), pack_s('v_'), "adamw_small_sharded")
    for out, packed in ((delta, ds_), (new_m, ms_), (new_v, vs_)):
        out['meta_tokens'] = packed[:nm]
        out['l0_conv_w'] = packed[nm:].reshape(a['l0_conv_w'].shape)
    pack_r = lambda pre: jnp.stack([jnp.pad(a[pre + n], (0, d - a[n].shape[0])) for n in SMALL_REPL])
    dr_, mr_, vr_ = adamw(pack_r(''), red[nm + CONV_W:nm + CONV_W + len(SMALL_REPL)], pack_r('m_'), pack_r('v_'), "adamw_small_repl")
    for out, packed in ((delta, dr_), (new_m, mr_), (new_v, vr_)):
        for i, n in enumerate(SMALL_REPL):
            out[n] = packed[i, :a[n].shape[0]]

    return (loss, grad_x, *[grads[n] for n in W_NAMES], *[delta[n] for n in W_NAMES],
            *[new_m[n] for n in W_NAMES], *[new_v[n] for n in W_NAMES])


def kernel(x, meta_tokens, l0_w_in, l0_conv_w, l0_conv_b, l0_w_a, l0_b_a, l0_w_x, l0_b_x, l0_lam, l0_w_out, l0_ln_g, l0_ln_b, l1_w_in, l1_w_grp, l1_scale, l1_w_out, l1_ln_g, l1_ln_b, l2_w_in, l2_q_norm, l2_w_uq, l2_kv_norm, l2_w_ukv, l2_w_out, l2_ln_g, l2_ln_b, l3_w_in, l3_w_out, l3_ln_g, l3_ln_b, loss_target, m_meta_tokens, m_l0_w_in, m_l0_conv_w, m_l0_conv_b, m_l0_w_a, m_l0_b_a, m_l0_w_x, m_l0_b_x, m_l0_lam, m_l0_w_out, m_l0_ln_g, m_l0_ln_b, m_l1_w_in, m_l1_w_grp, m_l1_scale, m_l1_w_out, m_l1_ln_g, m_l1_ln_b, m_l2_w_in, m_l2_q_norm, m_l2_w_uq, m_l2_kv_norm, m_l2_w_ukv, m_l2_w_out, m_l2_ln_g, m_l2_ln_b, m_l3_w_in, m_l3_w_out, m_l3_ln_g, m_l3_ln_b, v_meta_tokens, v_l0_w_in, v_l0_conv_w, v_l0_conv_b, v_l0_w_a, v_l0_b_a, v_l0_w_x, v_l0_b_x, v_l0_lam, v_l0_w_out, v_l0_ln_g, v_l0_ln_b, v_l1_w_in, v_l1_w_grp, v_l1_scale, v_l1_w_out, v_l1_ln_g, v_l1_ln_b, v_l2_w_in, v_l2_q_norm, v_l2_w_uq, v_l2_kv_norm, v_l2_w_ukv, v_l2_w_out, v_l2_ln_g, v_l2_ln_b, v_l3_w_in, v_l3_w_out, v_l3_ln_g, v_l3_ln_b):
    return _step(REAL, dict(locals()))
```

```python
import functools
import math
from typing import NamedTuple

import jax
import jax.numpy as jnp
import numpy as np
from jax import lax
from jax.experimental import pallas as pl
from jax.experimental.pallas import tpu as pltpu

F32 = jnp.float32
BF16 = jnp.bfloat16

LN_EPS = 1e-5
RMS_EPS = 1e-6
ROPE_BASE = 10000.0
LRU_C = 8.0
CONV_W = 4
HIST = 16
ADAM_LR, ADAM_B1, ADAM_B2, ADAM_EPS, ADAM_WD, ADAM_STEP = 0.001, 0.9, 0.999, 1e-08, 0.01, 10
VMEM_LIMIT_V7X = 56 * 1024 * 1024
MESH = pl.DeviceIdType.MESH


class Cfg(NamedTuple):
    d: int
    seq: int
    n_meta: int
    tp: int
    depth: int
    lru_heads: int
    pool_windows: tuple
    mla_heads: int
    nope: int
    rope: int
    q_lora: int
    kv_lora: int
    cw: int
    ret_heads: int
    chunk: int
    tq: int
    tkw: int
    tr: int
    tt: int
    tc: int
    tm: int
    tn: int
    tkc: int
    tkm: int
    la: int


REAL = Cfg(d=4096, seq=4096, n_meta=16, tp=4224, depth=4, lru_heads=16, pool_windows=(2, 4, 8, 16),
           mla_heads=32, nope=128, rope=64, q_lora=1024, kv_lora=512, cw=2048, ret_heads=16, chunk=384,
           tq=384, tkw=2, tr=128, tt=1056, tc=512, tm=1408, tn=1024, tkc=4096, tkm=4224, la=128)


def _div_tile(n, target, align):
    best = None
    for t in range(align, min(n, target) + 1, align):
        if n % t == 0:
            best = t
    return best or n


def _cparams(*sem):
    return pltpu.CompilerParams(dimension_semantics=sem, vmem_limit_bytes=VMEM_LIMIT_V7X)


def _sig(x):
    return 1.0 / (1.0 + jnp.exp(-x))


def _expm1(x):
    p = x * (1.0 + x * (1.0 / 2) * (1.0 + x * (1.0 / 3) * (1.0 + x * (1.0 / 4) * (1.0 + x * (1.0 / 5) * (1.0 + x * (1.0 / 6))))))
    return jnp.where(x > -0.1, p, jnp.exp(x) - 1.0)


def _colsum8(x):
    r, w = x.shape
    return x.reshape(r // 8, 8, w).sum(axis=0)


class Side(NamedTuple):
    ins: list
    inplace: list
    outs: list
    sems: list
    start: object
    finish: object


def _host_call(body, grid, in_specs, out_specs, out_shape, scratch_shapes, semantics, name, args, side=None):
    if side is None:
        return pl.pallas_call(body, grid=grid, in_specs=in_specs, out_specs=out_specs, out_shape=out_shape,
                              scratch_shapes=scratch_shapes, compiler_params=_cparams(*semantics), name=name)(*args), None
    n_in, n_out, n_scr = len(in_specs), len(out_specs), len(scratch_shapes)
    n_si, n_sp, n_so = len(side.ins), len(side.inplace), len(side.outs)

    def wrapped(*refs):
        it = iter(refs)
        take = lambda k: [next(it) for _ in range(k)]
        ins, s_ins, s_inplace_in, outs, s_inplace, s_outs, scr, sems = (take(n_in), take(n_si), take(n_sp), take(n_out),
                                                                        take(n_sp), take(n_so), take(n_scr), take(len(side.sems)))
        ids = [pl.program_id(i) for i in range(len(grid))]
        first = functools.reduce(jnp.logical_and, [i == 0 for i in ids])
        last = functools.reduce(jnp.logical_and, [i == g - 1 for i, g in zip(ids, grid)])

        @pl.when(first)
        def _():
            side.start(s_ins, s_inplace, s_outs, sems)

        body(*ins, *outs, *scr)

        @pl.when(last)
        def _():
            side.finish(s_ins, s_inplace, s_outs, sems)

    any_spec = pl.BlockSpec(memory_space=pl.ANY)
    out = pl.pallas_call(
        wrapped, grid=grid, in_specs=list(in_specs) + [any_spec] * (n_si + n_sp),
        out_specs=list(out_specs) + [any_spec] * (n_sp + n_so),
        out_shape=list(out_shape) + [jax.ShapeDtypeStruct(x.shape, x.dtype) for x in side.inplace] + list(side.outs),
        scratch_shapes=list(scratch_shapes) + list(side.sems),
        input_output_aliases={n_in + n_si + i: n_out + i for i in range(n_sp)},
        compiler_params=_cparams(*(("arbitrary",) * len(grid))), name=name)(*args, *side.ins, *side.inplace)
    return out[:n_out], out[n_out:]


def _mm_call(a, b, grid, a_spec, b_spec, o_spec, out_sds, acc_shape, dims, name, side=None):
    nk = grid[-1]
    kax = len(grid) - 1

    def product(a_ref, b_ref):
        return lax.dot_general(a_ref[...].astype(BF16), b_ref[...].astype(BF16), dims, preferred_element_type=F32)

    def body_one(a_ref, b_ref, o_ref):
        o_ref[...] = product(a_ref, b_ref).astype(o_ref.dtype)

    def body_acc(a_ref, b_ref, o_ref, acc_ref):
        k = pl.program_id(kax)

        @pl.when(k == 0)
        def _():
            acc_ref[...] = jnp.zeros_like(acc_ref)

        acc_ref[...] += product(a_ref, b_ref)

        @pl.when(k == nk - 1)
        def _():
            o_ref[...] = acc_ref[...].astype(o_ref.dtype)

    sem = ("parallel",) * kax + ("arbitrary",)
    body, scratch = (body_one, []) if nk == 1 else (body_acc, [pltpu.VMEM(acc_shape, F32)])
    outs, side_outs = _host_call(body, grid, [a_spec, b_spec], [o_spec], [out_sds], scratch, sem, name, (a, b), side)
    return outs[0] if side is None else (outs[0], side_outs)


def _out_tile(cfg, tk):
    return cfg.tn if tk <= cfg.tn else cfg.tn // 2


_NN = (((1,), (0,)), ((), ()))
_NT = (((1,), (1,)), ((), ()))
_TN = (((0,), (0,)), ((), ()))


def mm_nn(cfg, a, w, name, out_dtype=F32, a_col0=0, side=None):
    m = a.shape[0]
    g, k, ns = w.shape
    tm, tk = _div_tile(m, cfg.tm, 16), _div_tile(k, cfg.tkc, cfg.la)
    tn = _div_tile(ns, _out_tile(cfg, tk), cfg.la)
    npb, a0 = ns // tn, a_col0 // tk
    assert a_col0 % tk == 0
    return _mm_call(a, w, (m // tm, g * npb, k // tk),
                    pl.BlockSpec((tm, tk), lambda i, j, kk: (i, a0 + kk)),
                    pl.BlockSpec((None, tk, tn), lambda i, j, kk: (j // npb, kk, j % npb)),
                    pl.BlockSpec((tm, tn), lambda i, j, kk: (i, j)),
                    jax.ShapeDtypeStruct((m, g * ns), out_dtype), (tm, tn), _NN, name, side)


def mm_nt(cfg, dy, w, name, out_dtype=F32, side=None):
    m = dy.shape[0]
    g, k, ns = w.shape
    tm, tk = _div_tile(m, cfg.tm, 16), _div_tile(ns, cfg.tkc, cfg.la)
    tn = _div_tile(k, _out_tile(cfg, tk), cfg.la)
    kpb = ns // tk
    return _mm_call(dy, w, (m // tm, k // tn, g * kpb),
                    pl.BlockSpec((tm, tk), lambda i, j, kk: (i, kk)),
                    pl.BlockSpec((None, tn, tk), lambda i, j, kk: (kk // kpb, j, kk % kpb)),
                    pl.BlockSpec((tm, tn), lambda i, j, kk: (i, j)),
                    jax.ShapeDtypeStruct((m, k), out_dtype), (tm, tn), _NT, name, side)


def mm_tn(cfg, x, dy, g, name, x_col0=0, k=None, side=None):
    m = x.shape[0]
    k = x.shape[1] if k is None else k
    ns = dy.shape[1] // g
    tk = _div_tile(m, cfg.tkm, 16)
    tmo, tn = _div_tile(k, _out_tile(cfg, tk), cfg.la), _div_tile(ns, cfg.tn, cfg.la)
    npb, x0 = ns // tn, x_col0 // tmo
    assert x_col0 % tmo == 0
    return _mm_call(x, dy, (k // tmo, g * npb, m // tk),
                    pl.BlockSpec((tk, tmo), lambda i, j, kk: (kk, x0 + i)),
                    pl.BlockSpec((tk, tn), lambda i, j, kk: (kk, j)),
                    pl.BlockSpec((None, tmo, tn), lambda i, j, kk: (j // npb, i, j % npb)),
                    jax.ShapeDtypeStruct((g, k, ns), F32), (tmo, tn), _TN, name, side)


def bd_nn(cfg, x, w, name, out_dtype=F32):
    m = x.shape[0]
    g, kg, ng = w.shape
    tm, tn = _div_tile(m, cfg.tm, 16), _div_tile(ng, cfg.tn, cfg.la)
    npb = ng // tn
    return _mm_call(x, w, (m // tm, g * npb, 1),
                    pl.BlockSpec((tm, kg), lambda i, j, kk: (i, j // npb)),
                    pl.BlockSpec((None, kg, tn), lambda i, j, kk: (j // npb, 0, j % npb)),
                    pl.BlockSpec((tm, tn), lambda i, j, kk: (i, j)),
                    jax.ShapeDtypeStruct((m, g * ng), out_dtype), (tm, tn), _NN, name)


def bd_nt(cfg, dy, w, name, out_dtype=F32):
    m = dy.shape[0]
    g, kg, ng = w.shape
    tm, tn = _div_tile(m, cfg.tm, 16), _div_tile(kg, cfg.tn, cfg.la)
    npb = kg // tn
    return _mm_call(dy, w, (m // tm, g * npb, 1),
                    pl.BlockSpec((tm, ng), lambda i, j, kk: (i, j // npb)),
                    pl.BlockSpec((None, tn, ng), lambda i, j, kk: (j // npb, j % npb, 0)),
                    pl.BlockSpec((tm, tn), lambda i, j, kk: (i, j)),
                    jax.ShapeDtypeStruct((m, g * kg), out_dtype), (tm, tn), _NT, name)


def bd_tn(cfg, x, dy, g, name):
    m = x.shape[0]
    kg, ng = x.shape[1] // g, dy.shape[1] // g
    tmo, tn, tk = _div_tile(kg, cfg.tn, cfg.la), _div_tile(ng, cfg.tn, cfg.la), _div_tile(m, cfg.tm, 16)
    mpb, npb = kg // tmo, ng // tn
    return _mm_call(x, dy, (g * mpb, npb, m // tk),
                    pl.BlockSpec((tk, tmo), lambda i, j, kk: (kk, i)),
                    pl.BlockSpec((tk, tn), lambda i, j, kk: (kk, (i // mpb) * npb + j)),
                    pl.BlockSpec((None, tmo, tn), lambda i, j, kk: (i // mpb, i % mpb, j)),
                    jax.ShapeDtypeStruct((g, kg, ng), F32), (tmo, tn), _TN, name)


def _rowwise(cfg, body, ins, outs, name, n_acc=0, side=None):
    tr = cfg.tr
    n = cfg.tp // tr
    in_specs, args = [], []
    for spec in ins:
        arr = spec[0]
        if spec[1] is None:
            in_specs.append(pl.BlockSpec(arr.shape, lambda i, nd=arr.ndim: (0,) * nd))
        else:
            in_specs.append(pl.BlockSpec((tr, spec[1]), lambda i, cb=spec[2]: (i, cb)))
        args.append(arr)
    out_specs, out_shape = [], []
    for o in outs:
        if len(o) == 2:
            out_specs.append(pl.BlockSpec((tr, o[0]), lambda i: (i, 0)))
            out_shape.append(jax.ShapeDtypeStruct((cfg.tp, o[0]), o[1]))
        else:
            out_specs.append(pl.BlockSpec((8, o[0]), lambda i: (0, 0)))
            out_shape.append(jax.ShapeDtypeStruct((8, o[0]), F32))
    outs, side_outs = _host_call(body, (n,), in_specs, out_specs, out_shape, [], ("arbitrary" if n_acc else "parallel",),
                                 name, args, side)
    return outs if side is None else (outs, side_outs)


def ln_fwd(cfg, h, o, g, b, alpha, name, side=None):
    d = cfg.d

    def body(h_ref, o_ref, g_ref, b_ref, y_ref, yb_ref, xh_ref, rs_ref):
        z = alpha * h_ref[...] + o_ref[...]
        mu = jnp.mean(z, axis=-1, keepdims=True)
        zc = z - mu
        var = jnp.mean(zc * zc, axis=-1, keepdims=True)
        rstd = lax.rsqrt(var + LN_EPS)
        xh = zc * rstd
        y = xh * g_ref[...] + b_ref[...]
        y_ref[...] = y
        yb_ref[...] = y.astype(BF16)
        xh_ref[...] = xh
        rs_ref[...] = rstd

    return _rowwise(cfg, body, [(h, d, 0), (o, d, 0), (g, None), (b, None)], [(d, F32), (d, BF16), (d, F32), (1, F32)], name,
                    side=side)


def ln_bwd(cfg, d_res, d_mm, xhat, rstd, g, alpha, name, side=None):
    d = cfg.d
    two = d_res is not None

    def body(*refs):
        if two:
            dr_ref, dm_ref, xh_ref, rs_ref, g_ref, dz_ref, dzb_ref, dg_ref, db_ref = refs
            dy = alpha * dr_ref[...] + dm_ref[...]
        else:
            dm_ref, xh_ref, rs_ref, g_ref, dz_ref, dzb_ref, dg_ref, db_ref = refs
            dy = dm_ref[...]
        xh = xh_ref[...]

        @pl.when(pl.program_id(0) == 0)
        def _():
            dg_ref[...] = jnp.zeros_like(dg_ref)
            db_ref[...] = jnp.zeros_like(db_ref)

        dg_ref[...] += _colsum8(dy * xh)
        db_ref[...] += _colsum8(dy)
        dxh = dy * g_ref[...]
        m1 = jnp.mean(dxh, axis=-1, keepdims=True)
        m2 = jnp.mean(dxh * xh, axis=-1, keepdims=True)
        dz = rs_ref[...] * (dxh - m1 - xh * m2)
        dz_ref[...] = dz
        dzb_ref[...] = dz.astype(BF16)

    ins = ([(d_res, d, 0)] if two else []) + [(d_mm, d, 0), (xhat, d, 0), (rstd, 1, 0), (g, None)]
    return _rowwise(cfg, body, ins, [(d, F32), (d, BF16), (d,), (d,)], name, n_acc=2, side=side)


def loss_grad(cfg, y, tgt, name):
    d, tr = cfg.d, cfg.tr
    lo, hi = cfg.n_meta, cfg.n_meta + cfg.seq

    def body(y_ref, t_ref, dy_ref, acc_ref):
        i = pl.program_id(0)

        @pl.when(i == 0)
        def _():
            acc_ref[...] = jnp.zeros_like(acc_ref)

        row = i * tr + lax.broadcasted_iota(jnp.int32, (tr, 1), 0)
        err = jnp.where((row >= lo) & (row < hi), y_ref[...] - t_ref[...], 0.0)
        dy_ref[...] = err * (1.0 / d)
        acc_ref[...] += _colsum8(err * err)

    return _rowwise(cfg, body, [(y, d, 0), (tgt, d, 0)], [(d, F32), (d,)], name, n_acc=1)


def axpy(cfg, a, b, alpha, name, side=None):
    d = cfg.d

    def body(a_ref, b_ref, o_ref):
        o_ref[...] = alpha * a_ref[...] + b_ref[...]

    return _rowwise(cfg, body, [(a, d, 0), (b, d, 0)], [(d, F32)], name, side=side)


def _time_call(cfg, body, ins, outs, accs, scratch, name, reverse=False, groups=1, side=None):
    d = cfg.d
    tc = _div_tile(d // groups, cfg.tc, cfg.la)
    tt = _div_tile(cfg.tp, cfg.tt, 16)
    nc, nt = d // tc, cfg.tp // tt
    tmap = (lambda t: nt - 1 - t) if reverse else (lambda t: t)
    in_specs, args = [], []
    for spec in ins:
        arr = spec[0]
        if len(spec) == 2:
            in_specs.append(pl.BlockSpec((tt, tc), lambda c, t, off=spec[1] * nc: (tmap(t), off + c)))
        else:
            in_specs.append(pl.BlockSpec((arr.shape[0], tc), lambda c, t: (0, c)))
        args.append(arr)
    out_specs = [pl.BlockSpec((tt, tc), lambda c, t: (tmap(t), c)) for _ in outs]
    out_shape = [jax.ShapeDtypeStruct((cfg.tp, d), dt) for dt in outs]
    for _ in range(accs):
        out_specs.append(pl.BlockSpec((8, tc), lambda c, t: (0, c)))
        out_shape.append(jax.ShapeDtypeStruct((8, d), F32))
    outs, side_outs = _host_call(functools.partial(body, tt=tt, tc=tc, nt=nt), (nc, nt), in_specs, out_specs, out_shape,
                                 [pltpu.VMEM(s, F32) for s in scratch(tt, tc)], ("parallel", "arbitrary"), name, args, side)
    return outs if side is None else (outs, side_outs)


def _push_history(s_ref, new, t, tt):
    @pl.when(t == 0)
    def _():
        s_ref[pl.ds(0, HIST), :] = jnp.zeros((HIST, s_ref.shape[1]), F32)

    @pl.when(t > 0)
    def _():
        s_ref[pl.ds(0, HIST), :] = s_ref[pl.ds(tt, HIST), :]

    s_ref[pl.ds(HIST, tt), :] = new


def _push_future(s_ref, new, t, tt):
    @pl.when(t == 0)
    def _():
        s_ref[pl.ds(tt, HIST), :] = jnp.zeros((HIST, s_ref.shape[1]), F32)

    @pl.when(t > 0)
    def _():
        s_ref[pl.ds(tt, HIST), :] = s_ref[pl.ds(0, HIST), :]

    s_ref[pl.ds(0, tt), :] = new


def conv_fwd(cfg, ug, w4, cb, name):
    def body(x_ref, w_ref, b_ref, u_ref, ub_ref, s_ref, *, tt, tc, nt):
        t = pl.program_id(1)
        _push_history(s_ref, x_ref[...], t, tt)
        acc = b_ref[...] + w_ref[pl.ds(0, 1), :] * s_ref[pl.ds(HIST - 3, tt), :]
        for k in range(1, CONV_W):
            acc = acc + w_ref[pl.ds(k, 1), :] * s_ref[pl.ds(HIST - 3 + k, tt), :]
        u_ref[...] = acc
        ub_ref[...] = acc.astype(BF16)

    return _time_call(cfg, body, [(ug, 0), (w4,), (cb,)], [F32, BF16], 0, lambda tt, tc: [(tt + HIST, tc)], name)


def conv_bwd(cfg, du_a, du_b, du_c, ug, w4, name):
    def body(a_ref, b_ref, c_ref, x_ref, w_ref, dx_ref, acc_ref, s_ref, *, tt, tc, nt):
        t = pl.program_id(1)

        @pl.when(t == 0)
        def _():
            acc_ref[...] = jnp.zeros_like(acc_ref)

        du = a_ref[...] + b_ref[...] + c_ref[...]
        _push_future(s_ref, du, t, tt)
        x = x_ref[...]
        acc_ref[pl.ds(4, 1), :] += jnp.sum(du, axis=0, keepdims=True)
        dx = None
        for k in range(CONV_W):
            sh = s_ref[pl.ds(3 - k, tt), :]
            term = w_ref[pl.ds(k, 1), :] * sh
            dx = term if dx is None else dx + term
            acc_ref[pl.ds(k, 1), :] += jnp.sum(x * sh, axis=0, keepdims=True)
        dx_ref[...] = dx.astype(BF16)

    return _time_call(cfg, body, [(du_a, 0), (du_b, 0), (du_c, 0), (ug, 0), (w4,)], [BF16], 1,
                      lambda tt, tc: [(tt + HIST, tc)], name, reverse=True)


def _scan8(a, b, rows):
    for k in (1, 2, 4):
        ar = jnp.where(rows >= k, pltpu.roll(a, k, 0), 1.0)
        br = jnp.where(rows >= k, pltpu.roll(b, k, 0), 0.0)
        b = a * br + b
        a = a * ar
    return a, b


def _rscan8(a, b, rows):
    for k in (1, 2, 4):
        ar = jnp.where(rows < 8 - k, pltpu.roll(a, 8 - k, 0), 1.0)
        br = jnp.where(rows < 8 - k, pltpu.roll(b, 8 - k, 0), 0.0)
        b = b + a * br
        a = a * ar
    return a, b


def _lru_gates(u, pa, px, ba, bx, c_lam):
    r = _sig(pa + ba)
    i = _sig(px + bx)
    la = LRU_C * r * c_lam
    em = _expm1(2.0 * la)
    return r, i, jnp.exp(la), em, jnp.sqrt(-em)


def _neg_softplus_neg(lam):
    e = jnp.exp(-jnp.abs(lam))
    u = 1.0 + e
    l1p = jnp.where(u == 1.0, e, jnp.log(u) * (e / jnp.where(u == 1.0, 1.0, u - 1.0)))
    return -(jnp.maximum(-lam, 0.0) + l1p)


def lru_fwd(cfg, u, pa, px, ug, ba, bx, lam, name, side=None):
    def body(u_ref, pa_ref, px_ref, g_ref, ba_ref, bx_ref, lam_ref, hs_ref, hp_ref, y_ref, car_ref, *, tt, tc, nt):
        @pl.when(pl.program_id(1) == 0)
        def _():
            car_ref[...] = jnp.zeros_like(car_ref)

        rows = lax.broadcasted_iota(jnp.int32, (8, tc), 0)
        ba, bx = ba_ref[...], bx_ref[...]
        c_lam = _neg_softplus_neg(lam_ref[...])

        def step(it, h):
            ys = []
            for half in range(2):
                rs = pl.ds(pl.multiple_of(it * 16 + half * 8, 8), 8)
                uu = u_ref[rs, :]
                r, i, a, em, s = _lru_gates(uu, pa_ref[rs, :], px_ref[rs, :], ba, bx, c_lam)
                acum, b = _scan8(a, uu * i * s, rows)
                hs = acum * h + b
                hs_ref[rs, :] = hs
                hp_ref[rs, :] = jnp.where(rows >= 1, pltpu.roll(hs, 1, 0), h)
                g = g_ref[rs, :]
                ys.append(hs * (g * _sig(g)))
                h = jnp.broadcast_to(hs[7:8, :], (8, tc))
            y_ref[pl.ds(pl.multiple_of(it * 16, 16), 16), :] = jnp.concatenate(ys, axis=0).astype(BF16)
            return h

        car_ref[...] = lax.fori_loop(0, tt // 16, step, car_ref[...])

    return _time_call(cfg, body, [(u, 0), (pa, 0), (px, 0), (ug, 1), (ba,), (bx,), (lam,)], [F32, F32, BF16], 0,
                      lambda tt, tc: [(8, tc)], name, side=side)


def lru_bwd(cfg, dy, ug, hs, hprev, u, pa, px, ba, bx, lam, name):
    def body(dy_ref, g_ref, hs_ref, hp_ref, u_ref, pa_ref, px_ref, ba_ref, bx_ref, lam_ref,
             dg_ref, dpa_ref, dpx_ref, du_ref, acc_ref, ecar_ref, acar_ref, *, tt, tc, nt):
        @pl.when(pl.program_id(1) == 0)
        def _():
            ecar_ref[...] = jnp.zeros_like(ecar_ref)
            acar_ref[...] = jnp.zeros_like(acar_ref)
            acc_ref[...] = jnp.zeros_like(acc_ref)

        rows = lax.broadcasted_iota(jnp.int32, (8, tc), 0)
        ba, bx = ba_ref[...], bx_ref[...]
        c_lam = _neg_softplus_neg(lam_ref[...])

        def step(it, carry):
            ecar, acar, s_lam, s_ba, s_bx = carry
            jt = tt // 16 - 1 - it
            dgs, dpas, dpxs = [None, None], [None, None], [None, None]
            for half in (1, 0):
                rs = pl.ds(pl.multiple_of(jt * 16 + half * 8, 8), 8)
                uu = u_ref[rs, :]
                r, i, a, em, s = _lru_gates(uu, pa_ref[rs, :], px_ref[rs, :], ba, bx, c_lam)
                g = g_ref[rs, :]
                sg = _sig(g)
                dy = dy_ref[rs, :]
                dgs[half] = dy * hs_ref[rs, :] * (sg * (1.0 + g * (1.0 - sg)))
                a_next = jnp.where(rows < 7, pltpu.roll(a, 7, 0), acar)
                acum, e = _rscan8(a_next, dy * (g * sg), rows)
                e = e + acum * ecar
                ecar = jnp.broadcast_to(e[0:1, :], (8, tc))
                acar = jnp.broadcast_to(a[0:1, :], (8, tc))
                du_ref[rs, :] = e * i * s
                d_la = e * hp_ref[rs, :] * a - (e * uu * i) * ((em + 1.0) / s)
                s_lam = s_lam + d_la * (LRU_C * r)
                d_pa = d_la * (LRU_C * c_lam) * r * (1.0 - r)
                d_px = e * uu * s * i * (1.0 - i)
                s_ba = s_ba + d_pa
                s_bx = s_bx + d_px
                dpas[half], dpxs[half] = d_pa, d_px
            r16 = pl.ds(pl.multiple_of(jt * 16, 16), 16)
            dg_ref[r16, :] = jnp.concatenate(dgs, axis=0).astype(BF16)
            dpa_ref[r16, :] = jnp.concatenate(dpas, axis=0).astype(BF16)
            dpx_ref[r16, :] = jnp.concatenate(dpxs, axis=0).astype(BF16)
            return ecar, acar, s_lam, s_ba, s_bx

        z = jnp.zeros((8, tc), F32)
        ecar, acar, s_lam, s_ba, s_bx = lax.fori_loop(0, tt // 16, step, (ecar_ref[...], acar_ref[...], z, z, z))
        ecar_ref[...] = ecar
        acar_ref[...] = acar
        acc_ref[pl.ds(0, 1), :] += jnp.sum(s_lam, axis=0, keepdims=True) * _sig(-lam_ref[...])
        acc_ref[pl.ds(1, 1), :] += jnp.sum(s_ba, axis=0, keepdims=True)
        acc_ref[pl.ds(2, 1), :] += jnp.sum(s_bx, axis=0, keepdims=True)

    return _time_call(cfg, body, [(dy, 0), (ug, 1), (hs, 0), (hprev, 0), (u, 0), (pa, 0), (px, 0), (ba,), (bx,), (lam,)],
                      [BF16, BF16, BF16, F32], 1, lambda tt, tc: [(8, tc), (8, tc)], name, reverse=True)


def _pool_select(cfg, grp, fn):
    for gi, w in enumerate(cfg.pool_windows):
        @pl.when(grp == gi)
        def _(w=w):
            fn(w)


def pool_fwd(cfg, ug, name):
    ng = len(cfg.pool_windows)

    def body(x_ref, p_ref, s_ref, *, tt, tc, nt):
        t = pl.program_id(1)
        grp = pl.program_id(0) // (cfg.d // ng // tc)
        x = x_ref[...]
        _push_history(s_ref, x, t, tt)
        row1 = (t * tt + 1 + lax.broadcasted_iota(jnp.int32, (tt, 1), 0)).astype(F32)

        def write(w):
            ws = x
            for j in range(1, w):
                ws = ws + s_ref[pl.ds(HIST - j, tt), :]
            p_ref[...] = (ws / jnp.minimum(row1, float(w)) - x).astype(BF16)

        _pool_select(cfg, grp, write)

    return _time_call(cfg, body, [(ug, 0)], [BF16], 0, lambda tt, tc: [(tt + HIST, tc)], name, groups=ng)[0]


def pool_bwd(cfg, dp, name):
    ng = len(cfg.pool_windows)

    def body(dp_ref, du_ref, s_ref, *, tt, tc, nt):
        t = pl.program_id(1)
        grp = pl.program_id(0) // (cfg.d // ng // tc)
        dp = dp_ref[...]
        row1 = ((nt - 1 - t) * tt + 1 + lax.broadcasted_iota(jnp.int32, (tt, 1), 0)).astype(F32)

        def write(w):
            dm = dp / jnp.minimum(row1, float(w))
            _push_future(s_ref, dm, t, tt)
            ws = dm
            for j in range(1, w):
                ws = ws + s_ref[pl.ds(j, tt), :]
            du_ref[...] = (ws - dp).astype(BF16)

        _pool_select(cfg, grp, write)

    return _time_call(cfg, body, [(dp, 0)], [BF16], 0, lambda tt, tc: [(tt + HIST, tc)], name, reverse=True, groups=ng)[0]


def gate_fwd(cfg, v, gsrc, gblk, scale, name):
    d = cfg.d

    def body(*refs):
        if scale is None:
            v_ref, g_ref, y_ref = refs
            v = v_ref[...]
        else:
            v_ref, g_ref, s_ref, y_ref = refs
            v = v_ref[...] * s_ref[...]
        g = g_ref[...]
        y_ref[...] = (v * (g * _sig(g))).astype(BF16)

    ins = [(v, d, 0), (gsrc, d, gblk)] + ([] if scale is None else [(scale, None)])
    return _rowwise(cfg, body, ins, [(d, BF16)], name)[0]


def gate_bwd(cfg, dy, v, gsrc, gblk, scale, dv_dtype, name):
    d = cfg.d

    def body(*refs):
        if scale is None:
            dy_ref, v_ref, g_ref, dv_ref, dg_ref = refs
            vs = v_ref[...]
        else:
            dy_ref, v_ref, g_ref, s_ref, dv_ref, dg_ref, acc_ref = refs
            vs = v_ref[...] * s_ref[...]
        g = g_ref[...]
        sg = _sig(g)
        dy = dy_ref[...]
        dvs = dy * (g * sg)
        dg_ref[...] = (dy * vs * (sg * (1.0 + g * (1.0 - sg)))).astype(BF16)
        if scale is None:
            dv_ref[...] = dvs.astype(dv_dtype)
        else:
            @pl.when(pl.program_id(0) == 0)
            def _():
                acc_ref[...] = jnp.zeros_like(acc_ref)

            acc_ref[...] += _colsum8(dvs * v_ref[...])
            dv_ref[...] = (dvs * s_ref[...]).astype(dv_dtype)

    ins = [(dy, d, 0), (v, d, 0), (gsrc, d, gblk)] + ([] if scale is None else [(scale, None)])
    outs = [(d, dv_dtype), (d, BF16)] + ([] if scale is None else [(d,)])
    return _rowwise(cfg, body, ins, outs, name, n_acc=0 if scale is None else 1)


def _swap_halves(x, lo, half):
    w = x.shape[1]
    lane = lax.broadcasted_iota(jnp.int32, x.shape, 1)
    sw = jnp.where(lane < lo + half, pltpu.roll(x, w - half, 1), pltpu.roll(x, half, 1))
    return jnp.where((lane >= lo) & (lane < lo + 2 * half), sw, 0.0)


def mla_norm(cfg, proj, qn, kvn, name):
    ql, kl = cfg.q_lora, cfg.kv_lora

    def body(c_ref, qn_ref, kn_ref, q_ref, k_ref):
        cq = c_ref[:, 0:ql]
        ck = c_ref[:, ql:ql + kl]
        q_ref[...] = (cq * lax.rsqrt(jnp.mean(cq * cq, axis=-1, keepdims=True) + RMS_EPS) * qn_ref[...]).astype(BF16)
        k_ref[...] = (ck * lax.rsqrt(jnp.mean(ck * ck, axis=-1, keepdims=True) + RMS_EPS) * kn_ref[...]).astype(BF16)

    return _rowwise(cfg, body, [(proj, cfg.cw, cfg.d // cfg.cw), (qn, None), (kvn, None)], [(ql, BF16), (kl, BF16)], name)


def mla_norm_bwd(cfg, proj, d_cqn, d_ckvn, d_kr, qn, kvn, name):
    ql, kl, cw, npe = cfg.q_lora, cfg.kv_lora, cfg.cw, cfg.nope

    def one(x, dy, gamma):
        rstd = lax.rsqrt(jnp.mean(x * x, axis=-1, keepdims=True) + RMS_EPS)
        xn = x * rstd
        dxn = dy * gamma
        return rstd * (dxn - xn * jnp.mean(dxn * xn, axis=-1, keepdims=True)), dy * xn

    def body(c_ref, dq_ref, dk_ref, dkr_ref, qn_ref, kn_ref, dc_ref, acc_ref):
        @pl.when(pl.program_id(0) == 0)
        def _():
            acc_ref[...] = jnp.zeros_like(acc_ref)

        dcq, gq = one(c_ref[:, 0:ql], dq_ref[...], qn_ref[...])
        dck, gk = one(c_ref[:, ql:ql + kl], dk_ref[...], kn_ref[...])
        dc_ref[:, 0:ql] = dcq.astype(BF16)
        dc_ref[:, ql:ql + kl] = dck.astype(BF16)
        dc_ref[:, ql + kl:ql + kl + npe] = dkr_ref[...].astype(BF16)
        rest = cw - (ql + kl + npe)
        if rest:
            dc_ref[:, ql + kl + npe:cw] = jnp.zeros((dc_ref.shape[0], rest), BF16)
        acc_ref[:, 0:ql] += _colsum8(gq)
        acc_ref[:, ql:ql + kl] += _colsum8(gk)

    return _rowwise(cfg, body, [(proj, cw, cfg.d // cw), (d_cqn, ql, 0), (d_ckvn, kl, 0), (d_kr, npe, 0), (qn, None), (kvn, None)],
                    [(cw, BF16), (cw,)], name, n_acc=1)


def _mla_tables(cfg):
    r2, npe = cfg.rope // 2, cfg.nope
    inv = ROPE_BASE ** (-jnp.arange(0, cfg.rope, 2, dtype=F32) / cfg.rope)
    ang = jnp.arange(cfg.tp, dtype=F32)[:, None] * inv[None, :]
    cos, sin = jnp.cos(ang), jnp.sin(ang)
    one = jnp.ones((cfg.tp, npe - cfg.rope), F32)
    ck = jnp.concatenate([cos, cos, one], axis=1)
    sk = jnp.concatenate([-sin, sin, 0.0 * one], axis=1)
    cq = jnp.concatenate([jnp.ones((cfg.tp, npe), F32), ck], axis=1)
    sq = jnp.concatenate([jnp.zeros((cfg.tp, npe), F32), sk], axis=1)
    return cq, sq, ck, sk


def mla_prep(cfg, q_raw, kv_raw, proj, tabs, name):
    h, npe, r2, ql, kl = cfg.mla_heads, cfg.nope, cfg.rope // 2, cfg.q_lora, cfg.kv_lora
    hp = 2 * npe

    def body(q_ref, kv_ref, c_ref, cq_ref, sq_ref, ck_ref, sk_ref, qo_ref, ko_ref, vo_ref):
        kr = c_ref[:, ql + kl:ql + kl + npe]
        kr = (kr * ck_ref[...] + _swap_halves(kr, 0, r2) * sk_ref[...]).astype(BF16)
        cq, sq = cq_ref[...], sq_ref[...]
        for i in range(h):
            q = q_ref[:, i * hp:(i + 1) * hp]
            qo_ref[:, i * hp:(i + 1) * hp] = (q * cq + _swap_halves(q, npe, r2) * sq).astype(BF16)
            ko_ref[:, i * hp:i * hp + npe] = kv_ref[:, i * npe:(i + 1) * npe].astype(BF16)
            ko_ref[:, i * hp + npe:(i + 1) * hp] = kr
        vo_ref[...] = kv_ref[:, h * npe:2 * h * npe].astype(BF16)

    cq, sq, ck, sk = tabs
    return _rowwise(cfg, body, [(q_raw, h * hp, 0), (kv_raw, 2 * h * npe, 0), (proj, cfg.cw, cfg.d // cfg.cw),
                                (cq, hp, 0), (sq, hp, 0), (ck, npe, 0), (sk, npe, 0)],
                    [(h * hp, BF16), (h * hp, BF16), (h * npe, BF16)], name)


def mla_prep_bwd(cfg, dq_full, dk_full, tabs, name, side=None):
    h, npe, r2 = cfg.mla_heads, cfg.nope, cfg.rope // 2
    hp = 2 * npe

    def body(dq_ref, dk_ref, cq_ref, sq_ref, ck_ref, sk_ref, dqo_ref, dko_ref, dkr_ref):
        cq, sq = cq_ref[...], sq_ref[...]
        dkr = None
        for i in range(h):
            dq = dq_ref[:, i * hp:(i + 1) * hp]
            dqo_ref[:, i * hp:(i + 1) * hp] = (dq * cq + _swap_halves(dq * sq, npe, r2)).astype(BF16)
            dko_ref[:, i * npe:(i + 1) * npe] = dk_ref[:, i * hp:i * hp + npe].astype(BF16)
            part = dk_ref[:, i * hp + npe:(i + 1) * hp]
            dkr = part if dkr is None else dkr + part
        dkr_ref[...] = dkr * ck_ref[...] + _swap_halves(dkr * sk_ref[...], 0, r2)

    cq, sq, ck, sk = tabs
    return _rowwise(cfg, body, [(dq_full, h * hp, 0), (dk_full, h * hp, 0), (cq, hp, 0), (sq, hp, 0), (ck, npe, 0), (sk, npe, 0)],
                    [(h * hp, BF16), (h * npe, BF16), (npe, F32)], name, side=side)


def _attn_scores(cfg, q, k, diagonal):
    s = lax.dot_general(q, k, _NT, preferred_element_type=F32) * ((cfg.nope + cfg.rope) ** -0.5)
    if diagonal:
        row = lax.broadcasted_iota(jnp.int32, s.shape, 0)
        col = lax.broadcasted_iota(jnp.int32, s.shape, 1)
        s = jnp.where(row >= col, s, -1e30)
    return s


def _tile_rows(i, t):
    return pl.ds(pl.multiple_of(i * t, t), t)


def attn_fwd(cfg, q, k, v, name, side=None):
    h, npe, tq, wide = cfg.mla_heads, cfg.nope, cfg.tq, cfg.tkw
    hp, n = 2 * npe, cfg.tp // tq

    def body(q_ref, k_ref, v_ref, o_ref, lse_ref):
        qi = pl.program_id(1)
        q = q_ref[...]

        def tile(ki, carry, width, diagonal):
            m, l, acc = carry
            rk = _tile_rows(ki, width)
            s = _attn_scores(cfg, q, k_ref[rk, :], diagonal)
            m_new = jnp.maximum(m, jnp.max(s, axis=-1, keepdims=True))
            alpha = jnp.exp(m - m_new)
            p = jnp.exp(s - m_new)
            l = alpha * l + jnp.sum(p, axis=-1, keepdims=True)
            acc = alpha * acc + lax.dot_general(p.astype(BF16), v_ref[rk, :], _NN, preferred_element_type=F32)
            return m_new, l, acc

        init = (jnp.full((tq, 1), -1e30, F32), jnp.zeros((tq, 1), F32), jnp.zeros((tq, npe), F32))
        n_wide = qi // wide
        carry = lax.fori_loop(0, n_wide, lambda j, c: tile(j, c, wide * tq, False), init)
        carry = lax.fori_loop(n_wide * wide, qi, lambda ki, c: tile(ki, c, tq, False), carry)
        m, l, acc = tile(qi, carry, tq, True)
        o_ref[...] = acc / l
        lse_ref[...] = m + jnp.log(l)

    outs, side_outs = _host_call(
        body, (h, n),
        [pl.BlockSpec((tq, hp), lambda hh, qi: (qi, hh)),
         pl.BlockSpec((cfg.tp, hp), lambda hh, qi: (0, hh)),
         pl.BlockSpec((cfg.tp, npe), lambda hh, qi: (0, hh))],
        [pl.BlockSpec((tq, npe), lambda hh, qi: (qi, hh)),
         pl.BlockSpec((None, tq, 1), lambda hh, qi: (hh, qi, 0))],
        [jax.ShapeDtypeStruct((cfg.tp, h * npe), F32), jax.ShapeDtypeStruct((h, cfg.tp, 1), F32)],
        [], ("parallel", "arbitrary"), name, (q, k, v), side)
    return outs if side is None else (outs, side_outs)


def attn_bwd(cfg, q, k, v, do, o, lse, name, side=None):
    h, npe, tq, kw = cfg.mla_heads, cfg.nope, cfg.tq, cfg.tkw
    hp, n = 2 * npe, cfg.tp // tq
    sc = (cfg.nope + cfg.rope) ** -0.5

    def body(q_ref, k_ref, v_ref, do_ref, o_ref, lse_ref, dq_ref, dk_ref, dv_ref, dv_acc):
        dk_ref[...] = jnp.zeros_like(dk_ref)
        dv_acc[...] = jnp.zeros_like(dv_acc)

        def q_tile(qi, _):
            rq = _tile_rows(qi, tq)
            q, do, lse = q_ref[rq, :], do_ref[rq, :], lse_ref[rq, :]
            delta = jnp.sum(do.astype(F32) * o_ref[rq, :], axis=-1, keepdims=True)

            def tile(ki, dq, width, diagonal):
                rk = _tile_rows(ki, width)
                kk = k_ref[rk, :]
                p = jnp.exp(_attn_scores(cfg, q, kk, diagonal) - lse)
                dp = lax.dot_general(do, v_ref[rk, :], _NT, preferred_element_type=F32)
                ds = (p * (dp - delta) * sc).astype(BF16)
                dv_acc[rk, :] += lax.dot_general(p.astype(BF16), do, _TN, preferred_element_type=F32)
                dk_ref[rk, :] += lax.dot_general(ds, q, _TN, preferred_element_type=F32)
                return dq + lax.dot_general(ds, kk, _NN, preferred_element_type=F32)

            n_wide = qi // kw
            dq = lax.fori_loop(0, n_wide, lambda j, acc: tile(j, acc, kw * tq, False), jnp.zeros((tq, hp), F32))
            dq = lax.fori_loop(n_wide * kw, qi, lambda ki, acc: tile(ki, acc, tq, False), dq)
            dq_ref[rq, :] = tile(qi, dq, tq, True)
            return 0

        lax.fori_loop(0, n, q_tile, 0)
        dv_ref[...] = dv_acc[...].astype(BF16)

    wide = pl.BlockSpec((cfg.tp, hp), lambda hh: (0, hh))
    narrow = pl.BlockSpec((cfg.tp, npe), lambda hh: (0, hh))
    outs, side_outs = _host_call(
        body, (h,), [wide, wide, narrow, narrow, narrow, pl.BlockSpec((None, cfg.tp, 1), lambda hh: (hh, 0, 0))],
        [wide, wide, narrow],
        [jax.ShapeDtypeStruct((cfg.tp, h * hp), F32), jax.ShapeDtypeStruct((cfg.tp, h * hp), F32),
         jax.ShapeDtypeStruct((cfg.tp, h * npe), BF16)],
        [pltpu.VMEM((cfg.tp, npe), F32)], ("parallel",), name, (q, k, v, do, o, lse), side)
    return outs if side is None else (outs, side_outs)


def _ret_tables(cfg):
    c = cfg.chunk
    lg = jnp.log(1.0 - 2.0 ** (-5.0 - jnp.arange(cfg.ret_heads, dtype=F32)))[:, None, None]
    j = jnp.arange(c, dtype=F32)
    diff = j[:, None] - j[None, :]
    dm = jnp.where(diff >= 0, jnp.exp(jnp.maximum(diff, 0.0)[None] * lg), 0.0)
    lq = jnp.exp((j + 1.0)[None, :, None] * lg)
    lk = jnp.exp((c - 1.0 - j)[None, :, None] * lg)
    gc = jnp.exp(c * lg)
    return dm, lq, lk, gc


def _ret_rope_tables(cfg):
    dk = cfg.d // cfg.ret_heads
    inv = ROPE_BASE ** (-jnp.arange(0, dk, 2, dtype=F32) / dk)
    ang = jnp.arange(cfg.tp, dtype=F32)[:, None] * inv[None, :]
    return jnp.cos(ang), jnp.sin(ang)


def ret_prep(cfg, qkvg, cos, sin, name):
    d, h = cfg.d, cfg.ret_heads
    dk = d // h
    hd = dk // 2
    ksc = dk ** -0.5

    def body(q_ref, k_ref, v_ref, c_ref, s_ref, qo_ref, ko_ref, vo_ref):
        c, s = c_ref[...], s_ref[...]
        for i in range(h):
            for src, dst, f in ((q_ref, qo_ref, 1.0), (k_ref, ko_ref, ksc)):
                x1 = src[:, i * dk:i * dk + hd]
                x2 = src[:, i * dk + hd:(i + 1) * dk]
                dst[:, i * dk:i * dk + hd] = ((x1 * c - x2 * s) * f).astype(BF16)
                dst[:, i * dk + hd:(i + 1) * dk] = ((x2 * c + x1 * s) * f).astype(BF16)
        vo_ref[...] = v_ref[...].astype(BF16)

    return _rowwise(cfg, body, [(qkvg, d, 0), (qkvg, d, 1), (qkvg, d, 2), (cos, hd, 0), (sin, hd, 0)],
                    [(d, BF16), (d, BF16), (d, BF16)], name)


def ret_prep_bwd(cfg, dq, dk_, dv, dg, cos, sin, name):
    d, h = cfg.d, cfg.ret_heads
    dk = d // h
    hd = dk // 2
    ksc = dk ** -0.5

    def body(dq_ref, dk_ref, dv_ref, dg_ref, c_ref, s_ref, o_ref):
        c, s = c_ref[...], s_ref[...]
        for i in range(h):
            for src, off, f in ((dq_ref, 0, 1.0), (dk_ref, d, ksc)):
                y1 = src[:, i * dk:i * dk + hd]
                y2 = src[:, i * dk + hd:(i + 1) * dk]
                o_ref[:, off + i * dk:off + i * dk + hd] = ((y1 * c + y2 * s) * f).astype(BF16)
                o_ref[:, off + i * dk + hd:off + (i + 1) * dk] = ((y2 * c - y1 * s) * f).astype(BF16)
        o_ref[:, 2 * d:3 * d] = dv_ref[...]
        o_ref[:, 3 * d:4 * d] = dg_ref[...]

    return _rowwise(cfg, body, [(dq, d, 0), (dk_, d, 0), (dv, d, 0), (dg, d, 0), (cos, hd, 0), (sin, hd, 0)],
                    [(4 * d, BF16)], name)[0]


def _ret_specs(cfg, reverse):
    c, h = cfg.chunk, cfg.ret_heads
    dk = cfg.d // h
    n = cfg.tp // c
    tm = (lambda t: n - 1 - t) if reverse else (lambda t: t)
    blk = pl.BlockSpec((c, dk), lambda hh, t: (tm(t), hh))
    row = pl.BlockSpec((None, c, 1), lambda hh, t: (hh, tm(t), 0))
    tabs = [pl.BlockSpec((None, c, c), lambda hh, t: (hh, 0, 0)), pl.BlockSpec((None, c, 1), lambda hh, t: (hh, 0, 0)),
            pl.BlockSpec((None, c, 1), lambda hh, t: (hh, 0, 0)), pl.BlockSpec((None, 1, 1), lambda hh, t: (hh, 0, 0))]
    return blk, row, tabs, n, dk


def _ret_state_update(s_ref, k, v, lk, gc):
    kw = (k.astype(F32) * lk).astype(BF16)
    s_ref[...] = gc * s_ref[...] + lax.dot_general(kw, v, _TN, preferred_element_type=F32)


def ret_fwd(cfg, q, k, v, tabs, name):
    blk, row, tspecs, n, dk = _ret_specs(cfg, False)

    def body(q_ref, k_ref, v_ref, dm_ref, lq_ref, lk_ref, gc_ref, on_ref, rs_ref, s_ref):
        @pl.when(pl.program_id(1) == 0)
        def _():
            s_ref[...] = jnp.zeros_like(s_ref)

        qq, kk, vv = q_ref[...], k_ref[...], v_ref[...]
        a = lax.dot_general(qq, kk, _NT, preferred_element_type=F32) * dm_ref[...]
        o = lax.dot_general(a.astype(BF16), vv, _NN, preferred_element_type=F32)
        o = o + lax.dot_general(qq, s_ref[...].astype(BF16), _NN, preferred_element_type=F32) * lq_ref[...]
        _ret_state_update(s_ref, kk, vv, lk_ref[...], gc_ref[...])
        mu = jnp.mean(o, axis=-1, keepdims=True)
        oc = o - mu
        rstd = lax.rsqrt(jnp.mean(oc * oc, axis=-1, keepdims=True) + LN_EPS)
        on_ref[...] = oc * rstd
        rs_ref[...] = rstd

    return pl.pallas_call(body, grid=(cfg.ret_heads, n), in_specs=[blk, blk, blk] + tspecs, out_specs=[blk, row],
                          out_shape=[jax.ShapeDtypeStruct((cfg.tp, cfg.d), F32), jax.ShapeDtypeStruct((cfg.ret_heads, cfg.tp, 1), F32)],
                          scratch_shapes=[pltpu.VMEM((dk, dk), F32)], compiler_params=_cparams("parallel", "arbitrary"),
                          name=name)(q, k, v, *tabs)


def ret_bwd_q(cfg, q, k, v, d_on, on, rstd, tabs, name):
    blk, row, tspecs, n, dk = _ret_specs(cfg, False)

    def body(q_ref, k_ref, v_ref, don_ref, on_ref, rs_ref, dm_ref, lq_ref, lk_ref, gc_ref, do_ref, dq_ref, s_ref):
        @pl.when(pl.program_id(1) == 0)
        def _():
            s_ref[...] = jnp.zeros_like(s_ref)

        kk, vv = k_ref[...], v_ref[...]
        don, on = don_ref[...], on_ref[...]
        do = rs_ref[...] * (don - jnp.mean(don, axis=-1, keepdims=True) - on * jnp.mean(don * on, axis=-1, keepdims=True))
        dob = do.astype(BF16)
        do_ref[...] = dob
        ds = lax.dot_general(dob, vv, _NT, preferred_element_type=F32) * dm_ref[...]
        dq = lax.dot_general(ds.astype(BF16), kk, _NN, preferred_element_type=F32)
        dq_ref[...] = dq + lax.dot_general(dob, s_ref[...].astype(BF16), _NT, preferred_element_type=F32) * lq_ref[...]
        _ret_state_update(s_ref, kk, vv, lk_ref[...], gc_ref[...])

    return pl.pallas_call(body, grid=(cfg.ret_heads, n), in_specs=[blk, blk, blk, blk, blk, row] + tspecs, out_specs=[blk, blk],
                          out_shape=[jax.ShapeDtypeStruct((cfg.tp, cfg.d), BF16), jax.ShapeDtypeStruct((cfg.tp, cfg.d), F32)],
                          scratch_shapes=[pltpu.VMEM((dk, dk), F32)], compiler_params=_cparams("parallel", "arbitrary"),
                          name=name)(q, k, v, d_on, on, rstd, *tabs)


def ret_bwd_kv(cfg, q, k, v, do, tabs, name):
    blk, row, tspecs, n, dk = _ret_specs(cfg, True)

    def body(q_ref, k_ref, v_ref, do_ref, dm_ref, lq_ref, lk_ref, gc_ref, dk_ref, dv_ref, g_ref):
        @pl.when(pl.program_id(1) == 0)
        def _():
            g_ref[...] = jnp.zeros_like(g_ref)

        qq, kk, vv, dob = q_ref[...], k_ref[...], v_ref[...], do_ref[...]
        dm, lk = dm_ref[...], lk_ref[...]
        gb = g_ref[...].astype(BF16)
        a = lax.dot_general(qq, kk, _NT, preferred_element_type=F32) * dm
        ds = lax.dot_general(dob, vv, _NT, preferred_element_type=F32) * dm
        dkk = lax.dot_general(ds.astype(BF16), qq, _TN, preferred_element_type=F32)
        dk_ref[...] = dkk + lax.dot_general(vv, gb, _NT, preferred_element_type=F32) * lk
        kw = (kk.astype(F32) * lk).astype(BF16)
        dvv = lax.dot_general(a.astype(BF16), dob, _TN, preferred_element_type=F32)
        dv_ref[...] = (dvv + lax.dot_general(kw, gb, _NN, preferred_element_type=F32)).astype(BF16)
        qw = (qq.astype(F32) * lq_ref[...]).astype(BF16)
        g_ref[...] = gc_ref[...] * g_ref[...] + lax.dot_general(qw, dob, _TN, preferred_element_type=F32)

    return pl.pallas_call(body, grid=(cfg.ret_heads, n), in_specs=[blk, blk, blk, blk] + tspecs, out_specs=[blk, blk],
                          out_shape=[jax.ShapeDtypeStruct((cfg.tp, cfg.d), F32), jax.ShapeDtypeStruct((cfg.tp, cfg.d), BF16)],
                          scratch_shapes=[pltpu.VMEM((dk, dk), F32)], compiler_params=_cparams("parallel", "arbitrary"),
                          name=name)(q, k, v, do, *tabs)


def _flat2(a):
    return a.reshape(-1, a.shape[-1])


def _ew_call(body, ins, out_dtypes, name, side=None):
    r, c = ins[-1].shape[-2:]
    tr = _div_tile(r, max(16, (1 << 18) // c), 16)
    specs = []
    for a in ins:
        if a.ndim == 3:
            specs.append(pl.BlockSpec((a.shape[0], tr, c), lambda i: (0, i, 0)))
        else:
            specs.append(pl.BlockSpec((tr, c), lambda i: (i, 0)))
    outs, side_outs = _host_call(body, (r // tr,), specs, [pl.BlockSpec((tr, c), lambda i: (i, 0)) for _ in out_dtypes],
                                 [jax.ShapeDtypeStruct((r, c), dt) for dt in out_dtypes], [], ("parallel",), name, ins, side)
    return outs if side is None else (outs, side_outs)


def adamw(w, g, m, v, name, side=None):
    c1 = 1.0 - ADAM_B1 ** ADAM_STEP
    c2 = 1.0 - ADAM_B2 ** ADAM_STEP

    def body(w_ref, g_ref, m_ref, v_ref, d_ref, mo_ref, vo_ref):
        gg = g_ref[...]
        mn = ADAM_B1 * m_ref[...] + (1.0 - ADAM_B1) * gg
        vn = ADAM_B2 * v_ref[...] + (1.0 - ADAM_B2) * (gg * gg)
        d_ref[...] = -ADAM_LR * ((mn / c1) / (jnp.sqrt(vn / c2) + ADAM_EPS) + ADAM_WD * w_ref[...])
        mo_ref[...] = mn
        vo_ref[...] = vn

    got = _ew_call(body, [_flat2(w), _flat2(g), _flat2(m), _flat2(v)], [F32, F32, F32], name, side)
    outs, side_outs = (got, None) if side is None else got
    outs = [o.reshape(w.shape) for o in outs]
    return outs if side is None else (outs, side_outs)


def add_pair_bf16(g, theirs, core, name):
    c = g.shape[-1]
    g4, t3 = g.reshape(4, 2, -1, c), theirs.reshape(4, -1, c)
    rows = t3.shape[1]
    tr = _div_tile(rows, max(16, (1 << 18) // c), 16)

    def body(core_ref, g_ref, t_ref, o_ref):
        o_ref[...] = (g_ref[...] + t_ref[...]).astype(BF16)

    spec = pl.BlockSpec((None, tr, c), lambda q, i, core_ref: (q, i, 0))
    grid_spec = pltpu.PrefetchScalarGridSpec(
        num_scalar_prefetch=1, grid=(4, rows // tr),
        in_specs=[pl.BlockSpec((None, None, tr, c), lambda q, i, core_ref: (q, core_ref[0], i, 0)), spec], out_specs=spec)
    out = pl.pallas_call(body, grid_spec=grid_spec, out_shape=jax.ShapeDtypeStruct((4, rows, c), BF16),
                         compiler_params=_cparams("parallel", "parallel"), name=name)(core, g4, t3)
    return out.reshape(theirs.shape)


def sum_slots(parts, landed, chip_core, name):
    c = parts.shape[-1]
    p3, l3 = parts.reshape(4, -1, c), landed.reshape(3, -1, c)
    rows = p3.shape[1]
    tr = _div_tile(rows, max(16, (1 << 18) // c), 16)

    def body(cc_ref, p_ref, l_ref, o_ref):
        acc = p_ref[...].astype(F32)
        for s in range(3):
            acc = acc + l_ref[s].astype(F32)
        o_ref[...] = acc

    grid_spec = pltpu.PrefetchScalarGridSpec(
        num_scalar_prefetch=1, grid=(rows // tr,),
        in_specs=[pl.BlockSpec((None, tr, c), lambda i, cc: (cc[0], i, 0)), pl.BlockSpec((3, tr, c), lambda i, cc: (0, i, 0))],
        out_specs=pl.BlockSpec((None, tr, c), lambda i, cc: (cc[1], i, 0)))
    out = pl.pallas_call(body, grid_spec=grid_spec, out_shape=jax.ShapeDtypeStruct((2, rows, c), F32),
                         compiler_params=_cparams("parallel"), name=name)(chip_core, p3, l3)
    return out.reshape((2,) + parts.shape[2:])


def cast_place(w, chip, name):
    c = w.shape[-1]
    w2 = w.reshape(-1, c)
    rows = w2.shape[0]
    tr = _div_tile(rows, max(16, (1 << 18) // c), 16)

    def body(chip_ref, w_ref, o_ref):
        o_ref[...] = w_ref[...].astype(BF16)

    grid_spec = pltpu.PrefetchScalarGridSpec(
        num_scalar_prefetch=1, grid=(rows // tr,),
        in_specs=[pl.BlockSpec((tr, c), lambda i, chip_ref: (i, 0))],
        out_specs=pl.BlockSpec((None, tr, c), lambda i, chip_ref: (chip_ref[0], i, 0)))
    out = pl.pallas_call(body, grid_spec=grid_spec, out_shape=jax.ShapeDtypeStruct((4, rows, c), BF16),
                         compiler_params=_cparams("parallel"), name=name)(chip, w2)
    return out.reshape((4,) + w.shape)


def _coords():
    return lax.axis_index("x"), lax.axis_index("y"), lax.axis_index("c")


def _chip_peer(x, y, k):
    return (x ^ (k >> 1), y ^ (k & 1))


_ANY = pl.BlockSpec(memory_space=pl.ANY)
DMA_PIECE_BYTES = 2 << 20
DMA_MAX_PIECES = 32


def _piece_plan(shape, itemsize):
    want = max(1, min(DMA_MAX_PIECES, math.prod(shape) * itemsize // DMA_PIECE_BYTES))
    plan = []
    for ax, n in enumerate(shape[:-1]):
        if want <= 1:
            break
        rows_tiled = ax == len(shape) - 2
        k = max([1] + [c for c in range(2, min(n, want) + 1) if n % c == 0 and (not rows_tiled or (n // c) % 16 == 0)])
        if k > 1:
            plan.append((ax, k))
            want = -(-want // k)
    return plan


class _Copy:
    def __init__(self, src, dst, send_sem, recv_sem=None, device_id=None):
        self.src, self.dst, self.send_sem, self.recv_sem, self.device_id = src, dst, send_sem, recv_sem, device_id

    def _dma(self, src, dst):
        if self.device_id is None:
            return pltpu.make_async_copy(src, dst, self.send_sem)
        return pltpu.make_async_remote_copy(src_ref=src, dst_ref=dst, send_sem=self.send_sem, recv_sem=self.recv_sem,
                                            device_id=self.device_id, device_id_type=MESH)

    def start(self):
        shape = self.src.shape
        plan = _piece_plan(shape, jnp.dtype(self.src.dtype).itemsize)
        for pick in np.ndindex(*[k for _, k in plan]):
            idx = [slice(None)] * len(shape)
            for (ax, k), i in zip(plan, pick):
                step = shape[ax] // k
                idx[ax] = pl.ds(i * step, step)
            idx = tuple(idx)
            self._dma(self.src.at[idx], self.dst.at[idx]).start()

    def wait(self):
        self._dma(self.src, self.dst).wait()

    def wait_send(self):
        self._dma(self.src, self.dst).wait_send()

    def wait_recv(self):
        self._dma(self.src, self.dst).wait_recv()


def allreduce_small(x, name):
    r, _, w = x.shape

    def body(x_ref, o_ref, mine_ref, gat_ref, send_sems, recv_sems):
        mx, my, mc = _coords()
        me = 4 * mx + 2 * my + mc
        mine_ref[...] = jnp.sum(x_ref[...], axis=1)
        gat_ref[me] = mine_ref[...]
        copies = []
        for k in range(1, 8):
            peer = (mx ^ (k >> 2), my ^ ((k >> 1) & 1), mc ^ (k & 1))
            cp = pltpu.make_async_remote_copy(src_ref=mine_ref, dst_ref=gat_ref.at[me], send_sem=send_sems.at[k - 1],
                                              recv_sem=recv_sems.at[k - 1], device_id=peer, device_id_type=MESH)
            cp.start()
            copies.append(cp)
        for k in range(1, 8):
            pltpu.make_async_remote_copy(src_ref=mine_ref, dst_ref=gat_ref.at[me ^ k], send_sem=send_sems.at[k - 1],
                                         recv_sem=recv_sems.at[k - 1], device_id=(mx, my, mc), device_id_type=MESH).wait_recv()
        for cp in copies:
            cp.wait_send()
        acc = gat_ref[0]
        for s in range(1, 8):
            acc = acc + gat_ref[s]
        o_ref[...] = acc

    return pl.pallas_call(body, out_shape=jax.ShapeDtypeStruct((r, w), F32),
                          in_specs=[pl.BlockSpec(memory_space=pltpu.VMEM)], out_specs=pl.BlockSpec(memory_space=pltpu.VMEM),
                          scratch_shapes=[pltpu.VMEM((r, w), F32), pltpu.VMEM((8, r, w), F32),
                                          pltpu.SemaphoreType.DMA((7,)), pltpu.SemaphoreType.DMA((7,))],
                          compiler_params=pltpu.CompilerParams(vmem_limit_bytes=VMEM_LIMIT_V7X), name=name)(x)


def allgather_chips(shards, name):
    n = len(shards)

    def body(*refs):
        outs = refs[n:2 * n]
        s1_send, s1_recv, s2_send, s2_recv = refs[2 * n:]
        mx, my, mc = _coords()
        p = 2 * mx + my
        sib = (mx, my, 1 - mc)
        sends = []
        for a in range(n):
            h0 = outs[a].shape[1] // 2
            mine = outs[a].at[p, pl.ds(mc * h0, h0)]
            for k in (1, 2, 3):
                qx, qy = _chip_peer(mx, my, k)
                cp = _Copy(mine, mine, s1_send.at[a, k - 1], s1_recv.at[a, k - 1], (qx, qy, mc))
                cp.start()
                sends.append(cp)
        for a in range(n):
            h0 = outs[a].shape[1] // 2
            half = pl.ds(mc * h0, h0)
            for k in (1, 2, 3):
                landed = outs[a].at[p ^ k, half]
                _Copy(landed, landed, s1_send.at[a, k - 1], s1_recv.at[a, k - 1], sib).wait_recv()
                fw = _Copy(landed, landed, s2_send.at[a, k - 1], s2_recv.at[a, k - 1], sib)
                fw.start()
                sends.append(fw)
        for a in range(n):
            h0 = outs[a].shape[1] // 2
            other = pl.ds((1 - mc) * h0, h0)
            for k in (1, 2, 3):
                theirs = outs[a].at[p ^ k, other]
                _Copy(theirs, theirs, s2_send.at[a, k - 1], s2_recv.at[a, k - 1], sib).wait_recv()
        for cp in sends:
            cp.wait_send()

    return pl.pallas_call(body, out_shape=[jax.ShapeDtypeStruct(s.shape, s.dtype) for s in shards],
                          in_specs=[_ANY] * n, out_specs=[_ANY] * n, input_output_aliases={a: a for a in range(n)},
                          scratch_shapes=[pltpu.SemaphoreType.DMA((n, 3)), pltpu.SemaphoreType.DMA((n, 3)),
                                          pltpu.SemaphoreType.DMA((n, 3)), pltpu.SemaphoreType.DMA((n, 3))],
                          name=name)(*shards)


def ag_chips_side(bufs):
    n = len(bufs)

    def each(inplace, sems, act):
        s_send, s_recv = sems
        mx, my, mc = _coords()
        p = 2 * mx + my
        for a in range(n):
            h0 = inplace[a].shape[1] // 2
            half = pl.ds(mc * h0, h0)
            mine = inplace[a].at[p, half]
            for k in (1, 2, 3):
                qx, qy = _chip_peer(mx, my, k)
                act(_Copy(mine, mine, s_send.at[a, k - 1], s_recv.at[a, k - 1], (qx, qy, mc)),
                    _Copy(inplace[a].at[p ^ k, half], inplace[a].at[p ^ k, half], s_send.at[a, k - 1], s_recv.at[a, k - 1], (qx, qy, mc)))

    def start(ins, inplace, outs, sems):
        each(inplace, sems, lambda send, landed: send.start())

    def finish(ins, inplace, outs, sems):
        each(inplace, sems, lambda send, landed: (landed.wait_recv(), send.wait_send()))

    return Side([], list(bufs), [], [pltpu.SemaphoreType.DMA((n, 3)), pltpu.SemaphoreType.DMA((n, 3))], start, finish)


def ag_forward_side(bufs):
    n = len(bufs)

    def each(inplace, sems, act):
        s_send, s_recv = sems
        mx, my, mc = _coords()
        p = 2 * mx + my
        sib = (mx, my, 1 - mc)
        for a in range(n):
            h0 = inplace[a].shape[1] // 2
            for k in (1, 2, 3):
                landed = inplace[a].at[p ^ k, pl.ds(mc * h0, h0)]
                theirs = inplace[a].at[p ^ k, pl.ds((1 - mc) * h0, h0)]
                act(_Copy(landed, landed, s_send.at[a, k - 1], s_recv.at[a, k - 1], sib),
                    _Copy(theirs, theirs, s_send.at[a, k - 1], s_recv.at[a, k - 1], sib))

    def start(ins, inplace, outs, sems):
        each(inplace, sems, lambda send, theirs: send.start())

    def finish(ins, inplace, outs, sems):
        each(inplace, sems, lambda send, theirs: (theirs.wait_recv(), send.wait_send()))

    return Side([], list(bufs), [], [pltpu.SemaphoreType.DMA((n, 3)), pltpu.SemaphoreType.DMA((n, 3))], start, finish)


def rs_chips_side(parts):
    n = len(parts)

    def copies(ins, outs, sems):
        send_sems, recv_sems = sems
        mx, my, mc = _coords()
        p = 2 * mx + my
        return [_Copy(ins[a].at[p ^ k], outs[a].at[k - 1], send_sems.at[a, k - 1], recv_sems.at[a, k - 1],
                      (*_chip_peer(mx, my, k), mc)) for a in range(n) for k in (1, 2, 3)]

    def start(ins, inplace, outs, sems):
        for cp in copies(ins, outs, sems):
            cp.start()

    def finish(ins, inplace, outs, sems):
        for cp in copies(ins, outs, sems):
            cp.wait()

    return Side(list(parts), [], [jax.ShapeDtypeStruct((3,) + g.shape[1:], g.dtype) for g in parts],
                [pltpu.SemaphoreType.DMA((n, 3)), pltpu.SemaphoreType.DMA((n, 3))], start, finish)


def rs_pair_side(grads):
    n = len(grads)

    def copies(ins, outs, sems):
        send_sems, recv_sems = sems
        mx, my, mc = _coords()
        return [_Copy(ins[a].at[:, pl.ds(1 - mc, 1)], outs[a], send_sems.at[a], recv_sems.at[a], (mx, my, 1 - mc)) for a in range(n)]

    def start(ins, inplace, outs, sems):
        for cp in copies(ins, outs, sems):
            cp.start()

    def finish(ins, inplace, outs, sems):
        for cp in copies(ins, outs, sems):
            cp.wait()

    return Side(list(grads), [], [jax.ShapeDtypeStruct((4, 1) + g.shape[2:], g.dtype) for g in grads],
                [pltpu.SemaphoreType.DMA((n,)), pltpu.SemaphoreType.DMA((n,))], start, finish)


def exchange_halves(grads, name):
    n = len(grads)
    side = rs_pair_side(grads)

    def body(*refs):
        ins, outs, sems = refs[:n], refs[n:2 * n], refs[2 * n:]
        side.start(ins, [], outs, sems)
        side.finish(ins, [], outs, sems)

    return pl.pallas_call(body, out_shape=side.outs, in_specs=[_ANY] * n, out_specs=[_ANY] * n, scratch_shapes=side.sems,
                          name=name)(*grads)


def scatter_chips(parts, name):
    n = len(parts)
    side = rs_chips_side(parts)

    def body(*refs):
        ins, outs, sems = refs[:n], refs[n:2 * n], refs[2 * n:]
        side.start(ins, [], outs, sems)
        side.finish(ins, [], outs, sems)

    return pl.pallas_call(body, out_shape=side.outs, in_specs=[_ANY] * n, out_specs=[_ANY] * n, scratch_shapes=side.sems,
                          name=name)(*parts)


def rs_join_side(halves):
    n = len(halves)

    def each(inplace, sems, act):
        send_sems, recv_sems = sems
        mx, my, mc = _coords()
        for a in range(n):
            mine, theirs = inplace[a].at[pl.ds(mc, 1)], inplace[a].at[pl.ds(1 - mc, 1)]
            act(_Copy(mine, mine, send_sems.at[a], recv_sems.at[a], (mx, my, 1 - mc)),
                _Copy(theirs, theirs, send_sems.at[a], recv_sems.at[a], (mx, my, 1 - mc)))

    def start(ins, inplace, outs, sems):
        each(inplace, sems, lambda send, theirs: send.start())

    def finish(ins, inplace, outs, sems):
        each(inplace, sems, lambda send, theirs: (send.wait_send(), theirs.wait_recv()))

    return Side([], list(halves), [], [pltpu.SemaphoreType.DMA((n,)), pltpu.SemaphoreType.DMA((n,))], start, finish)


def join_halves(halves, name):
    n = len(halves)
    side = rs_join_side(halves)

    def body(*refs):
        outs, sems = refs[n:2 * n], refs[2 * n:]
        side.start([], outs, [], sems)
        side.finish([], outs, [], sems)

    return pl.pallas_call(body, out_shape=[jax.ShapeDtypeStruct(g.shape, g.dtype) for g in halves],
                          in_specs=[_ANY] * n, out_specs=[_ANY] * n, input_output_aliases={a: a for a in range(n)},
                          scratch_shapes=side.sems, name=name)(*halves)


def rs_pair_sums(grads, tag, call=None):
    out, theirs = (None, exchange_halves(grads, f"rs_pair_{tag}")) if call is None else call(rs_pair_side(grads))
    core = lax.axis_index("c").astype(jnp.int32).reshape(1)
    return out, [add_pair_bf16(g, t, core, f"rs_add_{tag}_{i}") for i, (g, t) in enumerate(zip(grads, theirs))]


def rs_finish(parts, landed, tag, call=None):
    chip_core = jnp.stack([2 * lax.axis_index("x") + lax.axis_index("y"), lax.axis_index("c")]).astype(jnp.int32)
    halves = [sum_slots(pt, ld, chip_core, f"rs_sum_{tag}_{i}") for i, (pt, ld) in enumerate(zip(parts, landed))]
    return (None, join_halves(halves, f"rs_join_{tag}")) if call is None else call(rs_join_side(halves))


W_NAMES = ['meta_tokens', 'l0_w_in', 'l0_conv_w', 'l0_conv_b', 'l0_w_a', 'l0_b_a', 'l0_w_x', 'l0_b_x', 'l0_lam', 'l0_w_out',
           'l0_ln_g', 'l0_ln_b', 'l1_w_in', 'l1_w_grp', 'l1_scale', 'l1_w_out', 'l1_ln_g', 'l1_ln_b', 'l2_w_in', 'l2_q_norm',
           'l2_w_uq', 'l2_kv_norm', 'l2_w_ukv', 'l2_w_out', 'l2_ln_g', 'l2_ln_b', 'l3_w_in', 'l3_w_out', 'l3_ln_g', 'l3_ln_b']
BIG = {0: ['l0_w_in', 'l0_w_a', 'l0_w_x', 'l0_w_out'], 1: ['l1_w_in', 'l1_w_grp', 'l1_w_out'],
       2: ['l2_w_in', 'l2_w_uq', 'l2_w_ukv', 'l2_w_out'], 3: ['l3_w_in', 'l3_w_out']}
SMALL_SHARDED = ['meta_tokens', 'l0_conv_w']
SMALL_REPL = ['l0_conv_b', 'l0_b_a', 'l0_b_x', 'l0_lam', 'l0_ln_g', 'l0_ln_b', 'l1_scale', 'l1_ln_g', 'l1_ln_b',
              'l2_q_norm', 'l2_kv_norm', 'l2_ln_g', 'l2_ln_b', 'l3_ln_g', 'l3_ln_b']


def _rows8(rows, width):
    out = []
    for r in rows:
        r = r.reshape(-1, r.shape[-1])
        out.append(jnp.pad(r, ((0, 8 - r.shape[0]), (0, width - r.shape[1]))))
    return jnp.stack(out)


def _unblock(g4, axis):
    return jnp.concatenate([g4[i] for i in range(4)], axis=axis)


def _block(full, axis):
    return jnp.stack(jnp.split(full, 4, axis=axis))


def _step(cfg, a):
    d, tp, nm, seq = cfg.d, cfg.tp, cfg.n_meta, cfg.seq
    alpha = (2.0 * cfg.depth) ** 0.25
    mx, my, mc = _coords()
    p = 2 * mx + my
    dq4 = d // 4
    hm, npe, ql, kl = cfg.mla_heads, cfg.nope, cfg.q_lora, cfg.kv_lora
    hp = 2 * npe
    qk = npe + cfg.rope
    vec = lambda name: a[name].reshape(1, -1)

    sm = jnp.concatenate([a['meta_tokens'], a['l0_conv_w'].reshape(CONV_W, dq4)], axis=0)
    placed = lax.dynamic_update_slice(jnp.zeros((nm + CONV_W, d), F32), sm * (mc == 0).astype(F32), (0, p * dq4))
    gathered = allreduce_small(_rows8([placed[i:i + 1] for i in range(nm + CONV_W)], d), "gather_small")
    meta_full, conv_w4 = gathered[:nm], gathered[nm:]

    chip = p.astype(jnp.int32).reshape(1)
    wg = {n: cast_place(a[n], chip, f"place_{n}") for names in BIG.values() for n in names}
    wg.update(zip(BIG[0], allgather_chips([wg[n] for n in BIG[0]], "ag_l0")))

    def hosted(call, names):
        out, got = call(ag_chips_side([wg[n] for n in names]))
        wg.update(zip(names, got))
        return out

    def forwarded(call, names):
        out, got = call(ag_forward_side([wg[n] for n in names]))
        wg.update(zip(names, got))
        return out

    heads_first = lambda w: jnp.moveaxis(w, 0, 1).reshape(w.shape[1], w.shape[0] * w.shape[2], w.shape[3])
    w_out = lambda i: wg[f'l{i}_w_out'].reshape(1, d, d)

    zpad = jnp.zeros((tp - nm - seq, d), F32)
    h0 = jnp.concatenate([meta_full, a['x'][0], zpad], axis=0)
    h0_bf = h0.astype(BF16)

    w0_in, w0_a, w0_x = wg['l0_w_in'], heads_first(wg['l0_w_a']), heads_first(wg['l0_w_x'])
    ug0 = hosted(lambda side: mm_nn(cfg, h0_bf, w0_in, "l0_in", side=side), ['l1_w_in'])
    u0, u0_bf = conv_fwd(cfg, ug0, conv_w4, vec('l0_conv_b'), "l0_conv")
    pa0 = bd_nn(cfg, u0_bf, w0_a, "l0_gate_a")
    px0 = bd_nn(cfg, u0_bf, w0_x, "l0_gate_x")
    hs0, hprev0, y0 = hosted(lambda side: lru_fwd(cfg, u0, pa0, px0, ug0, vec('l0_b_a'), vec('l0_b_x'), vec('l0_lam'), "l0_lru",
                                                  side=side), ['l1_w_out'])
    o0 = hosted(lambda side: mm_nn(cfg, y0, w_out(0), "l0_out", side=side), ['l1_w_grp'])
    h1, h1_bf, xh0, rs0 = forwarded(lambda side: ln_fwd(cfg, h0, o0, vec('l0_ln_g'), vec('l0_ln_b'), alpha, "l0_ln", side=side), BIG[1])

    w1_in, w1_grp = wg['l1_w_in'], heads_first(wg['l1_w_grp'])
    ug1 = hosted(lambda side: mm_nn(cfg, h1_bf, w1_in, "l1_in", side=side), ['l2_w_in'])
    p1 = pool_fwd(cfg, ug1, "l1_pool")
    mm1 = bd_nn(cfg, p1, w1_grp, "l1_grp")
    y1 = gate_fwd(cfg, mm1, ug1, 1, vec('l1_scale'), "l1_gate")
    o1 = hosted(lambda side: mm_nn(cfg, y1, w_out(1), "l1_out", side=side), ['l2_w_uq', 'l2_w_ukv'])
    h2, h2_bf, xh1, rs1 = forwarded(lambda side: ln_fwd(cfg, h1, o1, vec('l1_ln_g'), vec('l1_ln_b'), alpha, "l1_ln", side=side),
                                    ['l2_w_in', 'l2_w_uq', 'l2_w_ukv'])

    w2_in = jnp.pad(_unblock(wg['l2_w_in'], 1), ((0, 0), (0, cfg.cw - (ql + kl + cfg.rope))))[None]
    w2_uq = jnp.pad(_unblock(wg['l2_w_uq'], 1).reshape(ql, hm, qk), ((0, 0), (0, 0), (0, hp - qk))).reshape(1, ql, hm * hp)
    w2_ukv = _unblock(wg['l2_w_ukv'], 1).reshape(kl, hm, 2, npe).transpose(0, 2, 1, 3).reshape(1, kl, 2 * hm * npe)
    proj = hosted(lambda side: mm_nn(cfg, h2_bf, w2_in, "l2_in", side=side), ['l2_w_out'])
    cqn, ckvn = mla_norm(cfg, proj, vec('l2_q_norm'), vec('l2_kv_norm'), "l2_norm")
    q_raw = forwarded(lambda side: mm_nn(cfg, cqn, w2_uq, "l2_uq", side=side), ['l2_w_out'])
    kv_raw = mm_nn(cfg, ckvn, w2_ukv, "l2_ukv")
    tabs = _mla_tables(cfg)
    qf, kf, vb = mla_prep(cfg, q_raw, kv_raw, proj, tabs, "l2_prep")
    o_att, lse = hosted(lambda side: attn_fwd(cfg, qf, kf, vb, "l2_attn", side=side), ['l3_w_in', 'l3_w_out'])
    y2 = gate_fwd(cfg, o_att, proj, 0, None, "l2_gate")
    w2_out = w_out(2)
    o2 = forwarded(lambda side: mm_nn(cfg, y2, w2_out, "l2_out", side=side), BIG[3])
    h3, h3_bf, xh2, rs2 = ln_fwd(cfg, h2, o2, vec('l2_ln_g'), vec('l2_ln_b'), alpha, "l2_ln")

    w3_in = wg['l3_w_in']
    qkvg = mm_nn(cfg, h3_bf, w3_in, "l3_in")
    cos, sin = _ret_rope_tables(cfg)
    qr, kr, vr = ret_prep(cfg, qkvg, cos, sin, "l3_prep")
    rt = _ret_tables(cfg)
    on, rsr = ret_fwd(cfg, qr, kr, vr, rt, "l3_ret")
    y3 = gate_fwd(cfg, on, qkvg, 3, None, "l3_gate")
    o3 = mm_nn(cfg, y3, w_out(3), "l3_out")
    h4, _, xh3, rs3 = ln_fwd(cfg, h3, o3, vec('l3_ln_g'), vec('l3_ln_b'), alpha, "l3_ln")

    tgt = jnp.concatenate([jnp.zeros((nm, d), F32), a['loss_target'][0], zpad], axis=0)
    dy4, lacc = loss_grad(cfg, h4, tgt, "loss")
    loss = lax.psum(0.5 * jnp.sum(lacc) / d, ("x", "y", "c"))

    grads, small, parts, landed = {}, {}, {}, {}

    def rs_begin(layer, full, call=None):
        out, got = rs_pair_sums([g.reshape((4, 2, g.shape[1] // 2) + g.shape[2:]) for g in full], f"l{layer}", call)
        parts.update(zip(BIG[layer], got))
        return out

    def carried(call, names):
        out, got = call(rs_chips_side([parts[n] for n in names]))
        landed.update(zip(names, got))
        return out

    def rs_end(layer, call=None):
        names = BIG[layer]
        out, got = rs_finish([parts[n] for n in names], [landed[n] for n in names], f"l{layer}", call)
        for n, g in zip(names, got):
            grads[n] = g.reshape(a[n].shape)
        return out

    dz3, dz3_bf, small['l3_ln_g'], small['l3_ln_b'] = ln_bwd(cfg, None, dy4, xh3, rs3, vec('l3_ln_g'), alpha, "l3_ln_b")
    g_out = mm_tn(cfg, y3, dz3_bf, 1, "l3_dw_out")
    dyb = mm_nt(cfg, dz3_bf, w_out(3), "l3_dy")
    d_on, dg = gate_bwd(cfg, dyb, on, qkvg, 3, None, F32, "l3_gate_b")
    do_bf, dqr = ret_bwd_q(cfg, qr, kr, vr, d_on, on, rsr, rt, "l3_ret_bq")
    dkr, dvr = ret_bwd_kv(cfg, qr, kr, vr, do_bf, rt, "l3_ret_bkv")
    d_in = ret_prep_bwd(cfg, dqr, dkr, dvr, dg, cos, sin, "l3_prep_b")
    g_in = mm_tn(cfg, h3_bf, d_in, 4, "l3_dw_in")
    dh = mm_nt(cfg, d_in, w3_in, "l3_dh")
    g3 = [g_in, g_out.reshape(4, dq4, d)]

    dz2, dz2_bf, small['l2_ln_g'], small['l2_ln_b'] = ln_bwd(cfg, dz3, dh, xh2, rs2, vec('l2_ln_g'), alpha, "l2_ln_b")
    g_out = rs_begin(3, g3, lambda side: mm_tn(cfg, y2, dz2_bf, 1, "l2_dw_out", side=side))
    dyb = mm_nt(cfg, dz2_bf, w_out(2), "l2_dy")
    do_bf, dg = gate_bwd(cfg, dyb, o_att, proj, 0, None, BF16, "l2_gate_b")
    dq_full, dk_full, dv_bf = carried(lambda side: attn_bwd(cfg, qf, kf, vb, do_bf, o_att, lse, "l2_attn_b", side=side), BIG[3])
    dq_raw, dk_nope, d_kr = rs_end(3, lambda side: mla_prep_bwd(cfg, dq_full, dk_full, tabs, "l2_prep_b", side=side))
    dkv_raw = jnp.concatenate([dk_nope, dv_bf], axis=1)
    g_uq = mm_tn(cfg, cqn, dq_raw, 1, "l2_dw_uq")
    d_cqn = mm_nt(cfg, dq_raw, w2_uq, "l2_dcq")
    g_ukv = mm_tn(cfg, ckvn, dkv_raw, 1, "l2_dw_ukv")
    d_ckvn = mm_nt(cfg, dkv_raw, w2_ukv, "l2_dckv")
    d_c, nacc = mla_norm_bwd(cfg, proj, d_cqn, d_ckvn, d_kr, vec('l2_q_norm'), vec('l2_kv_norm'), "l2_norm_b")
    small['l2_q_norm'], small['l2_kv_norm'] = nacc[:, :ql], nacc[:, ql:ql + kl]
    d_in = jnp.concatenate([dg, d_c], axis=1)
    g_in = mm_tn(cfg, h2_bf, d_in, 1, "l2_dw_in")
    dh = mm_nt(cfg, d_in, w2_in, "l2_dh")
    g_in = _block(g_in[0][:, :d + ql + kl + cfg.rope], 1)
    g_uq = _block(g_uq.reshape(ql, hm, hp)[:, :, :qk].reshape(ql, hm * qk), 1)
    g_ukv = _block(g_ukv.reshape(kl, 2, hm, npe).transpose(0, 2, 1, 3).reshape(kl, 2 * hm * npe), 1)
    g2 = [g_in, g_uq, g_ukv, g_out.reshape(4, dq4, d)]

    dz1, dz1_bf, small['l1_ln_g'], small['l1_ln_b'] = ln_bwd(cfg, dz2, dh, xh1, rs1, vec('l1_ln_g'), alpha, "l1_ln_b")
    g_out = rs_begin(2, g2, lambda side: mm_tn(cfg, y1, dz1_bf, 1, "l1_dw_out", side=side))
    dyb = mm_nt(cfg, dz1_bf, w_out(1), "l1_dy")
    d_mm1, dg, small['l1_scale'] = gate_bwd(cfg, dyb, mm1, ug1, 1, vec('l1_scale'), BF16, "l1_gate_b")
    g_grp = bd_tn(cfg, p1, d_mm1, len(cfg.pool_windows), "l1_dw_grp")
    dp = bd_nt(cfg, d_mm1, w1_grp, "l1_dp")
    du = pool_bwd(cfg, dp, "l1_pool_b")
    d_in = jnp.concatenate([du, dg], axis=1)
    g_in = carried(lambda side: mm_tn(cfg, h1_bf, d_in, 4, "l1_dw_in", side=side), ['l2_w_in', 'l2_w_uq'])
    dh = carried(lambda side: mm_nt(cfg, d_in, w1_in, "l1_dh", side=side), ['l2_w_ukv', 'l2_w_out'])
    blocks_first = lambda g: jnp.moveaxis(g.reshape(g.shape[0], 4, g.shape[1] // 4, g.shape[2]), 1, 0)
    g1 = [g_in, blocks_first(g_grp), g_out.reshape(4, dq4, d)]

    dh1 = dh
    dz0, dz0_bf, small['l0_ln_g'], small['l0_ln_b'] = rs_end(
        2, lambda side: ln_bwd(cfg, dz1, dh1, xh0, rs0, vec('l0_ln_g'), alpha, "l0_ln_b", side=side))
    g_out = rs_begin(1, g1, lambda side: mm_tn(cfg, y0, dz0_bf, 1, "l0_dw_out", side=side))
    dyb = mm_nt(cfg, dz0_bf, w_out(0), "l0_dy")
    dg, dpa, dpx, du_dir, lacc0 = lru_bwd(cfg, dyb, ug0, hs0, hprev0, u0, pa0, px0, vec('l0_b_a'), vec('l0_b_x'), vec('l0_lam'), "l0_lru_b")
    g_a = bd_tn(cfg, u0_bf, dpa, cfg.lru_heads, "l0_dw_a")
    g_x = bd_tn(cfg, u0_bf, dpx, cfg.lru_heads, "l0_dw_x")
    du_a = bd_nt(cfg, dpa, w0_a, "l0_du_a")
    du_x = bd_nt(cfg, dpx, w0_x, "l0_du_x")
    du_pre, cacc = conv_bwd(cfg, du_dir, du_a, du_x, ug0, conv_w4, "l0_conv_b")
    d_in = jnp.concatenate([du_pre, dg], axis=1)
    g_in = carried(lambda side: mm_tn(cfg, h0_bf, d_in, 4, "l0_dw_in", side=side), ['l1_w_in'])
    dh = carried(lambda side: mm_nt(cfg, d_in, w0_in, "l0_dh", side=side), ['l1_w_grp', 'l1_w_out'])
    dh0 = rs_end(1, lambda side: axpy(cfg, dz0, dh, alpha, "dh0", side=side))[0]
    rs_begin(0, [g_in, blocks_first(g_a), blocks_first(g_x), g_out.reshape(4, dq4, d)])
    grad_x = dh0[nm:nm + seq][None]
    small['l0_lam'], small['l0_b_a'], small['l0_b_x'] = lacc0[0:1], lacc0[1:2], lacc0[2:3]
    small['l0_conv_b'] = cacc[4:5]

    rows = [dh0[i:i + 1] for i in range(nm)] + [cacc[k:k + 1] for k in range(CONV_W)] + [small[n] for n in SMALL_REPL]
    red = allreduce_small(_rows8(rows, d), "reduce_small")
    sh = lax.dynamic_slice(red[:nm + CONV_W], (0, p * dq4), (nm + CONV_W, dq4))
    grads['meta_tokens'] = sh[:nm]
    grads['l0_conv_w'] = sh[nm:].reshape(a['l0_conv_w'].shape)
    for i, n in enumerate(SMALL_REPL):
        grads[n] = red[nm + CONV_W + i, :a[n].shape[0]]

    delta, new_m, new_v = {}, {}, {}
    riders = {'l3_w_in': ['l0_w_in'], 'l1_w_in': ['l0_w_out'], 'l2_w_in': ['l0_w_a', 'l0_w_x']}
    for n in BIG[3] + BIG[2] + BIG[1]:
        update = lambda side=None, n=n: adamw(a[n], grads[n], a['m_' + n], a['v_' + n], f"adamw_{n}", side=side)
        delta[n], new_m[n], new_v[n] = carried(update, riders[n]) if n in riders else update()
    rs_end(0)
    for n in BIG[0]:
        delta[n], new_m[n], new_v[n] = adamw(a[n], grads[n], a['m_' + n], a['v_' + n], f"adamw_{n}")
    pack_s = lambda pre: jnp.concatenate([a[pre + 'meta_tokens'], a[pre + 'l0_conv_w'].reshape(CONV_W, dq4)], axis=0)
    ds_, ms_, vs_ = adamw(pack_s(''), sh, pack_s('m_'), pack_s('v_'), "adamw_small_sharded")
    for out, packed in ((delta, ds_), (new_m, ms_), (new_v, vs_)):
        out['meta_tokens'] = packed[:nm]
        out['l0_conv_w'] = packed[nm:].reshape(a['l0_conv_w'].shape)
    pack_r = lambda pre: jnp.stack([jnp.pad(a[pre + n], (0, d - a[n].shape[0])) for n in SMALL_REPL])
    dr_, mr_, vr_ = adamw(pack_r(''), red[nm + CONV_W:nm + CONV_W + len(SMALL_REPL)], pack_r('m_'), pack_r('v_'), "adamw_small_repl")
    for out, packed in ((delta, dr_), (new_m, mr_), (new_v, vr_)):
        for i, n in enumerate(SMALL_REPL):
            out[n] = packed[i, :a[n].shape[0]]

    return (loss, grad_x, *[grads[n] for n in W_NAMES], *[delta[n] for n in W_NAMES],
            *[new_m[n] for n in W_NAMES], *[new_v[n] for n in W_NAMES])


def kernel(x, meta_tokens, l0_w_in, l0_conv_w, l0_conv_b, l0_w_a, l0_b_a, l0_w_x, l0_b_x, l0_lam, l0_w_out, l0_ln_g, l0_ln_b, l1_w_in, l1_w_grp, l1_scale, l1_w_out, l1_ln_g, l1_ln_b, l2_w_in, l2_q_norm, l2_w_uq, l2_kv_norm, l2_w_ukv, l2_w_out, l2_ln_g, l2_ln_b, l3_w_in, l3_w_out, l3_ln_g, l3_ln_b, loss_target, m_meta_tokens, m_l0_w_in, m_l0_conv_w, m_l0_conv_b, m_l0_w_a, m_l0_b_a, m_l0_w_x, m_l0_b_x, m_l0_lam, m_l0_w_out, m_l0_ln_g, m_l0_ln_b, m_l1_w_in, m_l1_w_grp, m_l1_scale, m_l1_w_out, m_l1_ln_g, m_l1_ln_b, m_l2_w_in, m_l2_q_norm, m_l2_w_uq, m_l2_kv_norm, m_l2_w_ukv, m_l2_w_out, m_l2_ln_g, m_l2_ln_b, m_l3_w_in, m_l3_w_out, m_l3_ln_g, m_l3_ln_b, v_meta_tokens, v_l0_w_in, v_l0_conv_w, v_l0_conv_b, v_l0_w_a, v_l0_b_a, v_l0_w_x, v_l0_b_x, v_l0_lam, v_l0_w_out, v_l0_ln_g, v_l0_ln_b, v_l1_w_in, v_l1_w_grp, v_l1_scale, v_l1_w_out, v_l1_ln_g, v_l1_ln_b, v_l2_w_in, v_l2_q_norm, v_l2_w_uq, v_l2_kv_norm, v_l2_w_ukv, v_l2_w_out, v_l2_ln_g, v_l2_ln_b, v_l3_w_in, v_l3_w_out, v_l3_ln_g, v_l3_ln_b):
    return _step(REAL, dict(locals()))
```

```python
import functools
import math
from typing import NamedTuple

import jax
import jax.numpy as jnp
import numpy as np
from jax import lax
from jax.experimental import pallas as pl
from jax.experimental.pallas import tpu as pltpu

F32 = jnp.float32
BF16 = jnp.bfloat16

LN_EPS = 1e-5
RMS_EPS = 1e-6
ROPE_BASE = 10000.0
LRU_C = 8.0
CONV_W = 4
HIST = 16
ADAM_LR, ADAM_B1, ADAM_B2, ADAM_EPS, ADAM_WD, ADAM_STEP = 0.001, 0.9, 0.999, 1e-08, 0.01, 10
VMEM_LIMIT_V7X = 56 * 1024 * 1024
MESH = pl.DeviceIdType.MESH


class Cfg(NamedTuple):
    d: int
    seq: int
    n_meta: int
    tp: int
    depth: int
    lru_heads: int
    pool_windows: tuple
    mla_heads: int
    nope: int
    rope: int
    q_lora: int
    kv_lora: int
    cw: int
    ret_heads: int
    chunk: int
    tq: int
    tkw: int
    tr: int
    tt: int
    tc: int
    tm: int
    tn: int
    tkc: int
    tkm: int
    la: int


REAL = Cfg(d=4096, seq=4096, n_meta=16, tp=4224, depth=4, lru_heads=16, pool_windows=(2, 4, 8, 16),
           mla_heads=32, nope=128, rope=64, q_lora=1024, kv_lora=512, cw=2048, ret_heads=16, chunk=384,
           tq=384, tkw=2, tr=192, tt=1056, tc=512, tm=1408, tn=1024, tkc=4096, tkm=4224, la=128)


def _div_tile(n, target, align):
    best = None
    for t in range(align, min(n, target) + 1, align):
        if n % t == 0:
            best = t
    return best or n


def _cparams(*sem):
    return pltpu.CompilerParams(dimension_semantics=sem, vmem_limit_bytes=VMEM_LIMIT_V7X)


def _sig(x):
    return 1.0 / (1.0 + jnp.exp(-x))


def _expm1(x):
    p = x * (1.0 + x * (1.0 / 2) * (1.0 + x * (1.0 / 3) * (1.0 + x * (1.0 / 4) * (1.0 + x * (1.0 / 5) * (1.0 + x * (1.0 / 6))))))
    return jnp.where(x > -0.1, p, jnp.exp(x) - 1.0)


def _colsum8(x):
    r, w = x.shape
    return x.reshape(r // 8, 8, w).sum(axis=0)


class Side(NamedTuple):
    ins: list
    inplace: list
    outs: list
    sems: list
    start: object
    finish: object


def _host_call(body, grid, in_specs, out_specs, out_shape, scratch_shapes, semantics, name, args, side=None):
    if side is None:
        return pl.pallas_call(body, grid=grid, in_specs=in_specs, out_specs=out_specs, out_shape=out_shape,
                              scratch_shapes=scratch_shapes, compiler_params=_cparams(*semantics), name=name)(*args), None
    n_in, n_out, n_scr = len(in_specs), len(out_specs), len(scratch_shapes)
    n_si, n_sp, n_so = len(side.ins), len(side.inplace), len(side.outs)

    def wrapped(*refs):
        it = iter(refs)
        take = lambda k: [next(it) for _ in range(k)]
        ins, s_ins, s_inplace_in, outs, s_inplace, s_outs, scr, sems = (take(n_in), take(n_si), take(n_sp), take(n_out),
                                                                        take(n_sp), take(n_so), take(n_scr), take(len(side.sems)))
        ids = [pl.program_id(i) for i in range(len(grid))]
        first = functools.reduce(jnp.logical_and, [i == 0 for i in ids])
        last = functools.reduce(jnp.logical_and, [i == g - 1 for i, g in zip(ids, grid)])

        @pl.when(first)
        def _():
            side.start(s_ins, s_inplace, s_outs, sems)

        body(*ins, *outs, *scr)

        @pl.when(last)
        def _():
            side.finish(s_ins, s_inplace, s_outs, sems)

    any_spec = pl.BlockSpec(memory_space=pl.ANY)
    out = pl.pallas_call(
        wrapped, grid=grid, in_specs=list(in_specs) + [any_spec] * (n_si + n_sp),
        out_specs=list(out_specs) + [any_spec] * (n_sp + n_so),
        out_shape=list(out_shape) + [jax.ShapeDtypeStruct(x.shape, x.dtype) for x in side.inplace] + list(side.outs),
        scratch_shapes=list(scratch_shapes) + list(side.sems),
        input_output_aliases={n_in + n_si + i: n_out + i for i in range(n_sp)},
        compiler_params=_cparams(*(("arbitrary",) * len(grid))), name=name)(*args, *side.ins, *side.inplace)
    return out[:n_out], out[n_out:]


def _mm_call(a, b, grid, a_spec, b_spec, o_spec, out_sds, acc_shape, dims, name, side=None):
    nk = grid[-1]
    kax = len(grid) - 1

    def product(a_ref, b_ref):
        return lax.dot_general(a_ref[...].astype(BF16), b_ref[...].astype(BF16), dims, preferred_element_type=F32)

    def body_one(a_ref, b_ref, o_ref):
        o_ref[...] = product(a_ref, b_ref).astype(o_ref.dtype)

    def body_acc(a_ref, b_ref, o_ref, acc_ref):
        k = pl.program_id(kax)

        @pl.when(k == 0)
        def _():
            acc_ref[...] = jnp.zeros_like(acc_ref)

        acc_ref[...] += product(a_ref, b_ref)

        @pl.when(k == nk - 1)
        def _():
            o_ref[...] = acc_ref[...].astype(o_ref.dtype)

    sem = ("parallel",) * kax + ("arbitrary",)
    body, scratch = (body_one, []) if nk == 1 else (body_acc, [pltpu.VMEM(acc_shape, F32)])
    outs, side_outs = _host_call(body, grid, [a_spec, b_spec], [o_spec], [out_sds], scratch, sem, name, (a, b), side)
    return outs[0] if side is None else (outs[0], side_outs)


def _out_tile(cfg, tk):
    return cfg.tn if tk <= cfg.tn else cfg.tn // 2


_NN = (((1,), (0,)), ((), ()))
_NT = (((1,), (1,)), ((), ()))
_TN = (((0,), (0,)), ((), ()))


def mm_nn(cfg, a, w, name, out_dtype=F32, a_col0=0, side=None):
    m = a.shape[0]
    g, k, ns = w.shape
    tm, tk = _div_tile(m, cfg.tm, 16), _div_tile(k, cfg.tkc, cfg.la)
    tn = _div_tile(ns, _out_tile(cfg, tk), cfg.la)
    npb, a0 = ns // tn, a_col0 // tk
    assert a_col0 % tk == 0
    return _mm_call(a, w, (m // tm, g * npb, k // tk),
                    pl.BlockSpec((tm, tk), lambda i, j, kk: (i, a0 + kk)),
                    pl.BlockSpec((None, tk, tn), lambda i, j, kk: (j // npb, kk, j % npb)),
                    pl.BlockSpec((tm, tn), lambda i, j, kk: (i, j)),
                    jax.ShapeDtypeStruct((m, g * ns), out_dtype), (tm, tn), _NN, name, side)


def mm_nt(cfg, dy, w, name, out_dtype=F32, side=None):
    m = dy.shape[0]
    g, k, ns = w.shape
    tm, tk = _div_tile(m, cfg.tm, 16), _div_tile(ns, cfg.tkc, cfg.la)
    tn = _div_tile(k, _out_tile(cfg, tk), cfg.la)
    kpb = ns // tk
    return _mm_call(dy, w, (m // tm, k // tn, g * kpb),
                    pl.BlockSpec((tm, tk), lambda i, j, kk: (i, kk)),
                    pl.BlockSpec((None, tn, tk), lambda i, j, kk: (kk // kpb, j, kk % kpb)),
                    pl.BlockSpec((tm, tn), lambda i, j, kk: (i, j)),
                    jax.ShapeDtypeStruct((m, k), out_dtype), (tm, tn), _NT, name, side)


def mm_tn(cfg, x, dy, g, name, x_col0=0, k=None, side=None):
    m = x.shape[0]
    k = x.shape[1] if k is None else k
    ns = dy.shape[1] // g
    tk = _div_tile(m, cfg.tkm, 16)
    tmo, tn = _div_tile(k, _out_tile(cfg, tk), cfg.la), _div_tile(ns, cfg.tn, cfg.la)
    npb, x0 = ns // tn, x_col0 // tmo
    assert x_col0 % tmo == 0
    return _mm_call(x, dy, (k // tmo, g * npb, m // tk),
                    pl.BlockSpec((tk, tmo), lambda i, j, kk: (kk, x0 + i)),
                    pl.BlockSpec((tk, tn), lambda i, j, kk: (kk, j)),
                    pl.BlockSpec((None, tmo, tn), lambda i, j, kk: (j // npb, i, j % npb)),
                    jax.ShapeDtypeStruct((g, k, ns), F32), (tmo, tn), _TN, name, side)


def bd_nn(cfg, x, w, name, out_dtype=F32):
    m = x.shape[0]
    g, kg, ng = w.shape
    tm, tn = _div_tile(m, cfg.tm, 16), _div_tile(ng, cfg.tn, cfg.la)
    npb = ng // tn
    return _mm_call(x, w, (m // tm, g * npb, 1),
                    pl.BlockSpec((tm, kg), lambda i, j, kk: (i, j // npb)),
                    pl.BlockSpec((None, kg, tn), lambda i, j, kk: (j // npb, 0, j % npb)),
                    pl.BlockSpec((tm, tn), lambda i, j, kk: (i, j)),
                    jax.ShapeDtypeStruct((m, g * ng), out_dtype), (tm, tn), _NN, name)


def bd_nt(cfg, dy, w, name, out_dtype=F32):
    m = dy.shape[0]
    g, kg, ng = w.shape
    tm, tn = _div_tile(m, cfg.tm, 16), _div_tile(kg, cfg.tn, cfg.la)
    npb = kg // tn
    return _mm_call(dy, w, (m // tm, g * npb, 1),
                    pl.BlockSpec((tm, ng), lambda i, j, kk: (i, j // npb)),
                    pl.BlockSpec((None, tn, ng), lambda i, j, kk: (j // npb, j % npb, 0)),
                    pl.BlockSpec((tm, tn), lambda i, j, kk: (i, j)),
                    jax.ShapeDtypeStruct((m, g * kg), out_dtype), (tm, tn), _NT, name)


def bd_tn(cfg, x, dy, g, name):
    m = x.shape[0]
    kg, ng = x.shape[1] // g, dy.shape[1] // g
    tmo, tn, tk = _div_tile(kg, cfg.tn, cfg.la), _div_tile(ng, cfg.tn, cfg.la), _div_tile(m, cfg.tm, 16)
    mpb, npb = kg // tmo, ng // tn
    return _mm_call(x, dy, (g * mpb, npb, m // tk),
                    pl.BlockSpec((tk, tmo), lambda i, j, kk: (kk, i)),
                    pl.BlockSpec((tk, tn), lambda i, j, kk: (kk, (i // mpb) * npb + j)),
                    pl.BlockSpec((None, tmo, tn), lambda i, j, kk: (i // mpb, i % mpb, j)),
                    jax.ShapeDtypeStruct((g, kg, ng), F32), (tmo, tn), _TN, name)


def _rowwise(cfg, body, ins, outs, name, n_acc=0, side=None):
    tr = cfg.tr
    n = cfg.tp // tr
    in_specs, args = [], []
    for spec in ins:
        arr = spec[0]
        if spec[1] is None:
            in_specs.append(pl.BlockSpec(arr.shape, lambda i, nd=arr.ndim: (0,) * nd))
        else:
            in_specs.append(pl.BlockSpec((tr, spec[1]), lambda i, cb=spec[2]: (i, cb)))
        args.append(arr)
    out_specs, out_shape = [], []
    for o in outs:
        if len(o) == 2:
            out_specs.append(pl.BlockSpec((tr, o[0]), lambda i: (i, 0)))
            out_shape.append(jax.ShapeDtypeStruct((cfg.tp, o[0]), o[1]))
        else:
            out_specs.append(pl.BlockSpec((8, o[0]), lambda i: (0, 0)))
            out_shape.append(jax.ShapeDtypeStruct((8, o[0]), F32))
    outs, side_outs = _host_call(body, (n,), in_specs, out_specs, out_shape, [], ("arbitrary" if n_acc else "parallel",),
                                 name, args, side)
    return outs if side is None else (outs, side_outs)


def ln_fwd(cfg, h, o, g, b, alpha, name, side=None):
    d = cfg.d

    def body(h_ref, o_ref, g_ref, b_ref, y_ref, yb_ref, xh_ref, rs_ref):
        z = alpha * h_ref[...] + o_ref[...]
        mu = jnp.mean(z, axis=-1, keepdims=True)
        zc = z - mu
        var = jnp.mean(zc * zc, axis=-1, keepdims=True)
        rstd = lax.rsqrt(var + LN_EPS)
        xh = zc * rstd
        y = xh * g_ref[...] + b_ref[...]
        y_ref[...] = y
        yb_ref[...] = y.astype(BF16)
        xh_ref[...] = xh
        rs_ref[...] = rstd

    return _rowwise(cfg, body, [(h, d, 0), (o, d, 0), (g, None), (b, None)], [(d, F32), (d, BF16), (d, F32), (1, F32)], name,
                    side=side)


def ln_bwd(cfg, d_res, d_mm, xhat, rstd, g, alpha, name, side=None):
    d = cfg.d
    two = d_res is not None

    def body(*refs):
        if two:
            dr_ref, dm_ref, xh_ref, rs_ref, g_ref, dz_ref, dzb_ref, dg_ref, db_ref = refs
            dy = alpha * dr_ref[...] + dm_ref[...]
        else:
            dm_ref, xh_ref, rs_ref, g_ref, dz_ref, dzb_ref, dg_ref, db_ref = refs
            dy = dm_ref[...]
        xh = xh_ref[...]

        @pl.when(pl.program_id(0) == 0)
        def _():
            dg_ref[...] = jnp.zeros_like(dg_ref)
            db_ref[...] = jnp.zeros_like(db_ref)

        dg_ref[...] += _colsum8(dy * xh)
        db_ref[...] += _colsum8(dy)
        dxh = dy * g_ref[...]
        m1 = jnp.mean(dxh, axis=-1, keepdims=True)
        m2 = jnp.mean(dxh * xh, axis=-1, keepdims=True)
        dz = rs_ref[...] * (dxh - m1 - xh * m2)
        dz_ref[...] = dz
        dzb_ref[...] = dz.astype(BF16)

    ins = ([(d_res, d, 0)] if two else []) + [(d_mm, d, 0), (xhat, d, 0), (rstd, 1, 0), (g, None)]
    return _rowwise(cfg, body, ins, [(d, F32), (d, BF16), (d,), (d,)], name, n_acc=2, side=side)


def loss_grad(cfg, y, tgt, name):
    d, tr = cfg.d, cfg.tr
    lo, hi = cfg.n_meta, cfg.n_meta + cfg.seq

    def body(y_ref, t_ref, dy_ref, acc_ref):
        i = pl.program_id(0)

        @pl.when(i == 0)
        def _():
            acc_ref[...] = jnp.zeros_like(acc_ref)

        row = i * tr + lax.broadcasted_iota(jnp.int32, (tr, 1), 0)
        err = jnp.where((row >= lo) & (row < hi), y_ref[...] - t_ref[...], 0.0)
        dy_ref[...] = err * (1.0 / d)
        acc_ref[...] += _colsum8(err * err)

    return _rowwise(cfg, body, [(y, d, 0), (tgt, d, 0)], [(d, F32), (d,)], name, n_acc=1)


def axpy(cfg, a, b, alpha, name, side=None):
    d = cfg.d

    def body(a_ref, b_ref, o_ref):
        o_ref[...] = alpha * a_ref[...] + b_ref[...]

    return _rowwise(cfg, body, [(a, d, 0), (b, d, 0)], [(d, F32)], name, side=side)


def _time_call(cfg, body, ins, outs, accs, scratch, name, reverse=False, groups=1, side=None):
    d = cfg.d
    tc = _div_tile(d // groups, cfg.tc, cfg.la)
    tt = _div_tile(cfg.tp, cfg.tt, 16)
    nc, nt = d // tc, cfg.tp // tt
    tmap = (lambda t: nt - 1 - t) if reverse else (lambda t: t)
    in_specs, args = [], []
    for spec in ins:
        arr = spec[0]
        if len(spec) == 2:
            in_specs.append(pl.BlockSpec((tt, tc), lambda c, t, off=spec[1] * nc: (tmap(t), off + c)))
        else:
            in_specs.append(pl.BlockSpec((arr.shape[0], tc), lambda c, t: (0, c)))
        args.append(arr)
    out_specs = [pl.BlockSpec((tt, tc), lambda c, t: (tmap(t), c)) for _ in outs]
    out_shape = [jax.ShapeDtypeStruct((cfg.tp, d), dt) for dt in outs]
    for _ in range(accs):
        out_specs.append(pl.BlockSpec((8, tc), lambda c, t: (0, c)))
        out_shape.append(jax.ShapeDtypeStruct((8, d), F32))
    outs, side_outs = _host_call(functools.partial(body, tt=tt, tc=tc, nt=nt), (nc, nt), in_specs, out_specs, out_shape,
                                 [pltpu.VMEM(s, F32) for s in scratch(tt, tc)], ("parallel", "arbitrary"), name, args, side)
    return outs if side is None else (outs, side_outs)


def _push_history(s_ref, new, t, tt):
    @pl.when(t == 0)
    def _():
        s_ref[pl.ds(0, HIST), :] = jnp.zeros((HIST, s_ref.shape[1]), F32)

    @pl.when(t > 0)
    def _():
        s_ref[pl.ds(0, HIST), :] = s_ref[pl.ds(tt, HIST), :]

    s_ref[pl.ds(HIST, tt), :] = new


def _push_future(s_ref, new, t, tt):
    @pl.when(t == 0)
    def _():
        s_ref[pl.ds(tt, HIST), :] = jnp.zeros((HIST, s_ref.shape[1]), F32)

    @pl.when(t > 0)
    def _():
        s_ref[pl.ds(tt, HIST), :] = s_ref[pl.ds(0, HIST), :]

    s_ref[pl.ds(0, tt), :] = new


def conv_fwd(cfg, ug, w4, cb, name):
    def body(x_ref, w_ref, b_ref, u_ref, ub_ref, s_ref, *, tt, tc, nt):
        t = pl.program_id(1)
        _push_history(s_ref, x_ref[...], t, tt)
        acc = b_ref[...] + w_ref[pl.ds(0, 1), :] * s_ref[pl.ds(HIST - 3, tt), :]
        for k in range(1, CONV_W):
            acc = acc + w_ref[pl.ds(k, 1), :] * s_ref[pl.ds(HIST - 3 + k, tt), :]
        u_ref[...] = acc
        ub_ref[...] = acc.astype(BF16)

    return _time_call(cfg, body, [(ug, 0), (w4,), (cb,)], [F32, BF16], 0, lambda tt, tc: [(tt + HIST, tc)], name)


def conv_bwd(cfg, du_a, du_b, du_c, ug, w4, name):
    def body(a_ref, b_ref, c_ref, x_ref, w_ref, dx_ref, acc_ref, s_ref, *, tt, tc, nt):
        t = pl.program_id(1)

        @pl.when(t == 0)
        def _():
            acc_ref[...] = jnp.zeros_like(acc_ref)

        du = a_ref[...] + b_ref[...] + c_ref[...]
        _push_future(s_ref, du, t, tt)
        x = x_ref[...]
        acc_ref[pl.ds(4, 1), :] += jnp.sum(du, axis=0, keepdims=True)
        dx = None
        for k in range(CONV_W):
            sh = s_ref[pl.ds(3 - k, tt), :]
            term = w_ref[pl.ds(k, 1), :] * sh
            dx = term if dx is None else dx + term
            acc_ref[pl.ds(k, 1), :] += jnp.sum(x * sh, axis=0, keepdims=True)
        dx_ref[...] = dx.astype(BF16)

    return _time_call(cfg, body, [(du_a, 0), (du_b, 0), (du_c, 0), (ug, 0), (w4,)], [BF16], 1,
                      lambda tt, tc: [(tt + HIST, tc)], name, reverse=True)


def _scan8(a, b, rows):
    for k in (1, 2, 4):
        ar = jnp.where(rows >= k, pltpu.roll(a, k, 0), 1.0)
        br = jnp.where(rows >= k, pltpu.roll(b, k, 0), 0.0)
        b = a * br + b
        a = a * ar
    return a, b


def _rscan8(a, b, rows):
    for k in (1, 2, 4):
        ar = jnp.where(rows < 8 - k, pltpu.roll(a, 8 - k, 0), 1.0)
        br = jnp.where(rows < 8 - k, pltpu.roll(b, 8 - k, 0), 0.0)
        b = b + a * br
        a = a * ar
    return a, b


def _lru_gates(u, pa, px, ba, bx, c_lam):
    r = _sig(pa + ba)
    i = _sig(px + bx)
    la = LRU_C * r * c_lam
    em = _expm1(2.0 * la)
    return r, i, jnp.exp(la), em, jnp.sqrt(-em)


def _neg_softplus_neg(lam):
    e = jnp.exp(-jnp.abs(lam))
    u = 1.0 + e
    l1p = jnp.where(u == 1.0, e, jnp.log(u) * (e / jnp.where(u == 1.0, 1.0, u - 1.0)))
    return -(jnp.maximum(-lam, 0.0) + l1p)


def lru_fwd(cfg, u, pa, px, ug, ba, bx, lam, name, side=None):
    def body(u_ref, pa_ref, px_ref, g_ref, ba_ref, bx_ref, lam_ref, hs_ref, hp_ref, y_ref, car_ref, *, tt, tc, nt):
        @pl.when(pl.program_id(1) == 0)
        def _():
            car_ref[...] = jnp.zeros_like(car_ref)

        rows = lax.broadcasted_iota(jnp.int32, (8, tc), 0)
        ba, bx = ba_ref[...], bx_ref[...]
        c_lam = _neg_softplus_neg(lam_ref[...])

        def step(it, h):
            ys = []
            for half in range(2):
                rs = pl.ds(pl.multiple_of(it * 16 + half * 8, 8), 8)
                uu = u_ref[rs, :]
                r, i, a, em, s = _lru_gates(uu, pa_ref[rs, :], px_ref[rs, :], ba, bx, c_lam)
                acum, b = _scan8(a, uu * i * s, rows)
                hs = acum * h + b
                hs_ref[rs, :] = hs
                hp_ref[rs, :] = jnp.where(rows >= 1, pltpu.roll(hs, 1, 0), h)
                g = g_ref[rs, :]
                ys.append(hs * (g * _sig(g)))
                h = jnp.broadcast_to(hs[7:8, :], (8, tc))
            y_ref[pl.ds(pl.multiple_of(it * 16, 16), 16), :] = jnp.concatenate(ys, axis=0).astype(BF16)
            return h

        car_ref[...] = lax.fori_loop(0, tt // 16, step, car_ref[...])

    return _time_call(cfg, body, [(u, 0), (pa, 0), (px, 0), (ug, 1), (ba,), (bx,), (lam,)], [F32, F32, BF16], 0,
                      lambda tt, tc: [(8, tc)], name, side=side)


def lru_bwd(cfg, dy, ug, hs, hprev, u, pa, px, ba, bx, lam, name):
    def body(dy_ref, g_ref, hs_ref, hp_ref, u_ref, pa_ref, px_ref, ba_ref, bx_ref, lam_ref,
             dg_ref, dpa_ref, dpx_ref, du_ref, acc_ref, ecar_ref, acar_ref, *, tt, tc, nt):
        @pl.when(pl.program_id(1) == 0)
        def _():
            ecar_ref[...] = jnp.zeros_like(ecar_ref)
            acar_ref[...] = jnp.zeros_like(acar_ref)
            acc_ref[...] = jnp.zeros_like(acc_ref)

        rows = lax.broadcasted_iota(jnp.int32, (8, tc), 0)
        ba, bx = ba_ref[...], bx_ref[...]
        c_lam = _neg_softplus_neg(lam_ref[...])

        def step(it, carry):
            ecar, acar, s_lam, s_ba, s_bx = carry
            jt = tt // 16 - 1 - it
            dgs, dpas, dpxs = [None, None], [None, None], [None, None]
            for half in (1, 0):
                rs = pl.ds(pl.multiple_of(jt * 16 + half * 8, 8), 8)
                uu = u_ref[rs, :]
                r, i, a, em, s = _lru_gates(uu, pa_ref[rs, :], px_ref[rs, :], ba, bx, c_lam)
                g = g_ref[rs, :]
                sg = _sig(g)
                dy = dy_ref[rs, :]
                dgs[half] = dy * hs_ref[rs, :] * (sg * (1.0 + g * (1.0 - sg)))
                a_next = jnp.where(rows < 7, pltpu.roll(a, 7, 0), acar)
                acum, e = _rscan8(a_next, dy * (g * sg), rows)
                e = e + acum * ecar
                ecar = jnp.broadcast_to(e[0:1, :], (8, tc))
                acar = jnp.broadcast_to(a[0:1, :], (8, tc))
                du_ref[rs, :] = e * i * s
                d_la = e * hp_ref[rs, :] * a - (e * uu * i) * ((em + 1.0) / s)
                s_lam = s_lam + d_la * (LRU_C * r)
                d_pa = d_la * (LRU_C * c_lam) * r * (1.0 - r)
                d_px = e * uu * s * i * (1.0 - i)
                s_ba = s_ba + d_pa
                s_bx = s_bx + d_px
                dpas[half], dpxs[half] = d_pa, d_px
            r16 = pl.ds(pl.multiple_of(jt * 16, 16), 16)
            dg_ref[r16, :] = jnp.concatenate(dgs, axis=0).astype(BF16)
            dpa_ref[r16, :] = jnp.concatenate(dpas, axis=0).astype(BF16)
            dpx_ref[r16, :] = jnp.concatenate(dpxs, axis=0).astype(BF16)
            return ecar, acar, s_lam, s_ba, s_bx

        z = jnp.zeros((8, tc), F32)
        ecar, acar, s_lam, s_ba, s_bx = lax.fori_loop(0, tt // 16, step, (ecar_ref[...], acar_ref[...], z, z, z))
        ecar_ref[...] = ecar
        acar_ref[...] = acar
        acc_ref[pl.ds(0, 1), :] += jnp.sum(s_lam, axis=0, keepdims=True) * _sig(-lam_ref[...])
        acc_ref[pl.ds(1, 1), :] += jnp.sum(s_ba, axis=0, keepdims=True)
        acc_ref[pl.ds(2, 1), :] += jnp.sum(s_bx, axis=0, keepdims=True)

    return _time_call(cfg, body, [(dy, 0), (ug, 1), (hs, 0), (hprev, 0), (u, 0), (pa, 0), (px, 0), (ba,), (bx,), (lam,)],
                      [BF16, BF16, BF16, F32], 1, lambda tt, tc: [(8, tc), (8, tc)], name, reverse=True)


def _pool_select(cfg, grp, fn):
    for gi, w in enumerate(cfg.pool_windows):
        @pl.when(grp == gi)
        def _(w=w):
            fn(w)


def pool_fwd(cfg, ug, name):
    ng = len(cfg.pool_windows)

    def body(x_ref, p_ref, s_ref, *, tt, tc, nt):
        t = pl.program_id(1)
        grp = pl.program_id(0) // (cfg.d // ng // tc)
        x = x_ref[...]
        _push_history(s_ref, x, t, tt)
        row1 = (t * tt + 1 + lax.broadcasted_iota(jnp.int32, (tt, 1), 0)).astype(F32)

        def write(w):
            ws = x
            for j in range(1, w):
                ws = ws + s_ref[pl.ds(HIST - j, tt), :]
            p_ref[...] = (ws / jnp.minimum(row1, float(w)) - x).astype(BF16)

        _pool_select(cfg, grp, write)

    return _time_call(cfg, body, [(ug, 0)], [BF16], 0, lambda tt, tc: [(tt + HIST, tc)], name, groups=ng)[0]


def pool_bwd(cfg, dp, name):
    ng = len(cfg.pool_windows)

    def body(dp_ref, du_ref, s_ref, *, tt, tc, nt):
        t = pl.program_id(1)
        grp = pl.program_id(0) // (cfg.d // ng // tc)
        dp = dp_ref[...]
        row1 = ((nt - 1 - t) * tt + 1 + lax.broadcasted_iota(jnp.int32, (tt, 1), 0)).astype(F32)

        def write(w):
            dm = dp / jnp.minimum(row1, float(w))
            _push_future(s_ref, dm, t, tt)
            ws = dm
            for j in range(1, w):
                ws = ws + s_ref[pl.ds(j, tt), :]
            du_ref[...] = (ws - dp).astype(BF16)

        _pool_select(cfg, grp, write)

    return _time_call(cfg, body, [(dp, 0)], [BF16], 0, lambda tt, tc: [(tt + HIST, tc)], name, reverse=True, groups=ng)[0]


def gate_fwd(cfg, v, gsrc, gblk, scale, name):
    d = cfg.d

    def body(*refs):
        if scale is None:
            v_ref, g_ref, y_ref = refs
            v = v_ref[...]
        else:
            v_ref, g_ref, s_ref, y_ref = refs
            v = v_ref[...] * s_ref[...]
        g = g_ref[...]
        y_ref[...] = (v * (g * _sig(g))).astype(BF16)

    ins = [(v, d, 0), (gsrc, d, gblk)] + ([] if scale is None else [(scale, None)])
    return _rowwise(cfg, body, ins, [(d, BF16)], name)[0]


def gate_bwd(cfg, dy, v, gsrc, gblk, scale, dv_dtype, name):
    d = cfg.d

    def body(*refs):
        if scale is None:
            dy_ref, v_ref, g_ref, dv_ref, dg_ref = refs
            vs = v_ref[...]
        else:
            dy_ref, v_ref, g_ref, s_ref, dv_ref, dg_ref, acc_ref = refs
            vs = v_ref[...] * s_ref[...]
        g = g_ref[...]
        sg = _sig(g)
        dy = dy_ref[...]
        dvs = dy * (g * sg)
        dg_ref[...] = (dy * vs * (sg * (1.0 + g * (1.0 - sg)))).astype(BF16)
        if scale is None:
            dv_ref[...] = dvs.astype(dv_dtype)
        else:
            @pl.when(pl.program_id(0) == 0)
            def _():
                acc_ref[...] = jnp.zeros_like(acc_ref)

            acc_ref[...] += _colsum8(dvs * v_ref[...])
            dv_ref[...] = (dvs * s_ref[...]).astype(dv_dtype)

    ins = [(dy, d, 0), (v, d, 0), (gsrc, d, gblk)] + ([] if scale is None else [(scale, None)])
    outs = [(d, dv_dtype), (d, BF16)] + ([] if scale is None else [(d,)])
    return _rowwise(cfg, body, ins, outs, name, n_acc=0 if scale is None else 1)


def _swap_halves(x, lo, half):
    w = x.shape[1]
    lane = lax.broadcasted_iota(jnp.int32, x.shape, 1)
    sw = jnp.where(lane < lo + half, pltpu.roll(x, w - half, 1), pltpu.roll(x, half, 1))
    return jnp.where((lane >= lo) & (lane < lo + 2 * half), sw, 0.0)


def mla_norm(cfg, proj, qn, kvn, name):
    ql, kl = cfg.q_lora, cfg.kv_lora

    def body(c_ref, qn_ref, kn_ref, q_ref, k_ref):
        cq = c_ref[:, 0:ql]
        ck = c_ref[:, ql:ql + kl]
        q_ref[...] = (cq * lax.rsqrt(jnp.mean(cq * cq, axis=-1, keepdims=True) + RMS_EPS) * qn_ref[...]).astype(BF16)
        k_ref[...] = (ck * lax.rsqrt(jnp.mean(ck * ck, axis=-1, keepdims=True) + RMS_EPS) * kn_ref[...]).astype(BF16)

    return _rowwise(cfg, body, [(proj, cfg.cw, cfg.d // cfg.cw), (qn, None), (kvn, None)], [(ql, BF16), (kl, BF16)], name)


def mla_norm_bwd(cfg, proj, d_cqn, d_ckvn, d_kr, qn, kvn, name):
    ql, kl, cw, npe = cfg.q_lora, cfg.kv_lora, cfg.cw, cfg.nope

    def one(x, dy, gamma):
        rstd = lax.rsqrt(jnp.mean(x * x, axis=-1, keepdims=True) + RMS_EPS)
        xn = x * rstd
        dxn = dy * gamma
        return rstd * (dxn - xn * jnp.mean(dxn * xn, axis=-1, keepdims=True)), dy * xn

    def body(c_ref, dq_ref, dk_ref, dkr_ref, qn_ref, kn_ref, dc_ref, acc_ref):
        @pl.when(pl.program_id(0) == 0)
        def _():
            acc_ref[...] = jnp.zeros_like(acc_ref)

        dcq, gq = one(c_ref[:, 0:ql], dq_ref[...], qn_ref[...])
        dck, gk = one(c_ref[:, ql:ql + kl], dk_ref[...], kn_ref[...])
        dc_ref[:, 0:ql] = dcq.astype(BF16)
        dc_ref[:, ql:ql + kl] = dck.astype(BF16)
        dc_ref[:, ql + kl:ql + kl + npe] = dkr_ref[...].astype(BF16)
        rest = cw - (ql + kl + npe)
        if rest:
            dc_ref[:, ql + kl + npe:cw] = jnp.zeros((dc_ref.shape[0], rest), BF16)
        acc_ref[:, 0:ql] += _colsum8(gq)
        acc_ref[:, ql:ql + kl] += _colsum8(gk)

    return _rowwise(cfg, body, [(proj, cw, cfg.d // cw), (d_cqn, ql, 0), (d_ckvn, kl, 0), (d_kr, npe, 0), (qn, None), (kvn, None)],
                    [(cw, BF16), (cw,)], name, n_acc=1)


def _mla_tables(cfg):
    r2, npe = cfg.rope // 2, cfg.nope
    inv = ROPE_BASE ** (-jnp.arange(0, cfg.rope, 2, dtype=F32) / cfg.rope)
    ang = jnp.arange(cfg.tp, dtype=F32)[:, None] * inv[None, :]
    cos, sin = jnp.cos(ang), jnp.sin(ang)
    one = jnp.ones((cfg.tp, npe - cfg.rope), F32)
    ck = jnp.concatenate([cos, cos, one], axis=1)
    sk = jnp.concatenate([-sin, sin, 0.0 * one], axis=1)
    cq = jnp.concatenate([jnp.ones((cfg.tp, npe), F32), ck], axis=1)
    sq = jnp.concatenate([jnp.zeros((cfg.tp, npe), F32), sk], axis=1)
    return cq, sq, ck, sk


def mla_prep(cfg, q_raw, kv_raw, proj, tabs, name):
    h, npe, r2, ql, kl = cfg.mla_heads, cfg.nope, cfg.rope // 2, cfg.q_lora, cfg.kv_lora
    hp = 2 * npe

    def body(q_ref, kv_ref, c_ref, cq_ref, sq_ref, ck_ref, sk_ref, qo_ref, ko_ref, vo_ref):
        kr = c_ref[:, ql + kl:ql + kl + npe]
        kr = (kr * ck_ref[...] + _swap_halves(kr, 0, r2) * sk_ref[...]).astype(BF16)
        cq, sq = cq_ref[...], sq_ref[...]
        for i in range(h):
            q = q_ref[:, i * hp:(i + 1) * hp]
            qo_ref[:, i * hp:(i + 1) * hp] = (q * cq + _swap_halves(q, npe, r2) * sq).astype(BF16)
            ko_ref[:, i * hp:i * hp + npe] = kv_ref[:, i * npe:(i + 1) * npe].astype(BF16)
            ko_ref[:, i * hp + npe:(i + 1) * hp] = kr
        vo_ref[...] = kv_ref[:, h * npe:2 * h * npe].astype(BF16)

    cq, sq, ck, sk = tabs
    return _rowwise(cfg, body, [(q_raw, h * hp, 0), (kv_raw, 2 * h * npe, 0), (proj, cfg.cw, cfg.d // cfg.cw),
                                (cq, hp, 0), (sq, hp, 0), (ck, npe, 0), (sk, npe, 0)],
                    [(h * hp, BF16), (h * hp, BF16), (h * npe, BF16)], name)


def mla_prep_bwd(cfg, dq_full, dk_full, tabs, name, side=None):
    h, npe, r2 = cfg.mla_heads, cfg.nope, cfg.rope // 2
    hp = 2 * npe

    def body(dq_ref, dk_ref, cq_ref, sq_ref, ck_ref, sk_ref, dqo_ref, dko_ref, dkr_ref):
        cq, sq = cq_ref[...], sq_ref[...]
        dkr = None
        for i in range(h):
            dq = dq_ref[:, i * hp:(i + 1) * hp]
            dqo_ref[:, i * hp:(i + 1) * hp] = (dq * cq + _swap_halves(dq * sq, npe, r2)).astype(BF16)
            dko_ref[:, i * npe:(i + 1) * npe] = dk_ref[:, i * hp:i * hp + npe].astype(BF16)
            part = dk_ref[:, i * hp + npe:(i + 1) * hp]
            dkr = part if dkr is None else dkr + part
        dkr_ref[...] = dkr * ck_ref[...] + _swap_halves(dkr * sk_ref[...], 0, r2)

    cq, sq, ck, sk = tabs
    return _rowwise(cfg, body, [(dq_full, h * hp, 0), (dk_full, h * hp, 0), (cq, hp, 0), (sq, hp, 0), (ck, npe, 0), (sk, npe, 0)],
                    [(h * hp, BF16), (h * npe, BF16), (npe, F32)], name, side=side)


def _attn_scores(cfg, q, k, diagonal):
    s = lax.dot_general(q, k, _NT, preferred_element_type=F32) * ((cfg.nope + cfg.rope) ** -0.5)
    if diagonal:
        row = lax.broadcasted_iota(jnp.int32, s.shape, 0)
        col = lax.broadcasted_iota(jnp.int32, s.shape, 1)
        s = jnp.where(row >= col, s, -1e30)
    return s


def _tile_rows(i, t):
    return pl.ds(pl.multiple_of(i * t, t), t)


def attn_fwd(cfg, q, k, v, name, side=None):
    h, npe, tq, wide = cfg.mla_heads, cfg.nope, cfg.tq, cfg.tkw
    hp, n = 2 * npe, cfg.tp // tq

    def body(q_ref, k_ref, v_ref, o_ref, lse_ref):
        qi = pl.program_id(1)
        q = q_ref[...]

        def tile(ki, carry, width, diagonal):
            m, l, acc = carry
            rk = _tile_rows(ki, width)
            s = _attn_scores(cfg, q, k_ref[rk, :], diagonal)
            m_new = jnp.maximum(m, jnp.max(s, axis=-1, keepdims=True))
            alpha = jnp.exp(m - m_new)
            p = jnp.exp(s - m_new)
            l = alpha * l + jnp.sum(p, axis=-1, keepdims=True)
            acc = alpha * acc + lax.dot_general(p.astype(BF16), v_ref[rk, :], _NN, preferred_element_type=F32)
            return m_new, l, acc

        init = (jnp.full((tq, 1), -1e30, F32), jnp.zeros((tq, 1), F32), jnp.zeros((tq, npe), F32))
        n_wide = qi // wide
        carry = lax.fori_loop(0, n_wide, lambda j, c: tile(j, c, wide * tq, False), init)
        carry = lax.fori_loop(n_wide * wide, qi, lambda ki, c: tile(ki, c, tq, False), carry)
        m, l, acc = tile(qi, carry, tq, True)
        o_ref[...] = acc / l
        lse_ref[...] = m + jnp.log(l)

    outs, side_outs = _host_call(
        body, (h, n),
        [pl.BlockSpec((tq, hp), lambda hh, qi: (qi, hh)),
         pl.BlockSpec((cfg.tp, hp), lambda hh, qi: (0, hh)),
         pl.BlockSpec((cfg.tp, npe), lambda hh, qi: (0, hh))],
        [pl.BlockSpec((tq, npe), lambda hh, qi: (qi, hh)),
         pl.BlockSpec((None, tq, 1), lambda hh, qi: (hh, qi, 0))],
        [jax.ShapeDtypeStruct((cfg.tp, h * npe), F32), jax.ShapeDtypeStruct((h, cfg.tp, 1), F32)],
        [], ("parallel", "arbitrary"), name, (q, k, v), side)
    return outs if side is None else (outs, side_outs)


def attn_bwd(cfg, q, k, v, do, o, lse, name, side=None):
    h, npe, tq, kw = cfg.mla_heads, cfg.nope, cfg.tq, cfg.tkw
    hp, n = 2 * npe, cfg.tp // tq
    sc = (cfg.nope + cfg.rope) ** -0.5

    def body(q_ref, k_ref, v_ref, do_ref, o_ref, lse_ref, dq_ref, dk_ref, dv_ref, dv_acc):
        dk_ref[...] = jnp.zeros_like(dk_ref)
        dv_acc[...] = jnp.zeros_like(dv_acc)

        def q_tile(qi, _):
            rq = _tile_rows(qi, tq)
            q, do, lse = q_ref[rq, :], do_ref[rq, :], lse_ref[rq, :]
            delta = jnp.sum(do.astype(F32) * o_ref[rq, :], axis=-1, keepdims=True)

            def tile(ki, dq, width, diagonal):
                rk = _tile_rows(ki, width)
                kk = k_ref[rk, :]
                p = jnp.exp(_attn_scores(cfg, q, kk, diagonal) - lse)
                dp = lax.dot_general(do, v_ref[rk, :], _NT, preferred_element_type=F32)
                ds = (p * (dp - delta) * sc).astype(BF16)
                dv_acc[rk, :] += lax.dot_general(p.astype(BF16), do, _TN, preferred_element_type=F32)
                dk_ref[rk, :] += lax.dot_general(ds, q, _TN, preferred_element_type=F32)
                return dq + lax.dot_general(ds, kk, _NN, preferred_element_type=F32)

            n_wide = qi // kw
            dq = lax.fori_loop(0, n_wide, lambda j, acc: tile(j, acc, kw * tq, False), jnp.zeros((tq, hp), F32))
            dq = lax.fori_loop(n_wide * kw, qi, lambda ki, acc: tile(ki, acc, tq, False), dq)
            dq_ref[rq, :] = tile(qi, dq, tq, True)
            return 0

        lax.fori_loop(0, n, q_tile, 0)
        dv_ref[...] = dv_acc[...].astype(BF16)

    wide = pl.BlockSpec((cfg.tp, hp), lambda hh: (0, hh))
    narrow = pl.BlockSpec((cfg.tp, npe), lambda hh: (0, hh))
    outs, side_outs = _host_call(
        body, (h,), [wide, wide, narrow, narrow, narrow, pl.BlockSpec((None, cfg.tp, 1), lambda hh: (hh, 0, 0))],
        [wide, wide, narrow],
        [jax.ShapeDtypeStruct((cfg.tp, h * hp), F32), jax.ShapeDtypeStruct((cfg.tp, h * hp), F32),
         jax.ShapeDtypeStruct((cfg.tp, h * npe), BF16)],
        [pltpu.VMEM((cfg.tp, npe), F32)], ("parallel",), name, (q, k, v, do, o, lse), side)
    return outs if side is None else (outs, side_outs)


def _ret_tables(cfg):
    c = cfg.chunk
    lg = jnp.log(1.0 - 2.0 ** (-5.0 - jnp.arange(cfg.ret_heads, dtype=F32)))[:, None, None]
    j = jnp.arange(c, dtype=F32)
    diff = j[:, None] - j[None, :]
    dm = jnp.where(diff >= 0, jnp.exp(jnp.maximum(diff, 0.0)[None] * lg), 0.0)
    lq = jnp.exp((j + 1.0)[None, :, None] * lg)
    lk = jnp.exp((c - 1.0 - j)[None, :, None] * lg)
    gc = jnp.exp(c * lg)
    return dm, lq, lk, gc


def _ret_rope_tables(cfg):
    dk = cfg.d // cfg.ret_heads
    inv = ROPE_BASE ** (-jnp.arange(0, dk, 2, dtype=F32) / dk)
    ang = jnp.arange(cfg.tp, dtype=F32)[:, None] * inv[None, :]
    return jnp.cos(ang), jnp.sin(ang)


def ret_prep(cfg, qkvg, cos, sin, name):
    d, h = cfg.d, cfg.ret_heads
    dk = d // h
    hd = dk // 2
    ksc = dk ** -0.5

    def body(q_ref, k_ref, v_ref, c_ref, s_ref, qo_ref, ko_ref, vo_ref):
        c, s = c_ref[...], s_ref[...]
        for i in range(h):
            for src, dst, f in ((q_ref, qo_ref, 1.0), (k_ref, ko_ref, ksc)):
                x1 = src[:, i * dk:i * dk + hd]
                x2 = src[:, i * dk + hd:(i + 1) * dk]
                dst[:, i * dk:i * dk + hd] = ((x1 * c - x2 * s) * f).astype(BF16)
                dst[:, i * dk + hd:(i + 1) * dk] = ((x2 * c + x1 * s) * f).astype(BF16)
        vo_ref[...] = v_ref[...].astype(BF16)

    return _rowwise(cfg, body, [(qkvg, d, 0), (qkvg, d, 1), (qkvg, d, 2), (cos, hd, 0), (sin, hd, 0)],
                    [(d, BF16), (d, BF16), (d, BF16)], name)


def ret_prep_bwd(cfg, dq, dk_, dv, dg, cos, sin, name):
    d, h = cfg.d, cfg.ret_heads
    dk = d // h
    hd = dk // 2
    ksc = dk ** -0.5

    def body(dq_ref, dk_ref, dv_ref, dg_ref, c_ref, s_ref, o_ref):
        c, s = c_ref[...], s_ref[...]
        for i in range(h):
            for src, off, f in ((dq_ref, 0, 1.0), (dk_ref, d, ksc)):
                y1 = src[:, i * dk:i * dk + hd]
                y2 = src[:, i * dk + hd:(i + 1) * dk]
                o_ref[:, off + i * dk:off + i * dk + hd] = ((y1 * c + y2 * s) * f).astype(BF16)
                o_ref[:, off + i * dk + hd:off + (i + 1) * dk] = ((y2 * c - y1 * s) * f).astype(BF16)
        o_ref[:, 2 * d:3 * d] = dv_ref[...]
        o_ref[:, 3 * d:4 * d] = dg_ref[...]

    return _rowwise(cfg, body, [(dq, d, 0), (dk_, d, 0), (dv, d, 0), (dg, d, 0), (cos, hd, 0), (sin, hd, 0)],
                    [(4 * d, BF16)], name)[0]


def _ret_specs(cfg, reverse):
    c, h = cfg.chunk, cfg.ret_heads
    dk = cfg.d // h
    n = cfg.tp // c
    tm = (lambda t: n - 1 - t) if reverse else (lambda t: t)
    blk = pl.BlockSpec((c, dk), lambda hh, t: (tm(t), hh))
    row = pl.BlockSpec((None, c, 1), lambda hh, t: (hh, tm(t), 0))
    tabs = [pl.BlockSpec((None, c, c), lambda hh, t: (hh, 0, 0)), pl.BlockSpec((None, c, 1), lambda hh, t: (hh, 0, 0)),
            pl.BlockSpec((None, c, 1), lambda hh, t: (hh, 0, 0)), pl.BlockSpec((None, 1, 1), lambda hh, t: (hh, 0, 0))]
    return blk, row, tabs, n, dk


def _ret_state_update(s_ref, k, v, lk, gc):
    kw = (k.astype(F32) * lk).astype(BF16)
    s_ref[...] = gc * s_ref[...] + lax.dot_general(kw, v, _TN, preferred_element_type=F32)


def ret_fwd(cfg, q, k, v, tabs, name):
    blk, row, tspecs, n, dk = _ret_specs(cfg, False)

    def body(q_ref, k_ref, v_ref, dm_ref, lq_ref, lk_ref, gc_ref, on_ref, rs_ref, s_ref):
        @pl.when(pl.program_id(1) == 0)
        def _():
            s_ref[...] = jnp.zeros_like(s_ref)

        qq, kk, vv = q_ref[...], k_ref[...], v_ref[...]
        a = lax.dot_general(qq, kk, _NT, preferred_element_type=F32) * dm_ref[...]
        o = lax.dot_general(a.astype(BF16), vv, _NN, preferred_element_type=F32)
        o = o + lax.dot_general(qq, s_ref[...].astype(BF16), _NN, preferred_element_type=F32) * lq_ref[...]
        _ret_state_update(s_ref, kk, vv, lk_ref[...], gc_ref[...])
        mu = jnp.mean(o, axis=-1, keepdims=True)
        oc = o - mu
        rstd = lax.rsqrt(jnp.mean(oc * oc, axis=-1, keepdims=True) + LN_EPS)
        on_ref[...] = oc * rstd
        rs_ref[...] = rstd

    return pl.pallas_call(body, grid=(cfg.ret_heads, n), in_specs=[blk, blk, blk] + tspecs, out_specs=[blk, row],
                          out_shape=[jax.ShapeDtypeStruct((cfg.tp, cfg.d), F32), jax.ShapeDtypeStruct((cfg.ret_heads, cfg.tp, 1), F32)],
                          scratch_shapes=[pltpu.VMEM((dk, dk), F32)], compiler_params=_cparams("parallel", "arbitrary"),
                          name=name)(q, k, v, *tabs)


def ret_bwd_q(cfg, q, k, v, d_on, on, rstd, tabs, name):
    blk, row, tspecs, n, dk = _ret_specs(cfg, False)

    def body(q_ref, k_ref, v_ref, don_ref, on_ref, rs_ref, dm_ref, lq_ref, lk_ref, gc_ref, do_ref, dq_ref, s_ref):
        @pl.when(pl.program_id(1) == 0)
        def _():
            s_ref[...] = jnp.zeros_like(s_ref)

        kk, vv = k_ref[...], v_ref[...]
        don, on = don_ref[...], on_ref[...]
        do = rs_ref[...] * (don - jnp.mean(don, axis=-1, keepdims=True) - on * jnp.mean(don * on, axis=-1, keepdims=True))
        dob = do.astype(BF16)
        do_ref[...] = dob
        ds = lax.dot_general(dob, vv, _NT, preferred_element_type=F32) * dm_ref[...]
        dq = lax.dot_general(ds.astype(BF16), kk, _NN, preferred_element_type=F32)
        dq_ref[...] = dq + lax.dot_general(dob, s_ref[...].astype(BF16), _NT, preferred_element_type=F32) * lq_ref[...]
        _ret_state_update(s_ref, kk, vv, lk_ref[...], gc_ref[...])

    return pl.pallas_call(body, grid=(cfg.ret_heads, n), in_specs=[blk, blk, blk, blk, blk, row] + tspecs, out_specs=[blk, blk],
                          out_shape=[jax.ShapeDtypeStruct((cfg.tp, cfg.d), BF16), jax.ShapeDtypeStruct((cfg.tp, cfg.d), F32)],
                          scratch_shapes=[pltpu.VMEM((dk, dk), F32)], compiler_params=_cparams("parallel", "arbitrary"),
                          name=name)(q, k, v, d_on, on, rstd, *tabs)


def ret_bwd_kv(cfg, q, k, v, do, tabs, name):
    blk, row, tspecs, n, dk = _ret_specs(cfg, True)

    def body(q_ref, k_ref, v_ref, do_ref, dm_ref, lq_ref, lk_ref, gc_ref, dk_ref, dv_ref, g_ref):
        @pl.when(pl.program_id(1) == 0)
        def _():
            g_ref[...] = jnp.zeros_like(g_ref)

        qq, kk, vv, dob = q_ref[...], k_ref[...], v_ref[...], do_ref[...]
        dm, lk = dm_ref[...], lk_ref[...]
        gb = g_ref[...].astype(BF16)
        a = lax.dot_general(qq, kk, _NT, preferred_element_type=F32) * dm
        ds = lax.dot_general(dob, vv, _NT, preferred_element_type=F32) * dm
        dkk = lax.dot_general(ds.astype(BF16), qq, _TN, preferred_element_type=F32)
        dk_ref[...] = dkk + lax.dot_general(vv, gb, _NT, preferred_element_type=F32) * lk
        kw = (kk.astype(F32) * lk).astype(BF16)
        dvv = lax.dot_general(a.astype(BF16), dob, _TN, preferred_element_type=F32)
        dv_ref[...] = (dvv + lax.dot_general(kw, gb, _NN, preferred_element_type=F32)).astype(BF16)
        qw = (qq.astype(F32) * lq_ref[...]).astype(BF16)
        g_ref[...] = gc_ref[...] * g_ref[...] + lax.dot_general(qw, dob, _TN, preferred_element_type=F32)

    return pl.pallas_call(body, grid=(cfg.ret_heads, n), in_specs=[blk, blk, blk, blk] + tspecs, out_specs=[blk, blk],
                          out_shape=[jax.ShapeDtypeStruct((cfg.tp, cfg.d), F32), jax.ShapeDtypeStruct((cfg.tp, cfg.d), BF16)],
                          scratch_shapes=[pltpu.VMEM((dk, dk), F32)], compiler_params=_cparams("parallel", "arbitrary"),
                          name=name)(q, k, v, do, *tabs)


def _flat2(a):
    return a.reshape(-1, a.shape[-1])


def _ew_call(body, ins, out_dtypes, name, side=None):
    r, c = ins[-1].shape[-2:]
    tr = _div_tile(r, max(16, (1 << 18) // c), 16)
    specs = []
    for a in ins:
        if a.ndim == 3:
            specs.append(pl.BlockSpec((a.shape[0], tr, c), lambda i: (0, i, 0)))
        else:
            specs.append(pl.BlockSpec((tr, c), lambda i: (i, 0)))
    outs, side_outs = _host_call(body, (r // tr,), specs, [pl.BlockSpec((tr, c), lambda i: (i, 0)) for _ in out_dtypes],
                                 [jax.ShapeDtypeStruct((r, c), dt) for dt in out_dtypes], [], ("parallel",), name, ins, side)
    return outs if side is None else (outs, side_outs)


def adamw(w, g, m, v, name, side=None):
    c1 = 1.0 - ADAM_B1 ** ADAM_STEP
    c2 = 1.0 - ADAM_B2 ** ADAM_STEP

    def body(w_ref, g_ref, m_ref, v_ref, d_ref, mo_ref, vo_ref):
        gg = g_ref[...]
        mn = ADAM_B1 * m_ref[...] + (1.0 - ADAM_B1) * gg
        vn = ADAM_B2 * v_ref[...] + (1.0 - ADAM_B2) * (gg * gg)
        d_ref[...] = -ADAM_LR * ((mn / c1) / (jnp.sqrt(vn / c2) + ADAM_EPS) + ADAM_WD * w_ref[...])
        mo_ref[...] = mn
        vo_ref[...] = vn

    got = _ew_call(body, [_flat2(w), _flat2(g), _flat2(m), _flat2(v)], [F32, F32, F32], name, side)
    outs, side_outs = (got, None) if side is None else got
    outs = [o.reshape(w.shape) for o in outs]
    return outs if side is None else (outs, side_outs)


def add_pair_bf16(g, theirs, core, name):
    c = g.shape[-1]
    g4, t3 = g.reshape(4, 2, -1, c), theirs.reshape(4, -1, c)
    rows = t3.shape[1]
    tr = _div_tile(rows, max(16, (1 << 18) // c), 16)

    def body(core_ref, g_ref, t_ref, o_ref):
        o_ref[...] = (g_ref[...] + t_ref[...]).astype(BF16)

    spec = pl.BlockSpec((None, tr, c), lambda q, i, core_ref: (q, i, 0))
    grid_spec = pltpu.PrefetchScalarGridSpec(
        num_scalar_prefetch=1, grid=(4, rows // tr),
        in_specs=[pl.BlockSpec((None, None, tr, c), lambda q, i, core_ref: (q, core_ref[0], i, 0)), spec], out_specs=spec)
    out = pl.pallas_call(body, grid_spec=grid_spec, out_shape=jax.ShapeDtypeStruct((4, rows, c), BF16),
                         compiler_params=_cparams("parallel", "parallel"), name=name)(core, g4, t3)
    return out.reshape(theirs.shape)


def sum_slots(parts, landed, chip_core, name):
    c = parts.shape[-1]
    p3, l3 = parts.reshape(4, -1, c), landed.reshape(3, -1, c)
    rows = p3.shape[1]
    tr = _div_tile(rows, max(16, (1 << 18) // c), 16)

    def body(cc_ref, p_ref, l_ref, o_ref):
        acc = p_ref[...].astype(F32)
        for s in range(3):
            acc = acc + l_ref[s].astype(F32)
        o_ref[...] = acc

    grid_spec = pltpu.PrefetchScalarGridSpec(
        num_scalar_prefetch=1, grid=(rows // tr,),
        in_specs=[pl.BlockSpec((None, tr, c), lambda i, cc: (cc[0], i, 0)), pl.BlockSpec((3, tr, c), lambda i, cc: (0, i, 0))],
        out_specs=pl.BlockSpec((None, tr, c), lambda i, cc: (cc[1], i, 0)))
    out = pl.pallas_call(body, grid_spec=grid_spec, out_shape=jax.ShapeDtypeStruct((2, rows, c), F32),
                         compiler_params=_cparams("parallel"), name=name)(chip_core, p3, l3)
    return out.reshape((2,) + parts.shape[2:])


def cast_place(w, chip, name):
    c = w.shape[-1]
    w2 = w.reshape(-1, c)
    rows = w2.shape[0]
    tr = _div_tile(rows, max(16, (1 << 18) // c), 16)

    def body(chip_ref, w_ref, o_ref):
        o_ref[...] = w_ref[...].astype(BF16)

    grid_spec = pltpu.PrefetchScalarGridSpec(
        num_scalar_prefetch=1, grid=(rows // tr,),
        in_specs=[pl.BlockSpec((tr, c), lambda i, chip_ref: (i, 0))],
        out_specs=pl.BlockSpec((None, tr, c), lambda i, chip_ref: (chip_ref[0], i, 0)))
    out = pl.pallas_call(body, grid_spec=grid_spec, out_shape=jax.ShapeDtypeStruct((4, rows, c), BF16),
                         compiler_params=_cparams("parallel"), name=name)(chip, w2)
    return out.reshape((4,) + w.shape)


def _coords():
    return lax.axis_index("x"), lax.axis_index("y"), lax.axis_index("c")


def _chip_peer(x, y, k):
    return (x ^ (k >> 1), y ^ (k & 1))


_ANY = pl.BlockSpec(memory_space=pl.ANY)
DMA_PIECE_BYTES = 2 << 20
DMA_MAX_PIECES = 32


def _piece_plan(shape, itemsize):
    want = max(1, min(DMA_MAX_PIECES, math.prod(shape) * itemsize // DMA_PIECE_BYTES))
    plan = []
    for ax, n in enumerate(shape[:-1]):
        if want <= 1:
            break
        rows_tiled = ax == len(shape) - 2
        k = max([1] + [c for c in range(2, min(n, want) + 1) if n % c == 0 and (not rows_tiled or (n // c) % 16 == 0)])
        if k > 1:
            plan.append((ax, k))
            want = -(-want // k)
    return plan


class _Copy:
    def __init__(self, src, dst, send_sem, recv_sem=None, device_id=None):
        self.src, self.dst, self.send_sem, self.recv_sem, self.device_id = src, dst, send_sem, recv_sem, device_id

    def _dma(self, src, dst):
        if self.device_id is None:
            return pltpu.make_async_copy(src, dst, self.send_sem)
        return pltpu.make_async_remote_copy(src_ref=src, dst_ref=dst, send_sem=self.send_sem, recv_sem=self.recv_sem,
                                            device_id=self.device_id, device_id_type=MESH)

    def start(self):
        shape = self.src.shape
        plan = _piece_plan(shape, jnp.dtype(self.src.dtype).itemsize)
        for pick in np.ndindex(*[k for _, k in plan]):
            idx = [slice(None)] * len(shape)
            for (ax, k), i in zip(plan, pick):
                step = shape[ax] // k
                idx[ax] = pl.ds(i * step, step)
            idx = tuple(idx)
            self._dma(self.src.at[idx], self.dst.at[idx]).start()

    def wait(self):
        self._dma(self.src, self.dst).wait()

    def wait_send(self):
        self._dma(self.src, self.dst).wait_send()

    def wait_recv(self):
        self._dma(self.src, self.dst).wait_recv()


def allreduce_small(x, name):
    r, _, w = x.shape

    def body(x_ref, o_ref, mine_ref, gat_ref, send_sems, recv_sems):
        mx, my, mc = _coords()
        me = 4 * mx + 2 * my + mc
        mine_ref[...] = jnp.sum(x_ref[...], axis=1)
        gat_ref[me] = mine_ref[...]
        copies = []
        for k in range(1, 8):
            peer = (mx ^ (k >> 2), my ^ ((k >> 1) & 1), mc ^ (k & 1))
            cp = pltpu.make_async_remote_copy(src_ref=mine_ref, dst_ref=gat_ref.at[me], send_sem=send_sems.at[k - 1],
                                              recv_sem=recv_sems.at[k - 1], device_id=peer, device_id_type=MESH)
            cp.start()
            copies.append(cp)
        for k in range(1, 8):
            pltpu.make_async_remote_copy(src_ref=mine_ref, dst_ref=gat_ref.at[me ^ k], send_sem=send_sems.at[k - 1],
                                         recv_sem=recv_sems.at[k - 1], device_id=(mx, my, mc), device_id_type=MESH).wait_recv()
        for cp in copies:
            cp.wait_send()
        acc = gat_ref[0]
        for s in range(1, 8):
            acc = acc + gat_ref[s]
        o_ref[...] = acc

    return pl.pallas_call(body, out_shape=jax.ShapeDtypeStruct((r, w), F32),
                          in_specs=[pl.BlockSpec(memory_space=pltpu.VMEM)], out_specs=pl.BlockSpec(memory_space=pltpu.VMEM),
                          scratch_shapes=[pltpu.VMEM((r, w), F32), pltpu.VMEM((8, r, w), F32),
                                          pltpu.SemaphoreType.DMA((7,)), pltpu.SemaphoreType.DMA((7,))],
                          compiler_params=pltpu.CompilerParams(vmem_limit_bytes=VMEM_LIMIT_V7X), name=name)(x)


def allgather_chips(shards, name):
    n = len(shards)

    def body(*refs):
        outs = refs[n:2 * n]
        s1_send, s1_recv, s2_send, s2_recv = refs[2 * n:]
        mx, my, mc = _coords()
        p = 2 * mx + my
        sib = (mx, my, 1 - mc)
        sends = []
        for a in range(n):
            h0 = outs[a].shape[1] // 2
            mine = outs[a].at[p, pl.ds(mc * h0, h0)]
            for k in (1, 2, 3):
                qx, qy = _chip_peer(mx, my, k)
                cp = _Copy(mine, mine, s1_send.at[a, k - 1], s1_recv.at[a, k - 1], (qx, qy, mc))
                cp.start()
                sends.append(cp)
        for a in range(n):
            h0 = outs[a].shape[1] // 2
            half = pl.ds(mc * h0, h0)
            for k in (1, 2, 3):
                landed = outs[a].at[p ^ k, half]
                _Copy(landed, landed, s1_send.at[a, k - 1], s1_recv.at[a, k - 1], sib).wait_recv()
                fw = _Copy(landed, landed, s2_send.at[a, k - 1], s2_recv.at[a, k - 1], sib)
                fw.start()
                sends.append(fw)
        for a in range(n):
            h0 = outs[a].shape[1] // 2
            other = pl.ds((1 - mc) * h0, h0)
            for k in (1, 2, 3):
                theirs = outs[a].at[p ^ k, other]
                _Copy(theirs, theirs, s2_send.at[a, k - 1], s2_recv.at[a, k - 1], sib).wait_recv()
        for cp in sends:
            cp.wait_send()

    return pl.pallas_call(body, out_shape=[jax.ShapeDtypeStruct(s.shape, s.dtype) for s in shards],
                          in_specs=[_ANY] * n, out_specs=[_ANY] * n, input_output_aliases={a: a for a in range(n)},
                          scratch_shapes=[pltpu.SemaphoreType.DMA((n, 3)), pltpu.SemaphoreType.DMA((n, 3)),
                                          pltpu.SemaphoreType.DMA((n, 3)), pltpu.SemaphoreType.DMA((n, 3))],
                          name=name)(*shards)


def ag_chips_side(bufs):
    n = len(bufs)

    def each(inplace, sems, act):
        s_send, s_recv = sems
        mx, my, mc = _coords()
        p = 2 * mx + my
        for a in range(n):
            h0 = inplace[a].shape[1] // 2
            half = pl.ds(mc * h0, h0)
            mine = inplace[a].at[p, half]
            for k in (1, 2, 3):
                qx, qy = _chip_peer(mx, my, k)
                act(_Copy(mine, mine, s_send.at[a, k - 1], s_recv.at[a, k - 1], (qx, qy, mc)),
                    _Copy(inplace[a].at[p ^ k, half], inplace[a].at[p ^ k, half], s_send.at[a, k - 1], s_recv.at[a, k - 1], (qx, qy, mc)))

    def start(ins, inplace, outs, sems):
        each(inplace, sems, lambda send, landed: send.start())

    def finish(ins, inplace, outs, sems):
        each(inplace, sems, lambda send, landed: (landed.wait_recv(), send.wait_send()))

    return Side([], list(bufs), [], [pltpu.SemaphoreType.DMA((n, 3)), pltpu.SemaphoreType.DMA((n, 3))], start, finish)


def ag_forward_side(bufs):
    n = len(bufs)

    def each(inplace, sems, act):
        s_send, s_recv = sems
        mx, my, mc = _coords()
        p = 2 * mx + my
        sib = (mx, my, 1 - mc)
        for a in range(n):
            h0 = inplace[a].shape[1] // 2
            for k in (1, 2, 3):
                landed = inplace[a].at[p ^ k, pl.ds(mc * h0, h0)]
                theirs = inplace[a].at[p ^ k, pl.ds((1 - mc) * h0, h0)]
                act(_Copy(landed, landed, s_send.at[a, k - 1], s_recv.at[a, k - 1], sib),
                    _Copy(theirs, theirs, s_send.at[a, k - 1], s_recv.at[a, k - 1], sib))

    def start(ins, inplace, outs, sems):
        each(inplace, sems, lambda send, theirs: send.start())

    def finish(ins, inplace, outs, sems):
        each(inplace, sems, lambda send, theirs: (theirs.wait_recv(), send.wait_send()))

    return Side([], list(bufs), [], [pltpu.SemaphoreType.DMA((n, 3)), pltpu.SemaphoreType.DMA((n, 3))], start, finish)


def rs_chips_side(parts):
    n = len(parts)

    def copies(ins, outs, sems):
        send_sems, recv_sems = sems
        mx, my, mc = _coords()
        p = 2 * mx + my
        return [_Copy(ins[a].at[p ^ k], outs[a].at[k - 1], send_sems.at[a, k - 1], recv_sems.at[a, k - 1],
                      (*_chip_peer(mx, my, k), mc)) for a in range(n) for k in (1, 2, 3)]

    def start(ins, inplace, outs, sems):
        for cp in copies(ins, outs, sems):
            cp.start()

    def finish(ins, inplace, outs, sems):
        for cp in copies(ins, outs, sems):
            cp.wait()

    return Side(list(parts), [], [jax.ShapeDtypeStruct((3,) + g.shape[1:], g.dtype) for g in parts],
                [pltpu.SemaphoreType.DMA((n, 3)), pltpu.SemaphoreType.DMA((n, 3))], start, finish)


def rs_pair_side(grads):
    n = len(grads)

    def copies(ins, outs, sems):
        send_sems, recv_sems = sems
        mx, my, mc = _coords()
        return [_Copy(ins[a].at[:, pl.ds(1 - mc, 1)], outs[a], send_sems.at[a], recv_sems.at[a], (mx, my, 1 - mc)) for a in range(n)]

    def start(ins, inplace, outs, sems):
        for cp in copies(ins, outs, sems):
            cp.start()

    def finish(ins, inplace, outs, sems):
        for cp in copies(ins, outs, sems):
            cp.wait()

    return Side(list(grads), [], [jax.ShapeDtypeStruct((4, 1) + g.shape[2:], g.dtype) for g in grads],
                [pltpu.SemaphoreType.DMA((n,)), pltpu.SemaphoreType.DMA((n,))], start, finish)


def exchange_halves(grads, name):
    n = len(grads)
    side = rs_pair_side(grads)

    def body(*refs):
        ins, outs, sems = refs[:n], refs[n:2 * n], refs[2 * n:]
        side.start(ins, [], outs, sems)
        side.finish(ins, [], outs, sems)

    return pl.pallas_call(body, out_shape=side.outs, in_specs=[_ANY] * n, out_specs=[_ANY] * n, scratch_shapes=side.sems,
                          name=name)(*grads)


def scatter_chips(parts, name):
    n = len(parts)
    side = rs_chips_side(parts)

    def body(*refs):
        ins, outs, sems = refs[:n], refs[n:2 * n], refs[2 * n:]
        side.start(ins, [], outs, sems)
        side.finish(ins, [], outs, sems)

    return pl.pallas_call(body, out_shape=side.outs, in_specs=[_ANY] * n, out_specs=[_ANY] * n, scratch_shapes=side.sems,
                          name=name)(*parts)


def rs_join_side(halves):
    n = len(halves)

    def each(inplace, sems, act):
        send_sems, recv_sems = sems
        mx, my, mc = _coords()
        for a in range(n):
            mine, theirs = inplace[a].at[pl.ds(mc, 1)], inplace[a].at[pl.ds(1 - mc, 1)]
            act(_Copy(mine, mine, send_sems.at[a], recv_sems.at[a], (mx, my, 1 - mc)),
                _Copy(theirs, theirs, send_sems.at[a], recv_sems.at[a], (mx, my, 1 - mc)))

    def start(ins, inplace, outs, sems):
        each(inplace, sems, lambda send, theirs: send.start())

    def finish(ins, inplace, outs, sems):
        each(inplace, sems, lambda send, theirs: (send.wait_send(), theirs.wait_recv()))

    return Side([], list(halves), [], [pltpu.SemaphoreType.DMA((n,)), pltpu.SemaphoreType.DMA((n,))], start, finish)


def join_halves(halves, name):
    n = len(halves)
    side = rs_join_side(halves)

    def body(*refs):
        outs, sems = refs[n:2 * n], refs[2 * n:]
        side.start([], outs, [], sems)
        side.finish([], outs, [], sems)

    return pl.pallas_call(body, out_shape=[jax.ShapeDtypeStruct(g.shape, g.dtype) for g in halves],
                          in_specs=[_ANY] * n, out_specs=[_ANY] * n, input_output_aliases={a: a for a in range(n)},
                          scratch_shapes=side.sems, name=name)(*halves)


def rs_pair_sums(grads, tag, call=None):
    out, theirs = (None, exchange_halves(grads, f"rs_pair_{tag}")) if call is None else call(rs_pair_side(grads))
    core = lax.axis_index("c").astype(jnp.int32).reshape(1)
    return out, [add_pair_bf16(g, t, core, f"rs_add_{tag}_{i}") for i, (g, t) in enumerate(zip(grads, theirs))]


def rs_finish(parts, landed, tag, call=None):
    chip_core = jnp.stack([2 * lax.axis_index("x") + lax.axis_index("y"), lax.axis_index("c")]).astype(jnp.int32)
    halves = [sum_slots(pt, ld, chip_core, f"rs_sum_{tag}_{i}") for i, (pt, ld) in enumerate(zip(parts, landed))]
    return (None, join_halves(halves, f"rs_join_{tag}")) if call is None else call(rs_join_side(halves))


W_NAMES = ['meta_tokens', 'l0_w_in', 'l0_conv_w', 'l0_conv_b', 'l0_w_a', 'l0_b_a', 'l0_w_x', 'l0_b_x', 'l0_lam', 'l0_w_out',
           'l0_ln_g', 'l0_ln_b', 'l1_w_in', 'l1_w_grp', 'l1_scale', 'l1_w_out', 'l1_ln_g', 'l1_ln_b', 'l2_w_in', 'l2_q_norm',
           'l2_w_uq', 'l2_kv_norm', 'l2_w_ukv', 'l2_w_out', 'l2_ln_g', 'l2_ln_b', 'l3_w_in', 'l3_w_out', 'l3_ln_g', 'l3_ln_b']
BIG = {0: ['l0_w_in', 'l0_w_a', 'l0_w_x', 'l0_w_out'], 1: ['l1_w_in', 'l1_w_grp', 'l1_w_out'],
       2: ['l2_w_in', 'l2_w_uq', 'l2_w_ukv', 'l2_w_out'], 3: ['l3_w_in', 'l3_w_out']}
SMALL_SHARDED = ['meta_tokens', 'l0_conv_w']
SMALL_REPL = ['l0_conv_b', 'l0_b_a', 'l0_b_x', 'l0_lam', 'l0_ln_g', 'l0_ln_b', 'l1_scale', 'l1_ln_g', 'l1_ln_b',
              'l2_q_norm', 'l2_kv_norm', 'l2_ln_g', 'l2_ln_b', 'l3_ln_g', 'l3_ln_b']


def _rows8(rows, width):
    out = []
    for r in rows:
        r = r.reshape(-1, r.shape[-1])
        out.append(jnp.pad(r, ((0, 8 - r.shape[0]), (0, width - r.shape[1]))))
    return jnp.stack(out)


def _unblock(g4, axis):
    return jnp.concatenate([g4[i] for i in range(4)], axis=axis)


def _block(full, axis):
    return jnp.stack(jnp.split(full, 4, axis=axis))


def _step(cfg, a):
    d, tp, nm, seq = cfg.d, cfg.tp, cfg.n_meta, cfg.seq
    alpha = (2.0 * cfg.depth) ** 0.25
    mx, my, mc = _coords()
    p = 2 * mx + my
    dq4 = d // 4
    hm, npe, ql, kl = cfg.mla_heads, cfg.nope, cfg.q_lora, cfg.kv_lora
    hp = 2 * npe
    qk = npe + cfg.rope
    vec = lambda name: a[name].reshape(1, -1)

    sm = jnp.concatenate([a['meta_tokens'], a['l0_conv_w'].reshape(CONV_W, dq4)], axis=0)
    placed = lax.dynamic_update_slice(jnp.zeros((nm + CONV_W, d), F32), sm * (mc == 0).astype(F32), (0, p * dq4))
    gathered = allreduce_small(_rows8([placed[i:i + 1] for i in range(nm + CONV_W)], d), "gather_small")
    meta_full, conv_w4 = gathered[:nm], gathered[nm:]

    chip = p.astype(jnp.int32).reshape(1)
    wg = {n: cast_place(a[n], chip, f"place_{n}") for names in BIG.values() for n in names}
    wg.update(zip(BIG[0], allgather_chips([wg[n] for n in BIG[0]], "ag_l0")))

    def hosted(call, names):
        out, got = call(ag_chips_side([wg[n] for n in names]))
        wg.update(zip(names, got))
        return out

    def forwarded(call, names):
        out, got = call(ag_forward_side([wg[n] for n in names]))
        wg.update(zip(names, got))
        return out

    heads_first = lambda w: jnp.moveaxis(w, 0, 1).reshape(w.shape[1], w.shape[0] * w.shape[2], w.shape[3])
    w_out = lambda i: wg[f'l{i}_w_out'].reshape(1, d, d)

    zpad = jnp.zeros((tp - nm - seq, d), F32)
    h0 = jnp.concatenate([meta_full, a['x'][0], zpad], axis=0)
    h0_bf = h0.astype(BF16)

    w0_in, w0_a, w0_x = wg['l0_w_in'], heads_first(wg['l0_w_a']), heads_first(wg['l0_w_x'])
    ug0 = hosted(lambda side: mm_nn(cfg, h0_bf, w0_in, "l0_in", side=side), ['l1_w_in'])
    u0, u0_bf = conv_fwd(cfg, ug0, conv_w4, vec('l0_conv_b'), "l0_conv")
    pa0 = bd_nn(cfg, u0_bf, w0_a, "l0_gate_a")
    px0 = bd_nn(cfg, u0_bf, w0_x, "l0_gate_x")
    hs0, hprev0, y0 = hosted(lambda side: lru_fwd(cfg, u0, pa0, px0, ug0, vec('l0_b_a'), vec('l0_b_x'), vec('l0_lam'), "l0_lru",
                                                  side=side), ['l1_w_out'])
    o0 = hosted(lambda side: mm_nn(cfg, y0, w_out(0), "l0_out", side=side), ['l1_w_grp'])
    h1, h1_bf, xh0, rs0 = forwarded(lambda side: ln_fwd(cfg, h0, o0, vec('l0_ln_g'), vec('l0_ln_b'), alpha, "l0_ln", side=side), BIG[1])

    w1_in, w1_grp = wg['l1_w_in'], heads_first(wg['l1_w_grp'])
    ug1 = hosted(lambda side: mm_nn(cfg, h1_bf, w1_in, "l1_in", side=side), ['l2_w_in'])
    p1 = pool_fwd(cfg, ug1, "l1_pool")
    mm1 = bd_nn(cfg, p1, w1_grp, "l1_grp")
    y1 = gate_fwd(cfg, mm1, ug1, 1, vec('l1_scale'), "l1_gate")
    o1 = hosted(lambda side: mm_nn(cfg, y1, w_out(1), "l1_out", side=side), ['l2_w_uq', 'l2_w_ukv'])
    h2, h2_bf, xh1, rs1 = forwarded(lambda side: ln_fwd(cfg, h1, o1, vec('l1_ln_g'), vec('l1_ln_b'), alpha, "l1_ln", side=side),
                                    ['l2_w_in', 'l2_w_uq', 'l2_w_ukv'])

    w2_in = jnp.pad(_unblock(wg['l2_w_in'], 1), ((0, 0), (0, cfg.cw - (ql + kl + cfg.rope))))[None]
    w2_uq = jnp.pad(_unblock(wg['l2_w_uq'], 1).reshape(ql, hm, qk), ((0, 0), (0, 0), (0, hp - qk))).reshape(1, ql, hm * hp)
    w2_ukv = _unblock(wg['l2_w_ukv'], 1).reshape(kl, hm, 2, npe).transpose(0, 2, 1, 3).reshape(1, kl, 2 * hm * npe)
    proj = hosted(lambda side: mm_nn(cfg, h2_bf, w2_in, "l2_in", side=side), ['l2_w_out'])
    cqn, ckvn = mla_norm(cfg, proj, vec('l2_q_norm'), vec('l2_kv_norm'), "l2_norm")
    q_raw = forwarded(lambda side: mm_nn(cfg, cqn, w2_uq, "l2_uq", side=side), ['l2_w_out'])
    kv_raw = mm_nn(cfg, ckvn, w2_ukv, "l2_ukv")
    tabs = _mla_tables(cfg)
    qf, kf, vb = mla_prep(cfg, q_raw, kv_raw, proj, tabs, "l2_prep")
    o_att, lse = hosted(lambda side: attn_fwd(cfg, qf, kf, vb, "l2_attn", side=side), ['l3_w_in', 'l3_w_out'])
    y2 = gate_fwd(cfg, o_att, proj, 0, None, "l2_gate")
    w2_out = w_out(2)
    o2 = forwarded(lambda side: mm_nn(cfg, y2, w2_out, "l2_out", side=side), BIG[3])
    h3, h3_bf, xh2, rs2 = ln_fwd(cfg, h2, o2, vec('l2_ln_g'), vec('l2_ln_b'), alpha, "l2_ln")

    w3_in = wg['l3_w_in']
    qkvg = mm_nn(cfg, h3_bf, w3_in, "l3_in")
    cos, sin = _ret_rope_tables(cfg)
    qr, kr, vr = ret_prep(cfg, qkvg, cos, sin, "l3_prep")
    rt = _ret_tables(cfg)
    on, rsr = ret_fwd(cfg, qr, kr, vr, rt, "l3_ret")
    y3 = gate_fwd(cfg, on, qkvg, 3, None, "l3_gate")
    o3 = mm_nn(cfg, y3, w_out(3), "l3_out")
    h4, _, xh3, rs3 = ln_fwd(cfg, h3, o3, vec('l3_ln_g'), vec('l3_ln_b'), alpha, "l3_ln")

    tgt = jnp.concatenate([jnp.zeros((nm, d), F32), a['loss_target'][0], zpad], axis=0)
    dy4, lacc = loss_grad(cfg, h4, tgt, "loss")
    loss = lax.psum(0.5 * jnp.sum(lacc) / d, ("x", "y", "c"))

    grads, small, parts, landed = {}, {}, {}, {}

    def rs_begin(layer, full, call=None, names=None):
        names = BIG[layer] if names is None else names
        out, got = rs_pair_sums([g.reshape((4, 2, g.shape[1] // 2) + g.shape[2:]) for g in full], names[0], call)
        parts.update(zip(names, got))
        return out

    def carried(call, names):
        out, got = call(rs_chips_side([parts[n] for n in names]))
        landed.update(zip(names, got))
        return out

    def rs_end(layer, call=None):
        names = BIG[layer]
        out, got = rs_finish([parts[n] for n in names], [landed[n] for n in names], f"l{layer}", call)
        for n, g in zip(names, got):
            grads[n] = g.reshape(a[n].shape)
        return out

    dz3, dz3_bf, small['l3_ln_g'], small['l3_ln_b'] = ln_bwd(cfg, None, dy4, xh3, rs3, vec('l3_ln_g'), alpha, "l3_ln_b")
    g_out = mm_tn(cfg, y3, dz3_bf, 1, "l3_dw_out")
    dyb = mm_nt(cfg, dz3_bf, w_out(3), "l3_dy")
    d_on, dg = gate_bwd(cfg, dyb, on, qkvg, 3, None, F32, "l3_gate_b")
    do_bf, dqr = ret_bwd_q(cfg, qr, kr, vr, d_on, on, rsr, rt, "l3_ret_bq")
    dkr, dvr = ret_bwd_kv(cfg, qr, kr, vr, do_bf, rt, "l3_ret_bkv")
    d_in = ret_prep_bwd(cfg, dqr, dkr, dvr, dg, cos, sin, "l3_prep_b")
    g_in = mm_tn(cfg, h3_bf, d_in, 4, "l3_dw_in")
    dh = mm_nt(cfg, d_in, w3_in, "l3_dh")
    g3 = [g_in, g_out.reshape(4, dq4, d)]

    dz2, dz2_bf, small['l2_ln_g'], small['l2_ln_b'] = ln_bwd(cfg, dz3, dh, xh2, rs2, vec('l2_ln_g'), alpha, "l2_ln_b")
    g_out = rs_begin(3, g3[:1], lambda side: mm_tn(cfg, y2, dz2_bf, 1, "l2_dw_out", side=side), ['l3_w_in'])
    dyb = rs_begin(3, g3[1:], lambda side: mm_nt(cfg, dz2_bf, w_out(2), "l2_dy", side=side), ['l3_w_out'])
    do_bf, dg = gate_bwd(cfg, dyb, o_att, proj, 0, None, BF16, "l2_gate_b")
    dq_full, dk_full, dv_bf = carried(lambda side: attn_bwd(cfg, qf, kf, vb, do_bf, o_att, lse, "l2_attn_b", side=side), BIG[3])
    dq_raw, dk_nope, d_kr = rs_end(3, lambda side: mla_prep_bwd(cfg, dq_full, dk_full, tabs, "l2_prep_b", side=side))
    dkv_raw = jnp.concatenate([dk_nope, dv_bf], axis=1)
    g_uq = mm_tn(cfg, cqn, dq_raw, 1, "l2_dw_uq")
    d_cqn = mm_nt(cfg, dq_raw, w2_uq, "l2_dcq")
    g_ukv = mm_tn(cfg, ckvn, dkv_raw, 1, "l2_dw_ukv")
    d_ckvn = mm_nt(cfg, dkv_raw, w2_ukv, "l2_dckv")
    d_c, nacc = mla_norm_bwd(cfg, proj, d_cqn, d_ckvn, d_kr, vec('l2_q_norm'), vec('l2_kv_norm'), "l2_norm_b")
    small['l2_q_norm'], small['l2_kv_norm'] = nacc[:, :ql], nacc[:, ql:ql + kl]
    d_in = jnp.concatenate([dg, d_c], axis=1)
    g_in = mm_tn(cfg, h2_bf, d_in, 1, "l2_dw_in")
    dh = mm_nt(cfg, d_in, w2_in, "l2_dh")
    g_in = _block(g_in[0][:, :d + ql + kl + cfg.rope], 1)
    g_uq = _block(g_uq.reshape(ql, hm, hp)[:, :, :qk].reshape(ql, hm * qk), 1)
    g_ukv = _block(g_ukv.reshape(kl, 2, hm, npe).transpose(0, 2, 1, 3).reshape(kl, 2 * hm * npe), 1)
    g2 = [g_in, g_uq, g_ukv, g_out.reshape(4, dq4, d)]

    dz1, dz1_bf, small['l1_ln_g'], small['l1_ln_b'] = ln_bwd(cfg, dz2, dh, xh1, rs1, vec('l1_ln_g'), alpha, "l1_ln_b")
    g_out = rs_begin(2, g2, lambda side: mm_tn(cfg, y1, dz1_bf, 1, "l1_dw_out", side=side))
    dyb = mm_nt(cfg, dz1_bf, w_out(1), "l1_dy")
    d_mm1, dg, small['l1_scale'] = gate_bwd(cfg, dyb, mm1, ug1, 1, vec('l1_scale'), BF16, "l1_gate_b")
    g_grp = bd_tn(cfg, p1, d_mm1, len(cfg.pool_windows), "l1_dw_grp")
    dp = bd_nt(cfg, d_mm1, w1_grp, "l1_dp")
    du = pool_bwd(cfg, dp, "l1_pool_b")
    d_in = jnp.concatenate([du, dg], axis=1)
    g_in = carried(lambda side: mm_tn(cfg, h1_bf, d_in, 4, "l1_dw_in", side=side), ['l2_w_in', 'l2_w_uq'])
    dh = carried(lambda side: mm_nt(cfg, d_in, w1_in, "l1_dh", side=side), ['l2_w_ukv', 'l2_w_out'])
    blocks_first = lambda g: jnp.moveaxis(g.reshape(g.shape[0], 4, g.shape[1] // 4, g.shape[2]), 1, 0)
    g1 = [g_in, blocks_first(g_grp), g_out.reshape(4, dq4, d)]

    dh1 = dh
    dz0, dz0_bf, small['l0_ln_g'], small['l0_ln_b'] = rs_end(
        2, lambda side: ln_bwd(cfg, dz1, dh1, xh0, rs0, vec('l0_ln_g'), alpha, "l0_ln_b", side=side))
    g_out = rs_begin(1, g1, lambda side: mm_tn(cfg, y0, dz0_bf, 1, "l0_dw_out", side=side))
    dyb = mm_nt(cfg, dz0_bf, w_out(0), "l0_dy")
    dg, dpa, dpx, du_dir, lacc0 = lru_bwd(cfg, dyb, ug0, hs0, hprev0, u0, pa0, px0, vec('l0_b_a'), vec('l0_b_x'), vec('l0_lam'), "l0_lru_b")
    g_a = bd_tn(cfg, u0_bf, dpa, cfg.lru_heads, "l0_dw_a")
    g_x = bd_tn(cfg, u0_bf, dpx, cfg.lru_heads, "l0_dw_x")
    du_a = bd_nt(cfg, dpa, w0_a, "l0_du_a")
    du_x = bd_nt(cfg, dpx, w0_x, "l0_du_x")
    du_pre, cacc = conv_bwd(cfg, du_dir, du_a, du_x, ug0, conv_w4, "l0_conv_b")
    d_in = jnp.concatenate([du_pre, dg], axis=1)
    g_in = carried(lambda side: mm_tn(cfg, h0_bf, d_in, 4, "l0_dw_in", side=side), ['l1_w_in'])
    dh = carried(lambda side: mm_nt(cfg, d_in, w0_in, "l0_dh", side=side), ['l1_w_grp', 'l1_w_out'])
    dh0 = rs_end(1, lambda side: axpy(cfg, dz0, dh, alpha, "dh0", side=side))[0]
    rs_begin(0, [g_in, blocks_first(g_a), blocks_first(g_x), g_out.reshape(4, dq4, d)])
    grad_x = dh0[nm:nm + seq][None]
    small['l0_lam'], small['l0_b_a'], small['l0_b_x'] = lacc0[0:1], lacc0[1:2], lacc0[2:3]
    small['l0_conv_b'] = cacc[4:5]

    rows = [dh0[i:i + 1] for i in range(nm)] + [cacc[k:k + 1] for k in range(CONV_W)] + [small[n] for n in SMALL_REPL]
    red = allreduce_small(_rows8(rows, d), "reduce_small")
    sh = lax.dynamic_slice(red[:nm + CONV_W], (0, p * dq4), (nm + CONV_W, dq4))
    grads['meta_tokens'] = sh[:nm]
    grads['l0_conv_w'] = sh[nm:].reshape(a['l0_conv_w'].shape)
    for i, n in enumerate(SMALL_REPL):
        grads[n] = red[nm + CONV_W + i, :a[n].shape[0]]

    delta, new_m, new_v = {}, {}, {}
    riders = {'l3_w_in': ['l0_w_in'], 'l1_w_in': ['l0_w_out'], 'l2_w_in': ['l0_w_a', 'l0_w_x']}
    for n in BIG[3] + BIG[2] + BIG[1]:
        update = lambda side=None, n=n: adamw(a[n], grads[n], a['m_' + n], a['v_' + n], f"adamw_{n}", side=side)
        delta[n], new_m[n], new_v[n] = carried(update, riders[n]) if n in riders else update()
    rs_end(0)
    for n in BIG[0]:
        delta[n], new_m[n], new_v[n] = adamw(a[n], grads[n], a['m_' + n], a['v_' + n], f"adamw_{n}")
    pack_s = lambda pre: jnp.concatenate([a[pre + 'meta_tokens'], a[pre + 'l0_conv_w'].reshape(CONV_W, dq4)], axis=0)
    ds_, ms_, vs_ = adamw(pack_s(''), sh, pack_s('m_'), pack_s('v_'), "adamw_small_sharded")
    for out, packed in ((delta, ds_), (new_m, ms_), (new_v, vs_)):
        out['meta_tokens'] = packed[:nm]
        out['l0_conv_w'] = packed[nm:].reshape(a['l0_conv_w'].shape)
    pack_r = lambda pre: jnp.stack([jnp.pad(a[pre + n], (0, d - a[n].shape[0])) for n in SMALL_REPL])
    dr_, mr_, vr_ = adamw(pack_r(''), red[nm + CONV_W:nm + CONV_W + len(SMALL_REPL)], pack_r('m_'), pack_r('v_'), "adamw_small_repl")
    for out, packed in ((delta, dr_), (new_m, mr_), (new_v, vr_)):
        for i, n in enumerate(SMALL_REPL):
            out[n] = packed[i, :a[n].shape[0]]

    return (loss, grad_x, *[grads[n] for n in W_NAMES], *[delta[n] for n in W_NAMES],
            *[new_m[n] for n in W_NAMES], *[new_v[n] for n in W_NAMES])


def kernel(x, meta_tokens, l0_w_in, l0_conv_w, l0_conv_b, l0_w_a, l0_b_a, l0_w_x, l0_b_x, l0_lam, l0_w_out, l0_ln_g, l0_ln_b, l1_w_in, l1_w_grp, l1_scale, l1_w_out, l1_ln_g, l1_ln_b, l2_w_in, l2_q_norm, l2_w_uq, l2_kv_norm, l2_w_ukv, l2_w_out, l2_ln_g, l2_ln_b, l3_w_in, l3_w_out, l3_ln_g, l3_ln_b, loss_target, m_meta_tokens, m_l0_w_in, m_l0_conv_w, m_l0_conv_b, m_l0_w_a, m_l0_b_a, m_l0_w_x, m_l0_b_x, m_l0_lam, m_l0_w_out, m_l0_ln_g, m_l0_ln_b, m_l1_w_in, m_l1_w_grp, m_l1_scale, m_l1_w_out, m_l1_ln_g, m_l1_ln_b, m_l2_w_in, m_l2_q_norm, m_l2_w_uq, m_l2_kv_norm, m_l2_w_ukv, m_l2_w_out, m_l2_ln_g, m_l2_ln_b, m_l3_w_in, m_l3_w_out, m_l3_ln_g, m_l3_ln_b, v_meta_tokens, v_l0_w_in, v_l0_conv_w, v_l0_conv_b, v_l0_w_a, v_l0_b_a, v_l0_w_x, v_l0_b_x, v_l0_lam, v_l0_w_out, v_l0_ln_g, v_l0_ln_b, v_l1_w_in, v_l1_w_grp, v_l1_scale, v_l1_w_out, v_l1_ln_g, v_l1_ln_b, v_l2_w_in, v_l2_q_norm, v_l2_w_uq, v_l2_kv_norm, v_l2_w_ukv, v_l2_w_out, v_l2_ln_g, v_l2_ln_b, v_l3_w_in, v_l3_w_out, v_l3_ln_g, v_l3_ln_b):
    return _step(REAL, dict(locals()))
```

```python
import functools
import math
from typing import NamedTuple

import jax
import jax.numpy as jnp
import numpy as np
from jax import lax
from jax.experimental import pallas as pl
from jax.experimental.pallas import tpu as pltpu

F32 = jnp.float32
BF16 = jnp.bfloat16

LN_EPS = 1e-5
RMS_EPS = 1e-6
ROPE_BASE = 10000.0
LRU_C = 8.0
CONV_W = 4
HIST = 16
ADAM_LR, ADAM_B1, ADAM_B2, ADAM_EPS, ADAM_WD, ADAM_STEP = 0.001, 0.9, 0.999, 1e-08, 0.01, 10
VMEM_LIMIT_V7X = 56 * 1024 * 1024
MESH = pl.DeviceIdType.MESH


class Cfg(NamedTuple):
    d: int
    seq: int
    n_meta: int
    tp: int
    depth: int
    lru_heads: int
    pool_windows: tuple
    mla_heads: int
    nope: int
    rope: int
    q_lora: int
    kv_lora: int
    cw: int
    ret_heads: int
    chunk: int
    tq: int
    tkw: int
    tr: int
    tt: int
    tc: int
    tm: int
    tn: int
    tkc: int
    tkm: int
    la: int


REAL = Cfg(d=4096, seq=4096, n_meta=16, tp=4224, depth=4, lru_heads=16, pool_windows=(2, 4, 8, 16),
           mla_heads=32, nope=128, rope=64, q_lora=1024, kv_lora=512, cw=2048, ret_heads=16, chunk=384,
           tq=384, tkw=2, tr=192, tt=1056, tc=512, tm=1408, tn=1024, tkc=4096, tkm=4224, la=128)


def _div_tile(n, target, align):
    best = None
    for t in range(align, min(n, target) + 1, align):
        if n % t == 0:
            best = t
    return best or n


def _cparams(*sem):
    return pltpu.CompilerParams(dimension_semantics=sem, vmem_limit_bytes=VMEM_LIMIT_V7X)


def _sig(x):
    return 1.0 / (1.0 + jnp.exp(-x))


def _expm1(x):
    p = x * (1.0 + x * (1.0 / 2) * (1.0 + x * (1.0 / 3) * (1.0 + x * (1.0 / 4) * (1.0 + x * (1.0 / 5) * (1.0 + x * (1.0 / 6))))))
    return jnp.where(x > -0.1, p, jnp.exp(x) - 1.0)


def _colsum8(x):
    r, w = x.shape
    return x.reshape(r // 8, 8, w).sum(axis=0)


class Side(NamedTuple):
    ins: list
    inplace: list
    outs: list
    sems: list
    start: object
    finish: object


def _host_call(body, grid, in_specs, out_specs, out_shape, scratch_shapes, semantics, name, args, side=None):
    if side is None:
        return pl.pallas_call(body, grid=grid, in_specs=in_specs, out_specs=out_specs, out_shape=out_shape,
                              scratch_shapes=scratch_shapes, compiler_params=_cparams(*semantics), name=name)(*args), None
    n_in, n_out, n_scr = len(in_specs), len(out_specs), len(scratch_shapes)
    n_si, n_sp, n_so = len(side.ins), len(side.inplace), len(side.outs)

    def wrapped(*refs):
        it = iter(refs)
        take = lambda k: [next(it) for _ in range(k)]
        ins, s_ins, s_inplace_in, outs, s_inplace, s_outs, scr, sems = (take(n_in), take(n_si), take(n_sp), take(n_out),
                                                                        take(n_sp), take(n_so), take(n_scr), take(len(side.sems)))
        ids = [pl.program_id(i) for i in range(len(grid))]
        first = functools.reduce(jnp.logical_and, [i == 0 for i in ids])
        last = functools.reduce(jnp.logical_and, [i == g - 1 for i, g in zip(ids, grid)])

        @pl.when(first)
        def _():
            side.start(s_ins, s_inplace, s_outs, sems)

        body(*ins, *outs, *scr)

        @pl.when(last)
        def _():
            side.finish(s_ins, s_inplace, s_outs, sems)

    any_spec = pl.BlockSpec(memory_space=pl.ANY)
    out = pl.pallas_call(
        wrapped, grid=grid, in_specs=list(in_specs) + [any_spec] * (n_si + n_sp),
        out_specs=list(out_specs) + [any_spec] * (n_sp + n_so),
        out_shape=list(out_shape) + [jax.ShapeDtypeStruct(x.shape, x.dtype) for x in side.inplace] + list(side.outs),
        scratch_shapes=list(scratch_shapes) + list(side.sems),
        input_output_aliases={n_in + n_si + i: n_out + i for i in range(n_sp)},
        compiler_params=_cparams(*(("arbitrary",) * len(grid))), name=name)(*args, *side.ins, *side.inplace)
    return out[:n_out], out[n_out:]


def _mm_call(a, b, grid, a_spec, b_spec, o_spec, out_sds, acc_shape, dims, name, side=None):
    nk = grid[-1]
    kax = len(grid) - 1

    def product(a_ref, b_ref):
        return lax.dot_general(a_ref[...].astype(BF16), b_ref[...].astype(BF16), dims, preferred_element_type=F32)

    def body_one(a_ref, b_ref, o_ref):
        o_ref[...] = product(a_ref, b_ref).astype(o_ref.dtype)

    def body_acc(a_ref, b_ref, o_ref, acc_ref):
        k = pl.program_id(kax)

        @pl.when(k == 0)
        def _():
            acc_ref[...] = jnp.zeros_like(acc_ref)

        acc_ref[...] += product(a_ref, b_ref)

        @pl.when(k == nk - 1)
        def _():
            o_ref[...] = acc_ref[...].astype(o_ref.dtype)

    sem = ("parallel",) * kax + ("arbitrary",)
    body, scratch = (body_one, []) if nk == 1 else (body_acc, [pltpu.VMEM(acc_shape, F32)])
    outs, side_outs = _host_call(body, grid, [a_spec, b_spec], [o_spec], [out_sds], scratch, sem, name, (a, b), side)
    return outs[0] if side is None else (outs[0], side_outs)


def _out_tile(cfg, tk):
    return cfg.tn if tk <= cfg.tn else cfg.tn // 2


_NN = (((1,), (0,)), ((), ()))
_NT = (((1,), (1,)), ((), ()))
_TN = (((0,), (0,)), ((), ()))


def mm_nn(cfg, a, w, name, out_dtype=F32, a_col0=0, side=None):
    m = a.shape[0]
    g, k, ns = w.shape
    tm, tk = _div_tile(m, cfg.tm, 16), _div_tile(k, cfg.tkc, cfg.la)
    tn = _div_tile(ns, _out_tile(cfg, tk), cfg.la)
    npb, a0 = ns // tn, a_col0 // tk
    assert a_col0 % tk == 0
    return _mm_call(a, w, (m // tm, g * npb, k // tk),
                    pl.BlockSpec((tm, tk), lambda i, j, kk: (i, a0 + kk)),
                    pl.BlockSpec((None, tk, tn), lambda i, j, kk: (j // npb, kk, j % npb)),
                    pl.BlockSpec((tm, tn), lambda i, j, kk: (i, j)),
                    jax.ShapeDtypeStruct((m, g * ns), out_dtype), (tm, tn), _NN, name, side)


def mm_nt(cfg, dy, w, name, out_dtype=F32, side=None):
    m = dy.shape[0]
    g, k, ns = w.shape
    tm, tk = _div_tile(m, cfg.tm, 16), _div_tile(ns, cfg.tkc, cfg.la)
    tn = _div_tile(k, _out_tile(cfg, tk), cfg.la)
    kpb = ns // tk
    return _mm_call(dy, w, (m // tm, k // tn, g * kpb),
                    pl.BlockSpec((tm, tk), lambda i, j, kk: (i, kk)),
                    pl.BlockSpec((None, tn, tk), lambda i, j, kk: (kk // kpb, j, kk % kpb)),
                    pl.BlockSpec((tm, tn), lambda i, j, kk: (i, j)),
                    jax.ShapeDtypeStruct((m, k), out_dtype), (tm, tn), _NT, name, side)


def mm_tn(cfg, x, dy, g, name, x_col0=0, k=None, side=None):
    m = x.shape[0]
    k = x.shape[1] if k is None else k
    ns = dy.shape[1] // g
    tk = _div_tile(m, cfg.tkm, 16)
    tmo, tn = _div_tile(k, _out_tile(cfg, tk), cfg.la), _div_tile(ns, cfg.tn, cfg.la)
    npb, x0 = ns // tn, x_col0 // tmo
    assert x_col0 % tmo == 0
    return _mm_call(x, dy, (k // tmo, g * npb, m // tk),
                    pl.BlockSpec((tk, tmo), lambda i, j, kk: (kk, x0 + i)),
                    pl.BlockSpec((tk, tn), lambda i, j, kk: (kk, j)),
                    pl.BlockSpec((None, tmo, tn), lambda i, j, kk: (j // npb, i, j % npb)),
                    jax.ShapeDtypeStruct((g, k, ns), BF16), (tmo, tn), _TN, name, side)


def bd_nn(cfg, x, w, name, out_dtype=F32):
    m = x.shape[0]
    g, kg, ng = w.shape
    tm, tn = _div_tile(m, cfg.tm, 16), _div_tile(ng, cfg.tn, cfg.la)
    npb = ng // tn
    return _mm_call(x, w, (m // tm, g * npb, 1),
                    pl.BlockSpec((tm, kg), lambda i, j, kk: (i, j // npb)),
                    pl.BlockSpec((None, kg, tn), lambda i, j, kk: (j // npb, 0, j % npb)),
                    pl.BlockSpec((tm, tn), lambda i, j, kk: (i, j)),
                    jax.ShapeDtypeStruct((m, g * ng), out_dtype), (tm, tn), _NN, name)


def bd_nt(cfg, dy, w, name, out_dtype=F32):
    m = dy.shape[0]
    g, kg, ng = w.shape
    tm, tn = _div_tile(m, cfg.tm, 16), _div_tile(kg, cfg.tn, cfg.la)
    npb = kg // tn
    return _mm_call(dy, w, (m // tm, g * npb, 1),
                    pl.BlockSpec((tm, ng), lambda i, j, kk: (i, j // npb)),
                    pl.BlockSpec((None, tn, ng), lambda i, j, kk: (j // npb, j % npb, 0)),
                    pl.BlockSpec((tm, tn), lambda i, j, kk: (i, j)),
                    jax.ShapeDtypeStruct((m, g * kg), out_dtype), (tm, tn), _NT, name)


def bd_tn(cfg, x, dy, g, name):
    m = x.shape[0]
    kg, ng = x.shape[1] // g, dy.shape[1] // g
    tmo, tn, tk = _div_tile(kg, cfg.tn, cfg.la), _div_tile(ng, cfg.tn, cfg.la), _div_tile(m, cfg.tm, 16)
    mpb, npb = kg // tmo, ng // tn
    return _mm_call(x, dy, (g * mpb, npb, m // tk),
                    pl.BlockSpec((tk, tmo), lambda i, j, kk: (kk, i)),
                    pl.BlockSpec((tk, tn), lambda i, j, kk: (kk, (i // mpb) * npb + j)),
                    pl.BlockSpec((None, tmo, tn), lambda i, j, kk: (i // mpb, i % mpb, j)),
                    jax.ShapeDtypeStruct((g, kg, ng), F32), (tmo, tn), _TN, name)


def _rowwise(cfg, body, ins, outs, name, n_acc=0, side=None):
    tr = cfg.tr
    n = cfg.tp // tr
    in_specs, args = [], []
    for spec in ins:
        arr = spec[0]
        if spec[1] is None:
            in_specs.append(pl.BlockSpec(arr.shape, lambda i, nd=arr.ndim: (0,) * nd))
        else:
            in_specs.append(pl.BlockSpec((tr, spec[1]), lambda i, cb=spec[2]: (i, cb)))
        args.append(arr)
    out_specs, out_shape = [], []
    for o in outs:
        if len(o) == 2:
            out_specs.append(pl.BlockSpec((tr, o[0]), lambda i: (i, 0)))
            out_shape.append(jax.ShapeDtypeStruct((cfg.tp, o[0]), o[1]))
        else:
            out_specs.append(pl.BlockSpec((8, o[0]), lambda i: (0, 0)))
            out_shape.append(jax.ShapeDtypeStruct((8, o[0]), F32))
    outs, side_outs = _host_call(body, (n,), in_specs, out_specs, out_shape, [], ("arbitrary" if n_acc else "parallel",),
                                 name, args, side)
    return outs if side is None else (outs, side_outs)


def ln_fwd(cfg, h, o, g, b, alpha, name, side=None):
    d = cfg.d

    def body(h_ref, o_ref, g_ref, b_ref, y_ref, yb_ref, xh_ref, rs_ref):
        z = alpha * h_ref[...] + o_ref[...]
        mu = jnp.mean(z, axis=-1, keepdims=True)
        zc = z - mu
        var = jnp.mean(zc * zc, axis=-1, keepdims=True)
        rstd = lax.rsqrt(var + LN_EPS)
        xh = zc * rstd
        y = xh * g_ref[...] + b_ref[...]
        y_ref[...] = y
        yb_ref[...] = y.astype(BF16)
        xh_ref[...] = xh
        rs_ref[...] = rstd

    return _rowwise(cfg, body, [(h, d, 0), (o, d, 0), (g, None), (b, None)], [(d, F32), (d, BF16), (d, F32), (1, F32)], name,
                    side=side)


def ln_bwd(cfg, d_res, d_mm, xhat, rstd, g, alpha, name, side=None):
    d = cfg.d
    two = d_res is not None

    def body(*refs):
        if two:
            dr_ref, dm_ref, xh_ref, rs_ref, g_ref, dz_ref, dzb_ref, dg_ref, db_ref = refs
            dy = alpha * dr_ref[...] + dm_ref[...]
        else:
            dm_ref, xh_ref, rs_ref, g_ref, dz_ref, dzb_ref, dg_ref, db_ref = refs
            dy = dm_ref[...]
        xh = xh_ref[...]

        @pl.when(pl.program_id(0) == 0)
        def _():
            dg_ref[...] = jnp.zeros_like(dg_ref)
            db_ref[...] = jnp.zeros_like(db_ref)

        dg_ref[...] += _colsum8(dy * xh)
        db_ref[...] += _colsum8(dy)
        dxh = dy * g_ref[...]
        m1 = jnp.mean(dxh, axis=-1, keepdims=True)
        m2 = jnp.mean(dxh * xh, axis=-1, keepdims=True)
        dz = rs_ref[...] * (dxh - m1 - xh * m2)
        dz_ref[...] = dz
        dzb_ref[...] = dz.astype(BF16)

    ins = ([(d_res, d, 0)] if two else []) + [(d_mm, d, 0), (xhat, d, 0), (rstd, 1, 0), (g, None)]
    return _rowwise(cfg, body, ins, [(d, F32), (d, BF16), (d,), (d,)], name, n_acc=2, side=side)


def loss_grad(cfg, y, tgt, name):
    d, tr = cfg.d, cfg.tr
    lo, hi = cfg.n_meta, cfg.n_meta + cfg.seq

    def body(y_ref, t_ref, dy_ref, acc_ref):
        i = pl.program_id(0)

        @pl.when(i == 0)
        def _():
            acc_ref[...] = jnp.zeros_like(acc_ref)

        row = i * tr + lax.broadcasted_iota(jnp.int32, (tr, 1), 0)
        err = jnp.where((row >= lo) & (row < hi), y_ref[...] - t_ref[...], 0.0)
        dy_ref[...] = err * (1.0 / d)
        acc_ref[...] += _colsum8(err * err)

    return _rowwise(cfg, body, [(y, d, 0), (tgt, d, 0)], [(d, F32), (d,)], name, n_acc=1)


def axpy(cfg, a, b, alpha, name, side=None):
    d = cfg.d

    def body(a_ref, b_ref, o_ref):
        o_ref[...] = alpha * a_ref[...] + b_ref[...]

    return _rowwise(cfg, body, [(a, d, 0), (b, d, 0)], [(d, F32)], name, side=side)


def _time_call(cfg, body, ins, outs, accs, scratch, name, reverse=False, groups=1, side=None):
    d = cfg.d
    tc = _div_tile(d // groups, cfg.tc, cfg.la)
    tt = _div_tile(cfg.tp, cfg.tt, 16)
    nc, nt = d // tc, cfg.tp // tt
    tmap = (lambda t: nt - 1 - t) if reverse else (lambda t: t)
    in_specs, args = [], []
    for spec in ins:
        arr = spec[0]
        if len(spec) == 2:
            in_specs.append(pl.BlockSpec((tt, tc), lambda c, t, off=spec[1] * nc: (tmap(t), off + c)))
        else:
            in_specs.append(pl.BlockSpec((arr.shape[0], tc), lambda c, t: (0, c)))
        args.append(arr)
    out_specs = [pl.BlockSpec((tt, tc), lambda c, t: (tmap(t), c)) for _ in outs]
    out_shape = [jax.ShapeDtypeStruct((cfg.tp, d), dt) for dt in outs]
    for _ in range(accs):
        out_specs.append(pl.BlockSpec((8, tc), lambda c, t: (0, c)))
        out_shape.append(jax.ShapeDtypeStruct((8, d), F32))
    outs, side_outs = _host_call(functools.partial(body, tt=tt, tc=tc, nt=nt), (nc, nt), in_specs, out_specs, out_shape,
                                 [pltpu.VMEM(s, F32) for s in scratch(tt, tc)], ("parallel", "arbitrary"), name, args, side)
    return outs if side is None else (outs, side_outs)


def _push_history(s_ref, new, t, tt):
    @pl.when(t == 0)
    def _():
        s_ref[pl.ds(0, HIST), :] = jnp.zeros((HIST, s_ref.shape[1]), F32)

    @pl.when(t > 0)
    def _():
        s_ref[pl.ds(0, HIST), :] = s_ref[pl.ds(tt, HIST), :]

    s_ref[pl.ds(HIST, tt), :] = new


def _push_future(s_ref, new, t, tt):
    @pl.when(t == 0)
    def _():
        s_ref[pl.ds(tt, HIST), :] = jnp.zeros((HIST, s_ref.shape[1]), F32)

    @pl.when(t > 0)
    def _():
        s_ref[pl.ds(tt, HIST), :] = s_ref[pl.ds(0, HIST), :]

    s_ref[pl.ds(0, tt), :] = new


def conv_fwd(cfg, ug, w4, cb, name):
    def body(x_ref, w_ref, b_ref, u_ref, ub_ref, s_ref, *, tt, tc, nt):
        t = pl.program_id(1)
        _push_history(s_ref, x_ref[...], t, tt)
        acc = b_ref[...] + w_ref[pl.ds(0, 1), :] * s_ref[pl.ds(HIST - 3, tt), :]
        for k in range(1, CONV_W):
            acc = acc + w_ref[pl.ds(k, 1), :] * s_ref[pl.ds(HIST - 3 + k, tt), :]
        u_ref[...] = acc
        ub_ref[...] = acc.astype(BF16)

    return _time_call(cfg, body, [(ug, 0), (w4,), (cb,)], [F32, BF16], 0, lambda tt, tc: [(tt + HIST, tc)], name)


def conv_bwd(cfg, du_a, du_b, du_c, ug, w4, name):
    def body(a_ref, b_ref, c_ref, x_ref, w_ref, dx_ref, acc_ref, s_ref, *, tt, tc, nt):
        t = pl.program_id(1)

        @pl.when(t == 0)
        def _():
            acc_ref[...] = jnp.zeros_like(acc_ref)

        du = a_ref[...] + b_ref[...] + c_ref[...]
        _push_future(s_ref, du, t, tt)
        x = x_ref[...]
        acc_ref[pl.ds(4, 1), :] += jnp.sum(du, axis=0, keepdims=True)
        dx = None
        for k in range(CONV_W):
            sh = s_ref[pl.ds(3 - k, tt), :]
            term = w_ref[pl.ds(k, 1), :] * sh
            dx = term if dx is None else dx + term
            acc_ref[pl.ds(k, 1), :] += jnp.sum(x * sh, axis=0, keepdims=True)
        dx_ref[...] = dx.astype(BF16)

    return _time_call(cfg, body, [(du_a, 0), (du_b, 0), (du_c, 0), (ug, 0), (w4,)], [BF16], 1,
                      lambda tt, tc: [(tt + HIST, tc)], name, reverse=True)


def _scan8(a, b, rows):
    for k in (1, 2, 4):
        ar = jnp.where(rows >= k, pltpu.roll(a, k, 0), 1.0)
        br = jnp.where(rows >= k, pltpu.roll(b, k, 0), 0.0)
        b = a * br + b
        a = a * ar
    return a, b


def _rscan8(a, b, rows):
    for k in (1, 2, 4):
        ar = jnp.where(rows < 8 - k, pltpu.roll(a, 8 - k, 0), 1.0)
        br = jnp.where(rows < 8 - k, pltpu.roll(b, 8 - k, 0), 0.0)
        b = b + a * br
        a = a * ar
    return a, b


def _lru_gates(u, pa, px, ba, bx, c_lam):
    r = _sig(pa + ba)
    i = _sig(px + bx)
    la = LRU_C * r * c_lam
    em = _expm1(2.0 * la)
    return r, i, jnp.exp(la), em, jnp.sqrt(-em)


def _neg_softplus_neg(lam):
    e = jnp.exp(-jnp.abs(lam))
    u = 1.0 + e
    l1p = jnp.where(u == 1.0, e, jnp.log(u) * (e / jnp.where(u == 1.0, 1.0, u - 1.0)))
    return -(jnp.maximum(-lam, 0.0) + l1p)


def lru_fwd(cfg, u, pa, px, ug, ba, bx, lam, name, side=None):
    def body(u_ref, pa_ref, px_ref, g_ref, ba_ref, bx_ref, lam_ref, hs_ref, hp_ref, y_ref, car_ref, *, tt, tc, nt):
        @pl.when(pl.program_id(1) == 0)
        def _():
            car_ref[...] = jnp.zeros_like(car_ref)

        rows = lax.broadcasted_iota(jnp.int32, (8, tc), 0)
        ba, bx = ba_ref[...], bx_ref[...]
        c_lam = _neg_softplus_neg(lam_ref[...])

        def step(it, h):
            ys = []
            for half in range(2):
                rs = pl.ds(pl.multiple_of(it * 16 + half * 8, 8), 8)
                uu = u_ref[rs, :]
                r, i, a, em, s = _lru_gates(uu, pa_ref[rs, :], px_ref[rs, :], ba, bx, c_lam)
                acum, b = _scan8(a, uu * i * s, rows)
                hs = acum * h + b
                hs_ref[rs, :] = hs
                hp_ref[rs, :] = jnp.where(rows >= 1, pltpu.roll(hs, 1, 0), h)
                g = g_ref[rs, :]
                ys.append(hs * (g * _sig(g)))
                h = jnp.broadcast_to(hs[7:8, :], (8, tc))
            y_ref[pl.ds(pl.multiple_of(it * 16, 16), 16), :] = jnp.concatenate(ys, axis=0).astype(BF16)
            return h

        car_ref[...] = lax.fori_loop(0, tt // 16, step, car_ref[...])

    return _time_call(cfg, body, [(u, 0), (pa, 0), (px, 0), (ug, 1), (ba,), (bx,), (lam,)], [F32, F32, BF16], 0,
                      lambda tt, tc: [(8, tc)], name, side=side)


def lru_bwd(cfg, dy, ug, hs, hprev, u, pa, px, ba, bx, lam, name):
    def body(dy_ref, g_ref, hs_ref, hp_ref, u_ref, pa_ref, px_ref, ba_ref, bx_ref, lam_ref,
             dg_ref, dpa_ref, dpx_ref, du_ref, acc_ref, ecar_ref, acar_ref, *, tt, tc, nt):
        @pl.when(pl.program_id(1) == 0)
        def _():
            ecar_ref[...] = jnp.zeros_like(ecar_ref)
            acar_ref[...] = jnp.zeros_like(acar_ref)
            acc_ref[...] = jnp.zeros_like(acc_ref)

        rows = lax.broadcasted_iota(jnp.int32, (8, tc), 0)
        ba, bx = ba_ref[...], bx_ref[...]
        c_lam = _neg_softplus_neg(lam_ref[...])

        def step(it, carry):
            ecar, acar, s_lam, s_ba, s_bx = carry
            jt = tt // 16 - 1 - it
            dgs, dpas, dpxs = [None, None], [None, None], [None, None]
            for half in (1, 0):
                rs = pl.ds(pl.multiple_of(jt * 16 + half * 8, 8), 8)
                uu = u_ref[rs, :]
                r, i, a, em, s = _lru_gates(uu, pa_ref[rs, :], px_ref[rs, :], ba, bx, c_lam)
                g = g_ref[rs, :]
                sg = _sig(g)
                dy = dy_ref[rs, :]
                dgs[half] = dy * hs_ref[rs, :] * (sg * (1.0 + g * (1.0 - sg)))
                a_next = jnp.where(rows < 7, pltpu.roll(a, 7, 0), acar)
                acum, e = _rscan8(a_next, dy * (g * sg), rows)
                e = e + acum * ecar
                ecar = jnp.broadcast_to(e[0:1, :], (8, tc))
                acar = jnp.broadcast_to(a[0:1, :], (8, tc))
                du_ref[rs, :] = e * i * s
                d_la = e * hp_ref[rs, :] * a - (e * uu * i) * ((em + 1.0) / s)
                s_lam = s_lam + d_la * (LRU_C * r)
                d_pa = d_la * (LRU_C * c_lam) * r * (1.0 - r)
                d_px = e * uu * s * i * (1.0 - i)
                s_ba = s_ba + d_pa
                s_bx = s_bx + d_px
                dpas[half], dpxs[half] = d_pa, d_px
            r16 = pl.ds(pl.multiple_of(jt * 16, 16), 16)
            dg_ref[r16, :] = jnp.concatenate(dgs, axis=0).astype(BF16)
            dpa_ref[r16, :] = jnp.concatenate(dpas, axis=0).astype(BF16)
            dpx_ref[r16, :] = jnp.concatenate(dpxs, axis=0).astype(BF16)
            return ecar, acar, s_lam, s_ba, s_bx

        z = jnp.zeros((8, tc), F32)
        ecar, acar, s_lam, s_ba, s_bx = lax.fori_loop(0, tt // 16, step, (ecar_ref[...], acar_ref[...], z, z, z))
        ecar_ref[...] = ecar
        acar_ref[...] = acar
        acc_ref[pl.ds(0, 1), :] += jnp.sum(s_lam, axis=0, keepdims=True) * _sig(-lam_ref[...])
        acc_ref[pl.ds(1, 1), :] += jnp.sum(s_ba, axis=0, keepdims=True)
        acc_ref[pl.ds(2, 1), :] += jnp.sum(s_bx, axis=0, keepdims=True)

    return _time_call(cfg, body, [(dy, 0), (ug, 1), (hs, 0), (hprev, 0), (u, 0), (pa, 0), (px, 0), (ba,), (bx,), (lam,)],
                      [BF16, BF16, BF16, F32], 1, lambda tt, tc: [(8, tc), (8, tc)], name, reverse=True)


def _pool_select(cfg, grp, fn):
    for gi, w in enumerate(cfg.pool_windows):
        @pl.when(grp == gi)
        def _(w=w):
            fn(w)


def pool_fwd(cfg, ug, name):
    ng = len(cfg.pool_windows)

    def body(x_ref, p_ref, s_ref, *, tt, tc, nt):
        t = pl.program_id(1)
        grp = pl.program_id(0) // (cfg.d // ng // tc)
        x = x_ref[...]
        _push_history(s_ref, x, t, tt)
        row1 = (t * tt + 1 + lax.broadcasted_iota(jnp.int32, (tt, 1), 0)).astype(F32)

        def write(w):
            ws = x
            for j in range(1, w):
                ws = ws + s_ref[pl.ds(HIST - j, tt), :]
            p_ref[...] = (ws / jnp.minimum(row1, float(w)) - x).astype(BF16)

        _pool_select(cfg, grp, write)

    return _time_call(cfg, body, [(ug, 0)], [BF16], 0, lambda tt, tc: [(tt + HIST, tc)], name, groups=ng)[0]


def pool_bwd(cfg, dp, name):
    ng = len(cfg.pool_windows)

    def body(dp_ref, du_ref, s_ref, *, tt, tc, nt):
        t = pl.program_id(1)
        grp = pl.program_id(0) // (cfg.d // ng // tc)
        dp = dp_ref[...]
        row1 = ((nt - 1 - t) * tt + 1 + lax.broadcasted_iota(jnp.int32, (tt, 1), 0)).astype(F32)

        def write(w):
            dm = dp / jnp.minimum(row1, float(w))
            _push_future(s_ref, dm, t, tt)
            ws = dm
            for j in range(1, w):
                ws = ws + s_ref[pl.ds(j, tt), :]
            du_ref[...] = (ws - dp).astype(BF16)

        _pool_select(cfg, grp, write)

    return _time_call(cfg, body, [(dp, 0)], [BF16], 0, lambda tt, tc: [(tt + HIST, tc)], name, reverse=True, groups=ng)[0]


def gate_fwd(cfg, v, gsrc, gblk, scale, name):
    d = cfg.d

    def body(*refs):
        if scale is None:
            v_ref, g_ref, y_ref = refs
            v = v_ref[...]
        else:
            v_ref, g_ref, s_ref, y_ref = refs
            v = v_ref[...] * s_ref[...]
        g = g_ref[...]
        y_ref[...] = (v * (g * _sig(g))).astype(BF16)

    ins = [(v, d, 0), (gsrc, d, gblk)] + ([] if scale is None else [(scale, None)])
    return _rowwise(cfg, body, ins, [(d, BF16)], name)[0]


def gate_bwd(cfg, dy, v, gsrc, gblk, scale, dv_dtype, name):
    d = cfg.d

    def body(*refs):
        if scale is None:
            dy_ref, v_ref, g_ref, dv_ref, dg_ref = refs
            vs = v_ref[...]
        else:
            dy_ref, v_ref, g_ref, s_ref, dv_ref, dg_ref, acc_ref = refs
            vs = v_ref[...] * s_ref[...]
        g = g_ref[...]
        sg = _sig(g)
        dy = dy_ref[...]
        dvs = dy * (g * sg)
        dg_ref[...] = (dy * vs * (sg * (1.0 + g * (1.0 - sg)))).astype(BF16)
        if scale is None:
            dv_ref[...] = dvs.astype(dv_dtype)
        else:
            @pl.when(pl.program_id(0) == 0)
            def _():
                acc_ref[...] = jnp.zeros_like(acc_ref)

            acc_ref[...] += _colsum8(dvs * v_ref[...])
            dv_ref[...] = (dvs * s_ref[...]).astype(dv_dtype)

    ins = [(dy, d, 0), (v, d, 0), (gsrc, d, gblk)] + ([] if scale is None else [(scale, None)])
    outs = [(d, dv_dtype), (d, BF16)] + ([] if scale is None else [(d,)])
    return _rowwise(cfg, body, ins, outs, name, n_acc=0 if scale is None else 1)


def _swap_halves(x, lo, half):
    w = x.shape[1]
    lane = lax.broadcasted_iota(jnp.int32, x.shape, 1)
    sw = jnp.where(lane < lo + half, pltpu.roll(x, w - half, 1), pltpu.roll(x, half, 1))
    return jnp.where((lane >= lo) & (lane < lo + 2 * half), sw, 0.0)


def mla_norm(cfg, proj, qn, kvn, name):
    ql, kl = cfg.q_lora, cfg.kv_lora

    def body(c_ref, qn_ref, kn_ref, q_ref, k_ref):
        cq = c_ref[:, 0:ql]
        ck = c_ref[:, ql:ql + kl]
        q_ref[...] = (cq * lax.rsqrt(jnp.mean(cq * cq, axis=-1, keepdims=True) + RMS_EPS) * qn_ref[...]).astype(BF16)
        k_ref[...] = (ck * lax.rsqrt(jnp.mean(ck * ck, axis=-1, keepdims=True) + RMS_EPS) * kn_ref[...]).astype(BF16)

    return _rowwise(cfg, body, [(proj, cfg.cw, cfg.d // cfg.cw), (qn, None), (kvn, None)], [(ql, BF16), (kl, BF16)], name)


def mla_norm_bwd(cfg, proj, d_cqn, d_ckvn, d_kr, qn, kvn, name):
    ql, kl, cw, npe = cfg.q_lora, cfg.kv_lora, cfg.cw, cfg.nope

    def one(x, dy, gamma):
        rstd = lax.rsqrt(jnp.mean(x * x, axis=-1, keepdims=True) + RMS_EPS)
        xn = x * rstd
        dxn = dy * gamma
        return rstd * (dxn - xn * jnp.mean(dxn * xn, axis=-1, keepdims=True)), dy * xn

    def body(c_ref, dq_ref, dk_ref, dkr_ref, qn_ref, kn_ref, dc_ref, acc_ref):
        @pl.when(pl.program_id(0) == 0)
        def _():
            acc_ref[...] = jnp.zeros_like(acc_ref)

        dcq, gq = one(c_ref[:, 0:ql], dq_ref[...], qn_ref[...])
        dck, gk = one(c_ref[:, ql:ql + kl], dk_ref[...], kn_ref[...])
        dc_ref[:, 0:ql] = dcq.astype(BF16)
        dc_ref[:, ql:ql + kl] = dck.astype(BF16)
        dc_ref[:, ql + kl:ql + kl + npe] = dkr_ref[...].astype(BF16)
        rest = cw - (ql + kl + npe)
        if rest:
            dc_ref[:, ql + kl + npe:cw] = jnp.zeros((dc_ref.shape[0], rest), BF16)
        acc_ref[:, 0:ql] += _colsum8(gq)
        acc_ref[:, ql:ql + kl] += _colsum8(gk)

    return _rowwise(cfg, body, [(proj, cw, cfg.d // cw), (d_cqn, ql, 0), (d_ckvn, kl, 0), (d_kr, npe, 0), (qn, None), (kvn, None)],
                    [(cw, BF16), (cw,)], name, n_acc=1)


def _mla_tables(cfg):
    r2, npe = cfg.rope // 2, cfg.nope
    inv = ROPE_BASE ** (-jnp.arange(0, cfg.rope, 2, dtype=F32) / cfg.rope)
    ang = jnp.arange(cfg.tp, dtype=F32)[:, None] * inv[None, :]
    cos, sin = jnp.cos(ang), jnp.sin(ang)
    one = jnp.ones((cfg.tp, npe - cfg.rope), F32)
    ck = jnp.concatenate([cos, cos, one], axis=1)
    sk = jnp.concatenate([-sin, sin, 0.0 * one], axis=1)
    cq = jnp.concatenate([jnp.ones((cfg.tp, npe), F32), ck], axis=1)
    sq = jnp.concatenate([jnp.zeros((cfg.tp, npe), F32), sk], axis=1)
    return cq, sq, ck, sk


def mla_prep(cfg, q_raw, kv_raw, proj, tabs, name):
    h, npe, r2, ql, kl = cfg.mla_heads, cfg.nope, cfg.rope // 2, cfg.q_lora, cfg.kv_lora
    hp = 2 * npe

    def body(q_ref, kv_ref, c_ref, cq_ref, sq_ref, ck_ref, sk_ref, qo_ref, ko_ref, vo_ref):
        kr = c_ref[:, ql + kl:ql + kl + npe]
        kr = (kr * ck_ref[...] + _swap_halves(kr, 0, r2) * sk_ref[...]).astype(BF16)
        cq, sq = cq_ref[...], sq_ref[...]
        for i in range(h):
            q = q_ref[:, i * hp:(i + 1) * hp]
            qo_ref[:, i * hp:(i + 1) * hp] = (q * cq + _swap_halves(q, npe, r2) * sq).astype(BF16)
            ko_ref[:, i * hp:i * hp + npe] = kv_ref[:, i * npe:(i + 1) * npe].astype(BF16)
            ko_ref[:, i * hp + npe:(i + 1) * hp] = kr
        vo_ref[...] = kv_ref[:, h * npe:2 * h * npe].astype(BF16)

    cq, sq, ck, sk = tabs
    return _rowwise(cfg, body, [(q_raw, h * hp, 0), (kv_raw, 2 * h * npe, 0), (proj, cfg.cw, cfg.d // cfg.cw),
                                (cq, hp, 0), (sq, hp, 0), (ck, npe, 0), (sk, npe, 0)],
                    [(h * hp, BF16), (h * hp, BF16), (h * npe, BF16)], name)


def mla_prep_bwd(cfg, dq_full, dk_full, tabs, name, side=None):
    h, npe, r2 = cfg.mla_heads, cfg.nope, cfg.rope // 2
    hp = 2 * npe

    def body(dq_ref, dk_ref, cq_ref, sq_ref, ck_ref, sk_ref, dqo_ref, dko_ref, dkr_ref):
        cq, sq = cq_ref[...], sq_ref[...]
        dkr = None
        for i in range(h):
            dq = dq_ref[:, i * hp:(i + 1) * hp]
            dqo_ref[:, i * hp:(i + 1) * hp] = (dq * cq + _swap_halves(dq * sq, npe, r2)).astype(BF16)
            dko_ref[:, i * npe:(i + 1) * npe] = dk_ref[:, i * hp:i * hp + npe].astype(BF16)
            part = dk_ref[:, i * hp + npe:(i + 1) * hp]
            dkr = part if dkr is None else dkr + part
        dkr_ref[...] = dkr * ck_ref[...] + _swap_halves(dkr * sk_ref[...], 0, r2)

    cq, sq, ck, sk = tabs
    return _rowwise(cfg, body, [(dq_full, h * hp, 0), (dk_full, h * hp, 0), (cq, hp, 0), (sq, hp, 0), (ck, npe, 0), (sk, npe, 0)],
                    [(h * hp, BF16), (h * npe, BF16), (npe, F32)], name, side=side)


def _attn_scores(cfg, q, k, diagonal):
    s = lax.dot_general(q, k, _NT, preferred_element_type=F32) * ((cfg.nope + cfg.rope) ** -0.5)
    if diagonal:
        row = lax.broadcasted_iota(jnp.int32, s.shape, 0)
        col = lax.broadcasted_iota(jnp.int32, s.shape, 1)
        s = jnp.where(row >= col, s, -1e30)
    return s


def _tile_rows(i, t):
    return pl.ds(pl.multiple_of(i * t, t), t)


def attn_fwd(cfg, q, k, v, name, side=None):
    h, npe, tq, wide = cfg.mla_heads, cfg.nope, cfg.tq, cfg.tkw
    hp, n = 2 * npe, cfg.tp // tq

    def body(q_ref, k_ref, v_ref, o_ref, lse_ref):
        qi = pl.program_id(1)
        q = q_ref[...]

        def tile(ki, carry, width, diagonal):
            m, l, acc = carry
            rk = _tile_rows(ki, width)
            s = _attn_scores(cfg, q, k_ref[rk, :], diagonal)
            m_new = jnp.maximum(m, jnp.max(s, axis=-1, keepdims=True))
            alpha = jnp.exp(m - m_new)
            p = jnp.exp(s - m_new)
            l = alpha * l + jnp.sum(p, axis=-1, keepdims=True)
            acc = alpha * acc + lax.dot_general(p.astype(BF16), v_ref[rk, :], _NN, preferred_element_type=F32)
            return m_new, l, acc

        init = (jnp.full((tq, 1), -1e30, F32), jnp.zeros((tq, 1), F32), jnp.zeros((tq, npe), F32))
        n_wide = qi // wide
        carry = lax.fori_loop(0, n_wide, lambda j, c: tile(j, c, wide * tq, False), init)
        carry = lax.fori_loop(n_wide * wide, qi, lambda ki, c: tile(ki, c, tq, False), carry)
        m, l, acc = tile(qi, carry, tq, True)
        o_ref[...] = acc / l
        lse_ref[...] = m + jnp.log(l)

    outs, side_outs = _host_call(
        body, (h, n),
        [pl.BlockSpec((tq, hp), lambda hh, qi: (qi, hh)),
         pl.BlockSpec((cfg.tp, hp), lambda hh, qi: (0, hh)),
         pl.BlockSpec((cfg.tp, npe), lambda hh, qi: (0, hh))],
        [pl.BlockSpec((tq, npe), lambda hh, qi: (qi, hh)),
         pl.BlockSpec((None, tq, 1), lambda hh, qi: (hh, qi, 0))],
        [jax.ShapeDtypeStruct((cfg.tp, h * npe), F32), jax.ShapeDtypeStruct((h, cfg.tp, 1), F32)],
        [], ("parallel", "arbitrary"), name, (q, k, v), side)
    return outs if side is None else (outs, side_outs)


def attn_bwd(cfg, q, k, v, do, o, lse, name, side=None):
    h, npe, tq, kw = cfg.mla_heads, cfg.nope, cfg.tq, cfg.tkw
    hp, n = 2 * npe, cfg.tp // tq
    sc = (cfg.nope + cfg.rope) ** -0.5

    def body(q_ref, k_ref, v_ref, do_ref, o_ref, lse_ref, dq_ref, dk_ref, dv_ref, dv_acc):
        dk_ref[...] = jnp.zeros_like(dk_ref)
        dv_acc[...] = jnp.zeros_like(dv_acc)

        def q_tile(qi, _):
            rq = _tile_rows(qi, tq)
            q, do, lse = q_ref[rq, :], do_ref[rq, :], lse_ref[rq, :]
            delta = jnp.sum(do.astype(F32) * o_ref[rq, :], axis=-1, keepdims=True)

            def tile(ki, dq, width, diagonal):
                rk = _tile_rows(ki, width)
                kk = k_ref[rk, :]
                p = jnp.exp(_attn_scores(cfg, q, kk, diagonal) - lse)
                dp = lax.dot_general(do, v_ref[rk, :], _NT, preferred_element_type=F32)
                ds = (p * (dp - delta) * sc).astype(BF16)
                dv_acc[rk, :] += lax.dot_general(p.astype(BF16), do, _TN, preferred_element_type=F32)
                dk_ref[rk, :] += lax.dot_general(ds, q, _TN, preferred_element_type=F32)
                return dq + lax.dot_general(ds, kk, _NN, preferred_element_type=F32)

            n_wide = qi // kw
            dq = lax.fori_loop(0, n_wide, lambda j, acc: tile(j, acc, kw * tq, False), jnp.zeros((tq, hp), F32))
            dq = lax.fori_loop(n_wide * kw, qi, lambda ki, acc: tile(ki, acc, tq, False), dq)
            dq_ref[rq, :] = tile(qi, dq, tq, True)
            return 0

        lax.fori_loop(0, n, q_tile, 0)
        dv_ref[...] = dv_acc[...].astype(BF16)

    wide = pl.BlockSpec((cfg.tp, hp), lambda hh: (0, hh))
    narrow = pl.BlockSpec((cfg.tp, npe), lambda hh: (0, hh))
    outs, side_outs = _host_call(
        body, (h,), [wide, wide, narrow, narrow, narrow, pl.BlockSpec((None, cfg.tp, 1), lambda hh: (hh, 0, 0))],
        [wide, wide, narrow],
        [jax.ShapeDtypeStruct((cfg.tp, h * hp), F32), jax.ShapeDtypeStruct((cfg.tp, h * hp), F32),
         jax.ShapeDtypeStruct((cfg.tp, h * npe), BF16)],
        [pltpu.VMEM((cfg.tp, npe), F32)], ("parallel",), name, (q, k, v, do, o, lse), side)
    return outs if side is None else (outs, side_outs)


def _ret_tables(cfg):
    c = cfg.chunk
    lg = jnp.log(1.0 - 2.0 ** (-5.0 - jnp.arange(cfg.ret_heads, dtype=F32)))[:, None, None]
    j = jnp.arange(c, dtype=F32)
    diff = j[:, None] - j[None, :]
    dm = jnp.where(diff >= 0, jnp.exp(jnp.maximum(diff, 0.0)[None] * lg), 0.0)
    lq = jnp.exp((j + 1.0)[None, :, None] * lg)
    lk = jnp.exp((c - 1.0 - j)[None, :, None] * lg)
    gc = jnp.exp(c * lg)
    return dm, lq, lk, gc


def _ret_rope_tables(cfg):
    dk = cfg.d // cfg.ret_heads
    inv = ROPE_BASE ** (-jnp.arange(0, dk, 2, dtype=F32) / dk)
    ang = jnp.arange(cfg.tp, dtype=F32)[:, None] * inv[None, :]
    return jnp.cos(ang), jnp.sin(ang)


def ret_prep(cfg, qkvg, cos, sin, name):
    d, h = cfg.d, cfg.ret_heads
    dk = d // h
    hd = dk // 2
    ksc = dk ** -0.5

    def body(q_ref, k_ref, v_ref, c_ref, s_ref, qo_ref, ko_ref, vo_ref):
        c, s = c_ref[...], s_ref[...]
        for i in range(h):
            for src, dst, f in ((q_ref, qo_ref, 1.0), (k_ref, ko_ref, ksc)):
                x1 = src[:, i * dk:i * dk + hd]
                x2 = src[:, i * dk + hd:(i + 1) * dk]
                dst[:, i * dk:i * dk + hd] = ((x1 * c - x2 * s) * f).astype(BF16)
                dst[:, i * dk + hd:(i + 1) * dk] = ((x2 * c + x1 * s) * f).astype(BF16)
        vo_ref[...] = v_ref[...].astype(BF16)

    return _rowwise(cfg, body, [(qkvg, d, 0), (qkvg, d, 1), (qkvg, d, 2), (cos, hd, 0), (sin, hd, 0)],
                    [(d, BF16), (d, BF16), (d, BF16)], name)


def ret_prep_bwd(cfg, dq, dk_, dv, dg, cos, sin, name):
    d, h = cfg.d, cfg.ret_heads
    dk = d // h
    hd = dk // 2
    ksc = dk ** -0.5

    def body(dq_ref, dk_ref, dv_ref, dg_ref, c_ref, s_ref, o_ref):
        c, s = c_ref[...], s_ref[...]
        for i in range(h):
            for src, off, f in ((dq_ref, 0, 1.0), (dk_ref, d, ksc)):
                y1 = src[:, i * dk:i * dk + hd]
                y2 = src[:, i * dk + hd:(i + 1) * dk]
                o_ref[:, off + i * dk:off + i * dk + hd] = ((y1 * c + y2 * s) * f).astype(BF16)
                o_ref[:, off + i * dk + hd:off + (i + 1) * dk] = ((y2 * c - y1 * s) * f).astype(BF16)
        o_ref[:, 2 * d:3 * d] = dv_ref[...]
        o_ref[:, 3 * d:4 * d] = dg_ref[...]

    return _rowwise(cfg, body, [(dq, d, 0), (dk_, d, 0), (dv, d, 0), (dg, d, 0), (cos, hd, 0), (sin, hd, 0)],
                    [(4 * d, BF16)], name)[0]


def _ret_specs(cfg, reverse):
    c, h = cfg.chunk, cfg.ret_heads
    dk = cfg.d // h
    n = cfg.tp // c
    tm = (lambda t: n - 1 - t) if reverse else (lambda t: t)
    blk = pl.BlockSpec((c, dk), lambda hh, t: (tm(t), hh))
    row = pl.BlockSpec((None, c, 1), lambda hh, t: (hh, tm(t), 0))
    tabs = [pl.BlockSpec((None, c, c), lambda hh, t: (hh, 0, 0)), pl.BlockSpec((None, c, 1), lambda hh, t: (hh, 0, 0)),
            pl.BlockSpec((None, c, 1), lambda hh, t: (hh, 0, 0)), pl.BlockSpec((None, 1, 1), lambda hh, t: (hh, 0, 0))]
    return blk, row, tabs, n, dk


def _ret_state_update(s_ref, k, v, lk, gc):
    kw = (k.astype(F32) * lk).astype(BF16)
    s_ref[...] = gc * s_ref[...] + lax.dot_general(kw, v, _TN, preferred_element_type=F32)


def ret_fwd(cfg, q, k, v, tabs, name):
    blk, row, tspecs, n, dk = _ret_specs(cfg, False)

    def body(q_ref, k_ref, v_ref, dm_ref, lq_ref, lk_ref, gc_ref, on_ref, rs_ref, s_ref):
        @pl.when(pl.program_id(1) == 0)
        def _():
            s_ref[...] = jnp.zeros_like(s_ref)

        qq, kk, vv = q_ref[...], k_ref[...], v_ref[...]
        a = lax.dot_general(qq, kk, _NT, preferred_element_type=F32) * dm_ref[...]
        o = lax.dot_general(a.astype(BF16), vv, _NN, preferred_element_type=F32)
        o = o + lax.dot_general(qq, s_ref[...].astype(BF16), _NN, preferred_element_type=F32) * lq_ref[...]
        _ret_state_update(s_ref, kk, vv, lk_ref[...], gc_ref[...])
        mu = jnp.mean(o, axis=-1, keepdims=True)
        oc = o - mu
        rstd = lax.rsqrt(jnp.mean(oc * oc, axis=-1, keepdims=True) + LN_EPS)
        on_ref[...] = oc * rstd
        rs_ref[...] = rstd

    return pl.pallas_call(body, grid=(cfg.ret_heads, n), in_specs=[blk, blk, blk] + tspecs, out_specs=[blk, row],
                          out_shape=[jax.ShapeDtypeStruct((cfg.tp, cfg.d), F32), jax.ShapeDtypeStruct((cfg.ret_heads, cfg.tp, 1), F32)],
                          scratch_shapes=[pltpu.VMEM((dk, dk), F32)], compiler_params=_cparams("parallel", "arbitrary"),
                          name=name)(q, k, v, *tabs)


def ret_bwd_q(cfg, q, k, v, d_on, on, rstd, tabs, name):
    blk, row, tspecs, n, dk = _ret_specs(cfg, False)

    def body(q_ref, k_ref, v_ref, don_ref, on_ref, rs_ref, dm_ref, lq_ref, lk_ref, gc_ref, do_ref, dq_ref, s_ref):
        @pl.when(pl.program_id(1) == 0)
        def _():
            s_ref[...] = jnp.zeros_like(s_ref)

        kk, vv = k_ref[...], v_ref[...]
        don, on = don_ref[...], on_ref[...]
        do = rs_ref[...] * (don - jnp.mean(don, axis=-1, keepdims=True) - on * jnp.mean(don * on, axis=-1, keepdims=True))
        dob = do.astype(BF16)
        do_ref[...] = dob
        ds = lax.dot_general(dob, vv, _NT, preferred_element_type=F32) * dm_ref[...]
        dq = lax.dot_general(ds.astype(BF16), kk, _NN, preferred_element_type=F32)
        dq_ref[...] = dq + lax.dot_general(dob, s_ref[...].astype(BF16), _NT, preferred_element_type=F32) * lq_ref[...]
        _ret_state_update(s_ref, kk, vv, lk_ref[...], gc_ref[...])

    return pl.pallas_call(body, grid=(cfg.ret_heads, n), in_specs=[blk, blk, blk, blk, blk, row] + tspecs, out_specs=[blk, blk],
                          out_shape=[jax.ShapeDtypeStruct((cfg.tp, cfg.d), BF16), jax.ShapeDtypeStruct((cfg.tp, cfg.d), F32)],
                          scratch_shapes=[pltpu.VMEM((dk, dk), F32)], compiler_params=_cparams("parallel", "arbitrary"),
                          name=name)(q, k, v, d_on, on, rstd, *tabs)


def ret_bwd_kv(cfg, q, k, v, do, tabs, name):
    blk, row, tspecs, n, dk = _ret_specs(cfg, True)

    def body(q_ref, k_ref, v_ref, do_ref, dm_ref, lq_ref, lk_ref, gc_ref, dk_ref, dv_ref, g_ref):
        @pl.when(pl.program_id(1) == 0)
        def _():
            g_ref[...] = jnp.zeros_like(g_ref)

        qq, kk, vv, dob = q_ref[...], k_ref[...], v_ref[...], do_ref[...]
        dm, lk = dm_ref[...], lk_ref[...]
        gb = g_ref[...].astype(BF16)
        a = lax.dot_general(qq, kk, _NT, preferred_element_type=F32) * dm
        ds = lax.dot_general(dob, vv, _NT, preferred_element_type=F32) * dm
        dkk = lax.dot_general(ds.astype(BF16), qq, _TN, preferred_element_type=F32)
        dk_ref[...] = dkk + lax.dot_general(vv, gb, _NT, preferred_element_type=F32) * lk
        kw = (kk.astype(F32) * lk).astype(BF16)
        dvv = lax.dot_general(a.astype(BF16), dob, _TN, preferred_element_type=F32)
        dv_ref[...] = (dvv + lax.dot_general(kw, gb, _NN, preferred_element_type=F32)).astype(BF16)
        qw = (qq.astype(F32) * lq_ref[...]).astype(BF16)
        g_ref[...] = gc_ref[...] * g_ref[...] + lax.dot_general(qw, dob, _TN, preferred_element_type=F32)

    return pl.pallas_call(body, grid=(cfg.ret_heads, n), in_specs=[blk, blk, blk, blk] + tspecs, out_specs=[blk, blk],
                          out_shape=[jax.ShapeDtypeStruct((cfg.tp, cfg.d), F32), jax.ShapeDtypeStruct((cfg.tp, cfg.d), BF16)],
                          scratch_shapes=[pltpu.VMEM((dk, dk), F32)], compiler_params=_cparams("parallel", "arbitrary"),
                          name=name)(q, k, v, do, *tabs)


def _flat2(a):
    return a.reshape(-1, a.shape[-1])


def _ew_call(body, ins, out_dtypes, name, side=None):
    r, c = ins[-1].shape[-2:]
    tr = _div_tile(r, max(16, (1 << 18) // c), 16)
    specs = []
    for a in ins:
        if a.ndim == 3:
            specs.append(pl.BlockSpec((a.shape[0], tr, c), lambda i: (0, i, 0)))
        else:
            specs.append(pl.BlockSpec((tr, c), lambda i: (i, 0)))
    outs, side_outs = _host_call(body, (r // tr,), specs, [pl.BlockSpec((tr, c), lambda i: (i, 0)) for _ in out_dtypes],
                                 [jax.ShapeDtypeStruct((r, c), dt) for dt in out_dtypes], [], ("parallel",), name, ins, side)
    return outs if side is None else (outs, side_outs)


def adamw(w, g, m, v, name, side=None):
    c1 = 1.0 - ADAM_B1 ** ADAM_STEP
    c2 = 1.0 - ADAM_B2 ** ADAM_STEP

    def body(w_ref, g_ref, m_ref, v_ref, d_ref, mo_ref, vo_ref):
        gg = g_ref[...]
        mn = ADAM_B1 * m_ref[...] + (1.0 - ADAM_B1) * gg
        vn = ADAM_B2 * v_ref[...] + (1.0 - ADAM_B2) * (gg * gg)
        d_ref[...] = -ADAM_LR * ((mn / c1) / (jnp.sqrt(vn / c2) + ADAM_EPS) + ADAM_WD * w_ref[...])
        mo_ref[...] = mn
        vo_ref[...] = vn

    got = _ew_call(body, [_flat2(w), _flat2(g), _flat2(m), _flat2(v)], [F32, F32, F32], name, side)
    outs, side_outs = (got, None) if side is None else got
    outs = [o.reshape(w.shape) for o in outs]
    return outs if side is None else (outs, side_outs)


def add_pair_bf16(g, theirs, core, name):
    c = g.shape[-1]
    g4, t3 = g.reshape(4, 2, -1, c), theirs.reshape(4, -1, c)
    rows = t3.shape[1]
    tr = _div_tile(rows, max(16, (1 << 18) // c), 16)

    def body(core_ref, g_ref, t_ref, o_ref):
        o_ref[...] = (g_ref[...].astype(F32) + t_ref[...].astype(F32)).astype(BF16)

    spec = pl.BlockSpec((None, tr, c), lambda q, i, core_ref: (q, i, 0))
    grid_spec = pltpu.PrefetchScalarGridSpec(
        num_scalar_prefetch=1, grid=(4, rows // tr),
        in_specs=[pl.BlockSpec((None, None, tr, c), lambda q, i, core_ref: (q, core_ref[0], i, 0)), spec], out_specs=spec)
    out = pl.pallas_call(body, grid_spec=grid_spec, out_shape=jax.ShapeDtypeStruct((4, rows, c), BF16),
                         compiler_params=_cparams("parallel", "parallel"), name=name)(core, g4, t3)
    return out.reshape(theirs.shape)


def sum_slots(parts, landed, chip_core, name):
    c = parts.shape[-1]
    p3, l3 = parts.reshape(4, -1, c), landed.reshape(3, -1, c)
    rows = p3.shape[1]
    tr = _div_tile(rows, max(16, (1 << 18) // c), 16)

    def body(cc_ref, p_ref, l_ref, o_ref):
        acc = p_ref[...].astype(F32)
        for s in range(3):
            acc = acc + l_ref[s].astype(F32)
        o_ref[...] = acc

    grid_spec = pltpu.PrefetchScalarGridSpec(
        num_scalar_prefetch=1, grid=(rows // tr,),
        in_specs=[pl.BlockSpec((None, tr, c), lambda i, cc: (cc[0], i, 0)), pl.BlockSpec((3, tr, c), lambda i, cc: (0, i, 0))],
        out_specs=pl.BlockSpec((None, tr, c), lambda i, cc: (cc[1], i, 0)))
    out = pl.pallas_call(body, grid_spec=grid_spec, out_shape=jax.ShapeDtypeStruct((2, rows, c), F32),
                         compiler_params=_cparams("parallel"), name=name)(chip_core, p3, l3)
    return out.reshape((2,) + parts.shape[2:])


def cast_place(w, chip, name):
    c = w.shape[-1]
    w2 = w.reshape(-1, c)
    rows = w2.shape[0]
    tr = _div_tile(rows, max(16, (1 << 18) // c), 16)

    def body(chip_ref, w_ref, o_ref):
        o_ref[...] = w_ref[...].astype(BF16)

    grid_spec = pltpu.PrefetchScalarGridSpec(
        num_scalar_prefetch=1, grid=(rows // tr,),
        in_specs=[pl.BlockSpec((tr, c), lambda i, chip_ref: (i, 0))],
        out_specs=pl.BlockSpec((None, tr, c), lambda i, chip_ref: (chip_ref[0], i, 0)))
    out = pl.pallas_call(body, grid_spec=grid_spec, out_shape=jax.ShapeDtypeStruct((4, rows, c), BF16),
                         compiler_params=_cparams("parallel"), name=name)(chip, w2)
    return out.reshape((4,) + w.shape)


def _coords():
    return lax.axis_index("x"), lax.axis_index("y"), lax.axis_index("c")


def _chip_peer(x, y, k):
    return (x ^ (k >> 1), y ^ (k & 1))


_ANY = pl.BlockSpec(memory_space=pl.ANY)
DMA_PIECE_BYTES = 2 << 20
DMA_MAX_PIECES = 32


def _piece_plan(shape, itemsize):
    want = max(1, min(DMA_MAX_PIECES, math.prod(shape) * itemsize // DMA_PIECE_BYTES))
    plan = []
    for ax, n in enumerate(shape[:-1]):
        if want <= 1:
            break
        rows_tiled = ax == len(shape) - 2
        k = max([1] + [c for c in range(2, min(n, want) + 1) if n % c == 0 and (not rows_tiled or (n // c) % 16 == 0)])
        if k > 1:
            plan.append((ax, k))
            want = -(-want // k)
    return plan


class _Copy:
    def __init__(self, src, dst, send_sem, recv_sem=None, device_id=None):
        self.src, self.dst, self.send_sem, self.recv_sem, self.device_id = src, dst, send_sem, recv_sem, device_id

    def _dma(self, src, dst):
        if self.device_id is None:
            return pltpu.make_async_copy(src, dst, self.send_sem)
        return pltpu.make_async_remote_copy(src_ref=src, dst_ref=dst, send_sem=self.send_sem, recv_sem=self.recv_sem,
                                            device_id=self.device_id, device_id_type=MESH)

    def start(self):
        shape = self.src.shape
        plan = _piece_plan(shape, jnp.dtype(self.src.dtype).itemsize)
        for pick in np.ndindex(*[k for _, k in plan]):
            idx = [slice(None)] * len(shape)
            for (ax, k), i in zip(plan, pick):
                step = shape[ax] // k
                idx[ax] = pl.ds(i * step, step)
            idx = tuple(idx)
            self._dma(self.src.at[idx], self.dst.at[idx]).start()

    def wait(self):
        self._dma(self.src, self.dst).wait()

    def wait_send(self):
        self._dma(self.src, self.dst).wait_send()

    def wait_recv(self):
        self._dma(self.src, self.dst).wait_recv()


def allreduce_small(x, name):
    r, _, w = x.shape

    def body(x_ref, o_ref, mine_ref, gat_ref, send_sems, recv_sems):
        mx, my, mc = _coords()
        me = 4 * mx + 2 * my + mc
        mine_ref[...] = jnp.sum(x_ref[...], axis=1)
        gat_ref[me] = mine_ref[...]
        copies = []
        for k in range(1, 8):
            peer = (mx ^ (k >> 2), my ^ ((k >> 1) & 1), mc ^ (k & 1))
            cp = pltpu.make_async_remote_copy(src_ref=mine_ref, dst_ref=gat_ref.at[me], send_sem=send_sems.at[k - 1],
                                              recv_sem=recv_sems.at[k - 1], device_id=peer, device_id_type=MESH)
            cp.start()
            copies.append(cp)
        for k in range(1, 8):
            pltpu.make_async_remote_copy(src_ref=mine_ref, dst_ref=gat_ref.at[me ^ k], send_sem=send_sems.at[k - 1],
                                         recv_sem=recv_sems.at[k - 1], device_id=(mx, my, mc), device_id_type=MESH).wait_recv()
        for cp in copies:
            cp.wait_send()
        acc = gat_ref[0]
        for s in range(1, 8):
            acc = acc + gat_ref[s]
        o_ref[...] = acc

    return pl.pallas_call(body, out_shape=jax.ShapeDtypeStruct((r, w), F32),
                          in_specs=[pl.BlockSpec(memory_space=pltpu.VMEM)], out_specs=pl.BlockSpec(memory_space=pltpu.VMEM),
                          scratch_shapes=[pltpu.VMEM((r, w), F32), pltpu.VMEM((8, r, w), F32),
                                          pltpu.SemaphoreType.DMA((7,)), pltpu.SemaphoreType.DMA((7,))],
                          compiler_params=pltpu.CompilerParams(vmem_limit_bytes=VMEM_LIMIT_V7X), name=name)(x)


def allgather_chips(shards, name):
    n = len(shards)

    def body(*refs):
        outs = refs[n:2 * n]
        s1_send, s1_recv, s2_send, s2_recv = refs[2 * n:]
        mx, my, mc = _coords()
        p = 2 * mx + my
        sib = (mx, my, 1 - mc)
        sends = []
        for a in range(n):
            h0 = outs[a].shape[1] // 2
            mine = outs[a].at[p, pl.ds(mc * h0, h0)]
            for k in (1, 2, 3):
                qx, qy = _chip_peer(mx, my, k)
                cp = _Copy(mine, mine, s1_send.at[a, k - 1], s1_recv.at[a, k - 1], (qx, qy, mc))
                cp.start()
                sends.append(cp)
        for a in range(n):
            h0 = outs[a].shape[1] // 2
            half = pl.ds(mc * h0, h0)
            for k in (1, 2, 3):
                landed = outs[a].at[p ^ k, half]
                _Copy(landed, landed, s1_send.at[a, k - 1], s1_recv.at[a, k - 1], sib).wait_recv()
                fw = _Copy(landed, landed, s2_send.at[a, k - 1], s2_recv.at[a, k - 1], sib)
                fw.start()
                sends.append(fw)
        for a in range(n):
            h0 = outs[a].shape[1] // 2
            other = pl.ds((1 - mc) * h0, h0)
            for k in (1, 2, 3):
                theirs = outs[a].at[p ^ k, other]
                _Copy(theirs, theirs, s2_send.at[a, k - 1], s2_recv.at[a, k - 1], sib).wait_recv()
        for cp in sends:
            cp.wait_send()

    return pl.pallas_call(body, out_shape=[jax.ShapeDtypeStruct(s.shape, s.dtype) for s in shards],
                          in_specs=[_ANY] * n, out_specs=[_ANY] * n, input_output_aliases={a: a for a in range(n)},
                          scratch_shapes=[pltpu.SemaphoreType.DMA((n, 3)), pltpu.SemaphoreType.DMA((n, 3)),
                                          pltpu.SemaphoreType.DMA((n, 3)), pltpu.SemaphoreType.DMA((n, 3))],
                          name=name)(*shards)


def ag_chips_side(bufs):
    n = len(bufs)

    def each(inplace, sems, act):
        s_send, s_recv = sems
        mx, my, mc = _coords()
        p = 2 * mx + my
        for a in range(n):
            h0 = inplace[a].shape[1] // 2
            half = pl.ds(mc * h0, h0)
            mine = inplace[a].at[p, half]
            for k in (1, 2, 3):
                qx, qy = _chip_peer(mx, my, k)
                act(_Copy(mine, mine, s_send.at[a, k - 1], s_recv.at[a, k - 1], (qx, qy, mc)),
                    _Copy(inplace[a].at[p ^ k, half], inplace[a].at[p ^ k, half], s_send.at[a, k - 1], s_recv.at[a, k - 1], (qx, qy, mc)))

    def start(ins, inplace, outs, sems):
        each(inplace, sems, lambda send, landed: send.start())

    def finish(ins, inplace, outs, sems):
        each(inplace, sems, lambda send, landed: (landed.wait_recv(), send.wait_send()))

    return Side([], list(bufs), [], [pltpu.SemaphoreType.DMA((n, 3)), pltpu.SemaphoreType.DMA((n, 3))], start, finish)


def ag_forward_side(bufs):
    n = len(bufs)

    def each(inplace, sems, act):
        s_send, s_recv = sems
        mx, my, mc = _coords()
        p = 2 * mx + my
        sib = (mx, my, 1 - mc)
        for a in range(n):
            h0 = inplace[a].shape[1] // 2
            for k in (1, 2, 3):
                landed = inplace[a].at[p ^ k, pl.ds(mc * h0, h0)]
                theirs = inplace[a].at[p ^ k, pl.ds((1 - mc) * h0, h0)]
                act(_Copy(landed, landed, s_send.at[a, k - 1], s_recv.at[a, k - 1], sib),
                    _Copy(theirs, theirs, s_send.at[a, k - 1], s_recv.at[a, k - 1], sib))

    def start(ins, inplace, outs, sems):
        each(inplace, sems, lambda send, theirs: send.start())

    def finish(ins, inplace, outs, sems):
        each(inplace, sems, lambda send, theirs: (theirs.wait_recv(), send.wait_send()))

    return Side([], list(bufs), [], [pltpu.SemaphoreType.DMA((n, 3)), pltpu.SemaphoreType.DMA((n, 3))], start, finish)


def rs_chips_side(parts):
    n = len(parts)

    def copies(ins, outs, sems):
        send_sems, recv_sems = sems
        mx, my, mc = _coords()
        p = 2 * mx + my
        return [_Copy(ins[a].at[p ^ k], outs[a].at[k - 1], send_sems.at[a, k - 1], recv_sems.at[a, k - 1],
                      (*_chip_peer(mx, my, k), mc)) for a in range(n) for k in (1, 2, 3)]

    def start(ins, inplace, outs, sems):
        for cp in copies(ins, outs, sems):
            cp.start()

    def finish(ins, inplace, outs, sems):
        for cp in copies(ins, outs, sems):
            cp.wait()

    return Side(list(parts), [], [jax.ShapeDtypeStruct((3,) + g.shape[1:], g.dtype) for g in parts],
                [pltpu.SemaphoreType.DMA((n, 3)), pltpu.SemaphoreType.DMA((n, 3))], start, finish)


def rs_pair_side(grads):
    n = len(grads)

    def copies(ins, outs, sems):
        send_sems, recv_sems = sems
        mx, my, mc = _coords()
        return [_Copy(ins[a].at[:, pl.ds(1 - mc, 1)], outs[a], send_sems.at[a], recv_sems.at[a], (mx, my, 1 - mc)) for a in range(n)]

    def start(ins, inplace, outs, sems):
        for cp in copies(ins, outs, sems):
            cp.start()

    def finish(ins, inplace, outs, sems):
        for cp in copies(ins, outs, sems):
            cp.wait()

    return Side(list(grads), [], [jax.ShapeDtypeStruct((4, 1) + g.shape[2:], g.dtype) for g in grads],
                [pltpu.SemaphoreType.DMA((n,)), pltpu.SemaphoreType.DMA((n,))], start, finish)


def exchange_halves(grads, name):
    n = len(grads)
    side = rs_pair_side(grads)

    def body(*refs):
        ins, outs, sems = refs[:n], refs[n:2 * n], refs[2 * n:]
        side.start(ins, [], outs, sems)
        side.finish(ins, [], outs, sems)

    return pl.pallas_call(body, out_shape=side.outs, in_specs=[_ANY] * n, out_specs=[_ANY] * n, scratch_shapes=side.sems,
                          name=name)(*grads)


def scatter_chips(parts, name):
    n = len(parts)
    side = rs_chips_side(parts)

    def body(*refs):
        ins, outs, sems = refs[:n], refs[n:2 * n], refs[2 * n:]
        side.start(ins, [], outs, sems)
        side.finish(ins, [], outs, sems)

    return pl.pallas_call(body, out_shape=side.outs, in_specs=[_ANY] * n, out_specs=[_ANY] * n, scratch_shapes=side.sems,
                          name=name)(*parts)


def rs_join_side(halves):
    n = len(halves)

    def each(inplace, sems, act):
        send_sems, recv_sems = sems
        mx, my, mc = _coords()
        for a in range(n):
            mine, theirs = inplace[a].at[pl.ds(mc, 1)], inplace[a].at[pl.ds(1 - mc, 1)]
            act(_Copy(mine, mine, send_sems.at[a], recv_sems.at[a], (mx, my, 1 - mc)),
                _Copy(theirs, theirs, send_sems.at[a], recv_sems.at[a], (mx, my, 1 - mc)))

    def start(ins, inplace, outs, sems):
        each(inplace, sems, lambda send, theirs: send.start())

    def finish(ins, inplace, outs, sems):
        each(inplace, sems, lambda send, theirs: (send.wait_send(), theirs.wait_recv()))

    return Side([], list(halves), [], [pltpu.SemaphoreType.DMA((n,)), pltpu.SemaphoreType.DMA((n,))], start, finish)


def join_halves(halves, name):
    n = len(halves)
    side = rs_join_side(halves)

    def body(*refs):
        outs, sems = refs[n:2 * n], refs[2 * n:]
        side.start([], outs, [], sems)
        side.finish([], outs, [], sems)

    return pl.pallas_call(body, out_shape=[jax.ShapeDtypeStruct(g.shape, g.dtype) for g in halves],
                          in_specs=[_ANY] * n, out_specs=[_ANY] * n, input_output_aliases={a: a for a in range(n)},
                          scratch_shapes=side.sems, name=name)(*halves)


def rs_pair_sums(grads, tag, call=None):
    out, theirs = (None, exchange_halves(grads, f"rs_pair_{tag}")) if call is None else call(rs_pair_side(grads))
    core = lax.axis_index("c").astype(jnp.int32).reshape(1)
    return out, [add_pair_bf16(g, t, core, f"rs_add_{tag}_{i}") for i, (g, t) in enumerate(zip(grads, theirs))]


def rs_finish(parts, landed, tag, call=None):
    chip_core = jnp.stack([2 * lax.axis_index("x") + lax.axis_index("y"), lax.axis_index("c")]).astype(jnp.int32)
    halves = [sum_slots(pt, ld, chip_core, f"rs_sum_{tag}_{i}") for i, (pt, ld) in enumerate(zip(parts, landed))]
    return (None, join_halves(halves, f"rs_join_{tag}")) if call is None else call(rs_join_side(halves))


W_NAMES = ['meta_tokens', 'l0_w_in', 'l0_conv_w', 'l0_conv_b', 'l0_w_a', 'l0_b_a', 'l0_w_x', 'l0_b_x', 'l0_lam', 'l0_w_out',
           'l0_ln_g', 'l0_ln_b', 'l1_w_in', 'l1_w_grp', 'l1_scale', 'l1_w_out', 'l1_ln_g', 'l1_ln_b', 'l2_w_in', 'l2_q_norm',
           'l2_w_uq', 'l2_kv_norm', 'l2_w_ukv', 'l2_w_out', 'l2_ln_g', 'l2_ln_b', 'l3_w_in', 'l3_w_out', 'l3_ln_g', 'l3_ln_b']
BIG = {0: ['l0_w_in', 'l0_w_a', 'l0_w_x', 'l0_w_out'], 1: ['l1_w_in', 'l1_w_grp', 'l1_w_out'],
       2: ['l2_w_in', 'l2_w_uq', 'l2_w_ukv', 'l2_w_out'], 3: ['l3_w_in', 'l3_w_out']}
SMALL_SHARDED = ['meta_tokens', 'l0_conv_w']
SMALL_REPL = ['l0_conv_b', 'l0_b_a', 'l0_b_x', 'l0_lam', 'l0_ln_g', 'l0_ln_b', 'l1_scale', 'l1_ln_g', 'l1_ln_b',
              'l2_q_norm', 'l2_kv_norm', 'l2_ln_g', 'l2_ln_b', 'l3_ln_g', 'l3_ln_b']


def _rows8(rows, width):
    out = []
    for r in rows:
        r = r.reshape(-1, r.shape[-1])
        out.append(jnp.pad(r, ((0, 8 - r.shape[0]), (0, width - r.shape[1]))))
    return jnp.stack(out)


def _unblock(g4, axis):
    return jnp.concatenate([g4[i] for i in range(4)], axis=axis)


def _block(full, axis):
    return jnp.stack(jnp.split(full, 4, axis=axis))


def _step(cfg, a):
    d, tp, nm, seq = cfg.d, cfg.tp, cfg.n_meta, cfg.seq
    alpha = (2.0 * cfg.depth) ** 0.25
    mx, my, mc = _coords()
    p = 2 * mx + my
    dq4 = d // 4
    hm, npe, ql, kl = cfg.mla_heads, cfg.nope, cfg.q_lora, cfg.kv_lora
    hp = 2 * npe
    qk = npe + cfg.rope
    vec = lambda name: a[name].reshape(1, -1)

    sm = jnp.concatenate([a['meta_tokens'], a['l0_conv_w'].reshape(CONV_W, dq4)], axis=0)
    placed = lax.dynamic_update_slice(jnp.zeros((nm + CONV_W, d), F32), sm * (mc == 0).astype(F32), (0, p * dq4))
    gathered = allreduce_small(_rows8([placed[i:i + 1] for i in range(nm + CONV_W)], d), "gather_small")
    meta_full, conv_w4 = gathered[:nm], gathered[nm:]

    chip = p.astype(jnp.int32).reshape(1)
    wg = {n: cast_place(a[n], chip, f"place_{n}") for names in BIG.values() for n in names}
    wg.update(zip(BIG[0], allgather_chips([wg[n] for n in BIG[0]], "ag_l0")))

    def hosted(call, names):
        out, got = call(ag_chips_side([wg[n] for n in names]))
        wg.update(zip(names, got))
        return out

    def forwarded(call, names):
        out, got = call(ag_forward_side([wg[n] for n in names]))
        wg.update(zip(names, got))
        return out

    heads_first = lambda w: jnp.moveaxis(w, 0, 1).reshape(w.shape[1], w.shape[0] * w.shape[2], w.shape[3])
    w_out = lambda i: wg[f'l{i}_w_out'].reshape(1, d, d)

    zpad = jnp.zeros((tp - nm - seq, d), F32)
    h0 = jnp.concatenate([meta_full, a['x'][0], zpad], axis=0)
    h0_bf = h0.astype(BF16)

    w0_in, w0_a, w0_x = wg['l0_w_in'], heads_first(wg['l0_w_a']), heads_first(wg['l0_w_x'])
    ug0 = hosted(lambda side: mm_nn(cfg, h0_bf, w0_in, "l0_in", side=side), ['l1_w_in'])
    u0, u0_bf = conv_fwd(cfg, ug0, conv_w4, vec('l0_conv_b'), "l0_conv")
    pa0 = bd_nn(cfg, u0_bf, w0_a, "l0_gate_a")
    px0 = bd_nn(cfg, u0_bf, w0_x, "l0_gate_x")
    hs0, hprev0, y0 = hosted(lambda side: lru_fwd(cfg, u0, pa0, px0, ug0, vec('l0_b_a'), vec('l0_b_x'), vec('l0_lam'), "l0_lru",
                                                  side=side), ['l1_w_out'])
    o0 = hosted(lambda side: mm_nn(cfg, y0, w_out(0), "l0_out", side=side), ['l1_w_grp'])
    h1, h1_bf, xh0, rs0 = forwarded(lambda side: ln_fwd(cfg, h0, o0, vec('l0_ln_g'), vec('l0_ln_b'), alpha, "l0_ln", side=side), BIG[1])

    w1_in, w1_grp = wg['l1_w_in'], heads_first(wg['l1_w_grp'])
    ug1 = hosted(lambda side: mm_nn(cfg, h1_bf, w1_in, "l1_in", side=side), ['l2_w_in'])
    p1 = pool_fwd(cfg, ug1, "l1_pool")
    mm1 = bd_nn(cfg, p1, w1_grp, "l1_grp")
    y1 = gate_fwd(cfg, mm1, ug1, 1, vec('l1_scale'), "l1_gate")
    o1 = hosted(lambda side: mm_nn(cfg, y1, w_out(1), "l1_out", side=side), ['l2_w_uq', 'l2_w_ukv'])
    h2, h2_bf, xh1, rs1 = forwarded(lambda side: ln_fwd(cfg, h1, o1, vec('l1_ln_g'), vec('l1_ln_b'), alpha, "l1_ln", side=side),
                                    ['l2_w_in', 'l2_w_uq', 'l2_w_ukv'])

    w2_in = jnp.pad(_unblock(wg['l2_w_in'], 1), ((0, 0), (0, cfg.cw - (ql + kl + cfg.rope))))[None]
    w2_uq = jnp.pad(_unblock(wg['l2_w_uq'], 1).reshape(ql, hm, qk), ((0, 0), (0, 0), (0, hp - qk))).reshape(1, ql, hm * hp)
    w2_ukv = _unblock(wg['l2_w_ukv'], 1).reshape(kl, hm, 2, npe).transpose(0, 2, 1, 3).reshape(1, kl, 2 * hm * npe)
    proj = hosted(lambda side: mm_nn(cfg, h2_bf, w2_in, "l2_in", side=side), ['l2_w_out'])
    cqn, ckvn = mla_norm(cfg, proj, vec('l2_q_norm'), vec('l2_kv_norm'), "l2_norm")
    q_raw = forwarded(lambda side: mm_nn(cfg, cqn, w2_uq, "l2_uq", side=side), ['l2_w_out'])
    kv_raw = mm_nn(cfg, ckvn, w2_ukv, "l2_ukv")
    tabs = _mla_tables(cfg)
    qf, kf, vb = mla_prep(cfg, q_raw, kv_raw, proj, tabs, "l2_prep")
    o_att, lse = hosted(lambda side: attn_fwd(cfg, qf, kf, vb, "l2_attn", side=side), ['l3_w_in', 'l3_w_out'])
    y2 = gate_fwd(cfg, o_att, proj, 0, None, "l2_gate")
    w2_out = w_out(2)
    o2 = forwarded(lambda side: mm_nn(cfg, y2, w2_out, "l2_out", side=side), BIG[3])
    h3, h3_bf, xh2, rs2 = ln_fwd(cfg, h2, o2, vec('l2_ln_g'), vec('l2_ln_b'), alpha, "l2_ln")

    w3_in = wg['l3_w_in']
    qkvg = mm_nn(cfg, h3_bf, w3_in, "l3_in")
    cos, sin = _ret_rope_tables(cfg)
    qr, kr, vr = ret_prep(cfg, qkvg, cos, sin, "l3_prep")
    rt = _ret_tables(cfg)
    on, rsr = ret_fwd(cfg, qr, kr, vr, rt, "l3_ret")
    y3 = gate_fwd(cfg, on, qkvg, 3, None, "l3_gate")
    o3 = mm_nn(cfg, y3, w_out(3), "l3_out")
    h4, _, xh3, rs3 = ln_fwd(cfg, h3, o3, vec('l3_ln_g'), vec('l3_ln_b'), alpha, "l3_ln")

    tgt = jnp.concatenate([jnp.zeros((nm, d), F32), a['loss_target'][0], zpad], axis=0)
    dy4, lacc = loss_grad(cfg, h4, tgt, "loss")
    loss = lax.psum(0.5 * jnp.sum(lacc) / d, ("x", "y", "c"))

    grads, small, parts, landed = {}, {}, {}, {}

    def rs_begin(layer, full, call=None, names=None):
        names = BIG[layer] if names is None else names
        out, got = rs_pair_sums([g.reshape((4, 2, g.shape[1] // 2) + g.shape[2:]) for g in full], names[0], call)
        parts.update(zip(names, got))
        return out

    def carried(call, names):
        out, got = call(rs_chips_side([parts[n] for n in names]))
        landed.update(zip(names, got))
        return out

    def rs_end(layer, call=None):
        names = BIG[layer]
        out, got = rs_finish([parts[n] for n in names], [landed[n] for n in names], f"l{layer}", call)
        for n, g in zip(names, got):
            grads[n] = g.reshape(a[n].shape)
        return out

    dz3, dz3_bf, small['l3_ln_g'], small['l3_ln_b'] = ln_bwd(cfg, None, dy4, xh3, rs3, vec('l3_ln_g'), alpha, "l3_ln_b")
    g_out = mm_tn(cfg, y3, dz3_bf, 1, "l3_dw_out")
    dyb = mm_nt(cfg, dz3_bf, w_out(3), "l3_dy")
    d_on, dg = gate_bwd(cfg, dyb, on, qkvg, 3, None, F32, "l3_gate_b")
    do_bf, dqr = ret_bwd_q(cfg, qr, kr, vr, d_on, on, rsr, rt, "l3_ret_bq")
    dkr, dvr = ret_bwd_kv(cfg, qr, kr, vr, do_bf, rt, "l3_ret_bkv")
    d_in = ret_prep_bwd(cfg, dqr, dkr, dvr, dg, cos, sin, "l3_prep_b")
    g_in = mm_tn(cfg, h3_bf, d_in, 4, "l3_dw_in")
    dh = mm_nt(cfg, d_in, w3_in, "l3_dh")
    g3 = [g_in, g_out.reshape(4, dq4, d)]

    dz2, dz2_bf, small['l2_ln_g'], small['l2_ln_b'] = ln_bwd(cfg, dz3, dh, xh2, rs2, vec('l2_ln_g'), alpha, "l2_ln_b")
    g_out = rs_begin(3, g3[:1], lambda side: mm_tn(cfg, y2, dz2_bf, 1, "l2_dw_out", side=side), ['l3_w_in'])
    dyb = rs_begin(3, g3[1:], lambda side: mm_nt(cfg, dz2_bf, w_out(2), "l2_dy", side=side), ['l3_w_out'])
    do_bf, dg = gate_bwd(cfg, dyb, o_att, proj, 0, None, BF16, "l2_gate_b")
    dq_full, dk_full, dv_bf = carried(lambda side: attn_bwd(cfg, qf, kf, vb, do_bf, o_att, lse, "l2_attn_b", side=side), BIG[3])
    dq_raw, dk_nope, d_kr = rs_end(3, lambda side: mla_prep_bwd(cfg, dq_full, dk_full, tabs, "l2_prep_b", side=side))
    dkv_raw = jnp.concatenate([dk_nope, dv_bf], axis=1)
    g_uq = mm_tn(cfg, cqn, dq_raw, 1, "l2_dw_uq")
    d_cqn = mm_nt(cfg, dq_raw, w2_uq, "l2_dcq")
    g_ukv = mm_tn(cfg, ckvn, dkv_raw, 1, "l2_dw_ukv")
    d_ckvn = mm_nt(cfg, dkv_raw, w2_ukv, "l2_dckv")
    d_c, nacc = mla_norm_bwd(cfg, proj, d_cqn, d_ckvn, d_kr, vec('l2_q_norm'), vec('l2_kv_norm'), "l2_norm_b")
    small['l2_q_norm'], small['l2_kv_norm'] = nacc[:, :ql], nacc[:, ql:ql + kl]
    d_in = jnp.concatenate([dg, d_c], axis=1)
    g_in = mm_tn(cfg, h2_bf, d_in, 1, "l2_dw_in")
    dh = mm_nt(cfg, d_in, w2_in, "l2_dh")
    g_in = _block(g_in[0][:, :d + ql + kl + cfg.rope], 1)
    g_uq = _block(g_uq.reshape(ql, hm, hp)[:, :, :qk].reshape(ql, hm * qk), 1)
    g_ukv = _block(g_ukv.reshape(kl, 2, hm, npe).transpose(0, 2, 1, 3).reshape(kl, 2 * hm * npe), 1)
    g2 = [g_in, g_uq, g_ukv, g_out.reshape(4, dq4, d)]

    dz1, dz1_bf, small['l1_ln_g'], small['l1_ln_b'] = ln_bwd(cfg, dz2, dh, xh1, rs1, vec('l1_ln_g'), alpha, "l1_ln_b")
    g_out = rs_begin(2, g2, lambda side: mm_tn(cfg, y1, dz1_bf, 1, "l1_dw_out", side=side))
    dyb = mm_nt(cfg, dz1_bf, w_out(1), "l1_dy")
    d_mm1, dg, small['l1_scale'] = gate_bwd(cfg, dyb, mm1, ug1, 1, vec('l1_scale'), BF16, "l1_gate_b")
    g_grp = bd_tn(cfg, p1, d_mm1, len(cfg.pool_windows), "l1_dw_grp")
    dp = bd_nt(cfg, d_mm1, w1_grp, "l1_dp")
    du = pool_bwd(cfg, dp, "l1_pool_b")
    d_in = jnp.concatenate([du, dg], axis=1)
    g_in = carried(lambda side: mm_tn(cfg, h1_bf, d_in, 4, "l1_dw_in", side=side), ['l2_w_in', 'l2_w_uq'])
    dh = carried(lambda side: mm_nt(cfg, d_in, w1_in, "l1_dh", side=side), ['l2_w_ukv', 'l2_w_out'])
    blocks_first = lambda g: jnp.moveaxis(g.reshape(g.shape[0], 4, g.shape[1] // 4, g.shape[2]), 1, 0)
    g1 = [g_in, blocks_first(g_grp), g_out.reshape(4, dq4, d)]

    dh1 = dh
    dz0, dz0_bf, small['l0_ln_g'], small['l0_ln_b'] = rs_end(
        2, lambda side: ln_bwd(cfg, dz1, dh1, xh0, rs0, vec('l0_ln_g'), alpha, "l0_ln_b", side=side))
    g_out = rs_begin(1, g1, lambda side: mm_tn(cfg, y0, dz0_bf, 1, "l0_dw_out", side=side))
    dyb = mm_nt(cfg, dz0_bf, w_out(0), "l0_dy")
    dg, dpa, dpx, du_dir, lacc0 = lru_bwd(cfg, dyb, ug0, hs0, hprev0, u0, pa0, px0, vec('l0_b_a'), vec('l0_b_x'), vec('l0_lam'), "l0_lru_b")
    g_a = bd_tn(cfg, u0_bf, dpa, cfg.lru_heads, "l0_dw_a")
    g_x = bd_tn(cfg, u0_bf, dpx, cfg.lru_heads, "l0_dw_x")
    du_a = bd_nt(cfg, dpa, w0_a, "l0_du_a")
    du_x = bd_nt(cfg, dpx, w0_x, "l0_du_x")
    du_pre, cacc = conv_bwd(cfg, du_dir, du_a, du_x, ug0, conv_w4, "l0_conv_b")
    d_in = jnp.concatenate([du_pre, dg], axis=1)
    g_in = carried(lambda side: mm_tn(cfg, h0_bf, d_in, 4, "l0_dw_in", side=side), ['l1_w_in'])
    dh = carried(lambda side: mm_nt(cfg, d_in, w0_in, "l0_dh", side=side), ['l1_w_grp', 'l1_w_out'])
    dh0 = rs_end(1, lambda side: axpy(cfg, dz0, dh, alpha, "dh0", side=side))[0]
    rs_begin(0, [g_in, blocks_first(g_a), blocks_first(g_x), g_out.reshape(4, dq4, d)])
    grad_x = dh0[nm:nm + seq][None]
    small['l0_lam'], small['l0_b_a'], small['l0_b_x'] = lacc0[0:1], lacc0[1:2], lacc0[2:3]
    small['l0_conv_b'] = cacc[4:5]

    rows = [dh0[i:i + 1] for i in range(nm)] + [cacc[k:k + 1] for k in range(CONV_W)] + [small[n] for n in SMALL_REPL]
    red = allreduce_small(_rows8(rows, d), "reduce_small")
    sh = lax.dynamic_slice(red[:nm + CONV_W], (0, p * dq4), (nm + CONV_W, dq4))
    grads['meta_tokens'] = sh[:nm]
    grads['l0_conv_w'] = sh[nm:].reshape(a['l0_conv_w'].shape)
    for i, n in enumerate(SMALL_REPL):
        grads[n] = red[nm + CONV_W + i, :a[n].shape[0]]

    delta, new_m, new_v = {}, {}, {}
    riders = {'l3_w_in': ['l0_w_in'], 'l1_w_in': ['l0_w_out'], 'l2_w_in': ['l0_w_a', 'l0_w_x']}
    for n in BIG[3] + BIG[2] + BIG[1]:
        update = lambda side=None, n=n: adamw(a[n], grads[n], a['m_' + n], a['v_' + n], f"adamw_{n}", side=side)
        delta[n], new_m[n], new_v[n] = carried(update, riders[n]) if n in riders else update()
    rs_end(0)
    for n in BIG[0]:
        delta[n], new_m[n], new_v[n] = adamw(a[n], grads[n], a['m_' + n], a['v_' + n], f"adamw_{n}")
    pack_s = lambda pre: jnp.concatenate([a[pre + 'meta_tokens'], a[pre + 'l0_conv_w'].reshape(CONV_W, dq4)], axis=0)
    ds_, ms_, vs_ = adamw(pack_s(''), sh, pack_s('m_'), pack_s('v_'), "adamw_small_sharded")
    for out, packed in ((delta, ds_), (new_m, ms_), (new_v, vs_)):
        out['meta_tokens'] = packed[:nm]
        out['l0_conv_w'] = packed[nm:].reshape(a['l0_conv_w'].shape)
    pack_r = lambda pre: jnp.stack([jnp.pad(a[pre + n], (0, d - a[n].shape[0])) for n in SMALL_REPL])
    dr_, mr_, vr_ = adamw(pack_r(''), red[nm + CONV_W:nm + CONV_W + len(SMALL_REPL)], pack_r('m_'), pack_r('v_'), "adamw_small_repl")
    for out, packed in ((delta, dr_), (new_m, mr_), (new_v, vr_)):
        for i, n in enumerate(SMALL_REPL):
            out[n] = packed[i, :a[n].shape[0]]

    return (loss, grad_x, *[grads[n] for n in W_NAMES], *[delta[n] for n in W_NAMES],
            *[new_m[n] for n in W_NAMES], *[new_v[n] for n in W_NAMES])


def kernel(x, meta_tokens, l0_w_in, l0_conv_w, l0_conv_b, l0_w_a, l0_b_a, l0_w_x, l0_b_x, l0_lam, l0_w_out, l0_ln_g, l0_ln_b, l1_w_in, l1_w_grp, l1_scale, l1_w_out, l1_ln_g, l1_ln_b, l2_w_in, l2_q_norm, l2_w_uq, l2_kv_norm, l2_w_ukv, l2_w_out, l2_ln_g, l2_ln_b, l3_w_in, l3_w_out, l3_ln_g, l3_ln_b, loss_target, m_meta_tokens, m_l0_w_in, m_l0_conv_w, m_l0_conv_b, m_l0_w_a, m_l0_b_a, m_l0_w_x, m_l0_b_x, m_l0_lam, m_l0_w_out, m_l0_ln_g, m_l0_ln_b, m_l1_w_in, m_l1_w_grp, m_l1_scale, m_l1_w_out, m_l1_ln_g, m_l1_ln_b, m_l2_w_in, m_l2_q_norm, m_l2_w_uq, m_l2_kv_norm, m_l2_w_ukv, m_l2_w_out, m_l2_ln_g, m_l2_ln_b, m_l3_w_in, m_l3_w_out, m_l3_ln_g, m_l3_ln_b, v_meta_tokens, v_l0_w_in, v_l0_conv_w, v_l0_conv_b, v_l0_w_a, v_l0_b_a, v_l0_w_x, v_l0_b_x, v_l0_lam, v_l0_w_out, v_l0_ln_g, v_l0_ln_b, v_l1_w_in, v_l1_w_grp, v_l1_scale, v_l1_w_out, v_l1_ln_g, v_l1_ln_b, v_l2_w_in, v_l2_q_norm, v_l2_w_uq, v_l2_kv_norm, v_l2_w_ukv, v_l2_w_out, v_l2_ln_g, v_l2_ln_b, v_l3_w_in, v_l3_w_out, v_l3_ln_g, v_l3_ln_b):
    return _step(REAL, dict(locals()))
```
